```python
import math
import jax, jax.numpy as jnp
from jax import lax
import numpy as np

D_MODEL = 2048
BATCH = 8
SEQ = 8192
DEPTH = 1

D_MIX = D_MODEL
HEAD_DIM = 64
D_ATTN = D_MIX // 2
D_CONV = D_MIX - D_ATTN
N_Q_HEADS = D_ATTN // HEAD_DIM
N_KV_HEADS = 4
GQA_GROUP = N_Q_HEADS // N_KV_HEADS
D_KV = N_KV_HEADS * HEAD_DIM
WINDOW = 128
BLOCK = 128
CONV_WIDTH = 31
CONV_GROUPS = D_CONV // HEAD_DIM
N_BUCKETS = 32
MAX_DISTANCE = 128
LN_EPS = 1e-5
ALPHA = (2.0 * DEPTH) ** 0.25
BETA = (8.0 * DEPTH) ** -0.25

SPLIT_SIZES = (D_ATTN, D_KV, D_KV, D_ATTN, D_CONV, D_CONV, D_CONV)
D_IN = sum(SPLIT_SIZES)
SPLIT_POINTS = [int(s) for s in np.cumsum(SPLIT_SIZES)[:-1]]

kernel_name = "hybrid_conformer_swa_sink_deepnorm_adaln"


def layer_norm(x, g, b):
    xf = x.astype(jnp.float32)
    mu = jnp.mean(xf, axis=-1, keepdims=True)
    var = jnp.mean(jnp.square(xf - mu), axis=-1, keepdims=True)
    y = (xf - mu) * lax.rsqrt(var + LN_EPS)
    return (y * g.astype(jnp.float32) + b.astype(jnp.float32)).astype(x.dtype)


def t5_bucket(dist):
    max_exact = N_BUCKETS // 2
    d = jnp.maximum(dist, 1).astype(jnp.float32)
    large = max_exact + (jnp.log(d / max_exact) / math.log(MAX_DISTANCE / max_exact)
                         * (N_BUCKETS - max_exact)).astype(jnp.int32)
    large = jnp.minimum(large, N_BUCKETS - 1)
    return jnp.where(dist < max_exact, dist, large)


def banded_sink_attention(q, k, v, rel_bias, sinks):
    B, S = q.shape[0], q.shape[1]
    nb = S // BLOCK
    f32 = jnp.float32
    qb = q.astype(f32).reshape(B, nb, BLOCK, N_KV_HEADS, GQA_GROUP, HEAD_DIM)
    kb = k.astype(f32).reshape(B, nb, BLOCK, N_KV_HEADS, HEAD_DIM)
    vb = v.astype(f32).reshape(B, nb, BLOCK, N_KV_HEADS, HEAD_DIM)
    prev = lambda t: jnp.concatenate([jnp.zeros_like(t[:, :1]), t[:, :-1]], axis=1)
    kw = jnp.concatenate([prev(kb), kb], axis=2)
    vw = jnp.concatenate([prev(vb), vb], axis=2)
    scores = jnp.einsum('bnqhgd,bnkhd->bnhgqk', qb, kw) * (HEAD_DIM ** -0.5)

    qi = jnp.arange(BLOCK, dtype=jnp.int32)[:, None]
    kj = jnp.arange(2 * BLOCK, dtype=jnp.int32)[None, :]
    dist = qi + BLOCK - kj
    in_window = (dist >= 0) & (dist < WINDOW)
    bias = rel_bias.astype(f32)[t5_bucket(jnp.maximum(dist, 0))]
    bias = bias.transpose(2, 0, 1).reshape(N_KV_HEADS, GQA_GROUP, BLOCK, 2 * BLOCK)
    key_pos = jnp.arange(nb, dtype=jnp.int32)[:, None] * BLOCK - BLOCK + kj
    valid = in_window[None] & (key_pos[:, None, :] >= 0)

    scores = jnp.where(valid[None, :, None, None], scores + bias, jnp.finfo(f32).min)
    sink = jnp.broadcast_to(sinks.astype(f32).reshape(1, 1, N_KV_HEADS, GQA_GROUP, 1, 1),
                            scores.shape[:-1] + (1,))
    probs = jax.nn.softmax(jnp.concatenate([scores, sink], axis=-1), axis=-1)[..., :-1]
    out = jnp.einsum('bnhgqk,bnkhd->bnqhgd', probs, vw)
    return out.reshape(B, S, N_Q_HEADS * HEAD_DIM).astype(q.dtype)


def conformer_conv(glu_a, glu_b, conv_w, conv_b, ln_g, ln_b, w_pw, b_pw):
    u = glu_a * jax.nn.sigmoid(glu_b)
    u = lax.conv_general_dilated(
        u, conv_w.reshape(CONV_WIDTH, 1, D_CONV).astype(u.dtype),
        window_strides=(1,), padding=[(CONV_WIDTH - 1, 0)],
        dimension_numbers=('NWC', 'WIO', 'NWC'), feature_group_count=D_CONV) + conv_b
    u = jax.nn.silu(layer_norm(u, ln_g, ln_b))
    return u @ w_pw + b_pw


def _fwd_setup_inputs(seed: int = 0) -> dict:
    key = jax.random.key(seed)
    ks = jax.random.split(key, 20)
    n = jax.random.normal
    f32 = jnp.float32
    x = n(ks[0], (BATCH, SEQ, D_MODEL), f32)
    c = n(ks[1], (BATCH, D_MODEL), f32)
    w_ada = 0.5 * D_MODEL ** -0.5 * n(ks[2], (DEPTH, D_MODEL, 3 * D_MODEL), f32)
    b_ada = 0.01 * n(ks[3], (DEPTH, 3 * D_MODEL), f32)
    col_scale = jnp.ones((D_IN,), f32).at[D_ATTN + D_KV:D_ATTN + 2 * D_KV].set(BETA)
    w_in = D_MODEL ** -0.5 * n(ks[4], (DEPTH, D_MODEL, D_IN), f32) * col_scale
    rel_bias = 0.5 * n(ks[5], (N_BUCKETS, N_Q_HEADS), f32)
    sinks = n(ks[6], (DEPTH, N_Q_HEADS), f32)
    conv_w = CONV_WIDTH ** -0.5 * n(ks[7], (DEPTH, CONV_WIDTH, D_CONV), f32)
    conv_b = 0.01 * n(ks[8], (DEPTH, D_CONV), f32)
    conv_ln_g = 1.0 + 0.01 * n(ks[9], (DEPTH, D_CONV), f32)
    conv_ln_b = 0.01 * n(ks[10], (DEPTH, D_CONV), f32)
    w_pw = BETA * D_CONV ** -0.5 * n(ks[11], (DEPTH, D_CONV, D_CONV), f32)
    b_pw = 0.01 * n(ks[12], (DEPTH, D_CONV), f32)
    w_out = BETA * D_MIX ** -0.5 * n(ks[13], (DEPTH, D_MIX, D_MODEL), f32)
    ln_g = 1.0 + 0.01 * n(ks[14], (DEPTH, D_MODEL), f32)
    ln_b = 0.01 * n(ks[15], (DEPTH, D_MODEL), f32)
    return {"x": x, "c": c, "w_ada": w_ada, "b_ada": b_ada, "w_in": w_in,
            "rel_bias": rel_bias, "sinks": sinks, "conv_w": conv_w, "conv_b": conv_b,
            "conv_ln_g": conv_ln_g, "conv_ln_b": conv_ln_b, "w_pw": w_pw, "b_pw": b_pw,
            "w_out": w_out, "ln_g": ln_g, "ln_b": ln_b}


def _fwd_reference(x, c, w_ada, b_ada, w_in, rel_bias, sinks, conv_w, conv_b, conv_ln_g,
              conv_ln_b, w_pw, b_pw, w_out, ln_g, ln_b):
    c_act = jax.nn.silu(c)
    for l in range(DEPTH):
        mod = (c_act @ w_ada[l] + b_ada[l])[:, None, :]
        shift, scale, gate = jnp.split(mod, 3, axis=-1)
        h = x * (1.0 + scale) + shift
        proj = h @ w_in[l]
        q, k, v, g_attn, glu_a, glu_b, g_conv = jnp.split(proj, SPLIT_POINTS, axis=-1)
        y_attn = banded_sink_attention(q, k, v, rel_bias, sinks[l]) * jax.nn.silu(g_attn)
        y_conv = conformer_conv(glu_a, glu_b, conv_w[l], conv_b[l], conv_ln_g[l], conv_ln_b[l],
                                w_pw[l], b_pw[l]) * jax.nn.silu(g_conv)
        y = jnp.concatenate([y_attn, y_conv], axis=-1) @ w_out[l]
        x = layer_norm(ALPHA * x + gate * y, ln_g[l], ln_b[l])
    return x


import jax as _jax
import jax.numpy as _jnp

TWIN_FORMAT = 'train_step'
FWD_PARAMS = ['x', 'c', 'w_ada', 'b_ada', 'w_in', 'rel_bias', 'sinks', 'conv_w', 'conv_b', 'conv_ln_g', 'conv_ln_b', 'w_pw', 'b_pw', 'w_out', 'ln_g', 'ln_b']
TWIN_WEIGHTS = ['w_ada', 'b_ada', 'w_in', 'rel_bias', 'sinks', 'conv_w', 'conv_b', 'conv_ln_g', 'conv_ln_b', 'w_pw', 'b_pw', 'w_out', 'ln_g', 'ln_b']
TWIN_DIFF_INPUT = 'x'
TWIN_INPUTS = ['x', 'c', 'w_ada', 'b_ada', 'w_in', 'rel_bias', 'sinks', 'conv_w', 'conv_b', 'conv_ln_g', 'conv_ln_b', 'w_pw', 'b_pw', 'w_out', 'ln_g', 'ln_b', 'loss_target', 'm_w_ada', 'm_b_ada', 'm_w_in', 'm_rel_bias', 'm_sinks', 'm_conv_w', 'm_conv_b', 'm_conv_ln_g', 'm_conv_ln_b', 'm_w_pw', 'm_b_pw', 'm_w_out', 'm_ln_g', 'm_ln_b', 'v_w_ada', 'v_b_ada', 'v_w_in', 'v_rel_bias', 'v_sinks', 'v_conv_w', 'v_conv_b', 'v_conv_ln_g', 'v_conv_ln_b', 'v_w_pw', 'v_b_pw', 'v_w_out', 'v_ln_g', 'v_ln_b']
TWIN_OUTPUTS = ['loss', 'grad_x', 'grad_w_ada', 'grad_b_ada', 'grad_w_in', 'grad_rel_bias', 'grad_sinks', 'grad_conv_w', 'grad_conv_b', 'grad_conv_ln_g', 'grad_conv_ln_b', 'grad_w_pw', 'grad_b_pw', 'grad_w_out', 'grad_ln_g', 'grad_ln_b', 'delta_w_ada', 'delta_b_ada', 'delta_w_in', 'delta_rel_bias', 'delta_sinks', 'delta_conv_w', 'delta_conv_b', 'delta_conv_ln_g', 'delta_conv_ln_b', 'delta_w_pw', 'delta_b_pw', 'delta_w_out', 'delta_ln_g', 'delta_ln_b', 'new_m_w_ada', 'new_m_b_ada', 'new_m_w_in', 'new_m_rel_bias', 'new_m_sinks', 'new_m_conv_w', 'new_m_conv_b', 'new_m_conv_ln_g', 'new_m_conv_ln_b', 'new_m_w_pw', 'new_m_b_pw', 'new_m_w_out', 'new_m_ln_g', 'new_m_ln_b', 'new_v_w_ada', 'new_v_b_ada', 'new_v_w_in', 'new_v_rel_bias', 'new_v_sinks', 'new_v_conv_w', 'new_v_conv_b', 'new_v_conv_ln_g', 'new_v_conv_ln_b', 'new_v_w_pw', 'new_v_b_pw', 'new_v_w_out', 'new_v_ln_g', 'new_v_ln_b']
TWIN_LEAF_KINDS = {'loss': 'loss', 'grad_x': 'grad_x', 'grad_w_ada': 'grad_w', 'grad_b_ada': 'grad_w', 'grad_w_in': 'grad_w', 'grad_rel_bias': 'grad_w', 'grad_sinks': 'grad_w', 'grad_conv_w': 'grad_w', 'grad_conv_b': 'grad_w', 'grad_conv_ln_g': 'grad_w', 'grad_conv_ln_b': 'grad_w', 'grad_w_pw': 'grad_w', 'grad_b_pw': 'grad_w', 'grad_w_out': 'grad_w', 'grad_ln_g': 'grad_w', 'grad_ln_b': 'grad_w', 'delta_w_ada': 'delta_w', 'delta_b_ada': 'delta_w', 'delta_w_in': 'delta_w', 'delta_rel_bias': 'delta_w', 'delta_sinks': 'delta_w', 'delta_conv_w': 'delta_w', 'delta_conv_b': 'delta_w', 'delta_conv_ln_g': 'delta_w', 'delta_conv_ln_b': 'delta_w', 'delta_w_pw': 'delta_w', 'delta_b_pw': 'delta_w', 'delta_w_out': 'delta_w', 'delta_ln_g': 'delta_w', 'delta_ln_b': 'delta_w', 'new_m_w_ada': 'new_m', 'new_m_b_ada': 'new_m', 'new_m_w_in': 'new_m', 'new_m_rel_bias': 'new_m', 'new_m_sinks': 'new_m', 'new_m_conv_w': 'new_m', 'new_m_conv_b': 'new_m', 'new_m_conv_ln_g': 'new_m', 'new_m_conv_ln_b': 'new_m', 'new_m_w_pw': 'new_m', 'new_m_b_pw': 'new_m', 'new_m_w_out': 'new_m', 'new_m_ln_g': 'new_m', 'new_m_ln_b': 'new_m', 'new_v_w_ada': 'new_v', 'new_v_b_ada': 'new_v', 'new_v_w_in': 'new_v', 'new_v_rel_bias': 'new_v', 'new_v_sinks': 'new_v', 'new_v_conv_w': 'new_v', 'new_v_conv_b': 'new_v', 'new_v_conv_ln_g': 'new_v', 'new_v_conv_ln_b': 'new_v', 'new_v_w_pw': 'new_v', 'new_v_b_pw': 'new_v', 'new_v_w_out': 'new_v', 'new_v_ln_g': 'new_v', 'new_v_ln_b': 'new_v'}


def _forward(args):
    return _fwd_reference(*[args[k] for k in FWD_PARAMS])


def _output_shape():
    def fwd():
        inp = _fwd_setup_inputs(0)
        return _fwd_reference(*[inp[k] for k in FWD_PARAMS])
    out = _jax.eval_shape(fwd)
    return out.shape, out.dtype

N_MICROBATCH = 1
ADAM_LR = 0.001
ADAM_B1 = 0.9
ADAM_B2 = 0.999
ADAM_EPS = 1e-08
ADAM_WD = 0.01
ADAM_STEP = 10
PER_EXAMPLE_BATCH_AXIS = {'x': 0, 'c': 0, 'loss_target': 0}
SHARED_INPUTS = []
_WEIGHT_DTYPES = {'w_ada': _jnp.float32, 'b_ada': _jnp.float32, 'w_in': _jnp.float32, 'rel_bias': _jnp.float32, 'sinks': _jnp.float32, 'conv_w': _jnp.float32, 'conv_b': _jnp.float32, 'conv_ln_g': _jnp.float32, 'conv_ln_b': _jnp.float32, 'w_pw': _jnp.float32, 'b_pw': _jnp.float32, 'w_out': _jnp.float32, 'ln_g': _jnp.float32, 'ln_b': _jnp.float32}
MOMENT_SCALE = {'w_ada': 5.251338e-03, 'b_ada': 8.866382e-03, 'w_in': 3.856120e-03, 'rel_bias': 2.181777e-03, 'sinks': 1.627087e-03, 'conv_w': 4.583410e-03, 'conv_b': 8.895483e-03, 'conv_ln_g': 5.256680e-03, 'conv_ln_b': 4.574276e-03, 'w_pw': 7.389274e-03, 'b_pw': 1.281179e-02, 'w_out': 6.197076e-03, 'ln_g': 3.194600e+01, 'ln_b': 2.634796e-01}


def _to_microbatches(a, axis):
    t = _jnp.moveaxis(a, axis, 0)
    t = t.reshape((N_MICROBATCH, t.shape[0] // N_MICROBATCH) + t.shape[1:])
    return _jnp.moveaxis(t, 1, axis + 1)


def setup_inputs(seed: int = 0) -> dict:
    inp = _fwd_setup_inputs(seed)
    key = _jax.random.fold_in(_jax.random.key(seed), 7919)
    shape, _ = _output_shape()
    out = dict(inp)
    out["loss_target"] = _jax.random.normal(_jax.random.fold_in(key, 0), shape, _jnp.float32)
    for i, name in enumerate(TWIN_WEIGHTS):
        w = inp[name].astype(_jnp.float32)
        if MOMENT_SCALE is None:
            s = _jnp.sqrt(_jnp.mean(_jnp.square(w)) + 1e-30)
        else:
            s = MOMENT_SCALE[name]
        km, kv = _jax.random.split(_jax.random.fold_in(key, i + 1))
        out[name] = w
        out["m_" + name] = s * _jax.random.normal(km, w.shape, _jnp.float32)
        out["v_" + name] = (s * s) * _jax.random.uniform(kv, w.shape, _jnp.float32, 0.5, 1.5)
    if N_MICROBATCH > 1:
        for name, axis in PER_EXAMPLE_BATCH_AXIS.items():
            out[name] = _to_microbatches(out[name], axis)
    return {'x': out['x'], 'c': out['c'], 'w_ada': out['w_ada'], 'b_ada': out['b_ada'], 'w_in': out['w_in'], 'rel_bias': out['rel_bias'], 'sinks': out['sinks'], 'conv_w': out['conv_w'], 'conv_b': out['conv_b'], 'conv_ln_g': out['conv_ln_g'], 'conv_ln_b': out['conv_ln_b'], 'w_pw': out['w_pw'], 'b_pw': out['b_pw'], 'w_out': out['w_out'], 'ln_g': out['ln_g'], 'ln_b': out['ln_b'], 'loss_target': out['loss_target'], 'm_w_ada': out['m_w_ada'], 'm_b_ada': out['m_b_ada'], 'm_w_in': out['m_w_in'], 'm_rel_bias': out['m_rel_bias'], 'm_sinks': out['m_sinks'], 'm_conv_w': out['m_conv_w'], 'm_conv_b': out['m_conv_b'], 'm_conv_ln_g': out['m_conv_ln_g'], 'm_conv_ln_b': out['m_conv_ln_b'], 'm_w_pw': out['m_w_pw'], 'm_b_pw': out['m_b_pw'], 'm_w_out': out['m_w_out'], 'm_ln_g': out['m_ln_g'], 'm_ln_b': out['m_ln_b'], 'v_w_ada': out['v_w_ada'], 'v_b_ada': out['v_b_ada'], 'v_w_in': out['v_w_in'], 'v_rel_bias': out['v_rel_bias'], 'v_sinks': out['v_sinks'], 'v_conv_w': out['v_conv_w'], 'v_conv_b': out['v_conv_b'], 'v_conv_ln_g': out['v_conv_ln_g'], 'v_conv_ln_b': out['v_conv_ln_b'], 'v_w_pw': out['v_w_pw'], 'v_b_pw': out['v_b_pw'], 'v_w_out': out['v_w_out'], 'v_ln_g': out['v_ln_g'], 'v_ln_b': out['v_ln_b']}


def _loss(weights, diff, rest, loss_target):
    with _jax.named_scope("forward"):
        args = {**rest, TWIN_DIFF_INPUT: diff, **{k: w.astype(_WEIGHT_DTYPES[k]) for k, w in weights.items()}}
        y = _forward(args)
    with _jax.named_scope("loss_head"):
        err = _jnp.square(y.astype(_jnp.float32) - loss_target)
        return 0.5 * _jnp.sum(_jnp.mean(err, axis=-1)) if err.ndim else 0.5 * err


def _adamw(w, g, m, v):
    m = ADAM_B1 * m + (1.0 - ADAM_B1) * g
    v = ADAM_B2 * v + (1.0 - ADAM_B2) * _jnp.square(g)
    m_hat = m / (1.0 - ADAM_B1 ** ADAM_STEP)
    v_hat = v / (1.0 - ADAM_B2 ** ADAM_STEP)
    delta = -ADAM_LR * (m_hat / (_jnp.sqrt(v_hat) + ADAM_EPS) + ADAM_WD * w)
    return delta, m, v


def reference(x, c, w_ada, b_ada, w_in, rel_bias, sinks, conv_w, conv_b, conv_ln_g, conv_ln_b, w_pw, b_pw, w_out, ln_g, ln_b, loss_target, m_w_ada, m_b_ada, m_w_in, m_rel_bias, m_sinks, m_conv_w, m_conv_b, m_conv_ln_g, m_conv_ln_b, m_w_pw, m_b_pw, m_w_out, m_ln_g, m_ln_b, v_w_ada, v_b_ada, v_w_in, v_rel_bias, v_sinks, v_conv_w, v_conv_b, v_conv_ln_g, v_conv_ln_b, v_w_pw, v_b_pw, v_w_out, v_ln_g, v_ln_b):
    given = dict(x=x, c=c, w_ada=w_ada, b_ada=b_ada, w_in=w_in, rel_bias=rel_bias, sinks=sinks, conv_w=conv_w, conv_b=conv_b, conv_ln_g=conv_ln_g, conv_ln_b=conv_ln_b, w_pw=w_pw, b_pw=b_pw, w_out=w_out, ln_g=ln_g, ln_b=ln_b, loss_target=loss_target, m_w_ada=m_w_ada, m_b_ada=m_b_ada, m_w_in=m_w_in, m_rel_bias=m_rel_bias, m_sinks=m_sinks, m_conv_w=m_conv_w, m_conv_b=m_conv_b, m_conv_ln_g=m_conv_ln_g, m_conv_ln_b=m_conv_ln_b, m_w_pw=m_w_pw, m_b_pw=m_b_pw, m_w_out=m_w_out, m_ln_g=m_ln_g, m_ln_b=m_ln_b, v_w_ada=v_w_ada, v_b_ada=v_b_ada, v_w_in=v_w_in, v_rel_bias=v_rel_bias, v_sinks=v_sinks, v_conv_w=v_conv_w, v_conv_b=v_conv_b, v_conv_ln_g=v_conv_ln_g, v_conv_ln_b=v_conv_ln_b, v_w_pw=v_w_pw, v_b_pw=v_b_pw, v_w_out=v_w_out, v_ln_g=v_ln_g, v_ln_b=v_ln_b)
    weights = {n: given[n] for n in TWIN_WEIGHTS}
    shared = {n: given[n] for n in SHARED_INPUTS}
    per_example = {n: given[n] for n in ['x', 'c']}
    grad_fn = _jax.value_and_grad(_loss, argnums=(0, 1))

    def one_microbatch(ex, loss_target):
        ex = dict(ex)
        diff = ex.pop(TWIN_DIFF_INPUT)
        return grad_fn(weights, diff, {**shared, **ex}, loss_target)

    if N_MICROBATCH == 1:
        loss, (grad_w, grad_x) = one_microbatch(per_example, given["loss_target"])
    else:
        def body(carry, xs):
            loss_sum, grad_sum = carry
            l_k, (gw_k, gx_k) = one_microbatch(xs[0], xs[1])
            with _jax.named_scope("update"):
                return (loss_sum + l_k, _jax.tree.map(_jnp.add, grad_sum, gw_k)), gx_k

        init = (_jnp.zeros((), _jnp.float32), _jax.tree.map(_jnp.zeros_like, weights))
        (loss, grad_w), grad_x = _jax.lax.scan(body, init, (per_example, given["loss_target"]))
    with _jax.named_scope("update"):
        delta_w, new_m, new_v = {}, {}, {}
        for n in TWIN_WEIGHTS:
            delta_w[n], new_m[n], new_v[n] = _adamw(weights[n], grad_w[n], given["m_" + n], given["v_" + n])
    return (loss, grad_x, *[grad_w[n] for n in TWIN_WEIGHTS], *[delta_w[n] for n in TWIN_WEIGHTS],
            *[new_m[n] for n in TWIN_WEIGHTS], *[new_v[n] for n in TWIN_WEIGHTS])
```

```python
import functools
import math

import jax
import jax.numpy as jnp
import numpy as np
from jax import lax
from jax.experimental import pallas as pl
from jax.experimental.pallas import tpu as pltpu

F32, BF16, I32 = jnp.float32, jnp.bfloat16, jnp.int32

D_MODEL = 2048
D_ATTN = 1024
D_CONV = 1024
D_KV = 256
HEAD_DIM = 64
N_Q_HEADS = 16
N_KV_HEADS = 4
GQA = 4
BLOCK = 128
CONV_WIDTH = 31
CONV_ROWS = 32
HALO = 32
N_BUCKETS = 32
MAX_DISTANCE = 128
LN_EPS = 1e-5
ALPHA = 2.0 ** 0.25
D_IN = 5632
N_DEV = 8
NEG = -1e30

ADAM_LR, ADAM_B1, ADAM_B2, ADAM_EPS, ADAM_WD, ADAM_STEP = 0.001, 0.9, 0.999, 1e-08, 0.01, 10

NT_DIMS = (((1,), (1,)), ((), ()))
TN_DIMS = (((0,), (0,)), ((), ()))
MIB = 1 << 20


def _pc(body, name, **kw):
    return pl.pallas_call(body, name=name, **kw)


def _cp(vmem_mib=None, sem=None):
    kw = {}
    if vmem_mib is not None:
        kw["vmem_limit_bytes"] = vmem_mib * MIB
    if sem is not None:
        kw["dimension_semantics"] = sem
    return pltpu.CompilerParams(**kw)


def _sig(x):
    return 1.0 / (1.0 + jnp.exp(-x))


def _dsilu(x, s):
    return s * (1.0 + x * (1.0 - s))


def _modulate(x, mod):
    S = x.shape[0]
    tm = min(512, S)

    def body(x_ref, sh_ref, sc_ref, h_ref):
        h_ref[...] = (x_ref[...] * (1.0 + sc_ref[...]) + sh_ref[...]).astype(BF16)

    return _pc(
        body, "modulate", grid=(S // tm,),
        in_specs=[pl.BlockSpec((tm, D_MODEL), lambda i: (i, 0)),
                  pl.BlockSpec((1, D_MODEL), lambda i: (0, 0)),
                  pl.BlockSpec((1, D_MODEL), lambda i: (0, 1))],
        out_specs=pl.BlockSpec((tm, D_MODEL), lambda i: (i, 0)),
        out_shape=jax.ShapeDtypeStruct((S, D_MODEL), BF16),
        compiler_params=_cp(40, ("parallel",)),
    )(x, mod, mod)


def _matmul_nt(a, wt, row_off, n, out_dtype, name):
    S, K = a.shape
    tm, tn = min(1024, S), 512
    assert row_off % tn == 0 and n % tn == 0
    ob = row_off // tn

    def body(a_ref, w_ref, o_ref):
        o_ref[...] = lax.dot_general(a_ref[...], w_ref[...], NT_DIMS, preferred_element_type=F32).astype(out_dtype)

    return _pc(
        body, name, grid=(S // tm, n // tn),
        in_specs=[pl.BlockSpec((tm, K), lambda i, j: (i, 0)),
                  pl.BlockSpec((tn, K), lambda i, j: (ob + j, 0))],
        out_specs=pl.BlockSpec((tm, tn), lambda i, j: (i, j)),
        out_shape=jax.ShapeDtypeStruct((S, n), out_dtype),
        compiler_params=_cp(48, ("parallel", "arbitrary")),
    )(a, wt)


def _matmul_tn(a, b, out_prev, row_off, m_total, name):
    S, M = a.shape
    N = b.shape[1]
    tm, ts = 512, min(512, S)
    assert row_off % tm == 0 and M % tm == 0
    ob = row_off // tm
    ns = S // ts

    def body(*refs):
        a_ref, b_ref = refs[0], refs[1]
        o_ref, acc = refs[-2], refs[-1]
        s = pl.program_id(1)

        @pl.when(s == 0)
        def _():
            acc[...] = jnp.zeros_like(acc)

        acc[...] += lax.dot_general(a_ref[...], b_ref[...], TN_DIMS, preferred_element_type=F32)

        @pl.when(s == ns - 1)
        def _():
            o_ref[...] = acc[...].astype(BF16)

    in_specs = [pl.BlockSpec((ts, tm), lambda i, s: (s, i)),
                pl.BlockSpec((ts, N), lambda i, s: (s, 0))]
    args = [a, b]
    aliases = {}
    if out_prev is not None:
        in_specs.append(pl.BlockSpec(memory_space=pl.ANY))
        args.append(out_prev)
        aliases = {2: 0}
    return _pc(
        body, name, grid=(M // tm, ns),
        in_specs=in_specs,
        out_specs=pl.BlockSpec((tm, N), lambda i, s: (ob + i, 0)),
        out_shape=jax.ShapeDtypeStruct((m_total, N), BF16),
        scratch_shapes=[pltpu.VMEM((tm, N), F32)],
        input_output_aliases=aliases,
        compiler_params=_cp(48, ("parallel", "arbitrary")),
    )(*args)


def _bucket_map():
    qi = jnp.arange(BLOCK, dtype=I32)[:, None]
    kj = jnp.arange(2 * BLOCK, dtype=I32)[None, :]
    dist = qi + BLOCK - kj
    in_window = (dist >= 0) & (dist < BLOCK)
    d0 = jnp.maximum(dist, 0)
    max_exact = N_BUCKETS // 2
    d = jnp.maximum(d0, 1).astype(F32)
    large = max_exact + (jnp.log(d / max_exact) / math.log(MAX_DISTANCE / max_exact)
                         * (N_BUCKETS - max_exact)).astype(I32)
    large = jnp.minimum(large, N_BUCKETS - 1)
    bucket = jnp.where(d0 < max_exact, d0, large)
    return jnp.where(in_window, bucket, -1).astype(I32)


def _bias_table(rel_bias, bmap):
    def body(rb_ref, bm_ref, o_ref):
        h = pl.program_id(0)
        bm = bm_ref[...]
        acc = jnp.full((BLOCK, 2 * BLOCK), NEG, F32)
        for b in range(N_BUCKETS):
            acc = jnp.where(bm == b, rb_ref[b, h], acc)
        o_ref[0] = acc

    return _pc(
        body, "bias_table", grid=(N_Q_HEADS,),
        in_specs=[pl.BlockSpec(memory_space=pltpu.SMEM),
                  pl.BlockSpec((BLOCK, 2 * BLOCK), lambda h: (0, 0))],
        out_specs=pl.BlockSpec((1, BLOCK, 2 * BLOCK), lambda h: (h, 0, 0)),
        out_shape=jax.ShapeDtypeStruct((N_Q_HEADS, BLOCK, 2 * BLOCK), F32),
    )(rel_bias, bmap)


def _relbias_grad(dbias, bmap):
    def body(db_ref, bm_ref, o_ref):
        bm = bm_ref[...]
        x = db_ref[0]
        lane = lax.broadcasted_iota(I32, (1, 128), 1)
        row = jnp.zeros((1, 128), F32)
        for b in range(N_BUCKETS):
            row = jnp.where(lane == b, jnp.sum(jnp.where(bm == b, x, 0.0)), row)
        o_ref[0] = jnp.broadcast_to(row, (8, 128))

    out = _pc(
        body, "relbias_grad", grid=(N_Q_HEADS,),
        in_specs=[pl.BlockSpec((1, BLOCK, 2 * BLOCK), lambda h: (h, 0, 0)),
                  pl.BlockSpec((BLOCK, 2 * BLOCK), lambda h: (0, 0))],
        out_specs=pl.BlockSpec((1, 8, 128), lambda h: (h, 0, 0)),
        out_shape=jax.ShapeDtypeStruct((N_Q_HEADS, 8, 128), F32),
    )(dbias, bmap)
    return out[:, 0, :]


def _stack_heads(ref, hk):
    return jnp.concatenate([ref[:, pl.ds(256 * hk + 64 * g, 64)] for g in range(GQA)], axis=0)


def _sink_col(sink_ref, hk):
    row = lax.broadcasted_iota(I32, (GQA * BLOCK, 1), 0)
    s = jnp.full((GQA * BLOCK, 1), sink_ref[0, 4 * hk + 3], F32)
    for g in (2, 1, 0):
        s = jnp.where(row < (g + 1) * BLOCK, sink_ref[0, 4 * hk + g], s)
    return s


def _attn_probs(q4, kw, bias_ref, sink_ref, hk, first_mask):
    s = lax.dot_general(q4, kw, NT_DIMS, preferred_element_type=F32) * (HEAD_DIM ** -0.5)
    s = s + bias_ref[4 * hk:4 * hk + 4].reshape(GQA * BLOCK, 2 * BLOCK)
    s = jnp.where(first_mask, s, NEG)
    sink = _sink_col(sink_ref, hk)
    m = jnp.maximum(jnp.max(s, axis=1, keepdims=True), sink)
    e = jnp.exp(s - m)
    es = jnp.exp(sink - m)
    inv = 1.0 / (jnp.sum(e, axis=1, keepdims=True) + es)
    return e * inv, es * inv


def _attn_fwd(qkv, g_attn, bias, sinks):
    S = qkv.shape[0]
    nb = S // BLOCK

    def body(q_ref, kc_ref, kp_ref, vc_ref, vp_ref, g_ref, bias_ref, sink_ref, y_ref, a_ref):
        n = pl.program_id(0)
        kj = lax.broadcasted_iota(I32, (GQA * BLOCK, 2 * BLOCK), 1)
        first_mask = (n > 0) | (kj >= BLOCK)
        for hk in range(N_KV_HEADS):
            q4 = _stack_heads(q_ref, hk)
            ks = pl.ds(64 * hk, 64)
            kw = jnp.concatenate([kp_ref[:, ks], kc_ref[:, ks]], axis=0)
            vw = jnp.concatenate([vp_ref[:, ks], vc_ref[:, ks]], axis=0)
            p, _ = _attn_probs(q4, kw, bias_ref, sink_ref, hk, first_mask)
            o4 = jnp.dot(p.astype(BF16), vw, preferred_element_type=F32)
            for g in range(GQA):
                sl = pl.ds(256 * hk + 64 * g, 64)
                og = o4[BLOCK * g:BLOCK * (g + 1)]
                gg = g_ref[:, sl]
                a_ref[:, sl] = og
                y_ref[:, sl] = (og * (gg * _sig(gg))).astype(BF16)

    prev = lambda n: jnp.maximum(n - 1, 0)
    return _pc(
        body, "attn_fwd", grid=(nb,),
        in_specs=[pl.BlockSpec((BLOCK, D_ATTN), lambda n: (n, 0)),
                  pl.BlockSpec((BLOCK, D_KV), lambda n: (n, 4)),
                  pl.BlockSpec((BLOCK, D_KV), lambda n: (prev(n), 4)),
                  pl.BlockSpec((BLOCK, D_KV), lambda n: (n, 5)),
                  pl.BlockSpec((BLOCK, D_KV), lambda n: (prev(n), 5)),
                  pl.BlockSpec((BLOCK, D_ATTN), lambda n: (n, 0)),
                  pl.BlockSpec((N_Q_HEADS, BLOCK, 2 * BLOCK), lambda n: (0, 0, 0)),
                  pl.BlockSpec(memory_space=pltpu.SMEM)],
        out_specs=[pl.BlockSpec((BLOCK, D_ATTN), lambda n: (n, 0)),
                   pl.BlockSpec((BLOCK, D_ATTN), lambda n: (n, 0))],
        out_shape=[jax.ShapeDtypeStruct((S, 2 * D_ATTN), BF16),
                   jax.ShapeDtypeStruct((S, D_ATTN), F32)],
        compiler_params=_cp(40, ("parallel",)),
    )(qkv, qkv, qkv, qkv, qkv, g_attn, bias, sinks)


def _attn_bwd(qkv, g_attn, a_out, dycat, bias, sinks):
    S = qkv.shape[0]
    nb = S // BLOCK
    R = GQA * BLOCK

    def body(q_ref, kc_ref, kp_ref, vc_ref, vp_ref, g_ref, a_ref, dy_ref, bias_ref, sink_ref,
             dqkv_ref, dg_ref, dbias_ref, dsink_ref,
             dq_scr, dq_new, dk_scr, dv_scr, dkw_scr, dvw_scr, ds_scr):
        n = pl.program_id(0)

        @pl.when(n == 0)
        def _():
            dbias_ref[...] = jnp.zeros_like(dbias_ref)
            ds_scr[...] = jnp.zeros_like(ds_scr)
            dq_scr[...] = jnp.zeros_like(dq_scr)
            dk_scr[...] = jnp.zeros_like(dk_scr)
            dv_scr[...] = jnp.zeros_like(dv_scr)

        @pl.when(n < nb)
        def _():
            kj = lax.broadcasted_iota(I32, (R, 2 * BLOCK), 1)
            first_mask = (n > 0) | (kj >= BLOCK)
            for hk in range(N_KV_HEADS):
                q4 = _stack_heads(q_ref, hk)
                ks = pl.ds(64 * hk, 64)
                kw = jnp.concatenate([kp_ref[:, ks], kc_ref[:, ks]], axis=0)
                vw = jnp.concatenate([vp_ref[:, ks], vc_ref[:, ks]], axis=0)
                p, psink = _attn_probs(q4, kw, bias_ref, sink_ref, hk, first_mask)
                da_parts, a_parts = [], []
                for g in range(GQA):
                    sl = pl.ds(256 * hk + 64 * g, 64)
                    gg = g_ref[:, sl]
                    sg = _sig(gg)
                    dyg = dy_ref[:, sl]
                    ag = a_ref[:, sl]
                    da_parts.append(dyg * (gg * sg))
                    a_parts.append(ag)
                    dg_ref[:, sl] = (dyg * ag * _dsilu(gg, sg)).astype(BF16)
                da4 = jnp.concatenate(da_parts, axis=0)
                a4 = jnp.concatenate(a_parts, axis=0)
                delta = jnp.sum(da4 * a4, axis=1, keepdims=True)
                da4b = da4.astype(BF16)
                dp = lax.dot_general(da4b, vw, NT_DIMS, preferred_element_type=F32)
                ds = p * (dp - delta)
                ds_scr[hk] += -psink * delta
                dbias_ref[4 * hk:4 * hk + 4] += ds.reshape(GQA, BLOCK, 2 * BLOCK)
                dsb = ds.astype(BF16)
                dq4 = jnp.dot(dsb, kw, preferred_element_type=F32) * (HEAD_DIM ** -0.5)
                for g in range(GQA):
                    dq_new[:, pl.ds(256 * hk + 64 * g, 64)] = dq4[BLOCK * g:BLOCK * (g + 1)].astype(BF16)
                dkw_scr[:, ks] = lax.dot_general(dsb, q4, TN_DIMS, preferred_element_type=F32) * (HEAD_DIM ** -0.5)
                dvw_scr[:, ks] = lax.dot_general(p.astype(BF16), da4b, TN_DIMS, preferred_element_type=F32)

        @pl.when(n == nb)
        def _():
            dkw_scr[0:BLOCK, :] = jnp.zeros((BLOCK, D_KV), F32)
            dvw_scr[0:BLOCK, :] = jnp.zeros((BLOCK, D_KV), F32)

        dqkv_ref[:, 0:D_ATTN] = dq_scr[...]
        dqkv_ref[:, D_ATTN:D_ATTN + D_KV] = (dk_scr[...] + dkw_scr[0:BLOCK, :]).astype(BF16)
        dqkv_ref[:, D_ATTN + D_KV:D_ATTN + 2 * D_KV] = (dv_scr[...] + dvw_scr[0:BLOCK, :]).astype(BF16)
        dq_scr[...] = dq_new[...]
        dk_scr[...] = dkw_scr[BLOCK:2 * BLOCK, :]
        dv_scr[...] = dvw_scr[BLOCK:2 * BLOCK, :]

        @pl.when(n == nb)
        def _():
            lane = lax.broadcasted_iota(I32, (1, 128), 1)
            row = jnp.zeros((1, 128), F32)
            for hk in range(N_KV_HEADS):
                col = ds_scr[hk]
                for g in range(GQA):
                    row = jnp.where(lane == 4 * hk + g, jnp.sum(col[BLOCK * g:BLOCK * (g + 1)]), row)
            dsink_ref[...] = jnp.broadcast_to(row, (8, 128))

    cur = lambda n: jnp.minimum(n, nb - 1)
    prev = lambda n: jnp.clip(n - 1, 0, nb - 1)
    return _pc(
        body, "attn_bwd", grid=(nb + 1,),
        in_specs=[pl.BlockSpec((BLOCK, D_ATTN), lambda n: (cur(n), 0)),
                  pl.BlockSpec((BLOCK, D_KV), lambda n: (cur(n), 4)),
                  pl.BlockSpec((BLOCK, D_KV), lambda n: (prev(n), 4)),
                  pl.BlockSpec((BLOCK, D_KV), lambda n: (cur(n), 5)),
                  pl.BlockSpec((BLOCK, D_KV), lambda n: (prev(n), 5)),
                  pl.BlockSpec((BLOCK, D_ATTN), lambda n: (cur(n), 0)),
                  pl.BlockSpec((BLOCK, D_ATTN), lambda n: (cur(n), 0)),
                  pl.BlockSpec((BLOCK, D_ATTN), lambda n: (cur(n), 0)),
                  pl.BlockSpec((N_Q_HEADS, BLOCK, 2 * BLOCK), lambda n: (0, 0, 0)),
                  pl.BlockSpec(memory_space=pltpu.SMEM)],
        out_specs=[pl.BlockSpec((BLOCK, D_ATTN + 2 * D_KV), lambda n: (prev(n), 0)),
                   pl.BlockSpec((BLOCK, D_ATTN), lambda n: (cur(n), 0)),
                   pl.BlockSpec((N_Q_HEADS, BLOCK, 2 * BLOCK), lambda n: (0, 0, 0)),
                   pl.BlockSpec((8, 128), lambda n: (0, 0))],
        out_shape=[jax.ShapeDtypeStruct((S, D_ATTN + 2 * D_KV), BF16),
                   jax.ShapeDtypeStruct((S, D_ATTN), BF16),
                   jax.ShapeDtypeStruct((N_Q_HEADS, BLOCK, 2 * BLOCK), F32),
                   jax.ShapeDtypeStruct((8, 128), F32)],
        scratch_shapes=[pltpu.VMEM((BLOCK, D_ATTN), BF16), pltpu.VMEM((BLOCK, D_ATTN), BF16),
                        pltpu.VMEM((BLOCK, D_KV), F32), pltpu.VMEM((BLOCK, D_KV), F32),
                        pltpu.VMEM((2 * BLOCK, D_KV), F32), pltpu.VMEM((2 * BLOCK, D_KV), F32),
                        pltpu.VMEM((N_KV_HEADS, R, 1), F32)],
        compiler_params=_cp(48, ("arbitrary",)),
    )(qkv, qkv, qkv, qkv, qkv, g_attn, a_out, dycat, bias, sinks)


def _conv_tile(S):
    return min(256, S)


def _shifted(win, s):
    return win if s == 0 else pltpu.roll(win, win.shape[0] - s, axis=0)


def _conv_fwd(glu, g_conv, ycat_in, conv_w, conv_b, ln_g, ln_b, w_pw, b_pw):
    S = glu.shape[0]
    T = _conv_tile(S)
    hb = T // HALO

    def body(a_ref, ah_ref, b_ref, bh_ref, gc_ref, cw_ref, cb_ref, lg_ref, lb_ref, wpw_ref, bpw_ref, ycat_in_ref,
             y_ref, u1_ref, u3_ref, p_ref):
        i = pl.program_id(0)
        u0c = a_ref[...] * _sig(b_ref[...])
        u0h = jnp.where(i > 0, ah_ref[...] * _sig(bh_ref[...]), 0.0)
        win = jnp.concatenate([u0h, u0c], axis=0)
        acc = jnp.zeros((T, D_CONV), F32)
        for s in range(8):
            ws = _shifted(win, s)
            for aa in range(5):
                j = 8 * aa + s - 2
                if 0 <= j < CONV_WIDTH:
                    acc = acc + ws[8 * aa:8 * aa + T] * cw_ref[j:j + 1, :]
        u1 = acc + cb_ref[...]
        u1_ref[...] = u1
        mu = jnp.mean(u1, axis=1, keepdims=True)
        uc = u1 - mu
        rstd = lax.rsqrt(jnp.mean(uc * uc, axis=1, keepdims=True) + LN_EPS)
        u2 = uc * rstd * lg_ref[...] + lb_ref[...]
        u3 = (u2 * _sig(u2)).astype(BF16)
        u3_ref[...] = u3
        p = jnp.dot(u3, wpw_ref[...], preferred_element_type=F32) + bpw_ref[...]
        p_ref[...] = p
        gc = gc_ref[...]
        y_ref[...] = (p * (gc * _sig(gc))).astype(BF16)

    halo = lambda i: jnp.maximum(i * hb - 1, 0)
    vec = pl.BlockSpec((1, D_CONV), lambda i: (0, 0))
    return _pc(
        body, "conv_fwd", grid=(S // T,),
        in_specs=[pl.BlockSpec((T, D_CONV), lambda i: (i, 0)),
                  pl.BlockSpec((HALO, D_CONV), lambda i: (halo(i), 0)),
                  pl.BlockSpec((T, D_CONV), lambda i: (i, 1)),
                  pl.BlockSpec((HALO, D_CONV), lambda i: (halo(i), 1)),
                  pl.BlockSpec((T, D_CONV), lambda i: (i, 0)),
                  pl.BlockSpec((CONV_ROWS, D_CONV), lambda i: (0, 0)),
                  vec, vec, vec,
                  pl.BlockSpec((D_CONV, D_CONV), lambda i: (0, 0)),
                  vec,
                  pl.BlockSpec(memory_space=pl.ANY)],
        out_specs=[pl.BlockSpec((T, D_CONV), lambda i: (i, 1)),
                   pl.BlockSpec((T, D_CONV), lambda i: (i, 0)),
                   pl.BlockSpec((T, D_CONV), lambda i: (i, 0)),
                   pl.BlockSpec((T, D_CONV), lambda i: (i, 0))],
        out_shape=[jax.ShapeDtypeStruct((S, 2 * D_CONV), BF16),
                   jax.ShapeDtypeStruct((S, D_CONV), F32),
                   jax.ShapeDtypeStruct((S, D_CONV), BF16),
                   jax.ShapeDtypeStruct((S, D_CONV), F32)],
        input_output_aliases={11: 0},
        compiler_params=_cp(48, ("parallel",)),
    )(glu, glu, glu, glu, g_conv, conv_w, conv_b, ln_g, ln_b, w_pw, b_pw, ycat_in)


def _conv_bwd_a(dycat, g_conv, p_out, u1, ln_g, ln_b, w_pw):
    S = u1.shape[0]
    T = _conv_tile(S)

    def body(dy_ref, gc_ref, p_ref, u1_ref, lg_ref, lb_ref, wpw_ref,
             dp_ref, dgc_ref, du1_ref, gbpw_ref, glg_ref, glb_ref, gcb_ref):
        i = pl.program_id(0)

        @pl.when(i == 0)
        def _():
            for r in (gbpw_ref, glg_ref, glb_ref, gcb_ref):
                r[...] = jnp.zeros_like(r)

        dy = dy_ref[...]
        gc = gc_ref[...]
        sg = _sig(gc)
        dp = dy * (gc * sg)
        dgc_ref[...] = (dy * p_ref[...] * _dsilu(gc, sg)).astype(BF16)
        gbpw_ref[...] += jnp.sum(dp, axis=0, keepdims=True)
        dpb = dp.astype(BF16)
        dp_ref[...] = dpb
        du3 = lax.dot_general(dpb, wpw_ref[...], NT_DIMS, preferred_element_type=F32)
        u1 = u1_ref[...]
        mu = jnp.mean(u1, axis=1, keepdims=True)
        uc = u1 - mu
        rstd = lax.rsqrt(jnp.mean(uc * uc, axis=1, keepdims=True) + LN_EPS)
        uh = uc * rstd
        lg = lg_ref[...]
        u2 = uh * lg + lb_ref[...]
        s2 = _sig(u2)
        du2 = du3 * _dsilu(u2, s2)
        glg_ref[...] += jnp.sum(du2 * uh, axis=0, keepdims=True)
        glb_ref[...] += jnp.sum(du2, axis=0, keepdims=True)
        duh = du2 * lg
        du1 = rstd * (duh - jnp.mean(duh, axis=1, keepdims=True) - uh * jnp.mean(duh * uh, axis=1, keepdims=True))
        du1_ref[...] = du1
        gcb_ref[...] += jnp.sum(du1, axis=0, keepdims=True)

    vec = pl.BlockSpec((1, D_CONV), lambda i: (0, 0))
    tile = pl.BlockSpec((T, D_CONV), lambda i: (i, 0))
    vshape = jax.ShapeDtypeStruct((1, D_CONV), F32)
    return _pc(
        body, "conv_bwd_a", grid=(S // T,),
        in_specs=[pl.BlockSpec((T, D_CONV), lambda i: (i, 1)), tile, tile, tile, vec, vec,
                  pl.BlockSpec((D_CONV, D_CONV), lambda i: (0, 0))],
        out_specs=[tile, tile, tile, vec, vec, vec, vec],
        out_shape=[jax.ShapeDtypeStruct((S, D_CONV), BF16), jax.ShapeDtypeStruct((S, D_CONV), BF16),
                   jax.ShapeDtypeStruct((S, D_CONV), F32), vshape, vshape, vshape, vshape],
        compiler_params=_cp(48, ("arbitrary",)),
    )(dycat, g_conv, p_out, u1, ln_g, ln_b, w_pw)


def _conv_bwd_b(du1, glu, conv_w):
    S = du1.shape[0]
    T = _conv_tile(S)
    hb = T // HALO
    nt = S // T
    last_h = S // HALO - 1

    def body(du_ref, dun_ref, a_ref, ah_ref, b_ref, bh_ref, cw_ref, dab_ref, gw_ref):
        i = pl.program_id(0)

        @pl.when(i == 0)
        def _():
            gw_ref[...] = jnp.zeros_like(gw_ref)

        a = a_ref[...]
        sb = _sig(b_ref[...])
        u0h = jnp.where(i > 0, ah_ref[...] * _sig(bh_ref[...]), 0.0)
        win = jnp.concatenate([u0h, a * sb], axis=0)
        du1c = du_ref[...]
        du1n = jnp.where(i < nt - 1, dun_ref[...], 0.0)
        win2 = jnp.concatenate([du1c, du1n], axis=0)
        acc = jnp.zeros((T, D_CONV), F32)
        for s in range(8):
            w2 = _shifted(win2, s)
            for aa in range(4):
                j = CONV_WIDTH - 1 - (8 * aa + s)
                if 0 <= j < CONV_WIDTH:
                    acc = acc + w2[8 * aa:8 * aa + T] * cw_ref[j:j + 1, :]
        dab_ref[:, 0:D_CONV] = (acc * sb).astype(BF16)
        dab_ref[:, D_CONV:2 * D_CONV] = (acc * a * sb * (1.0 - sb)).astype(BF16)
        for s in range(8):
            ws = _shifted(win, s)
            for aa in range(5):
                j = 8 * aa + s - 2
                if 0 <= j < CONV_WIDTH:
                    gw_ref[j:j + 1, :] += jnp.sum(du1c * ws[8 * aa:8 * aa + T], axis=0, keepdims=True)

    halo = lambda i: jnp.maximum(i * hb - 1, 0)
    nxt = lambda i: jnp.minimum((i + 1) * hb, last_h)
    return _pc(
        body, "conv_bwd_b", grid=(nt,),
        in_specs=[pl.BlockSpec((T, D_CONV), lambda i: (i, 0)),
                  pl.BlockSpec((HALO, D_CONV), lambda i: (nxt(i), 0)),
                  pl.BlockSpec((T, D_CONV), lambda i: (i, 0)),
                  pl.BlockSpec((HALO, D_CONV), lambda i: (halo(i), 0)),
                  pl.BlockSpec((T, D_CONV), lambda i: (i, 1)),
                  pl.BlockSpec((HALO, D_CONV), lambda i: (halo(i), 1)),
                  pl.BlockSpec((CONV_ROWS, D_CONV), lambda i: (0, 0))],
        out_specs=[pl.BlockSpec((T, 2 * D_CONV), lambda i: (i, 0)),
                   pl.BlockSpec((CONV_ROWS, D_CONV), lambda i: (0, 0))],
        out_shape=[jax.ShapeDtypeStruct((S, 2 * D_CONV), BF16),
                   jax.ShapeDtypeStruct((CONV_ROWS, D_CONV), F32)],
        compiler_params=_cp(48, ("arbitrary",)),
    )(du1, du1, glu, glu, glu, glu, conv_w)


def _outproj_ln(ycat, w_out, x, target, mod, ln_g, ln_b):
    S = x.shape[0]
    tm = min(256, S)

    def body(yc_ref, w_ref, x_ref, t_ref, gate_ref, lg_ref, lb_ref,
             dz_ref, dy_ref, dyc_ref, loss_ref, glg_ref, glb_ref, dgate_ref):
        i = pl.program_id(0)

        @pl.when(i == 0)
        def _():
            for r in (loss_ref, glg_ref, glb_ref, dgate_ref):
                r[...] = jnp.zeros_like(r)

        w = w_ref[...]
        y = jnp.dot(yc_ref[...], w, preferred_element_type=F32)
        gate = gate_ref[...]
        z = ALPHA * x_ref[...] + gate * y
        mu = jnp.mean(z, axis=1, keepdims=True)
        zc = z - mu
        rstd = lax.rsqrt(jnp.mean(zc * zc, axis=1, keepdims=True) + LN_EPS)
        zh = zc * rstd
        lg = lg_ref[...]
        err = zh * lg + lb_ref[...] - t_ref[...]
        loss_ref[...] += 0.5 * jnp.sum(jnp.sum(err * err, axis=1, keepdims=True)) / D_MODEL
        dout = err * (1.0 / D_MODEL)
        glg_ref[...] += jnp.sum(dout * zh, axis=0, keepdims=True)
        glb_ref[...] += jnp.sum(dout, axis=0, keepdims=True)
        dzh = dout * lg
        dz = rstd * (dzh - jnp.mean(dzh, axis=1, keepdims=True) - zh * jnp.mean(dzh * zh, axis=1, keepdims=True))
        dz_ref[...] = dz
        dgate_ref[...] += jnp.sum(dz * y, axis=0, keepdims=True)
        dy = (dz * gate).astype(BF16)
        dy_ref[...] = dy
        dyc_ref[...] = lax.dot_general(dy, w, NT_DIMS, preferred_element_type=F32)

    vec = pl.BlockSpec((1, D_MODEL), lambda i: (0, 0))
    tile = pl.BlockSpec((tm, D_MODEL), lambda i: (i, 0))
    vshape = jax.ShapeDtypeStruct((1, D_MODEL), F32)
    return _pc(
        body, "outproj_ln", grid=(S // tm,),
        in_specs=[tile, pl.BlockSpec((D_MODEL, D_MODEL), lambda i: (0, 0)), tile, tile,
                  pl.BlockSpec((1, D_MODEL), lambda i: (0, 2)), vec, vec],
        out_specs=[tile, tile, tile, pl.BlockSpec((8, 128), lambda i: (0, 0)), vec, vec, vec],
        out_shape=[jax.ShapeDtypeStruct((S, D_MODEL), F32), jax.ShapeDtypeStruct((S, D_MODEL), BF16),
                   jax.ShapeDtypeStruct((S, D_MODEL), F32), jax.ShapeDtypeStruct((8, 128), F32),
                   vshape, vshape, vshape],
        compiler_params=_cp(56, ("arbitrary",)),
    )(ycat, w_out, x, target, mod, ln_g, ln_b)


def _dh_kernel(segs, wt, dz, x, mod):
    S = x.shape[0]
    tm, tk = min(512, S), 512
    bounds = []
    k0 = 0
    for a in segs:
        assert a.shape[1] % tk == 0
        bounds.append((k0, a.shape[1] // tk))
        k0 += a.shape[1] // tk
    nk = k0
    assert nk * tk == wt.shape[0]
    nseg = len(segs)

    def body(*refs):
        seg_refs = refs[:nseg]
        w_ref, dz_ref, x_ref, sc_ref = refs[nseg:nseg + 4]
        gx_ref, dsh_ref, dsc_ref, acc = refs[nseg + 4:]
        i, k = pl.program_id(0), pl.program_id(1)

        @pl.when((i == 0) & (k == 0))
        def _():
            dsh_ref[...] = jnp.zeros_like(dsh_ref)
            dsc_ref[...] = jnp.zeros_like(dsc_ref)

        @pl.when(k == 0)
        def _():
            acc[...] = jnp.zeros_like(acc)

        for (s0, sn), r in zip(bounds, seg_refs):
            @pl.when((k >= s0) & (k < s0 + sn))
            def _(r=r):
                acc[...] += jnp.dot(r[...], w_ref[...], preferred_element_type=F32)

        @pl.when(k == nk - 1)
        def _():
            dh = acc[...]
            gx_ref[...] = ALPHA * dz_ref[...] + dh * (1.0 + sc_ref[...])
            dsh_ref[...] += jnp.sum(dh, axis=0, keepdims=True)
            dsc_ref[...] += jnp.sum(dh * x_ref[...], axis=0, keepdims=True)

    def seg_spec(s0, sn):
        return pl.BlockSpec((tm, tk), lambda i, k: (i, jnp.clip(k - s0, 0, sn - 1)))

    tile = pl.BlockSpec((tm, D_MODEL), lambda i, k: (i, 0))
    vec = pl.BlockSpec((1, D_MODEL), lambda i, k: (0, 0))
    vshape = jax.ShapeDtypeStruct((1, D_MODEL), F32)
    return _pc(
        body, "dh_gradx", grid=(S // tm, nk),
        in_specs=[seg_spec(*b) for b in bounds] + [
            pl.BlockSpec((tk, D_MODEL), lambda i, k: (k, 0)), tile, tile,
            pl.BlockSpec((1, D_MODEL), lambda i, k: (0, 1))],
        out_specs=[tile, vec, vec],
        out_shape=[jax.ShapeDtypeStruct((S, D_MODEL), F32), vshape, vshape],
        scratch_shapes=[pltpu.VMEM((tm, D_MODEL), F32)],
        compiler_params=_cp(56, ("arbitrary", "arbitrary")),
    )(*segs, wt, dz, x, mod)


def _row_tile(rows, cols):
    if rows * cols * 4 <= 2 * MIB or rows % 8:
        return rows
    tr = max(8, (2 * MIB // (cols * 4)) // 8 * 8)
    while rows % tr:
        tr -= 8
    return tr


def _sum8(recv, name):
    _, R, C = recv.shape
    tr = _row_tile(R, C)

    def body(r_ref, o_ref):
        acc = r_ref[0].astype(F32)
        for d in range(1, N_DEV):
            acc = acc + r_ref[d].astype(F32)
        o_ref[...] = acc

    return _pc(
        body, name, grid=(R // tr,),
        in_specs=[pl.BlockSpec((N_DEV, tr, C), lambda i: (0, i, 0))],
        out_specs=pl.BlockSpec((tr, C), lambda i: (i, 0)),
        out_shape=jax.ShapeDtypeStruct((R, C), F32),
        compiler_params=_cp(40, ("parallel",)),
    )(recv)


def _adamw(w, g, m, v, name):
    R, C = w.shape
    tr = _row_tile(R, C)

    def body(w_ref, g_ref, m_ref, v_ref, d_ref, nm_ref, nv_ref):
        g_ = g_ref[...]
        m_ = ADAM_B1 * m_ref[...] + (1.0 - ADAM_B1) * g_
        v_ = ADAM_B2 * v_ref[...] + (1.0 - ADAM_B2) * (g_ * g_)
        m_hat = m_ / (1.0 - ADAM_B1 ** ADAM_STEP)
        v_hat = v_ / (1.0 - ADAM_B2 ** ADAM_STEP)
        d_ref[...] = -ADAM_LR * (m_hat / (jnp.sqrt(v_hat) + ADAM_EPS) + ADAM_WD * w_ref[...])
        nm_ref[...] = m_
        nv_ref[...] = v_

    spec = pl.BlockSpec((tr, C), lambda i: (i, 0))
    shape = jax.ShapeDtypeStruct((R, C), F32)
    return _pc(
        body, name, grid=(R // tr,),
        in_specs=[spec] * 4, out_specs=[spec] * 3, out_shape=[shape] * 3,
        compiler_params=_cp(40, ("parallel",)),
    )(w, g, m, v)


def _small_mm(a, b, name):
    def body(a_ref, b_ref, o_ref):
        o_ref[...] = jnp.dot(a_ref[...], b_ref[...], preferred_element_type=F32)

    return _pc(body, name, out_shape=jax.ShapeDtypeStruct((a.shape[0], b.shape[1]), F32),
               compiler_params=_cp(40))(a, b)


def _me():
    return lax.axis_index("x"), lax.axis_index("y"), lax.axis_index("c")


def _peer(k):
    x, y, c = _me()
    px = 1 - x if k & 4 else x
    py = 1 - y if k & 2 else y
    pc = 1 - c if k & 1 else c
    return (px, py, pc), 4 * px + 2 * py + pc


def _remote(src, dst, send_sem, recv_sem, dev):
    return pltpu.make_async_remote_copy(src_ref=src, dst_ref=dst, send_sem=send_sem, recv_sem=recv_sem,
                                        device_id=dev, device_id_type=pl.DeviceIdType.MESH)


HBM_SPEC = pl.BlockSpec(memory_space=pl.ANY)
VMEM_SPEC = pl.BlockSpec(memory_space=pltpu.VMEM)


def _all_gather_hbm(shards, name):
    nt = len(shards)

    def body(*refs):
        src = refs[:nt]
        dst = refs[nt:2 * nt]
        send_sems, recv_sems, local_sems = refs[2 * nt:]
        x, y, c = _me()
        me = 4 * x + 2 * y + c

        def rows(t, blk):
            r = src[t].shape[0]
            return dst[t].at[pl.ds(pl.multiple_of(blk * r, 8), r), :]

        local = [pltpu.make_async_copy(src[t], rows(t, me), local_sems.at[t]) for t in range(nt)]
        for cp in local:
            cp.start()
        sends = []
        for k in range(1, N_DEV):
            dev, _ = _peer(k)
            for t in range(nt):
                cp = _remote(src[t], rows(t, me), send_sems.at[t, k - 1], recv_sems.at[t, k - 1], dev)
                cp.start()
                sends.append(cp)
        for k in range(1, N_DEV):
            dev, blk = _peer(k)
            for t in range(nt):
                _remote(src[t], rows(t, blk), send_sems.at[t, k - 1], recv_sems.at[t, k - 1], dev).wait_recv()
        for cp in sends:
            cp.wait_send()
        for cp in local:
            cp.wait()

    return _pc(
        body, name,
        in_specs=[HBM_SPEC] * nt, out_specs=[HBM_SPEC] * nt,
        out_shape=[jax.ShapeDtypeStruct((N_DEV * s.shape[0], s.shape[1]), s.dtype) for s in shards],
        scratch_shapes=[pltpu.SemaphoreType.DMA((nt, N_DEV - 1)), pltpu.SemaphoreType.DMA((nt, N_DEV - 1)),
                        pltpu.SemaphoreType.DMA((nt,))],
        compiler_params=pltpu.CompilerParams(has_side_effects=True),
    )(*shards)


def _reduce_scatter_hbm(fulls, name):
    nt = len(fulls)

    def body(*refs):
        src = refs[:nt]
        dst = refs[nt:2 * nt]
        send_sems, recv_sems, local_sems = refs[2 * nt:]
        x, y, c = _me()
        me = 4 * x + 2 * y + c

        def rows(ref, t, blk):
            r = src[t].shape[0] // N_DEV
            return ref[t].at[pl.ds(pl.multiple_of(blk * r, 8), r), :]

        local = [pltpu.make_async_copy(rows(src, t, me), rows(dst, t, me), local_sems.at[t]) for t in range(nt)]
        for cp in local:
            cp.start()
        sends = []
        for k in range(1, N_DEV):
            dev, blk = _peer(k)
            for t in range(nt):
                cp = _remote(rows(src, t, blk), rows(dst, t, me), send_sems.at[t, k - 1], recv_sems.at[t, k - 1], dev)
                cp.start()
                sends.append(cp)
        for k in range(1, N_DEV):
            dev, blk = _peer(k)
            for t in range(nt):
                _remote(rows(src, t, me), rows(dst, t, blk), send_sems.at[t, k - 1], recv_sems.at[t, k - 1],
                        dev).wait_recv()
        for cp in sends:
            cp.wait_send()
        for cp in local:
            cp.wait()

    return _pc(
        body, name,
        in_specs=[HBM_SPEC] * nt, out_specs=[HBM_SPEC] * nt,
        out_shape=[jax.ShapeDtypeStruct(f.shape, f.dtype) for f in fulls],
        scratch_shapes=[pltpu.SemaphoreType.DMA((nt, N_DEV - 1)), pltpu.SemaphoreType.DMA((nt, N_DEV - 1)),
                        pltpu.SemaphoreType.DMA((nt,))],
        compiler_params=pltpu.CompilerParams(has_side_effects=True),
    )(*fulls)


def _ada_fwd(c, w_ada, b_ada_cols):
    ncol = w_ada.shape[1]

    def body(c_ref, w_ref, b_ref, mod_ref, call_ref, cact, cmat, mloc, send1, recv1, send2, recv2):
        x, y, z = _me()
        me = 4 * x + 2 * y + z
        cv = c_ref[...]
        cact[...] = cv * _sig(cv)
        call_ref[me] = cact[...]
        sends = []
        for k in range(1, N_DEV):
            dev, _ = _peer(k)
            cp = _remote(cact, call_ref.at[me], send1.at[k - 1], recv1.at[k - 1], dev)
            cp.start()
            sends.append(cp)
        for k in range(1, N_DEV):
            dev, blk = _peer(k)
            _remote(cact, call_ref.at[blk], send1.at[k - 1], recv1.at[k - 1], dev).wait_recv()
        cmat[...] = jnp.zeros_like(cmat)
        for b in range(N_DEV):
            cmat[b:b + 1, :] = call_ref[b]
        m = jnp.dot(cmat[...].astype(BF16), w_ref[...].astype(BF16), preferred_element_type=F32) + b_ref[...]
        for b in range(N_DEV):
            mloc[b] = m[b:b + 1, :]
        mod_ref[me] = mloc[me]
        for k in range(1, N_DEV):
            dev, blk = _peer(k)
            cp = _remote(mloc.at[blk], mod_ref.at[me], send2.at[k - 1], recv2.at[k - 1], dev)
            cp.start()
            sends.append(cp)
        for k in range(1, N_DEV):
            dev, blk = _peer(k)
            _remote(mloc.at[me], mod_ref.at[blk], send2.at[k - 1], recv2.at[k - 1], dev).wait_recv()
        for cp in sends:
            cp.wait_send()

    return _pc(
        body, "ada_fwd",
        in_specs=[VMEM_SPEC] * 3, out_specs=[VMEM_SPEC] * 2,
        out_shape=[jax.ShapeDtypeStruct((N_DEV, 1, ncol), F32), jax.ShapeDtypeStruct((N_DEV, 1, D_MODEL), F32)],
        scratch_shapes=[pltpu.VMEM((1, D_MODEL), F32), pltpu.VMEM((16, D_MODEL), F32),
                        pltpu.VMEM((N_DEV, 1, ncol), F32)] + [pltpu.SemaphoreType.DMA((N_DEV - 1,))] * 4,
        compiler_params=pltpu.CompilerParams(has_side_effects=True, vmem_limit_bytes=40 * MIB),
    )(c, w_ada, b_ada_cols)


def _small_gather(vec):
    n = vec.shape[1]

    def body(v_ref, all_ref, sum_ref, send, recv):
        x, y, z = _me()
        me = 4 * x + 2 * y + z
        all_ref[me] = v_ref[...]
        sends = []
        for k in range(1, N_DEV):
            dev, _ = _peer(k)
            cp = _remote(v_ref, all_ref.at[me], send.at[k - 1], recv.at[k - 1], dev)
            cp.start()
            sends.append(cp)
        for k in range(1, N_DEV):
            dev, blk = _peer(k)
            _remote(v_ref, all_ref.at[blk], send.at[k - 1], recv.at[k - 1], dev).wait_recv()
        acc = all_ref[0]
        for d in range(1, N_DEV):
            acc = acc + all_ref[d]
        sum_ref[...] = acc
        for cp in sends:
            cp.wait_send()

    return _pc(
        body, "small_gather",
        in_specs=[VMEM_SPEC], out_specs=[VMEM_SPEC] * 2,
        out_shape=[jax.ShapeDtypeStruct((N_DEV, 1, n), F32), jax.ShapeDtypeStruct((1, n), F32)],
        scratch_shapes=[pltpu.SemaphoreType.DMA((N_DEV - 1,))] * 2,
        compiler_params=pltpu.CompilerParams(has_side_effects=True),
    )(vec)


def _local_step(x, target, mod, wt_in, w_out, w_pw, conv_w, rel_bias, sinks, conv_b, conv_ln_g, conv_ln_b, b_pw,
                ln_g, ln_b):
    bmap = _bucket_map()
    bias = _bias_table(rel_bias, bmap)
    h = _modulate(x, mod)
    qkv = _matmul_nt(h, wt_in, 0, 1536, BF16, "inproj_qkv")
    g_attn = _matmul_nt(h, wt_in, 1536, 1024, F32, "inproj_gattn")
    glu = _matmul_nt(h, wt_in, 2560, 2048, F32, "inproj_glu")
    g_conv = _matmul_nt(h, wt_in, 4608, 1024, F32, "inproj_gconv")
    ycat0, a_out = _attn_fwd(qkv, g_attn, bias, sinks)
    ycat, u1, u3, p_out = _conv_fwd(glu, g_conv, ycat0, conv_w, conv_b, conv_ln_g, conv_ln_b, w_pw, b_pw)
    dz, dy, dycat, loss, g_ln_g, g_ln_b, dgate = _outproj_ln(ycat, w_out, x, target, mod, ln_g, ln_b)
    gw_out = _matmul_tn(ycat, dy, None, 0, D_MODEL, "grad_w_out")
    dp, dgc, du1, g_bpw, g_clg, g_clb, g_cb = _conv_bwd_a(dycat, g_conv, p_out, u1, conv_ln_g, conv_ln_b, w_pw)
    gw_pw = _matmul_tn(u3, dp, None, 0, D_CONV, "grad_w_pw")
    dab, g_cw = _conv_bwd_b(du1, glu, conv_w)
    dqkv, dga, dbias, dsink = _attn_bwd(qkv, g_attn, a_out, dycat, bias, sinks)
    g_rb = _relbias_grad(dbias, bmap)
    grad_x, dshift, dscale = _dh_kernel([dqkv, dga, dab, dgc], wt_in, dz, x, mod)
    gwt_in = _matmul_tn(dqkv, h, None, 0, D_IN, "grad_w_in_qkv")
    gwt_in = _matmul_tn(dga, h, gwt_in, 1536, D_IN, "grad_w_in_gattn")
    gwt_in = _matmul_tn(dab, h, gwt_in, 2560, D_IN, "grad_w_in_glu")
    gwt_in = _matmul_tn(dgc, h, gwt_in, 4608, D_IN, "grad_w_in_gconv")
    dmod = jnp.concatenate([dshift, dscale, dgate], axis=1)
    small = dict(dmod=dmod, b_pw=g_bpw, conv_ln_g=g_clg, conv_ln_b=g_clb, conv_b=g_cb, ln_g=g_ln_g, ln_b=g_ln_b,
                 rel_bias=g_rb[:, :N_BUCKETS].reshape(1, N_BUCKETS * N_Q_HEADS),
                 sinks=dsink[0:1, :], loss=loss[0:1, :])
    return grad_x, gwt_in, gw_out, gw_pw, g_cw, small


SMALL_FIELDS = (("dmod", 3 * D_MODEL), ("b_pw", D_CONV), ("conv_ln_g", D_CONV), ("conv_ln_b", D_CONV),
                ("conv_b", D_CONV), ("ln_g", D_MODEL), ("ln_b", D_MODEL), ("rel_bias", N_BUCKETS * N_Q_HEADS),
                ("sinks", 128), ("loss", 128))


def _pack(fields):
    parts = []
    for name, width in SMALL_FIELDS:
        v = fields[name].reshape(1, -1).astype(F32)
        if v.shape[1] < width:
            v = jnp.pad(v, ((0, 0), (0, width - v.shape[1])))
        parts.append(v)
    return jnp.concatenate(parts, axis=1)


def _unpack(vec):
    out, off = {}, 0
    for name, width in SMALL_FIELDS:
        out[name] = vec[:, off:off + width]
        off += width
    return out


def kernel(x, c, w_ada, b_ada, w_in, rel_bias, sinks, conv_w, conv_b, conv_ln_g, conv_ln_b, w_pw, b_pw, w_out, ln_g, ln_b, loss_target, m_w_ada, m_b_ada, m_w_in, m_rel_bias, m_sinks, m_conv_w, m_conv_b, m_conv_ln_g, m_conv_ln_b, m_w_pw, m_b_pw, m_w_out, m_ln_g, m_ln_b, v_w_ada, v_b_ada, v_w_in, v_rel_bias, v_sinks, v_conv_w, v_conv_b, v_conv_ln_g, v_conv_ln_b, v_w_pw, v_b_pw, v_w_out, v_ln_g, v_ln_b):
    xi, yi, ci = _me()
    me = 4 * xi + 2 * yi + ci
    ncol = w_ada.shape[2]

    wt_in_loc = w_in[0].T.astype(BF16)
    conv_w_loc = jnp.pad(conv_w[0], ((0, CONV_ROWS - CONV_WIDTH), (0, 0)))
    wt_in_full, w_out_full, w_pw_full, conv_w_blocks = _all_gather_hbm(
        [wt_in_loc, w_out[0].astype(BF16), w_pw[0].astype(BF16), conv_w_loc], "gather_w")
    conv_w_full = conv_w_blocks.reshape(N_DEV, CONV_ROWS, 128).transpose(1, 0, 2).reshape(CONV_ROWS, D_CONV)

    b_ada_cols = lax.dynamic_slice(b_ada, (0, me * ncol), (1, ncol))
    mod_blocks, c_all = _ada_fwd(c, w_ada[0], b_ada_cols)
    mod = mod_blocks.reshape(1, 3 * D_MODEL)

    grad_x, gwt_in, gw_out, gw_pw, g_cw, small = _local_step(
        x[0], loss_target[0], mod, wt_in_full, w_out_full, w_pw_full, conv_w_full, rel_bias, sinks,
        conv_b, conv_ln_g, conv_ln_b, b_pw, ln_g, ln_b)

    gathered, summed = _small_gather(_pack(small))
    tot = _unpack(summed)
    dmod_all = gathered[:, 0, :3 * D_MODEL]
    loss = tot["loss"][0, 0]

    ct = jnp.zeros((D_MODEL, 128), BF16).at[:, :N_DEV].set(c_all[:, 0, :].T.astype(BF16))
    dm = jnp.zeros((128, ncol), BF16).at[:N_DEV, :].set(
        lax.dynamic_slice(dmod_all, (0, me * ncol), (N_DEV, ncol)).astype(BF16))
    g_w_ada = _small_mm(ct, dm, "grad_w_ada")

    g_cw_blocks = g_cw.reshape(CONV_ROWS, N_DEV, 128).transpose(1, 0, 2).reshape(N_DEV * CONV_ROWS, 128)
    r_in, r_out, r_pw, r_cw = _reduce_scatter_hbm([gwt_in, gw_out, gw_pw, g_cw_blocks], "reduce_scatter")
    g_w_in = _sum8(r_in.reshape(N_DEV, D_IN // N_DEV, D_MODEL), "sum_w_in").T
    g_w_out = _sum8(r_out.reshape(N_DEV, D_MODEL // N_DEV, D_MODEL), "sum_w_out")
    g_w_pw = _sum8(r_pw.reshape(N_DEV, D_CONV // N_DEV, D_CONV), "sum_w_pw")
    g_conv_w = _sum8(r_cw.reshape(N_DEV, CONV_ROWS, 128), "sum_conv_w")[:CONV_WIDTH]

    grads = {"w_ada": g_w_ada, "w_in": g_w_in, "conv_w": g_conv_w, "w_pw": g_w_pw, "w_out": g_w_out}
    params = {"w_ada": (w_ada, m_w_ada, v_w_ada), "w_in": (w_in, m_w_in, v_w_in),
              "conv_w": (conv_w, m_conv_w, v_conv_w), "w_pw": (w_pw, m_w_pw, v_w_pw),
              "w_out": (w_out, m_w_out, v_w_out)}
    res = {}
    for name, g in grads.items():
        w_, m_, v_ = params[name]
        d_, nm_, nv_ = _adamw(w_[0], g, m_[0], v_[0], "adamw_" + name)
        res[name] = (g[None], d_[None], nm_[None], nv_[None])

    small_params = {"b_ada": (b_ada, m_b_ada, v_b_ada), "b_pw": (b_pw, m_b_pw, v_b_pw),
                    "conv_ln_g": (conv_ln_g, m_conv_ln_g, v_conv_ln_g),
                    "conv_ln_b": (conv_ln_b, m_conv_ln_b, v_conv_ln_b), "conv_b": (conv_b, m_conv_b, v_conv_b),
                    "ln_g": (ln_g, m_ln_g, v_ln_g), "ln_b": (ln_b, m_ln_b, v_ln_b),
                    "rel_bias": (rel_bias, m_rel_bias, v_rel_bias), "sinks": (sinks, m_sinks, v_sinks)}
    to_small = lambda n, a: a.T if n == "rel_bias" else a
    key_of = {"b_ada": "dmod"}
    packs = []
    for j in range(3):
        fields = {key_of.get(n, n): to_small(n, t[j]) for n, t in small_params.items()}
        fields["loss"] = jnp.zeros((1, 1), F32)
        packs.append(_pack(fields))
    gsum = summed
    d_s, nm_s, nv_s = _adamw(packs[0], gsum, packs[1], packs[2], "adamw_small")
    outs_small = [_unpack(a) for a in (gsum, d_s, nm_s, nv_s)]
    for n, t in small_params.items():
        shape = t[0].shape
        vals = []
        for o in outs_small:
            a = o[key_of.get(n, n)]
            if n == "rel_bias":
                a = a.reshape(N_Q_HEADS, N_BUCKETS).T
            else:
                a = a[:, :shape[1]].reshape(shape)
            vals.append(a)
        res[n] = tuple(vals)

    order = ["w_ada", "b_ada", "w_in", "rel_bias", "sinks", "conv_w", "conv_b", "conv_ln_g", "conv_ln_b", "w_pw",
             "b_pw", "w_out", "ln_g", "ln_b"]
    out = [loss, grad_x[None]]
    for j in range(4):
        out += [res[n][j] for n in order]
    return tuple(out)
```

```python
import functools
import math

import jax
import jax.numpy as jnp
import numpy as np
from jax import lax
from jax.experimental import pallas as pl
from jax.experimental.pallas import tpu as pltpu

F32, BF16, I32 = jnp.float32, jnp.bfloat16, jnp.int32

D_MODEL = 2048
D_ATTN = 1024
D_CONV = 1024
D_KV = 256
HEAD_DIM = 64
N_Q_HEADS = 16
N_KV_HEADS = 4
GQA = 4
BLOCK = 128
CONV_WIDTH = 31
CONV_ROWS = 32
HALO = 32
N_BUCKETS = 32
MAX_DISTANCE = 128
LN_EPS = 1e-5
ALPHA = 2.0 ** 0.25
D_IN = 5632
N_DEV = 8
NEG = -1e30

ADAM_LR, ADAM_B1, ADAM_B2, ADAM_EPS, ADAM_WD, ADAM_STEP = 0.001, 0.9, 0.999, 1e-08, 0.01, 10

NT_DIMS = (((1,), (1,)), ((), ()))
TN_DIMS = (((0,), (0,)), ((), ()))
MIB = 1 << 20


def _pc(body, name, **kw):
    return pl.pallas_call(body, name=name, **kw)


def _cp(vmem_mib=None, sem=None):
    kw = {}
    if vmem_mib is not None:
        kw["vmem_limit_bytes"] = vmem_mib * MIB
    if sem is not None:
        kw["dimension_semantics"] = sem
    return pltpu.CompilerParams(**kw)


def _sig(x):
    return 1.0 / (1.0 + jnp.exp(-x))


def _dsilu(x, s):
    return s * (1.0 + x * (1.0 - s))


def _me():
    return lax.axis_index("x"), lax.axis_index("y"), lax.axis_index("c")


def _peer(k):
    x, y, c = _me()
    px = 1 - x if k & 4 else x
    py = 1 - y if k & 2 else y
    pc = 1 - c if k & 1 else c
    return (px, py, pc), 4 * px + 2 * py + pc


def _remote(src, dst, send_sem, recv_sem, dev):
    return pltpu.make_async_remote_copy(src_ref=src, dst_ref=dst, send_sem=send_sem, recv_sem=recv_sem,
                                        device_id=dev, device_id_type=pl.DeviceIdType.MESH)


HBM_SPEC = pl.BlockSpec(memory_space=pl.ANY)
VMEM_SPEC = pl.BlockSpec(memory_space=pltpu.VMEM)


def _comm_scratch(nt):
    return [pltpu.SemaphoreType.DMA((nt, N_DEV - 1)), pltpu.SemaphoreType.DMA((nt, N_DEV - 1)),
            pltpu.SemaphoreType.DMA((nt,))]


def _gather_shapes(shards):
    return [jax.ShapeDtypeStruct((N_DEV * s.shape[0], s.shape[1]), s.dtype) for s in shards]


def _block(ref, blk, rows):
    return ref.at[pl.ds(pl.multiple_of(blk * rows, 8), rows), :]


def _gather_copies(src, dst, sems):
    send_sems, recv_sems, local_sems = sems
    x, y, c = _me()
    me = 4 * x + 2 * y + c
    nt = len(src)
    local = [pltpu.make_async_copy(src[t], _block(dst[t], me, src[t].shape[0]), local_sems.at[t]) for t in range(nt)]
    sends, arrivals = [], []
    for k in range(1, N_DEV):
        dev, blk = _peer(k)
        for t in range(nt):
            r = src[t].shape[0]
            pair = (send_sems.at[t, k - 1], recv_sems.at[t, k - 1], dev)
            sends.append(_remote(src[t], _block(dst[t], me, r), *pair))
            arrivals.append(_remote(src[t], _block(dst[t], blk, r), *pair))
    return local, sends, arrivals


def _scatter_copies(src, dst, sems):
    send_sems, recv_sems, local_sems = sems
    x, y, c = _me()
    me = 4 * x + 2 * y + c
    nt = len(src)
    rows = [s.shape[0] // N_DEV for s in src]
    local = [pltpu.make_async_copy(_block(src[t], me, rows[t]), _block(dst[t], me, rows[t]), local_sems.at[t])
             for t in range(nt)]
    sends, arrivals = [], []
    for k in range(1, N_DEV):
        dev, blk = _peer(k)
        for t in range(nt):
            pair = (send_sems.at[t, k - 1], recv_sems.at[t, k - 1], dev)
            sends.append(_remote(_block(src[t], blk, rows[t]), _block(dst[t], me, rows[t]), *pair))
            arrivals.append(_remote(_block(src[t], me, rows[t]), _block(dst[t], blk, rows[t]), *pair))
    return local, sends, arrivals


def _comm_start(cps):
    local, sends, _ = cps
    for cp in local + sends:
        cp.start()


def _comm_wait(cps):
    local, sends, arrivals = cps
    for cp in arrivals:
        cp.wait_recv()
    for cp in sends:
        cp.wait_send()
    for cp in local:
        cp.wait()


def _modulate(x, mod):
    S = x.shape[0]
    tm = min(512, S)

    def body(x_ref, sh_ref, sc_ref, h_ref):
        h_ref[...] = (x_ref[...] * (1.0 + sc_ref[...]) + sh_ref[...]).astype(BF16)

    return _pc(
        body, "modulate", grid=(S // tm,),
        in_specs=[pl.BlockSpec((tm, D_MODEL), lambda i: (i, 0)),
                  pl.BlockSpec((1, D_MODEL), lambda i: (0, 0)),
                  pl.BlockSpec((1, D_MODEL), lambda i: (0, 1))],
        out_specs=pl.BlockSpec((tm, D_MODEL), lambda i: (i, 0)),
        out_shape=jax.ShapeDtypeStruct((S, D_MODEL), BF16),
        compiler_params=_cp(40, ("parallel",)),
    )(x, mod, mod)


def _matmul_nt(a, wt, row_off, n, out_dtype, name, gather=()):
    S, K = a.shape
    tm, tn = min(1024, S), 512
    assert row_off % tn == 0 and n % tn == 0
    ob = row_off // tn
    ni, nj = S // tm, n // tn
    ng = len(gather)

    def body(*refs):
        a_ref, w_ref = refs[0], refs[1]
        o_ref = refs[2 + ng]
        if ng:
            i, j = pl.program_id(0), pl.program_id(1)
            cps = _gather_copies(refs[2:2 + ng], refs[3 + ng:3 + 2 * ng], refs[3 + 2 * ng:])

            @pl.when((i == 0) & (j == 0))
            def _():
                _comm_start(cps)

        o_ref[...] = lax.dot_general(a_ref[...], w_ref[...], NT_DIMS, preferred_element_type=F32).astype(out_dtype)
        if ng:
            @pl.when((i == ni - 1) & (j == nj - 1))
            def _():
                _comm_wait(cps)

    res = _pc(
        body, name, grid=(ni, nj),
        in_specs=[pl.BlockSpec((tm, K), lambda i, j: (i, 0)),
                  pl.BlockSpec((tn, K), lambda i, j: (ob + j, 0))] + [HBM_SPEC] * ng,
        out_specs=[pl.BlockSpec((tm, tn), lambda i, j: (i, j))] + [HBM_SPEC] * ng,
        out_shape=[jax.ShapeDtypeStruct((S, n), out_dtype)] + _gather_shapes(gather),
        scratch_shapes=_comm_scratch(ng) if ng else [],
        compiler_params=_cp(48, ("arbitrary", "arbitrary") if ng else ("parallel", "arbitrary")),
    )(a, wt, *gather)
    return res[0], res[1:]


def _matmul_tn(a, b, out_prev, row_off, m_total, name):
    S, M = a.shape
    N = b.shape[1]
    tm, ts = 512, min(512, S)
    assert row_off % tm == 0 and M % tm == 0
    ob = row_off // tm
    ns = S // ts

    def body(*refs):
        a_ref, b_ref = refs[0], refs[1]
        o_ref, acc = refs[-2], refs[-1]
        s = pl.program_id(1)

        @pl.when(s == 0)
        def _():
            acc[...] = jnp.zeros_like(acc)

        acc[...] += lax.dot_general(a_ref[...], b_ref[...], TN_DIMS, preferred_element_type=F32)

        @pl.when(s == ns - 1)
        def _():
            o_ref[...] = acc[...].astype(BF16)

    in_specs = [pl.BlockSpec((ts, tm), lambda i, s: (s, i)),
                pl.BlockSpec((ts, N), lambda i, s: (s, 0))]
    args = [a, b]
    aliases = {}
    if out_prev is not None:
        in_specs.append(pl.BlockSpec(memory_space=pl.ANY))
        args.append(out_prev)
        aliases = {2: 0}
    return _pc(
        body, name, grid=(M // tm, ns),
        in_specs=in_specs,
        out_specs=pl.BlockSpec((tm, N), lambda i, s: (ob + i, 0)),
        out_shape=jax.ShapeDtypeStruct((m_total, N), BF16),
        scratch_shapes=[pltpu.VMEM((tm, N), F32)],
        input_output_aliases=aliases,
        compiler_params=_cp(48, ("parallel", "arbitrary")),
    )(*args)


def _bucket_map():
    qi = jnp.arange(BLOCK, dtype=I32)[:, None]
    kj = jnp.arange(2 * BLOCK, dtype=I32)[None, :]
    dist = qi + BLOCK - kj
    in_window = (dist >= 0) & (dist < BLOCK)
    d0 = jnp.maximum(dist, 0)
    max_exact = N_BUCKETS // 2
    d = jnp.maximum(d0, 1).astype(F32)
    large = max_exact + (jnp.log(d / max_exact) / math.log(MAX_DISTANCE / max_exact)
                         * (N_BUCKETS - max_exact)).astype(I32)
    large = jnp.minimum(large, N_BUCKETS - 1)
    bucket = jnp.where(d0 < max_exact, d0, large)
    return jnp.where(in_window, bucket, -1).astype(I32)


def _bias_table(rel_bias, bmap):
    def body(rb_ref, bm_ref, o_ref):
        h = pl.program_id(0)
        bm = bm_ref[...]
        acc = jnp.full((BLOCK, 2 * BLOCK), NEG, F32)
        for b in range(N_BUCKETS):
            acc = jnp.where(bm == b, rb_ref[b, h], acc)
        o_ref[0] = acc

    return _pc(
        body, "bias_table", grid=(N_Q_HEADS,),
        in_specs=[pl.BlockSpec(memory_space=pltpu.SMEM),
                  pl.BlockSpec((BLOCK, 2 * BLOCK), lambda h: (0, 0))],
        out_specs=pl.BlockSpec((1, BLOCK, 2 * BLOCK), lambda h: (h, 0, 0)),
        out_shape=jax.ShapeDtypeStruct((N_Q_HEADS, BLOCK, 2 * BLOCK), F32),
    )(rel_bias, bmap)


def _relbias_grad(dbias, bmap):
    def body(db_ref, bm_ref, o_ref):
        bm = bm_ref[...]
        x = db_ref[0]
        lane = lax.broadcasted_iota(I32, (1, 128), 1)
        row = jnp.zeros((1, 128), F32)
        for b in range(N_BUCKETS):
            row = jnp.where(lane == b, jnp.sum(jnp.where(bm == b, x, 0.0)), row)
        o_ref[0] = jnp.broadcast_to(row, (8, 128))

    out = _pc(
        body, "relbias_grad", grid=(N_Q_HEADS,),
        in_specs=[pl.BlockSpec((1, BLOCK, 2 * BLOCK), lambda h: (h, 0, 0)),
                  pl.BlockSpec((BLOCK, 2 * BLOCK), lambda h: (0, 0))],
        out_specs=pl.BlockSpec((1, 8, 128), lambda h: (h, 0, 0)),
        out_shape=jax.ShapeDtypeStruct((N_Q_HEADS, 8, 128), F32),
    )(dbias, bmap)
    return out[:, 0, :]


def _stack_heads(ref, hk):
    return jnp.concatenate([ref[:, pl.ds(256 * hk + 64 * g, 64)] for g in range(GQA)], axis=0)


def _sink_col(sink_ref, hk):
    row = lax.broadcasted_iota(I32, (GQA * BLOCK, 1), 0)
    s = jnp.full((GQA * BLOCK, 1), sink_ref[0, 4 * hk + 3], F32)
    for g in (2, 1, 0):
        s = jnp.where(row < (g + 1) * BLOCK, sink_ref[0, 4 * hk + g], s)
    return s


def _attn_probs(q4, kw, bias_ref, sink_ref, hk, first_mask):
    s = lax.dot_general(q4, kw, NT_DIMS, preferred_element_type=F32) * (HEAD_DIM ** -0.5)
    s = s + bias_ref[4 * hk:4 * hk + 4].reshape(GQA * BLOCK, 2 * BLOCK)
    s = jnp.where(first_mask, s, NEG)
    sink = _sink_col(sink_ref, hk)
    m = jnp.maximum(jnp.max(s, axis=1, keepdims=True), sink)
    e = jnp.exp(s - m)
    es = jnp.exp(sink - m)
    inv = 1.0 / (jnp.sum(e, axis=1, keepdims=True) + es)
    return e * inv, es * inv


def _attn_fwd(qkv, g_attn, bias, sinks):
    S = qkv.shape[0]
    nb = S // BLOCK

    def body(q_ref, kc_ref, kp_ref, vc_ref, vp_ref, g_ref, bias_ref, sink_ref, y_ref, a_ref):
        n = pl.program_id(0)
        kj = lax.broadcasted_iota(I32, (GQA * BLOCK, 2 * BLOCK), 1)
        first_mask = (n > 0) | (kj >= BLOCK)
        for hk in range(N_KV_HEADS):
            q4 = _stack_heads(q_ref, hk)
            ks = pl.ds(64 * hk, 64)
            kw = jnp.concatenate([kp_ref[:, ks], kc_ref[:, ks]], axis=0)
            vw = jnp.concatenate([vp_ref[:, ks], vc_ref[:, ks]], axis=0)
            p, _ = _attn_probs(q4, kw, bias_ref, sink_ref, hk, first_mask)
            o4 = jnp.dot(p.astype(BF16), vw, preferred_element_type=F32)
            for g in range(GQA):
                sl = pl.ds(256 * hk + 64 * g, 64)
                og = o4[BLOCK * g:BLOCK * (g + 1)]
                gg = g_ref[:, sl]
                a_ref[:, sl] = og
                y_ref[:, sl] = (og * (gg * _sig(gg))).astype(BF16)

    prev = lambda n: jnp.maximum(n - 1, 0)
    return _pc(
        body, "attn_fwd", grid=(nb,),
        in_specs=[pl.BlockSpec((BLOCK, D_ATTN), lambda n: (n, 0)),
                  pl.BlockSpec((BLOCK, D_KV), lambda n: (n, 4)),
                  pl.BlockSpec((BLOCK, D_KV), lambda n: (prev(n), 4)),
                  pl.BlockSpec((BLOCK, D_KV), lambda n: (n, 5)),
                  pl.BlockSpec((BLOCK, D_KV), lambda n: (prev(n), 5)),
                  pl.BlockSpec((BLOCK, D_ATTN), lambda n: (n, 0)),
                  pl.BlockSpec((N_Q_HEADS, BLOCK, 2 * BLOCK), lambda n: (0, 0, 0)),
                  pl.BlockSpec(memory_space=pltpu.SMEM)],
        out_specs=[pl.BlockSpec((BLOCK, D_ATTN), lambda n: (n, 0)),
                   pl.BlockSpec((BLOCK, D_ATTN), lambda n: (n, 0))],
        out_shape=[jax.ShapeDtypeStruct((S, 2 * D_ATTN), BF16),
                   jax.ShapeDtypeStruct((S, D_ATTN), F32)],
        compiler_params=_cp(40, ("parallel",)),
    )(qkv, qkv, qkv, qkv, qkv, g_attn, bias, sinks)


def _attn_bwd(qkv, g_attn, a_out, dycat, bias, sinks, scatter=()):
    S = qkv.shape[0]
    nb = S // BLOCK
    R = GQA * BLOCK
    ns = len(scatter)

    def body(*refs):
        (q_ref, kc_ref, kp_ref, vc_ref, vp_ref, g_ref, a_ref, dy_ref, bias_ref, sink_ref) = refs[:10]
        dqkv_ref, dg_ref, dbias_ref, dsink_ref = refs[10 + ns:14 + ns]
        dq_scr, dq_new, dk_scr, dv_scr, dkw_scr, dvw_scr, ds_scr = refs[14 + 2 * ns:21 + 2 * ns]
        n = pl.program_id(0)
        if ns:
            cps = _scatter_copies(refs[10:10 + ns], refs[14 + ns:14 + 2 * ns], refs[21 + 2 * ns:])

            @pl.when(n == 0)
            def _():
                _comm_start(cps)

        @pl.when(n == 0)
        def _():
            dbias_ref[...] = jnp.zeros_like(dbias_ref)
            ds_scr[...] = jnp.zeros_like(ds_scr)
            dq_scr[...] = jnp.zeros_like(dq_scr)
            dk_scr[...] = jnp.zeros_like(dk_scr)
            dv_scr[...] = jnp.zeros_like(dv_scr)

        @pl.when(n < nb)
        def _():
            kj = lax.broadcasted_iota(I32, (R, 2 * BLOCK), 1)
            first_mask = (n > 0) | (kj >= BLOCK)
            for hk in range(N_KV_HEADS):
                q4 = _stack_heads(q_ref, hk)
                ks = pl.ds(64 * hk, 64)
                kw = jnp.concatenate([kp_ref[:, ks], kc_ref[:, ks]], axis=0)
                vw = jnp.concatenate([vp_ref[:, ks], vc_ref[:, ks]], axis=0)
                p, psink = _attn_probs(q4, kw, bias_ref, sink_ref, hk, first_mask)
                da_parts, a_parts = [], []
                for g in range(GQA):
                    sl = pl.ds(256 * hk + 64 * g, 64)
                    gg = g_ref[:, sl]
                    sg = _sig(gg)
                    dyg = dy_ref[:, sl]
                    ag = a_ref[:, sl]
                    da_parts.append(dyg * (gg * sg))
                    a_parts.append(ag)
                    dg_ref[:, sl] = (dyg * ag * _dsilu(gg, sg)).astype(BF16)
                da4 = jnp.concatenate(da_parts, axis=0)
                a4 = jnp.concatenate(a_parts, axis=0)
                delta = jnp.sum(da4 * a4, axis=1, keepdims=True)
                da4b = da4.astype(BF16)
                dp = lax.dot_general(da4b, vw, NT_DIMS, preferred_element_type=F32)
                ds = p * (dp - delta)
                ds_scr[hk] += -psink * delta
                dbias_ref[4 * hk:4 * hk + 4] += ds.reshape(GQA, BLOCK, 2 * BLOCK)
                dsb = ds.astype(BF16)
                dq4 = jnp.dot(dsb, kw, preferred_element_type=F32) * (HEAD_DIM ** -0.5)
                for g in range(GQA):
                    dq_new[:, pl.ds(256 * hk + 64 * g, 64)] = dq4[BLOCK * g:BLOCK * (g + 1)].astype(BF16)
                dkw_scr[:, ks] = lax.dot_general(dsb, q4, TN_DIMS, preferred_element_type=F32) * (HEAD_DIM ** -0.5)
                dvw_scr[:, ks] = lax.dot_general(p.astype(BF16), da4b, TN_DIMS, preferred_element_type=F32)

        @pl.when(n == nb)
        def _():
            dkw_scr[0:BLOCK, :] = jnp.zeros((BLOCK, D_KV), F32)
            dvw_scr[0:BLOCK, :] = jnp.zeros((BLOCK, D_KV), F32)

        dqkv_ref[:, 0:D_ATTN] = dq_scr[...]
        dqkv_ref[:, D_ATTN:D_ATTN + D_KV] = (dk_scr[...] + dkw_scr[0:BLOCK, :]).astype(BF16)
        dqkv_ref[:, D_ATTN + D_KV:D_ATTN + 2 * D_KV] = (dv_scr[...] + dvw_scr[0:BLOCK, :]).astype(BF16)
        dq_scr[...] = dq_new[...]
        dk_scr[...] = dkw_scr[BLOCK:2 * BLOCK, :]
        dv_scr[...] = dvw_scr[BLOCK:2 * BLOCK, :]

        @pl.when(n == nb)
        def _():
            lane = lax.broadcasted_iota(I32, (1, 128), 1)
            row = jnp.zeros((1, 128), F32)
            for hk in range(N_KV_HEADS):
                col = ds_scr[hk]
                for g in range(GQA):
                    row = jnp.where(lane == 4 * hk + g, jnp.sum(col[BLOCK * g:BLOCK * (g + 1)]), row)
            dsink_ref[...] = jnp.broadcast_to(row, (8, 128))
            if ns:
                _comm_wait(cps)

    cur = lambda n: jnp.minimum(n, nb - 1)
    prev = lambda n: jnp.clip(n - 1, 0, nb - 1)
    res = _pc(
        body, "attn_bwd", grid=(nb + 1,),
        in_specs=[pl.BlockSpec((BLOCK, D_ATTN), lambda n: (cur(n), 0)),
                  pl.BlockSpec((BLOCK, D_KV), lambda n: (cur(n), 4)),
                  pl.BlockSpec((BLOCK, D_KV), lambda n: (prev(n), 4)),
                  pl.BlockSpec((BLOCK, D_KV), lambda n: (cur(n), 5)),
                  pl.BlockSpec((BLOCK, D_KV), lambda n: (prev(n), 5)),
                  pl.BlockSpec((BLOCK, D_ATTN), lambda n: (cur(n), 0)),
                  pl.BlockSpec((BLOCK, D_ATTN), lambda n: (cur(n), 0)),
                  pl.BlockSpec((BLOCK, D_ATTN), lambda n: (cur(n), 0)),
                  pl.BlockSpec((N_Q_HEADS, BLOCK, 2 * BLOCK), lambda n: (0, 0, 0)),
                  pl.BlockSpec(memory_space=pltpu.SMEM)] + [HBM_SPEC] * ns,
        out_specs=[pl.BlockSpec((BLOCK, D_ATTN + 2 * D_KV), lambda n: (prev(n), 0)),
                   pl.BlockSpec((BLOCK, D_ATTN), lambda n: (cur(n), 0)),
                   pl.BlockSpec((N_Q_HEADS, BLOCK, 2 * BLOCK), lambda n: (0, 0, 0)),
                   pl.BlockSpec((8, 128), lambda n: (0, 0))] + [HBM_SPEC] * ns,
        out_shape=[jax.ShapeDtypeStruct((S, D_ATTN + 2 * D_KV), BF16),
                   jax.ShapeDtypeStruct((S, D_ATTN), BF16),
                   jax.ShapeDtypeStruct((N_Q_HEADS, BLOCK, 2 * BLOCK), F32),
                   jax.ShapeDtypeStruct((8, 128), F32)] + [jax.ShapeDtypeStruct(f.shape, f.dtype) for f in scatter],
        scratch_shapes=[pltpu.VMEM((BLOCK, D_ATTN), BF16), pltpu.VMEM((BLOCK, D_ATTN), BF16),
                        pltpu.VMEM((BLOCK, D_KV), F32), pltpu.VMEM((BLOCK, D_KV), F32),
                        pltpu.VMEM((2 * BLOCK, D_KV), F32), pltpu.VMEM((2 * BLOCK, D_KV), F32),
                        pltpu.VMEM((N_KV_HEADS, R, 1), F32)] + (_comm_scratch(ns) if ns else []),
        compiler_params=_cp(48, ("arbitrary",)),
    )(qkv, qkv, qkv, qkv, qkv, g_attn, a_out, dycat, bias, sinks, *scatter)
    return res[:4], res[4:]


def _conv_tile(S):
    return min(256, S)


def _shifted(win, s):
    return win if s == 0 else pltpu.roll(win, win.shape[0] - s, axis=0)


def _conv_fwd(glu, g_conv, ycat_in, conv_w, conv_b, ln_g, ln_b, w_pw, b_pw):
    S = glu.shape[0]
    T = _conv_tile(S)
    hb = T // HALO

    def body(a_ref, ah_ref, b_ref, bh_ref, gc_ref, cw_ref, cb_ref, lg_ref, lb_ref, wpw_ref, bpw_ref, ycat_in_ref,
             y_ref, u1_ref, u3_ref, p_ref):
        i = pl.program_id(0)
        u0c = a_ref[...] * _sig(b_ref[...])
        u0h = jnp.where(i > 0, ah_ref[...] * _sig(bh_ref[...]), 0.0)
        win = jnp.concatenate([u0h, u0c], axis=0)
        acc = jnp.zeros((T, D_CONV), F32)
        for s in range(8):
            ws = _shifted(win, s)
            for aa in range(5):
                j = 8 * aa + s - 2
                if 0 <= j < CONV_WIDTH:
                    acc = acc + ws[8 * aa:8 * aa + T] * cw_ref[j:j + 1, :]
        u1 = acc + cb_ref[...]
        u1_ref[...] = u1
        mu = jnp.mean(u1, axis=1, keepdims=True)
        uc = u1 - mu
        rstd = lax.rsqrt(jnp.mean(uc * uc, axis=1, keepdims=True) + LN_EPS)
        u2 = uc * rstd * lg_ref[...] + lb_ref[...]
        u3 = (u2 * _sig(u2)).astype(BF16)
        u3_ref[...] = u3
        p = jnp.dot(u3, wpw_ref[...], preferred_element_type=F32) + bpw_ref[...]
        p_ref[...] = p
        gc = gc_ref[...]
        y_ref[...] = (p * (gc * _sig(gc))).astype(BF16)

    halo = lambda i: jnp.maximum(i * hb - 1, 0)
    vec = pl.BlockSpec((1, D_CONV), lambda i: (0, 0))
    return _pc(
        body, "conv_fwd", grid=(S // T,),
        in_specs=[pl.BlockSpec((T, D_CONV), lambda i: (i, 0)),
                  pl.BlockSpec((HALO, D_CONV), lambda i: (halo(i), 0)),
                  pl.BlockSpec((T, D_CONV), lambda i: (i, 1)),
                  pl.BlockSpec((HALO, D_CONV), lambda i: (halo(i), 1)),
                  pl.BlockSpec((T, D_CONV), lambda i: (i, 0)),
                  pl.BlockSpec((CONV_ROWS, D_CONV), lambda i: (0, 0)),
                  vec, vec, vec,
                  pl.BlockSpec((D_CONV, D_CONV), lambda i: (0, 0)),
                  vec,
                  pl.BlockSpec(memory_space=pl.ANY)],
        out_specs=[pl.BlockSpec((T, D_CONV), lambda i: (i, 1)),
                   pl.BlockSpec((T, D_CONV), lambda i: (i, 0)),
                   pl.BlockSpec((T, D_CONV), lambda i: (i, 0)),
                   pl.BlockSpec((T, D_CONV), lambda i: (i, 0))],
        out_shape=[jax.ShapeDtypeStruct((S, 2 * D_CONV), BF16),
                   jax.ShapeDtypeStruct((S, D_CONV), F32),
                   jax.ShapeDtypeStruct((S, D_CONV), BF16),
                   jax.ShapeDtypeStruct((S, D_CONV), F32)],
        input_output_aliases={11: 0},
        compiler_params=_cp(48, ("parallel",)),
    )(glu, glu, glu, glu, g_conv, conv_w, conv_b, ln_g, ln_b, w_pw, b_pw, ycat_in)


def _conv_bwd_a(dycat, g_conv, p_out, u1, ln_g, ln_b, w_pw):
    S = u1.shape[0]
    T = _conv_tile(S)

    def body(dy_ref, gc_ref, p_ref, u1_ref, lg_ref, lb_ref, wpw_ref,
             dp_ref, dgc_ref, du1_ref, gbpw_ref, glg_ref, glb_ref, gcb_ref):
        i = pl.program_id(0)

        @pl.when(i == 0)
        def _():
            for r in (gbpw_ref, glg_ref, glb_ref, gcb_ref):
                r[...] = jnp.zeros_like(r)

        dy = dy_ref[...]
        gc = gc_ref[...]
        sg = _sig(gc)
        dp = dy * (gc * sg)
        dgc_ref[...] = (dy * p_ref[...] * _dsilu(gc, sg)).astype(BF16)
        gbpw_ref[...] += jnp.sum(dp, axis=0, keepdims=True)
        dpb = dp.astype(BF16)
        dp_ref[...] = dpb
        du3 = lax.dot_general(dpb, wpw_ref[...], NT_DIMS, preferred_element_type=F32)
        u1 = u1_ref[...]
        mu = jnp.mean(u1, axis=1, keepdims=True)
        uc = u1 - mu
        rstd = lax.rsqrt(jnp.mean(uc * uc, axis=1, keepdims=True) + LN_EPS)
        uh = uc * rstd
        lg = lg_ref[...]
        u2 = uh * lg + lb_ref[...]
        s2 = _sig(u2)
        du2 = du3 * _dsilu(u2, s2)
        glg_ref[...] += jnp.sum(du2 * uh, axis=0, keepdims=True)
        glb_ref[...] += jnp.sum(du2, axis=0, keepdims=True)
        duh = du2 * lg
        du1 = rstd * (duh - jnp.mean(duh, axis=1, keepdims=True) - uh * jnp.mean(duh * uh, axis=1, keepdims=True))
        du1_ref[...] = du1
        gcb_ref[...] += jnp.sum(du1, axis=0, keepdims=True)

    vec = pl.BlockSpec((1, D_CONV), lambda i: (0, 0))
    tile = pl.BlockSpec((T, D_CONV), lambda i: (i, 0))
    vshape = jax.ShapeDtypeStruct((1, D_CONV), F32)
    return _pc(
        body, "conv_bwd_a", grid=(S // T,),
        in_specs=[pl.BlockSpec((T, D_CONV), lambda i: (i, 1)), tile, tile, tile, vec, vec,
                  pl.BlockSpec((D_CONV, D_CONV), lambda i: (0, 0))],
        out_specs=[tile, tile, tile, vec, vec, vec, vec],
        out_shape=[jax.ShapeDtypeStruct((S, D_CONV), BF16), jax.ShapeDtypeStruct((S, D_CONV), BF16),
                   jax.ShapeDtypeStruct((S, D_CONV), F32), vshape, vshape, vshape, vshape],
        compiler_params=_cp(48, ("arbitrary",)),
    )(dycat, g_conv, p_out, u1, ln_g, ln_b, w_pw)


def _conv_bwd_b(du1, glu, conv_w):
    S = du1.shape[0]
    T = _conv_tile(S)
    hb = T // HALO
    nt = S // T
    last_h = S // HALO - 1

    def body(du_ref, dun_ref, a_ref, ah_ref, b_ref, bh_ref, cw_ref, dab_ref, gw_ref):
        i = pl.program_id(0)

        @pl.when(i == 0)
        def _():
            gw_ref[...] = jnp.zeros_like(gw_ref)

        a = a_ref[...]
        sb = _sig(b_ref[...])
        u0h = jnp.where(i > 0, ah_ref[...] * _sig(bh_ref[...]), 0.0)
        win = jnp.concatenate([u0h, a * sb], axis=0)
        du1c = du_ref[...]
        du1n = jnp.where(i < nt - 1, dun_ref[...], 0.0)
        win2 = jnp.concatenate([du1c, du1n], axis=0)
        acc = jnp.zeros((T, D_CONV), F32)
        for s in range(8):
            w2 = _shifted(win2, s)
            for aa in range(4):
                j = CONV_WIDTH - 1 - (8 * aa + s)
                if 0 <= j < CONV_WIDTH:
                    acc = acc + w2[8 * aa:8 * aa + T] * cw_ref[j:j + 1, :]
        dab_ref[:, 0:D_CONV] = (acc * sb).astype(BF16)
        dab_ref[:, D_CONV:2 * D_CONV] = (acc * a * sb * (1.0 - sb)).astype(BF16)
        for s in range(8):
            ws = _shifted(win, s)
            for aa in range(5):
                j = 8 * aa + s - 2
                if 0 <= j < CONV_WIDTH:
                    gw_ref[j:j + 1, :] += jnp.sum(du1c * ws[8 * aa:8 * aa + T], axis=0, keepdims=True)

    halo = lambda i: jnp.maximum(i * hb - 1, 0)
    nxt = lambda i: jnp.minimum((i + 1) * hb, last_h)
    return _pc(
        body, "conv_bwd_b", grid=(nt,),
        in_specs=[pl.BlockSpec((T, D_CONV), lambda i: (i, 0)),
                  pl.BlockSpec((HALO, D_CONV), lambda i: (nxt(i), 0)),
                  pl.BlockSpec((T, D_CONV), lambda i: (i, 0)),
                  pl.BlockSpec((HALO, D_CONV), lambda i: (halo(i), 0)),
                  pl.BlockSpec((T, D_CONV), lambda i: (i, 1)),
                  pl.BlockSpec((HALO, D_CONV), lambda i: (halo(i), 1)),
                  pl.BlockSpec((CONV_ROWS, D_CONV), lambda i: (0, 0))],
        out_specs=[pl.BlockSpec((T, 2 * D_CONV), lambda i: (i, 0)),
                   pl.BlockSpec((CONV_ROWS, D_CONV), lambda i: (0, 0))],
        out_shape=[jax.ShapeDtypeStruct((S, 2 * D_CONV), BF16),
                   jax.ShapeDtypeStruct((CONV_ROWS, D_CONV), F32)],
        compiler_params=_cp(48, ("arbitrary",)),
    )(du1, du1, glu, glu, glu, glu, conv_w)


def _outproj_ln(ycat, w_out, x, target, mod, ln_g, ln_b):
    S = x.shape[0]
    tm = min(256, S)

    def body(yc_ref, w_ref, x_ref, t_ref, gate_ref, lg_ref, lb_ref,
             dz_ref, dy_ref, dyc_ref, loss_ref, glg_ref, glb_ref, dgate_ref):
        i = pl.program_id(0)

        @pl.when(i == 0)
        def _():
            for r in (loss_ref, glg_ref, glb_ref, dgate_ref):
                r[...] = jnp.zeros_like(r)

        w = w_ref[...]
        y = jnp.dot(yc_ref[...], w, preferred_element_type=F32)
        gate = gate_ref[...]
        z = ALPHA * x_ref[...] + gate * y
        mu = jnp.mean(z, axis=1, keepdims=True)
        zc = z - mu
        rstd = lax.rsqrt(jnp.mean(zc * zc, axis=1, keepdims=True) + LN_EPS)
        zh = zc * rstd
        lg = lg_ref[...]
        err = zh * lg + lb_ref[...] - t_ref[...]
        loss_ref[...] += 0.5 * jnp.sum(jnp.sum(err * err, axis=1, keepdims=True)) / D_MODEL
        dout = err * (1.0 / D_MODEL)
        glg_ref[...] += jnp.sum(dout * zh, axis=0, keepdims=True)
        glb_ref[...] += jnp.sum(dout, axis=0, keepdims=True)
        dzh = dout * lg
        dz = rstd * (dzh - jnp.mean(dzh, axis=1, keepdims=True) - zh * jnp.mean(dzh * zh, axis=1, keepdims=True))
        dz_ref[...] = dz
        dgate_ref[...] += jnp.sum(dz * y, axis=0, keepdims=True)
        dy = (dz * gate).astype(BF16)
        dy_ref[...] = dy
        dyc_ref[...] = lax.dot_general(dy, w, NT_DIMS, preferred_element_type=F32)

    vec = pl.BlockSpec((1, D_MODEL), lambda i: (0, 0))
    tile = pl.BlockSpec((tm, D_MODEL), lambda i: (i, 0))
    vshape = jax.ShapeDtypeStruct((1, D_MODEL), F32)
    return _pc(
        body, "outproj_ln", grid=(S // tm,),
        in_specs=[tile, pl.BlockSpec((D_MODEL, D_MODEL), lambda i: (0, 0)), tile, tile,
                  pl.BlockSpec((1, D_MODEL), lambda i: (0, 2)), vec, vec],
        out_specs=[tile, tile, tile, pl.BlockSpec((8, 128), lambda i: (0, 0)), vec, vec, vec],
        out_shape=[jax.ShapeDtypeStruct((S, D_MODEL), F32), jax.ShapeDtypeStruct((S, D_MODEL), BF16),
                   jax.ShapeDtypeStruct((S, D_MODEL), F32), jax.ShapeDtypeStruct((8, 128), F32),
                   vshape, vshape, vshape],
        compiler_params=_cp(56, ("arbitrary",)),
    )(ycat, w_out, x, target, mod, ln_g, ln_b)


def _dh_kernel(segs, wt, dz, x, mod, scatter=()):
    S = x.shape[0]
    tm, tk = min(512, S), 512
    bounds = []
    k0 = 0
    for a in segs:
        assert a.shape[1] % tk == 0
        bounds.append((k0, a.shape[1] // tk))
        k0 += a.shape[1] // tk
    nk = k0
    assert nk * tk == wt.shape[0]
    nseg = len(segs)
    ns = len(scatter)
    ni = S // tm

    def body(*refs):
        seg_refs = refs[:nseg]
        w_ref, dz_ref, x_ref, sc_ref = refs[nseg:nseg + 4]
        outs = refs[nseg + 4 + ns:]
        gx_ref, dsh_ref, dsc_ref = outs[:3]
        acc = outs[3 + ns]
        i, k = pl.program_id(0), pl.program_id(1)
        if ns:
            cps = _scatter_copies(refs[nseg + 4:nseg + 4 + ns], outs[3:3 + ns], outs[4 + ns:])

        @pl.when((i == 0) & (k == 0))
        def _():
            dsh_ref[...] = jnp.zeros_like(dsh_ref)
            dsc_ref[...] = jnp.zeros_like(dsc_ref)
            if ns:
                _comm_start(cps)

        @pl.when(k == 0)
        def _():
            acc[...] = jnp.zeros_like(acc)

        for (s0, sn), r in zip(bounds, seg_refs):
            @pl.when((k >= s0) & (k < s0 + sn))
            def _(r=r):
                acc[...] += jnp.dot(r[...], w_ref[...], preferred_element_type=F32)

        @pl.when(k == nk - 1)
        def _():
            dh = acc[...]
            gx_ref[...] = ALPHA * dz_ref[...] + dh * (1.0 + sc_ref[...])
            dsh_ref[...] += jnp.sum(dh, axis=0, keepdims=True)
            dsc_ref[...] += jnp.sum(dh * x_ref[...], axis=0, keepdims=True)

        if ns:
            @pl.when((i == ni - 1) & (k == nk - 1))
            def _():
                _comm_wait(cps)

    def seg_spec(s0, sn):
        return pl.BlockSpec((tm, tk), lambda i, k: (i, jnp.clip(k - s0, 0, sn - 1)))

    tile = pl.BlockSpec((tm, D_MODEL), lambda i, k: (i, 0))
    vec = pl.BlockSpec((1, D_MODEL), lambda i, k: (0, 0))
    vshape = jax.ShapeDtypeStruct((1, D_MODEL), F32)
    res = _pc(
        body, "dh_gradx", grid=(ni, nk),
        in_specs=[seg_spec(*b) for b in bounds] + [
            pl.BlockSpec((tk, D_MODEL), lambda i, k: (k, 0)), tile, tile,
            pl.BlockSpec((1, D_MODEL), lambda i, k: (0, 1))] + [HBM_SPEC] * ns,
        out_specs=[tile, vec, vec] + [HBM_SPEC] * ns,
        out_shape=[jax.ShapeDtypeStruct((S, D_MODEL), F32), vshape, vshape]
        + [jax.ShapeDtypeStruct(f.shape, f.dtype) for f in scatter],
        scratch_shapes=[pltpu.VMEM((tm, D_MODEL), F32)] + (_comm_scratch(ns) if ns else []),
        compiler_params=_cp(56, ("arbitrary", "arbitrary")),
    )(*segs, wt, dz, x, mod, *scatter)
    return res[:3], res[3:]


def _row_tile(rows, cols):
    if rows * cols * 4 <= 2 * MIB or rows % 8:
        return rows
    tr = max(8, (2 * MIB // (cols * 4)) // 8 * 8)
    while rows % tr:
        tr -= 8
    return tr


def _sum8(recv, name):
    _, R, C = recv.shape
    tr = _row_tile(R, C)

    def body(r_ref, o_ref):
        acc = r_ref[0].astype(F32)
        for d in range(1, N_DEV):
            acc = acc + r_ref[d].astype(F32)
        o_ref[...] = acc

    return _pc(
        body, name, grid=(R // tr,),
        in_specs=[pl.BlockSpec((N_DEV, tr, C), lambda i: (0, i, 0))],
        out_specs=pl.BlockSpec((tr, C), lambda i: (i, 0)),
        out_shape=jax.ShapeDtypeStruct((R, C), F32),
        compiler_params=_cp(40, ("parallel",)),
    )(recv)


def _adamw(w, g, m, v, name):
    R, C = w.shape
    tr = _row_tile(R, C)

    def body(w_ref, g_ref, m_ref, v_ref, d_ref, nm_ref, nv_ref):
        g_ = g_ref[...]
        m_ = ADAM_B1 * m_ref[...] + (1.0 - ADAM_B1) * g_
        v_ = ADAM_B2 * v_ref[...] + (1.0 - ADAM_B2) * (g_ * g_)
        m_hat = m_ / (1.0 - ADAM_B1 ** ADAM_STEP)
        v_hat = v_ / (1.0 - ADAM_B2 ** ADAM_STEP)
        d_ref[...] = -ADAM_LR * (m_hat / (jnp.sqrt(v_hat) + ADAM_EPS) + ADAM_WD * w_ref[...])
        nm_ref[...] = m_
        nv_ref[...] = v_

    spec = pl.BlockSpec((tr, C), lambda i: (i, 0))
    shape = jax.ShapeDtypeStruct((R, C), F32)
    return _pc(
        body, name, grid=(R // tr,),
        in_specs=[spec] * 4, out_specs=[spec] * 3, out_shape=[shape] * 3,
        compiler_params=_cp(40, ("parallel",)),
    )(w, g, m, v)


def _small_mm(a, b, name):
    def body(a_ref, b_ref, o_ref):
        o_ref[...] = jnp.dot(a_ref[...], b_ref[...], preferred_element_type=F32)

    return _pc(body, name, out_shape=jax.ShapeDtypeStruct((a.shape[0], b.shape[1]), F32),
               compiler_params=_cp(40))(a, b)


def _all_gather_hbm(shards, name):
    nt = len(shards)

    def body(*refs):
        cps = _gather_copies(refs[:nt], refs[nt:2 * nt], refs[2 * nt:])
        _comm_start(cps)
        _comm_wait(cps)

    return _pc(
        body, name,
        in_specs=[HBM_SPEC] * nt, out_specs=[HBM_SPEC] * nt,
        out_shape=_gather_shapes(shards), scratch_shapes=_comm_scratch(nt),
        compiler_params=pltpu.CompilerParams(has_side_effects=True),
    )(*shards)


def _ada_fwd(c, w_ada, b_ada_cols):
    ncol = w_ada.shape[1]

    def body(c_ref, w_ref, b_ref, mod_ref, call_ref, cact, cmat, mloc, send1, recv1, send2, recv2):
        x, y, z = _me()
        me = 4 * x + 2 * y + z
        cv = c_ref[...]
        cact[...] = cv * _sig(cv)
        call_ref[me] = cact[...]
        sends = []
        for k in range(1, N_DEV):
            dev, _ = _peer(k)
            cp = _remote(cact, call_ref.at[me], send1.at[k - 1], recv1.at[k - 1], dev)
            cp.start()
            sends.append(cp)
        for k in range(1, N_DEV):
            dev, blk = _peer(k)
            _remote(cact, call_ref.at[blk], send1.at[k - 1], recv1.at[k - 1], dev).wait_recv()
        cmat[...] = jnp.zeros_like(cmat)
        for b in range(N_DEV):
            cmat[b:b + 1, :] = call_ref[b]
        m = jnp.dot(cmat[...].astype(BF16), w_ref[...].astype(BF16), preferred_element_type=F32) + b_ref[...]
        for b in range(N_DEV):
            mloc[b] = m[b:b + 1, :]
        mod_ref[me] = mloc[me]
        for k in range(1, N_DEV):
            dev, blk = _peer(k)
            cp = _remote(mloc.at[blk], mod_ref.at[me], send2.at[k - 1], recv2.at[k - 1], dev)
            cp.start()
            sends.append(cp)
        for k in range(1, N_DEV):
            dev, blk = _peer(k)
            _remote(mloc.at[me], mod_ref.at[blk], send2.at[k - 1], recv2.at[k - 1], dev).wait_recv()
        for cp in sends:
            cp.wait_send()

    return _pc(
        body, "ada_fwd",
        in_specs=[VMEM_SPEC] * 3, out_specs=[VMEM_SPEC] * 2,
        out_shape=[jax.ShapeDtypeStruct((N_DEV, 1, ncol), F32), jax.ShapeDtypeStruct((N_DEV, 1, D_MODEL), F32)],
        scratch_shapes=[pltpu.VMEM((1, D_MODEL), F32), pltpu.VMEM((16, D_MODEL), F32),
                        pltpu.VMEM((N_DEV, 1, ncol), F32)] + [pltpu.SemaphoreType.DMA((N_DEV - 1,))] * 4,
        compiler_params=pltpu.CompilerParams(has_side_effects=True, vmem_limit_bytes=40 * MIB),
    )(c, w_ada, b_ada_cols)


def _small_gather(vec):
    n = vec.shape[1]

    def body(v_ref, all_ref, sum_ref, send, recv):
        x, y, z = _me()
        me = 4 * x + 2 * y + z
        all_ref[me] = v_ref[...]
        sends = []
        for k in range(1, N_DEV):
            dev, _ = _peer(k)
            cp = _remote(v_ref, all_ref.at[me], send.at[k - 1], recv.at[k - 1], dev)
            cp.start()
            sends.append(cp)
        for k in range(1, N_DEV):
            dev, blk = _peer(k)
            _remote(v_ref, all_ref.at[blk], send.at[k - 1], recv.at[k - 1], dev).wait_recv()
        acc = all_ref[0]
        for d in range(1, N_DEV):
            acc = acc + all_ref[d]
        sum_ref[...] = acc
        for cp in sends:
            cp.wait_send()

    return _pc(
        body, "small_gather",
        in_specs=[VMEM_SPEC], out_specs=[VMEM_SPEC] * 2,
        out_shape=[jax.ShapeDtypeStruct((N_DEV, 1, n), F32), jax.ShapeDtypeStruct((1, n), F32)],
        scratch_shapes=[pltpu.SemaphoreType.DMA((N_DEV - 1,))] * 2,
        compiler_params=pltpu.CompilerParams(has_side_effects=True),
    )(vec)


def _local_step(x, target, mod, wt_in, w_out_loc, w_pw_loc, conv_w_loc, rel_bias, sinks, conv_b, conv_ln_g,
                conv_ln_b, b_pw, ln_g, ln_b):
    bmap = _bucket_map()
    bias = _bias_table(rel_bias, bmap)
    h = _modulate(x, mod)
    qkv, _ = _matmul_nt(h, wt_in, 0, 1536, BF16, "inproj_qkv")
    g_attn, (w_pw, conv_w_blocks) = _matmul_nt(h, wt_in, 1536, 1024, F32, "inproj_gattn",
                                                gather=(w_pw_loc, conv_w_loc))
    glu, (w_out,) = _matmul_nt(h, wt_in, 2560, 2048, F32, "inproj_glu", gather=(w_out_loc,))
    g_conv, _ = _matmul_nt(h, wt_in, 4608, 1024, F32, "inproj_gconv")
    conv_w = conv_w_blocks.reshape(N_DEV, CONV_ROWS, 128).transpose(1, 0, 2).reshape(CONV_ROWS, D_CONV)
    ycat0, a_out = _attn_fwd(qkv, g_attn, bias, sinks)
    ycat, u1, u3, p_out = _conv_fwd(glu, g_conv, ycat0, conv_w, conv_b, conv_ln_g, conv_ln_b, w_pw, b_pw)
    dz, dy, dycat, loss, g_ln_g, g_ln_b, dgate = _outproj_ln(ycat, w_out, x, target, mod, ln_g, ln_b)
    gw_out = _matmul_tn(ycat, dy, None, 0, D_MODEL, "grad_w_out")
    dp, dgc, du1, g_bpw, g_clg, g_clb, g_cb = _conv_bwd_a(dycat, g_conv, p_out, u1, conv_ln_g, conv_ln_b, w_pw)
    gw_pw = _matmul_tn(u3, dp, None, 0, D_CONV, "grad_w_pw")
    dab, g_cw = _conv_bwd_b(du1, glu, conv_w)
    (dqkv, dga, dbias, dsink), (r_out, r_pw) = _attn_bwd(qkv, g_attn, a_out, dycat, bias, sinks,
                                                         scatter=(gw_out, gw_pw))
    g_rb = _relbias_grad(dbias, bmap)
    gwt_in = _matmul_tn(dqkv, h, None, 0, D_IN, "grad_w_in_qkv")
    gwt_in = _matmul_tn(dga, h, gwt_in, 1536, D_IN, "grad_w_in_gattn")
    gwt_in = _matmul_tn(dab, h, gwt_in, 2560, D_IN, "grad_w_in_glu")
    gwt_in = _matmul_tn(dgc, h, gwt_in, 4608, D_IN, "grad_w_in_gconv")
    g_cw_blocks = g_cw.reshape(CONV_ROWS, N_DEV, 128).transpose(1, 0, 2).reshape(N_DEV * CONV_ROWS, 128)
    (grad_x, dshift, dscale), (r_in, r_cw) = _dh_kernel([dqkv, dga, dab, dgc], wt_in, dz, x, mod,
                                                        scatter=(gwt_in, g_cw_blocks))
    dmod = jnp.concatenate([dshift, dscale, dgate], axis=1)
    small = dict(dmod=dmod, b_pw=g_bpw, conv_ln_g=g_clg, conv_ln_b=g_clb, conv_b=g_cb, ln_g=g_ln_g, ln_b=g_ln_b,
                 rel_bias=g_rb[:, :N_BUCKETS].reshape(1, N_BUCKETS * N_Q_HEADS),
                 sinks=dsink[0:1, :], loss=loss[0:1, :])
    return grad_x, r_in, r_out, r_pw, r_cw, small


SMALL_FIELDS = (("dmod", 3 * D_MODEL), ("b_pw", D_CONV), ("conv_ln_g", D_CONV), ("conv_ln_b", D_CONV),
                ("conv_b", D_CONV), ("ln_g", D_MODEL), ("ln_b", D_MODEL), ("rel_bias", N_BUCKETS * N_Q_HEADS),
                ("sinks", 128), ("loss", 128))


def _pack(fields):
    parts = []
    for name, width in SMALL_FIELDS:
        v = fields[name].reshape(1, -1).astype(F32)
        if v.shape[1] < width:
            v = jnp.pad(v, ((0, 0), (0, width - v.shape[1])))
        parts.append(v)
    return jnp.concatenate(parts, axis=1)


def _unpack(vec):
    out, off = {}, 0
    for name, width in SMALL_FIELDS:
        out[name] = vec[:, off:off + width]
        off += width
    return out


def kernel(x, c, w_ada, b_ada, w_in, rel_bias, sinks, conv_w, conv_b, conv_ln_g, conv_ln_b, w_pw, b_pw, w_out, ln_g, ln_b, loss_target, m_w_ada, m_b_ada, m_w_in, m_rel_bias, m_sinks, m_conv_w, m_conv_b, m_conv_ln_g, m_conv_ln_b, m_w_pw, m_b_pw, m_w_out, m_ln_g, m_ln_b, v_w_ada, v_b_ada, v_w_in, v_rel_bias, v_sinks, v_conv_w, v_conv_b, v_conv_ln_g, v_conv_ln_b, v_w_pw, v_b_pw, v_w_out, v_ln_g, v_ln_b):
    xi, yi, ci = _me()
    me = 4 * xi + 2 * yi + ci
    ncol = w_ada.shape[2]

    wt_in_loc = w_in[0].T.astype(BF16)
    conv_w_loc = jnp.pad(conv_w[0], ((0, CONV_ROWS - CONV_WIDTH), (0, 0)))
    (wt_in_full,) = _all_gather_hbm([wt_in_loc], "gather_w_in")

    b_ada_cols = lax.dynamic_slice(b_ada, (0, me * ncol), (1, ncol))
    mod_blocks, c_all = _ada_fwd(c, w_ada[0], b_ada_cols)
    mod = mod_blocks.reshape(1, 3 * D_MODEL)

    grad_x, r_in, r_out, r_pw, r_cw, small = _local_step(
        x[0], loss_target[0], mod, wt_in_full, w_out[0].astype(BF16), w_pw[0].astype(BF16), conv_w_loc, rel_bias,
        sinks, conv_b, conv_ln_g, conv_ln_b, b_pw, ln_g, ln_b)

    gathered, summed = _small_gather(_pack(small))
    tot = _unpack(summed)
    dmod_all = gathered[:, 0, :3 * D_MODEL]
    loss = tot["loss"][0, 0]

    ct = jnp.zeros((D_MODEL, 128), BF16).at[:, :N_DEV].set(c_all[:, 0, :].T.astype(BF16))
    dm = jnp.zeros((128, ncol), BF16).at[:N_DEV, :].set(
        lax.dynamic_slice(dmod_all, (0, me * ncol), (N_DEV, ncol)).astype(BF16))
    g_w_ada = _small_mm(ct, dm, "grad_w_ada")

    g_w_in = _sum8(r_in.reshape(N_DEV, D_IN // N_DEV, D_MODEL), "sum_w_in").T
    g_w_out = _sum8(r_out.reshape(N_DEV, D_MODEL // N_DEV, D_MODEL), "sum_w_out")
    g_w_pw = _sum8(r_pw.reshape(N_DEV, D_CONV // N_DEV, D_CONV), "sum_w_pw")
    g_conv_w = _sum8(r_cw.reshape(N_DEV, CONV_ROWS, 128), "sum_conv_w")[:CONV_WIDTH]

    grads = {"w_ada": g_w_ada, "w_in": g_w_in, "conv_w": g_conv_w, "w_pw": g_w_pw, "w_out": g_w_out}
    params = {"w_ada": (w_ada, m_w_ada, v_w_ada), "w_in": (w_in, m_w_in, v_w_in),
              "conv_w": (conv_w, m_conv_w, v_conv_w), "w_pw": (w_pw, m_w_pw, v_w_pw),
              "w_out": (w_out, m_w_out, v_w_out)}
    res = {}
    for name, g in grads.items():
        w_, m_, v_ = params[name]
        d_, nm_, nv_ = _adamw(w_[0], g, m_[0], v_[0], "adamw_" + name)
        res[name] = (g[None], d_[None], nm_[None], nv_[None])

    small_params = {"b_ada": (b_ada, m_b_ada, v_b_ada), "b_pw": (b_pw, m_b_pw, v_b_pw),
                    "conv_ln_g": (conv_ln_g, m_conv_ln_g, v_conv_ln_g),
                    "conv_ln_b": (conv_ln_b, m_conv_ln_b, v_conv_ln_b), "conv_b": (conv_b, m_conv_b, v_conv_b),
                    "ln_g": (ln_g, m_ln_g, v_ln_g), "ln_b": (ln_b, m_ln_b, v_ln_b),
                    "rel_bias": (rel_bias, m_rel_bias, v_rel_bias), "sinks": (sinks, m_sinks, v_sinks)}
    to_small = lambda n, a: a.T if n == "rel_bias" else a
    key_of = {"b_ada": "dmod"}
    packs = []
    for j in range(3):
        fields = {key_of.get(n, n): to_small(n, t[j]) for n, t in small_params.items()}
        fields["loss"] = jnp.zeros((1, 1), F32)
        packs.append(_pack(fields))
    gsum = summed
    d_s, nm_s, nv_s = _adamw(packs[0], gsum, packs[1], packs[2], "adamw_small")
    outs_small = [_unpack(a) for a in (gsum, d_s, nm_s, nv_s)]
    for n, t in small_params.items():
        shape = t[0].shape
        vals = []
        for o in outs_small:
            a = o[key_of.get(n, n)]
            if n == "rel_bias":
                a = a.reshape(N_Q_HEADS, N_BUCKETS).T
            else:
                a = a[:, :shape[1]].reshape(shape)
            vals.append(a)
        res[n] = tuple(vals)

    order = ["w_ada", "b_ada", "w_in", "rel_bias", "sinks", "conv_w", "conv_b", "conv_ln_g", "conv_ln_b", "w_pw",
             "b_pw", "w_out", "ln_g", "ln_b"]
    out = [loss, grad_x[None]]
    for j in range(4):
        out += [res[n][j] for n in order]
    return tuple(out)
```

```python
import functools
import math

import jax
import jax.numpy as jnp
import numpy as np
from jax import lax
from jax.experimental import pallas as pl
from jax.experimental.pallas import tpu as pltpu

F32, BF16, I32 = jnp.float32, jnp.bfloat16, jnp.int32

D_MODEL = 2048
D_ATTN = 1024
D_CONV = 1024
D_KV = 256
HEAD_DIM = 64
N_Q_HEADS = 16
N_KV_HEADS = 4
GQA = 4
BLOCK = 128
CONV_WIDTH = 31
CONV_ROWS = 32
HALO = 32
N_BUCKETS = 32
MAX_DISTANCE = 128
LN_EPS = 1e-5
ALPHA = 2.0 ** 0.25
D_IN = 5632
N_DEV = 8
NEG = -1e30

ADAM_LR, ADAM_B1, ADAM_B2, ADAM_EPS, ADAM_WD, ADAM_STEP = 0.001, 0.9, 0.999, 1e-08, 0.01, 10

NT_DIMS = (((1,), (1,)), ((), ()))
TN_DIMS = (((0,), (0,)), ((), ()))
MIB = 1 << 20


def _pc(body, name, **kw):
    return pl.pallas_call(body, name=name, **kw)


def _cp(vmem_mib=None, sem=None):
    kw = {}
    if vmem_mib is not None:
        kw["vmem_limit_bytes"] = vmem_mib * MIB
    if sem is not None:
        kw["dimension_semantics"] = sem
    return pltpu.CompilerParams(**kw)


def _sig(x):
    return 1.0 / (1.0 + jnp.exp(-x))


def _dsilu(x, s):
    return s * (1.0 + x * (1.0 - s))


def _me():
    return lax.axis_index("x"), lax.axis_index("y"), lax.axis_index("c")


def _peer(k):
    x, y, c = _me()
    px = 1 - x if k & 4 else x
    py = 1 - y if k & 2 else y
    pc = 1 - c if k & 1 else c
    return (px, py, pc), 4 * px + 2 * py + pc


def _remote(src, dst, send_sem, recv_sem, dev):
    return pltpu.make_async_remote_copy(src_ref=src, dst_ref=dst, send_sem=send_sem, recv_sem=recv_sem,
                                        device_id=dev, device_id_type=pl.DeviceIdType.MESH)


HBM_SPEC = pl.BlockSpec(memory_space=pl.ANY)
VMEM_SPEC = pl.BlockSpec(memory_space=pltpu.VMEM)


def _comm_scratch(nt):
    return [pltpu.SemaphoreType.DMA((nt, N_DEV - 1)), pltpu.SemaphoreType.DMA((nt, N_DEV - 1)),
            pltpu.SemaphoreType.DMA((nt,))]


def _gather_shapes(shards):
    return [jax.ShapeDtypeStruct((N_DEV * s.shape[0], s.shape[1]), s.dtype) for s in shards]


def _block(ref, blk, rows):
    return ref.at[pl.ds(pl.multiple_of(blk * rows, 8), rows), :]


def _gather_copies(src, dst, sems):
    send_sems, recv_sems, local_sems = sems
    x, y, c = _me()
    me = 4 * x + 2 * y + c
    nt = len(src)
    local = [pltpu.make_async_copy(src[t], _block(dst[t], me, src[t].shape[0]), local_sems.at[t]) for t in range(nt)]
    sends, arrivals = [], []
    for k in range(1, N_DEV):
        dev, blk = _peer(k)
        for t in range(nt):
            r = src[t].shape[0]
            pair = (send_sems.at[t, k - 1], recv_sems.at[t, k - 1], dev)
            sends.append(_remote(src[t], _block(dst[t], me, r), *pair))
            arrivals.append(_remote(src[t], _block(dst[t], blk, r), *pair))
    return local, sends, arrivals


def _scatter_copies(src, dst, sems):
    send_sems, recv_sems, local_sems = sems
    x, y, c = _me()
    me = 4 * x + 2 * y + c
    nt = len(src)
    rows = [s.shape[0] // N_DEV for s in src]
    local = [pltpu.make_async_copy(_block(src[t], me, rows[t]), _block(dst[t], me, rows[t]), local_sems.at[t])
             for t in range(nt)]
    sends, arrivals = [], []
    for k in range(1, N_DEV):
        dev, blk = _peer(k)
        for t in range(nt):
            pair = (send_sems.at[t, k - 1], recv_sems.at[t, k - 1], dev)
            sends.append(_remote(_block(src[t], blk, rows[t]), _block(dst[t], me, rows[t]), *pair))
            arrivals.append(_remote(_block(src[t], me, rows[t]), _block(dst[t], blk, rows[t]), *pair))
    return local, sends, arrivals


def _comm_start(cps):
    local, sends, _ = cps
    for cp in local + sends:
        cp.start()


def _comm_wait(cps):
    local, sends, arrivals = cps
    for cp in arrivals:
        cp.wait_recv()
    for cp in sends:
        cp.wait_send()
    for cp in local:
        cp.wait()


IN_PIECES = ((0, 1536, BF16), (1536, 1024, F32), (2560, 2048, F32), (4608, 1024, F32))


def _inproj(x, mod, wt, gather):
    S = x.shape[0]
    tm = min(256, S)
    ni = S // tm
    ng = len(gather)
    npc = len(IN_PIECES)

    def body(*refs):
        x_ref, sh_ref, sc_ref, w_ref = refs[:4]
        outs = refs[4 + ng:]
        h_ref, piece_refs = outs[0], outs[1:1 + npc]
        i = pl.program_id(0)
        if ng:
            cps = _gather_copies(refs[4:4 + ng], outs[1 + npc:1 + npc + ng], outs[1 + npc + ng:])

            @pl.when(i == 0)
            def _():
                _comm_start(cps)

        h = (x_ref[...] * (1.0 + sc_ref[...]) + sh_ref[...]).astype(BF16)
        h_ref[...] = h
        for (r0, n, dt), o_ref in zip(IN_PIECES, piece_refs):
            o_ref[...] = lax.dot_general(h, w_ref[r0:r0 + n, :], NT_DIMS, preferred_element_type=F32).astype(dt)
        if ng:
            @pl.when(i == ni - 1)
            def _():
                _comm_wait(cps)

    tile = lambda n: pl.BlockSpec((tm, n), lambda i: (i, 0))
    res = _pc(
        body, "inproj", grid=(ni,),
        in_specs=[tile(D_MODEL),
                  pl.BlockSpec((1, D_MODEL), lambda i: (0, 0)),
                  pl.BlockSpec((1, D_MODEL), lambda i: (0, 1)),
                  pl.BlockSpec((D_IN, D_MODEL), lambda i: (0, 0), pipeline_mode=pl.Buffered(1))] + [HBM_SPEC] * ng,
        out_specs=[tile(D_MODEL)] + [tile(n) for _, n, _ in IN_PIECES] + [HBM_SPEC] * ng,
        out_shape=[jax.ShapeDtypeStruct((S, D_MODEL), BF16)]
        + [jax.ShapeDtypeStruct((S, n), dt) for _, n, dt in IN_PIECES] + _gather_shapes(gather),
        scratch_shapes=_comm_scratch(ng) if ng else [],
        compiler_params=_cp(56, ("arbitrary",)),
    )(x, mod, mod, wt, *gather)
    return res[:1 + npc], res[1 + npc:]


def _matmul_tn(a, b, out_prev, row_off, m_total, name):
    S, M = a.shape
    N = b.shape[1]
    tm, ts = 512, min(2048, S)
    assert row_off % tm == 0 and M % tm == 0
    ob = row_off // tm
    ns = S // ts

    def body(*refs):
        a_ref, b_ref = refs[0], refs[1]
        o_ref, acc = refs[-2], refs[-1]
        s = pl.program_id(1)

        @pl.when(s == 0)
        def _():
            acc[...] = jnp.zeros_like(acc)

        acc[...] += lax.dot_general(a_ref[...], b_ref[...], TN_DIMS, preferred_element_type=F32)

        @pl.when(s == ns - 1)
        def _():
            o_ref[...] = acc[...].astype(BF16)

    in_specs = [pl.BlockSpec((ts, tm), lambda i, s: (s, i)),
                pl.BlockSpec((ts, N), lambda i, s: (s, 0))]
    args = [a, b]
    aliases = {}
    if out_prev is not None:
        in_specs.append(pl.BlockSpec(memory_space=pl.ANY))
        args.append(out_prev)
        aliases = {2: 0}
    return _pc(
        body, name, grid=(M // tm, ns),
        in_specs=in_specs,
        out_specs=pl.BlockSpec((tm, N), lambda i, s: (ob + i, 0)),
        out_shape=jax.ShapeDtypeStruct((m_total, N), BF16),
        scratch_shapes=[pltpu.VMEM((tm, N), F32)],
        input_output_aliases=aliases,
        compiler_params=_cp(48, ("parallel", "arbitrary")),
    )(*args)


def _bucket_map():
    qi = jnp.arange(BLOCK, dtype=I32)[:, None]
    kj = jnp.arange(2 * BLOCK, dtype=I32)[None, :]
    dist = qi + BLOCK - kj
    in_window = (dist >= 0) & (dist < BLOCK)
    d0 = jnp.maximum(dist, 0)
    max_exact = N_BUCKETS // 2
    d = jnp.maximum(d0, 1).astype(F32)
    large = max_exact + (jnp.log(d / max_exact) / math.log(MAX_DISTANCE / max_exact)
                         * (N_BUCKETS - max_exact)).astype(I32)
    large = jnp.minimum(large, N_BUCKETS - 1)
    bucket = jnp.where(d0 < max_exact, d0, large)
    return jnp.where(in_window, bucket, -1).astype(I32)


def _bias_table(rel_bias, bmap):
    def body(rb_ref, bm_ref, o_ref):
        h = pl.program_id(0)
        bm = bm_ref[...]
        acc = jnp.full((BLOCK, 2 * BLOCK), NEG, F32)
        for b in range(N_BUCKETS):
            acc = jnp.where(bm == b, rb_ref[b, h], acc)
        o_ref[0] = acc

    return _pc(
        body, "bias_table", grid=(N_Q_HEADS,),
        in_specs=[pl.BlockSpec(memory_space=pltpu.SMEM),
                  pl.BlockSpec((BLOCK, 2 * BLOCK), lambda h: (0, 0))],
        out_specs=pl.BlockSpec((1, BLOCK, 2 * BLOCK), lambda h: (h, 0, 0)),
        out_shape=jax.ShapeDtypeStruct((N_Q_HEADS, BLOCK, 2 * BLOCK), F32),
    )(rel_bias, bmap)


def _relbias_grad(dbias, bmap):
    def body(db_ref, bm_ref, o_ref):
        bm = bm_ref[...]
        x = db_ref[0]
        lane = lax.broadcasted_iota(I32, (1, 128), 1)
        row = jnp.zeros((1, 128), F32)
        for b in range(N_BUCKETS):
            row = jnp.where(lane == b, jnp.sum(jnp.where(bm == b, x, 0.0)), row)
        o_ref[0] = jnp.broadcast_to(row, (8, 128))

    out = _pc(
        body, "relbias_grad", grid=(N_Q_HEADS,),
        in_specs=[pl.BlockSpec((1, BLOCK, 2 * BLOCK), lambda h: (h, 0, 0)),
                  pl.BlockSpec((BLOCK, 2 * BLOCK), lambda h: (0, 0))],
        out_specs=pl.BlockSpec((1, 8, 128), lambda h: (h, 0, 0)),
        out_shape=jax.ShapeDtypeStruct((N_Q_HEADS, 8, 128), F32),
    )(dbias, bmap)
    return out[:, 0, :]


def _stack_heads(ref, hk):
    return jnp.concatenate([ref[:, pl.ds(256 * hk + 64 * g, 64)] for g in range(GQA)], axis=0)


def _sink_col(sink_ref, hk):
    row = lax.broadcasted_iota(I32, (GQA * BLOCK, 1), 0)
    s = jnp.full((GQA * BLOCK, 1), sink_ref[0, 4 * hk + 3], F32)
    for g in (2, 1, 0):
        s = jnp.where(row < (g + 1) * BLOCK, sink_ref[0, 4 * hk + g], s)
    return s


def _attn_probs(q4, kw, bias_ref, sink_ref, hk, first_mask):
    s = lax.dot_general(q4, kw, NT_DIMS, preferred_element_type=F32) * (HEAD_DIM ** -0.5)
    s = s + bias_ref[4 * hk:4 * hk + 4].reshape(GQA * BLOCK, 2 * BLOCK)
    s = jnp.where(first_mask, s, NEG)
    sink = _sink_col(sink_ref, hk)
    m = jnp.maximum(jnp.max(s, axis=1, keepdims=True), sink)
    e = jnp.exp(s - m)
    es = jnp.exp(sink - m)
    inv = 1.0 / (jnp.sum(e, axis=1, keepdims=True) + es)
    return e * inv, es * inv


def _attn_fwd(qkv, g_attn, bias, sinks):
    S = qkv.shape[0]
    nb = S // BLOCK

    def body(q_ref, kc_ref, kp_ref, vc_ref, vp_ref, g_ref, bias_ref, sink_ref, y_ref, a_ref):
        n = pl.program_id(0)
        kj = lax.broadcasted_iota(I32, (GQA * BLOCK, 2 * BLOCK), 1)
        first_mask = (n > 0) | (kj >= BLOCK)
        for hk in range(N_KV_HEADS):
            q4 = _stack_heads(q_ref, hk)
            ks = pl.ds(64 * hk, 64)
            kw = jnp.concatenate([kp_ref[:, ks], kc_ref[:, ks]], axis=0)
            vw = jnp.concatenate([vp_ref[:, ks], vc_ref[:, ks]], axis=0)
            p, _ = _attn_probs(q4, kw, bias_ref, sink_ref, hk, first_mask)
            o4 = jnp.dot(p.astype(BF16), vw, preferred_element_type=F32)
            for g in range(GQA):
                sl = pl.ds(256 * hk + 64 * g, 64)
                og = o4[BLOCK * g:BLOCK * (g + 1)]
                gg = g_ref[:, sl]
                a_ref[:, sl] = og
                y_ref[:, sl] = (og * (gg * _sig(gg))).astype(BF16)

    prev = lambda n: jnp.maximum(n - 1, 0)
    return _pc(
        body, "attn_fwd", grid=(nb,),
        in_specs=[pl.BlockSpec((BLOCK, D_ATTN), lambda n: (n, 0)),
                  pl.BlockSpec((BLOCK, D_KV), lambda n: (n, 4)),
                  pl.BlockSpec((BLOCK, D_KV), lambda n: (prev(n), 4)),
                  pl.BlockSpec((BLOCK, D_KV), lambda n: (n, 5)),
                  pl.BlockSpec((BLOCK, D_KV), lambda n: (prev(n), 5)),
                  pl.BlockSpec((BLOCK, D_ATTN), lambda n: (n, 0)),
                  pl.BlockSpec((N_Q_HEADS, BLOCK, 2 * BLOCK), lambda n: (0, 0, 0)),
                  pl.BlockSpec(memory_space=pltpu.SMEM)],
        out_specs=[pl.BlockSpec((BLOCK, D_ATTN), lambda n: (n, 0)),
                   pl.BlockSpec((BLOCK, D_ATTN), lambda n: (n, 0))],
        out_shape=[jax.ShapeDtypeStruct((S, 2 * D_ATTN), BF16),
                   jax.ShapeDtypeStruct((S, D_ATTN), F32)],
        compiler_params=_cp(40, ("parallel",)),
    )(qkv, qkv, qkv, qkv, qkv, g_attn, bias, sinks)


def _attn_bwd(qkv, g_attn, a_out, dycat, bias, sinks, scatter=()):
    S = qkv.shape[0]
    nb = S // BLOCK
    R = GQA * BLOCK
    ns = len(scatter)

    def body(*refs):
        (q_ref, kc_ref, kp_ref, vc_ref, vp_ref, g_ref, a_ref, dy_ref, bias_ref, sink_ref) = refs[:10]
        dqkv_ref, dg_ref, dbias_ref, dsink_ref = refs[10 + ns:14 + ns]
        dq_scr, dq_new, dk_scr, dv_scr, dkw_scr, dvw_scr, ds_scr = refs[14 + 2 * ns:21 + 2 * ns]
        n = pl.program_id(0)
        if ns:
            cps = _scatter_copies(refs[10:10 + ns], refs[14 + ns:14 + 2 * ns], refs[21 + 2 * ns:])

            @pl.when(n == 0)
            def _():
                _comm_start(cps)

        @pl.when(n == 0)
        def _():
            dbias_ref[...] = jnp.zeros_like(dbias_ref)
            ds_scr[...] = jnp.zeros_like(ds_scr)
            dq_scr[...] = jnp.zeros_like(dq_scr)
            dk_scr[...] = jnp.zeros_like(dk_scr)
            dv_scr[...] = jnp.zeros_like(dv_scr)

        @pl.when(n < nb)
        def _():
            kj = lax.broadcasted_iota(I32, (R, 2 * BLOCK), 1)
            first_mask = (n > 0) | (kj >= BLOCK)
            for hk in range(N_KV_HEADS):
                q4 = _stack_heads(q_ref, hk)
                ks = pl.ds(64 * hk, 64)
                kw = jnp.concatenate([kp_ref[:, ks], kc_ref[:, ks]], axis=0)
                vw = jnp.concatenate([vp_ref[:, ks], vc_ref[:, ks]], axis=0)
                p, psink = _attn_probs(q4, kw, bias_ref, sink_ref, hk, first_mask)
                da_parts, a_parts = [], []
                for g in range(GQA):
                    sl = pl.ds(256 * hk + 64 * g, 64)
                    gg = g_ref[:, sl]
                    sg = _sig(gg)
                    dyg = dy_ref[:, sl]
                    ag = a_ref[:, sl]
                    da_parts.append(dyg * (gg * sg))
                    a_parts.append(ag)
                    dg_ref[:, sl] = (dyg * ag * _dsilu(gg, sg)).astype(BF16)
                da4 = jnp.concatenate(da_parts, axis=0)
                a4 = jnp.concatenate(a_parts, axis=0)
                delta = jnp.sum(da4 * a4, axis=1, keepdims=True)
                da4b = da4.astype(BF16)
                dp = lax.dot_general(da4b, vw, NT_DIMS, preferred_element_type=F32)
                ds = p * (dp - delta)
                ds_scr[hk] += -psink * delta
                dbias_ref[4 * hk:4 * hk + 4] += ds.reshape(GQA, BLOCK, 2 * BLOCK)
                dsb = ds.astype(BF16)
                dq4 = jnp.dot(dsb, kw, preferred_element_type=F32) * (HEAD_DIM ** -0.5)
                for g in range(GQA):
                    dq_new[:, pl.ds(256 * hk + 64 * g, 64)] = dq4[BLOCK * g:BLOCK * (g + 1)].astype(BF16)
                dkw_scr[:, ks] = lax.dot_general(dsb, q4, TN_DIMS, preferred_element_type=F32) * (HEAD_DIM ** -0.5)
                dvw_scr[:, ks] = lax.dot_general(p.astype(BF16), da4b, TN_DIMS, preferred_element_type=F32)

        @pl.when(n == nb)
        def _():
            dkw_scr[0:BLOCK, :] = jnp.zeros((BLOCK, D_KV), F32)
            dvw_scr[0:BLOCK, :] = jnp.zeros((BLOCK, D_KV), F32)

        dqkv_ref[:, 0:D_ATTN] = dq_scr[...]
        dqkv_ref[:, D_ATTN:D_ATTN + D_KV] = (dk_scr[...] + dkw_scr[0:BLOCK, :]).astype(BF16)
        dqkv_ref[:, D_ATTN + D_KV:D_ATTN + 2 * D_KV] = (dv_scr[...] + dvw_scr[0:BLOCK, :]).astype(BF16)
        dq_scr[...] = dq_new[...]
        dk_scr[...] = dkw_scr[BLOCK:2 * BLOCK, :]
        dv_scr[...] = dvw_scr[BLOCK:2 * BLOCK, :]

        @pl.when(n == nb)
        def _():
            lane = lax.broadcasted_iota(I32, (1, 128), 1)
            row = jnp.zeros((1, 128), F32)
            for hk in range(N_KV_HEADS):
                col = ds_scr[hk]
                for g in range(GQA):
                    row = jnp.where(lane == 4 * hk + g, jnp.sum(col[BLOCK * g:BLOCK * (g + 1)]), row)
            dsink_ref[...] = jnp.broadcast_to(row, (8, 128))
            if ns:
                _comm_wait(cps)

    cur = lambda n: jnp.minimum(n, nb - 1)
    prev = lambda n: jnp.clip(n - 1, 0, nb - 1)
    res = _pc(
        body, "attn_bwd", grid=(nb + 1,),
        in_specs=[pl.BlockSpec((BLOCK, D_ATTN), lambda n: (cur(n), 0)),
                  pl.BlockSpec((BLOCK, D_KV), lambda n: (cur(n), 4)),
                  pl.BlockSpec((BLOCK, D_KV), lambda n: (prev(n), 4)),
                  pl.BlockSpec((BLOCK, D_KV), lambda n: (cur(n), 5)),
                  pl.BlockSpec((BLOCK, D_KV), lambda n: (prev(n), 5)),
                  pl.BlockSpec((BLOCK, D_ATTN), lambda n: (cur(n), 0)),
                  pl.BlockSpec((BLOCK, D_ATTN), lambda n: (cur(n), 0)),
                  pl.BlockSpec((BLOCK, D_ATTN), lambda n: (cur(n), 0)),
                  pl.BlockSpec((N_Q_HEADS, BLOCK, 2 * BLOCK), lambda n: (0, 0, 0)),
                  pl.BlockSpec(memory_space=pltpu.SMEM)] + [HBM_SPEC] * ns,
        out_specs=[pl.BlockSpec((BLOCK, D_ATTN + 2 * D_KV), lambda n: (prev(n), 0)),
                   pl.BlockSpec((BLOCK, D_ATTN), lambda n: (cur(n), 0)),
                   pl.BlockSpec((N_Q_HEADS, BLOCK, 2 * BLOCK), lambda n: (0, 0, 0)),
                   pl.BlockSpec((8, 128), lambda n: (0, 0))] + [HBM_SPEC] * ns,
        out_shape=[jax.ShapeDtypeStruct((S, D_ATTN + 2 * D_KV), BF16),
                   jax.ShapeDtypeStruct((S, D_ATTN), BF16),
                   jax.ShapeDtypeStruct((N_Q_HEADS, BLOCK, 2 * BLOCK), F32),
                   jax.ShapeDtypeStruct((8, 128), F32)] + [jax.ShapeDtypeStruct(f.shape, f.dtype) for f in scatter],
        scratch_shapes=[pltpu.VMEM((BLOCK, D_ATTN), BF16), pltpu.VMEM((BLOCK, D_ATTN), BF16),
                        pltpu.VMEM((BLOCK, D_KV), F32), pltpu.VMEM((BLOCK, D_KV), F32),
                        pltpu.VMEM((2 * BLOCK, D_KV), F32), pltpu.VMEM((2 * BLOCK, D_KV), F32),
                        pltpu.VMEM((N_KV_HEADS, R, 1), F32)] + (_comm_scratch(ns) if ns else []),
        compiler_params=_cp(48, ("arbitrary",)),
    )(qkv, qkv, qkv, qkv, qkv, g_attn, a_out, dycat, bias, sinks, *scatter)
    return res[:4], res[4:]


def _conv_tile(S):
    return min(256, S)


def _shifted(win, s):
    return win if s == 0 else pltpu.roll(win, win.shape[0] - s, axis=0)


def _conv_fwd(glu, g_conv, ycat_in, conv_w, conv_b, ln_g, ln_b, w_pw, b_pw):
    S = glu.shape[0]
    T = _conv_tile(S)
    hb = T // HALO

    def body(a_ref, ah_ref, b_ref, bh_ref, gc_ref, cw_ref, cb_ref, lg_ref, lb_ref, wpw_ref, bpw_ref, ycat_in_ref,
             y_ref, u1_ref, u3_ref, p_ref):
        i = pl.program_id(0)
        u0c = a_ref[...] * _sig(b_ref[...])
        u0h = jnp.where(i > 0, ah_ref[...] * _sig(bh_ref[...]), 0.0)
        win = jnp.concatenate([u0h, u0c], axis=0)
        acc = jnp.zeros((T, D_CONV), F32)
        for s in range(8):
            ws = _shifted(win, s)
            for aa in range(5):
                j = 8 * aa + s - 2
                if 0 <= j < CONV_WIDTH:
                    acc = acc + ws[8 * aa:8 * aa + T] * cw_ref[j:j + 1, :]
        u1 = acc + cb_ref[...]
        u1_ref[...] = u1
        mu = jnp.mean(u1, axis=1, keepdims=True)
        uc = u1 - mu
        rstd = lax.rsqrt(jnp.mean(uc * uc, axis=1, keepdims=True) + LN_EPS)
        u2 = uc * rstd * lg_ref[...] + lb_ref[...]
        u3 = (u2 * _sig(u2)).astype(BF16)
        u3_ref[...] = u3
        p = jnp.dot(u3, wpw_ref[...], preferred_element_type=F32) + bpw_ref[...]
        p_ref[...] = p
        gc = gc_ref[...]
        y_ref[...] = (p * (gc * _sig(gc))).astype(BF16)

    halo = lambda i: jnp.maximum(i * hb - 1, 0)
    vec = pl.BlockSpec((1, D_CONV), lambda i: (0, 0))
    return _pc(
        body, "conv_fwd", grid=(S // T,),
        in_specs=[pl.BlockSpec((T, D_CONV), lambda i: (i, 0)),
                  pl.BlockSpec((HALO, D_CONV), lambda i: (halo(i), 0)),
                  pl.BlockSpec((T, D_CONV), lambda i: (i, 1)),
                  pl.BlockSpec((HALO, D_CONV), lambda i: (halo(i), 1)),
                  pl.BlockSpec((T, D_CONV), lambda i: (i, 0)),
                  pl.BlockSpec((CONV_ROWS, D_CONV), lambda i: (0, 0)),
                  vec, vec, vec,
                  pl.BlockSpec((D_CONV, D_CONV), lambda i: (0, 0)),
                  vec,
                  pl.BlockSpec(memory_space=pl.ANY)],
        out_specs=[pl.BlockSpec((T, D_CONV), lambda i: (i, 1)),
                   pl.BlockSpec((T, D_CONV), lambda i: (i, 0)),
                   pl.BlockSpec((T, D_CONV), lambda i: (i, 0)),
                   pl.BlockSpec((T, D_CONV), lambda i: (i, 0))],
        out_shape=[jax.ShapeDtypeStruct((S, 2 * D_CONV), BF16),
                   jax.ShapeDtypeStruct((S, D_CONV), F32),
                   jax.ShapeDtypeStruct((S, D_CONV), BF16),
                   jax.ShapeDtypeStruct((S, D_CONV), F32)],
        input_output_aliases={11: 0},
        compiler_params=_cp(48, ("parallel",)),
    )(glu, glu, glu, glu, g_conv, conv_w, conv_b, ln_g, ln_b, w_pw, b_pw, ycat_in)


def _conv_bwd_a(dycat, g_conv, p_out, u1, ln_g, ln_b, w_pw):
    S = u1.shape[0]
    T = _conv_tile(S)

    def body(dy_ref, gc_ref, p_ref, u1_ref, lg_ref, lb_ref, wpw_ref,
             dp_ref, dgc_ref, du1_ref, gbpw_ref, glg_ref, glb_ref, gcb_ref):
        i = pl.program_id(0)

        @pl.when(i == 0)
        def _():
            for r in (gbpw_ref, glg_ref, glb_ref, gcb_ref):
                r[...] = jnp.zeros_like(r)

        dy = dy_ref[...]
        gc = gc_ref[...]
        sg = _sig(gc)
        dp = dy * (gc * sg)
        dgc_ref[...] = (dy * p_ref[...] * _dsilu(gc, sg)).astype(BF16)
        gbpw_ref[...] += jnp.sum(dp, axis=0, keepdims=True)
        dpb = dp.astype(BF16)
        dp_ref[...] = dpb
        du3 = lax.dot_general(dpb, wpw_ref[...], NT_DIMS, preferred_element_type=F32)
        u1 = u1_ref[...]
        mu = jnp.mean(u1, axis=1, keepdims=True)
        uc = u1 - mu
        rstd = lax.rsqrt(jnp.mean(uc * uc, axis=1, keepdims=True) + LN_EPS)
        uh = uc * rstd
        lg = lg_ref[...]
        u2 = uh * lg + lb_ref[...]
        s2 = _sig(u2)
        du2 = du3 * _dsilu(u2, s2)
        glg_ref[...] += jnp.sum(du2 * uh, axis=0, keepdims=True)
        glb_ref[...] += jnp.sum(du2, axis=0, keepdims=True)
        duh = du2 * lg
        du1 = rstd * (duh - jnp.mean(duh, axis=1, keepdims=True) - uh * jnp.mean(duh * uh, axis=1, keepdims=True))
        du1_ref[...] = du1
        gcb_ref[...] += jnp.sum(du1, axis=0, keepdims=True)

    vec = pl.BlockSpec((1, D_CONV), lambda i: (0, 0))
    tile = pl.BlockSpec((T, D_CONV), lambda i: (i, 0))
    vshape = jax.ShapeDtypeStruct((1, D_CONV), F32)
    return _pc(
        body, "conv_bwd_a", grid=(S // T,),
        in_specs=[pl.BlockSpec((T, D_CONV), lambda i: (i, 1)), tile, tile, tile, vec, vec,
                  pl.BlockSpec((D_CONV, D_CONV), lambda i: (0, 0))],
        out_specs=[tile, tile, tile, vec, vec, vec, vec],
        out_shape=[jax.ShapeDtypeStruct((S, D_CONV), BF16), jax.ShapeDtypeStruct((S, D_CONV), BF16),
                   jax.ShapeDtypeStruct((S, D_CONV), F32), vshape, vshape, vshape, vshape],
        compiler_params=_cp(48, ("arbitrary",)),
    )(dycat, g_conv, p_out, u1, ln_g, ln_b, w_pw)


def _conv_bwd_b(du1, glu, conv_w):
    S = du1.shape[0]
    T = _conv_tile(S)
    hb = T // HALO
    nt = S // T
    last_h = S // HALO - 1

    def body(du_ref, dun_ref, a_ref, ah_ref, b_ref, bh_ref, cw_ref, dab_ref, gw_ref):
        i = pl.program_id(0)

        @pl.when(i == 0)
        def _():
            gw_ref[...] = jnp.zeros_like(gw_ref)

        a = a_ref[...]
        sb = _sig(b_ref[...])
        u0h = jnp.where(i > 0, ah_ref[...] * _sig(bh_ref[...]), 0.0)
        win = jnp.concatenate([u0h, a * sb], axis=0)
        du1c = du_ref[...]
        du1n = jnp.where(i < nt - 1, dun_ref[...], 0.0)
        win2 = jnp.concatenate([du1c, du1n], axis=0)
        acc = jnp.zeros((T, D_CONV), F32)
        for s in range(8):
            w2 = _shifted(win2, s)
            for aa in range(4):
                j = CONV_WIDTH - 1 - (8 * aa + s)
                if 0 <= j < CONV_WIDTH:
                    acc = acc + w2[8 * aa:8 * aa + T] * cw_ref[j:j + 1, :]
        dab_ref[:, 0:D_CONV] = (acc * sb).astype(BF16)
        dab_ref[:, D_CONV:2 * D_CONV] = (acc * a * sb * (1.0 - sb)).astype(BF16)
        for s in range(8):
            ws = _shifted(win, s)
            for aa in range(5):
                j = 8 * aa + s - 2
                if 0 <= j < CONV_WIDTH:
                    gw_ref[j:j + 1, :] += jnp.sum(du1c * ws[8 * aa:8 * aa + T], axis=0, keepdims=True)

    halo = lambda i: jnp.maximum(i * hb - 1, 0)
    nxt = lambda i: jnp.minimum((i + 1) * hb, last_h)
    return _pc(
        body, "conv_bwd_b", grid=(nt,),
        in_specs=[pl.BlockSpec((T, D_CONV), lambda i: (i, 0)),
                  pl.BlockSpec((HALO, D_CONV), lambda i: (nxt(i), 0)),
                  pl.BlockSpec((T, D_CONV), lambda i: (i, 0)),
                  pl.BlockSpec((HALO, D_CONV), lambda i: (halo(i), 0)),
                  pl.BlockSpec((T, D_CONV), lambda i: (i, 1)),
                  pl.BlockSpec((HALO, D_CONV), lambda i: (halo(i), 1)),
                  pl.BlockSpec((CONV_ROWS, D_CONV), lambda i: (0, 0))],
        out_specs=[pl.BlockSpec((T, 2 * D_CONV), lambda i: (i, 0)),
                   pl.BlockSpec((CONV_ROWS, D_CONV), lambda i: (0, 0))],
        out_shape=[jax.ShapeDtypeStruct((S, 2 * D_CONV), BF16),
                   jax.ShapeDtypeStruct((CONV_ROWS, D_CONV), F32)],
        compiler_params=_cp(48, ("arbitrary",)),
    )(du1, du1, glu, glu, glu, glu, conv_w)


def _outproj_ln(ycat, w_out, x, target, mod, ln_g, ln_b):
    S = x.shape[0]
    tm = min(256, S)

    def body(yc_ref, w_ref, x_ref, t_ref, gate_ref, lg_ref, lb_ref,
             dz_ref, dy_ref, dyc_ref, loss_ref, glg_ref, glb_ref, dgate_ref):
        i = pl.program_id(0)

        @pl.when(i == 0)
        def _():
            for r in (loss_ref, glg_ref, glb_ref, dgate_ref):
                r[...] = jnp.zeros_like(r)

        w = w_ref[...]
        y = jnp.dot(yc_ref[...], w, preferred_element_type=F32)
        gate = gate_ref[...]
        z = ALPHA * x_ref[...] + gate * y
        mu = jnp.mean(z, axis=1, keepdims=True)
        zc = z - mu
        rstd = lax.rsqrt(jnp.mean(zc * zc, axis=1, keepdims=True) + LN_EPS)
        zh = zc * rstd
        lg = lg_ref[...]
        err = zh * lg + lb_ref[...] - t_ref[...]
        loss_ref[...] += 0.5 * jnp.sum(jnp.sum(err * err, axis=1, keepdims=True)) / D_MODEL
        dout = err * (1.0 / D_MODEL)
        glg_ref[...] += jnp.sum(dout * zh, axis=0, keepdims=True)
        glb_ref[...] += jnp.sum(dout, axis=0, keepdims=True)
        dzh = dout * lg
        dz = rstd * (dzh - jnp.mean(dzh, axis=1, keepdims=True) - zh * jnp.mean(dzh * zh, axis=1, keepdims=True))
        dz_ref[...] = dz
        dgate_ref[...] += jnp.sum(dz * y, axis=0, keepdims=True)
        dy = (dz * gate).astype(BF16)
        dy_ref[...] = dy
        dyc_ref[...] = lax.dot_general(dy, w, NT_DIMS, preferred_element_type=F32)

    vec = pl.BlockSpec((1, D_MODEL), lambda i: (0, 0))
    tile = pl.BlockSpec((tm, D_MODEL), lambda i: (i, 0))
    vshape = jax.ShapeDtypeStruct((1, D_MODEL), F32)
    return _pc(
        body, "outproj_ln", grid=(S // tm,),
        in_specs=[tile, pl.BlockSpec((D_MODEL, D_MODEL), lambda i: (0, 0)), tile, tile,
                  pl.BlockSpec((1, D_MODEL), lambda i: (0, 2)), vec, vec],
        out_specs=[tile, tile, tile, pl.BlockSpec((8, 128), lambda i: (0, 0)), vec, vec, vec],
        out_shape=[jax.ShapeDtypeStruct((S, D_MODEL), F32), jax.ShapeDtypeStruct((S, D_MODEL), BF16),
                   jax.ShapeDtypeStruct((S, D_MODEL), F32), jax.ShapeDtypeStruct((8, 128), F32),
                   vshape, vshape, vshape],
        compiler_params=_cp(56, ("arbitrary",)),
    )(ycat, w_out, x, target, mod, ln_g, ln_b)


def _dh_kernel(segs, wt, dz, x, mod, scatter=()):
    S = x.shape[0]
    tm = min(256, S)
    row0 = [0]
    for a in segs:
        row0.append(row0[-1] + a.shape[1])
    assert row0[-1] == wt.shape[0]
    nseg = len(segs)
    ns = len(scatter)
    ni = S // tm

    def body(*refs):
        seg_refs = refs[:nseg]
        w_ref, dz_ref, x_ref, sc_ref = refs[nseg:nseg + 4]
        outs = refs[nseg + 4 + ns:]
        gx_ref, dsh_ref, dsc_ref = outs[:3]
        i = pl.program_id(0)
        if ns:
            cps = _scatter_copies(refs[nseg + 4:nseg + 4 + ns], outs[3:3 + ns], outs[3 + ns:])

        @pl.when(i == 0)
        def _():
            dsh_ref[...] = jnp.zeros_like(dsh_ref)
            dsc_ref[...] = jnp.zeros_like(dsc_ref)
            if ns:
                _comm_start(cps)

        dh = jnp.dot(seg_refs[0][...], w_ref[row0[0]:row0[1], :], preferred_element_type=F32)
        for t in range(1, nseg):
            dh = dh + jnp.dot(seg_refs[t][...], w_ref[row0[t]:row0[t + 1], :], preferred_element_type=F32)
        gx_ref[...] = ALPHA * dz_ref[...] + dh * (1.0 + sc_ref[...])
        dsh_ref[...] += jnp.sum(dh, axis=0, keepdims=True)
        dsc_ref[...] += jnp.sum(dh * x_ref[...], axis=0, keepdims=True)

        if ns:
            @pl.when(i == ni - 1)
            def _():
                _comm_wait(cps)

    tile = pl.BlockSpec((tm, D_MODEL), lambda i: (i, 0))
    vec = pl.BlockSpec((1, D_MODEL), lambda i: (0, 0))
    vshape = jax.ShapeDtypeStruct((1, D_MODEL), F32)
    res = _pc(
        body, "dh_gradx", grid=(ni,),
        in_specs=[pl.BlockSpec((tm, a.shape[1]), lambda i: (i, 0)) for a in segs] + [
            pl.BlockSpec(wt.shape, lambda i: (0, 0), pipeline_mode=pl.Buffered(1)), tile, tile,
            pl.BlockSpec((1, D_MODEL), lambda i: (0, 1))] + [HBM_SPEC] * ns,
        out_specs=[tile, vec, vec] + [HBM_SPEC] * ns,
        out_shape=[jax.ShapeDtypeStruct((S, D_MODEL), F32), vshape, vshape]
        + [jax.ShapeDtypeStruct(f.shape, f.dtype) for f in scatter],
        scratch_shapes=_comm_scratch(ns) if ns else [],
        compiler_params=_cp(58, ("arbitrary",)),
    )(*segs, wt, dz, x, mod, *scatter)
    return res[:3], res[3:]


def _row_tile(rows, cols):
    if rows * cols * 4 <= 2 * MIB or rows % 8:
        return rows
    tr = max(8, (2 * MIB // (cols * 4)) // 8 * 8)
    while rows % tr:
        tr -= 8
    return tr


def _sum8(recv, name):
    _, R, C = recv.shape
    tr = _row_tile(R, C)

    def body(r_ref, o_ref):
        acc = r_ref[0].astype(F32)
        for d in range(1, N_DEV):
            acc = acc + r_ref[d].astype(F32)
        o_ref[...] = acc

    return _pc(
        body, name, grid=(R // tr,),
        in_specs=[pl.BlockSpec((N_DEV, tr, C), lambda i: (0, i, 0))],
        out_specs=pl.BlockSpec((tr, C), lambda i: (i, 0)),
        out_shape=jax.ShapeDtypeStruct((R, C), F32),
        compiler_params=_cp(40, ("parallel",)),
    )(recv)


def _adamw(w, g, m, v, name):
    R, C = w.shape
    tr = _row_tile(R, C)

    def body(w_ref, g_ref, m_ref, v_ref, d_ref, nm_ref, nv_ref):
        g_ = g_ref[...]
        m_ = ADAM_B1 * m_ref[...] + (1.0 - ADAM_B1) * g_
        v_ = ADAM_B2 * v_ref[...] + (1.0 - ADAM_B2) * (g_ * g_)
        m_hat = m_ / (1.0 - ADAM_B1 ** ADAM_STEP)
        v_hat = v_ / (1.0 - ADAM_B2 ** ADAM_STEP)
        d_ref[...] = -ADAM_LR * (m_hat / (jnp.sqrt(v_hat) + ADAM_EPS) + ADAM_WD * w_ref[...])
        nm_ref[...] = m_
        nv_ref[...] = v_

    spec = pl.BlockSpec((tr, C), lambda i: (i, 0))
    shape = jax.ShapeDtypeStruct((R, C), F32)
    return _pc(
        body, name, grid=(R // tr,),
        in_specs=[spec] * 4, out_specs=[spec] * 3, out_shape=[shape] * 3,
        compiler_params=_cp(40, ("parallel",)),
    )(w, g, m, v)


def _small_mm(a, b, name):
    def body(a_ref, b_ref, o_ref):
        o_ref[...] = jnp.dot(a_ref[...], b_ref[...], preferred_element_type=F32)

    return _pc(body, name, out_shape=jax.ShapeDtypeStruct((a.shape[0], b.shape[1]), F32),
               compiler_params=_cp(40))(a, b)


def _all_gather_two_level(shards, name):
    nt = len(shards)

    def body(*refs):
        src, dst = refs[:nt], refs[nt:2 * nt]
        send_sems, recv_sems, local_sems = refs[2 * nt:]
        x, y, c = _me()
        sibling = (x, y, 1 - c)
        chips = [(1 - x, y), (x, 1 - y), (1 - x, 1 - y)]

        def blk(t, px, py, pc):
            return _block(dst[t], 4 * px + 2 * py + pc, src[t].shape[0])

        def copy(t, k, to, block, from_src):
            return _remote(src[t] if from_src else blk(t, *block), blk(t, *block),
                           send_sems.at[t, k], recv_sems.at[t, k], to)

        local = [pltpu.make_async_copy(src[t], blk(t, x, y, c), local_sems.at[t]) for t in range(nt)]
        for cp in local:
            cp.start()
        sends = []
        for t in range(nt):
            sends.append(copy(t, 0, sibling, (x, y, c), True))
            sends += [copy(t, 1 + j, (*chip, c), (x, y, c), True) for j, chip in enumerate(chips)]
        for cp in sends:
            cp.start()
        for j, chip in enumerate(chips):
            for t in range(nt):
                copy(t, 1 + j, (x, y, c), (*chip, c), False).wait_recv()
                fwd = copy(t, 4 + j, sibling, (*chip, c), False)
                fwd.start()
                sends.append(fwd)
        for t in range(nt):
            copy(t, 0, (x, y, c), (x, y, 1 - c), False).wait_recv()
            for j, chip in enumerate(chips):
                copy(t, 4 + j, (x, y, c), (*chip, 1 - c), False).wait_recv()
        for cp in sends:
            cp.wait_send()
        for cp in local:
            cp.wait()

    return _pc(
        body, name,
        in_specs=[HBM_SPEC] * nt, out_specs=[HBM_SPEC] * nt,
        out_shape=_gather_shapes(shards), scratch_shapes=_comm_scratch(nt),
        compiler_params=pltpu.CompilerParams(has_side_effects=True),
    )(*shards)


def _ada_fwd(c, w_ada, b_ada_cols):
    ncol = w_ada.shape[1]

    def body(c_ref, w_ref, b_ref, mod_ref, call_ref, cact, cmat, mloc, send1, recv1, send2, recv2):
        x, y, z = _me()
        me = 4 * x + 2 * y + z
        cv = c_ref[...]
        cact[...] = cv * _sig(cv)
        call_ref[me] = cact[...]
        sends = []
        for k in range(1, N_DEV):
            dev, _ = _peer(k)
            cp = _remote(cact, call_ref.at[me], send1.at[k - 1], recv1.at[k - 1], dev)
            cp.start()
            sends.append(cp)
        for k in range(1, N_DEV):
            dev, blk = _peer(k)
            _remote(cact, call_ref.at[blk], send1.at[k - 1], recv1.at[k - 1], dev).wait_recv()
        cmat[...] = jnp.zeros_like(cmat)
        for b in range(N_DEV):
            cmat[b:b + 1, :] = call_ref[b]
        m = jnp.dot(cmat[...].astype(BF16), w_ref[...].astype(BF16), preferred_element_type=F32) + b_ref[...]
        for b in range(N_DEV):
            mloc[b] = m[b:b + 1, :]
        mod_ref[me] = mloc[me]
        for k in range(1, N_DEV):
            dev, blk = _peer(k)
            cp = _remote(mloc.at[blk], mod_ref.at[me], send2.at[k - 1], recv2.at[k - 1], dev)
            cp.start()
            sends.append(cp)
        for k in range(1, N_DEV):
            dev, blk = _peer(k)
            _remote(mloc.at[me], mod_ref.at[blk], send2.at[k - 1], recv2.at[k - 1], dev).wait_recv()
        for cp in sends:
            cp.wait_send()

    return _pc(
        body, "ada_fwd",
        in_specs=[VMEM_SPEC] * 3, out_specs=[VMEM_SPEC] * 2,
        out_shape=[jax.ShapeDtypeStruct((N_DEV, 1, ncol), F32), jax.ShapeDtypeStruct((N_DEV, 1, D_MODEL), F32)],
        scratch_shapes=[pltpu.VMEM((1, D_MODEL), F32), pltpu.VMEM((16, D_MODEL), F32),
                        pltpu.VMEM((N_DEV, 1, ncol), F32)] + [pltpu.SemaphoreType.DMA((N_DEV - 1,))] * 4,
        compiler_params=pltpu.CompilerParams(has_side_effects=True, vmem_limit_bytes=40 * MIB),
    )(c, w_ada, b_ada_cols)


def _small_gather(vec):
    n = vec.shape[1]

    def body(v_ref, all_ref, sum_ref, send, recv):
        x, y, z = _me()
        me = 4 * x + 2 * y + z
        all_ref[me] = v_ref[...]
        sends = []
        for k in range(1, N_DEV):
            dev, _ = _peer(k)
            cp = _remote(v_ref, all_ref.at[me], send.at[k - 1], recv.at[k - 1], dev)
            cp.start()
            sends.append(cp)
        for k in range(1, N_DEV):
            dev, blk = _peer(k)
            _remote(v_ref, all_ref.at[blk], send.at[k - 1], recv.at[k - 1], dev).wait_recv()
        acc = all_ref[0]
        for d in range(1, N_DEV):
            acc = acc + all_ref[d]
        sum_ref[...] = acc
        for cp in sends:
            cp.wait_send()

    return _pc(
        body, "small_gather",
        in_specs=[VMEM_SPEC], out_specs=[VMEM_SPEC] * 2,
        out_shape=[jax.ShapeDtypeStruct((N_DEV, 1, n), F32), jax.ShapeDtypeStruct((1, n), F32)],
        scratch_shapes=[pltpu.SemaphoreType.DMA((N_DEV - 1,))] * 2,
        compiler_params=pltpu.CompilerParams(has_side_effects=True),
    )(vec)


def _local_step(x, target, mod, wt_in, w_out_loc, w_pw_loc, conv_w_loc, rel_bias, sinks, conv_b, conv_ln_g,
                conv_ln_b, b_pw, ln_g, ln_b):
    bmap = _bucket_map()
    bias = _bias_table(rel_bias, bmap)
    (h, qkv, g_attn, glu, g_conv), (w_out, w_pw, conv_w_blocks) = _inproj(
        x, mod, wt_in, (w_out_loc, w_pw_loc, conv_w_loc))
    conv_w = conv_w_blocks.reshape(N_DEV, CONV_ROWS, 128).transpose(1, 0, 2).reshape(CONV_ROWS, D_CONV)
    ycat0, a_out = _attn_fwd(qkv, g_attn, bias, sinks)
    ycat, u1, u3, p_out = _conv_fwd(glu, g_conv, ycat0, conv_w, conv_b, conv_ln_g, conv_ln_b, w_pw, b_pw)
    dz, dy, dycat, loss, g_ln_g, g_ln_b, dgate = _outproj_ln(ycat, w_out, x, target, mod, ln_g, ln_b)
    gw_out = _matmul_tn(ycat, dy, None, 0, D_MODEL, "grad_w_out")
    dp, dgc, du1, g_bpw, g_clg, g_clb, g_cb = _conv_bwd_a(dycat, g_conv, p_out, u1, conv_ln_g, conv_ln_b, w_pw)
    gw_pw = _matmul_tn(u3, dp, None, 0, D_CONV, "grad_w_pw")
    dab, g_cw = _conv_bwd_b(du1, glu, conv_w)
    (dqkv, dga, dbias, dsink), (r_out, r_pw) = _attn_bwd(qkv, g_attn, a_out, dycat, bias, sinks,
                                                         scatter=(gw_out, gw_pw))
    g_rb = _relbias_grad(dbias, bmap)
    gwt_in = _matmul_tn(dqkv, h, None, 0, D_IN, "grad_w_in_qkv")
    gwt_in = _matmul_tn(dga, h, gwt_in, 1536, D_IN, "grad_w_in_gattn")
    gwt_in = _matmul_tn(dab, h, gwt_in, 2560, D_IN, "grad_w_in_glu")
    gwt_in = _matmul_tn(dgc, h, gwt_in, 4608, D_IN, "grad_w_in_gconv")
    g_cw_blocks = g_cw.reshape(CONV_ROWS, N_DEV, 128).transpose(1, 0, 2).reshape(N_DEV * CONV_ROWS, 128)
    (grad_x, dshift, dscale), (r_in, r_cw) = _dh_kernel([dqkv, dga, dab, dgc], wt_in, dz, x, mod,
                                                        scatter=(gwt_in, g_cw_blocks))
    dmod = jnp.concatenate([dshift, dscale, dgate], axis=1)
    small = dict(dmod=dmod, b_pw=g_bpw, conv_ln_g=g_clg, conv_ln_b=g_clb, conv_b=g_cb, ln_g=g_ln_g, ln_b=g_ln_b,
                 rel_bias=g_rb[:, :N_BUCKETS].reshape(1, N_BUCKETS * N_Q_HEADS),
                 sinks=dsink[0:1, :], loss=loss[0:1, :])
    return grad_x, r_in, r_out, r_pw, r_cw, small


SMALL_FIELDS = (("dmod", 3 * D_MODEL), ("b_pw", D_CONV), ("conv_ln_g", D_CONV), ("conv_ln_b", D_CONV),
                ("conv_b", D_CONV), ("ln_g", D_MODEL), ("ln_b", D_MODEL), ("rel_bias", N_BUCKETS * N_Q_HEADS),
                ("sinks", 128), ("loss", 128))


def _pack(fields):
    parts = []
    for name, width in SMALL_FIELDS:
        v = fields[name].reshape(1, -1).astype(F32)
        if v.shape[1] < width:
            v = jnp.pad(v, ((0, 0), (0, width - v.shape[1])))
        parts.append(v)
    return jnp.concatenate(parts, axis=1)


def _unpack(vec):
    out, off = {}, 0
    for name, width in SMALL_FIELDS:
        out[name] = vec[:, off:off + width]
        off += width
    return out


def kernel(x, c, w_ada, b_ada, w_in, rel_bias, sinks, conv_w, conv_b, conv_ln_g, conv_ln_b, w_pw, b_pw, w_out, ln_g, ln_b, loss_target, m_w_ada, m_b_ada, m_w_in, m_rel_bias, m_sinks, m_conv_w, m_conv_b, m_conv_ln_g, m_conv_ln_b, m_w_pw, m_b_pw, m_w_out, m_ln_g, m_ln_b, v_w_ada, v_b_ada, v_w_in, v_rel_bias, v_sinks, v_conv_w, v_conv_b, v_conv_ln_g, v_conv_ln_b, v_w_pw, v_b_pw, v_w_out, v_ln_g, v_ln_b):
    xi, yi, ci = _me()
    me = 4 * xi + 2 * yi + ci
    ncol = w_ada.shape[2]

    wt_in_loc = w_in[0].T.astype(BF16)
    conv_w_loc = jnp.pad(conv_w[0], ((0, CONV_ROWS - CONV_WIDTH), (0, 0)))
    (wt_in_full,) = _all_gather_two_level([wt_in_loc], "gather_w_in")

    b_ada_cols = lax.dynamic_slice(b_ada, (0, me * ncol), (1, ncol))
    mod_blocks, c_all = _ada_fwd(c, w_ada[0], b_ada_cols)
    mod = mod_blocks.reshape(1, 3 * D_MODEL)

    grad_x, r_in, r_out, r_pw, r_cw, small = _local_step(
        x[0], loss_target[0], mod, wt_in_full, w_out[0].astype(BF16), w_pw[0].astype(BF16), conv_w_loc, rel_bias,
        sinks, conv_b, conv_ln_g, conv_ln_b, b_pw, ln_g, ln_b)

    gathered, summed = _small_gather(_pack(small))
    tot = _unpack(summed)
    dmod_all = gathered[:, 0, :3 * D_MODEL]
    loss = tot["loss"][0, 0]

    ct = jnp.zeros((D_MODEL, 128), BF16).at[:, :N_DEV].set(c_all[:, 0, :].T.astype(BF16))
    dm = jnp.zeros((128, ncol), BF16).at[:N_DEV, :].set(
        lax.dynamic_slice(dmod_all, (0, me * ncol), (N_DEV, ncol)).astype(BF16))
    g_w_ada = _small_mm(ct, dm, "grad_w_ada")

    g_w_in = _sum8(r_in.reshape(N_DEV, D_IN // N_DEV, D_MODEL), "sum_w_in").T
    g_w_out = _sum8(r_out.reshape(N_DEV, D_MODEL // N_DEV, D_MODEL), "sum_w_out")
    g_w_pw = _sum8(r_pw.reshape(N_DEV, D_CONV // N_DEV, D_CONV), "sum_w_pw")
    g_conv_w = _sum8(r_cw.reshape(N_DEV, CONV_ROWS, 128), "sum_conv_w")[:CONV_WIDTH]

    grads = {"w_ada": g_w_ada, "w_in": g_w_in, "conv_w": g_conv_w, "w_pw": g_w_pw, "w_out": g_w_out}
    params = {"w_ada": (w_ada, m_w_ada, v_w_ada), "w_in": (w_in, m_w_in, v_w_in),
              "conv_w": (conv_w, m_conv_w, v_conv_w), "w_pw": (w_pw, m_w_pw, v_w_pw),
              "w_out": (w_out, m_w_out, v_w_out)}
    res = {}
    for name, g in grads.items():
        w_, m_, v_ = params[name]
        d_, nm_, nv_ = _adamw(w_[0], g, m_[0], v_[0], "adamw_" + name)
        res[name] = (g[None], d_[None], nm_[None], nv_[None])

    small_params = {"b_ada": (b_ada, m_b_ada, v_b_ada), "b_pw": (b_pw, m_b_pw, v_b_pw),
                    "conv_ln_g": (conv_ln_g, m_conv_ln_g, v_conv_ln_g),
                    "conv_ln_b": (conv_ln_b, m_conv_ln_b, v_conv_ln_b), "conv_b": (conv_b, m_conv_b, v_conv_b),
                    "ln_g": (ln_g, m_ln_g, v_ln_g), "ln_b": (ln_b, m_ln_b, v_ln_b),
                    "rel_bias": (rel_bias, m_rel_bias, v_rel_bias), "sinks": (sinks, m_sinks, v_sinks)}
    to_small = lambda n, a: a.T if n == "rel_bias" else a
    key_of = {"b_ada": "dmod"}
    packs = []
    for j in range(3):
        fields = {key_of.get(n, n): to_small(n, t[j]) for n, t in small_params.items()}
        fields["loss"] = jnp.zeros((1, 1), F32)
        packs.append(_pack(fields))
    gsum = summed
    d_s, nm_s, nv_s = _adamw(packs[0], gsum, packs[1], packs[2], "adamw_small")
    outs_small = [_unpack(a) for a in (gsum, d_s, nm_s, nv_s)]
    for n, t in small_params.items():
        shape = t[0].shape
        vals = []
        for o in outs_small:
            a = o[key_of.get(n, n)]
            if n == "rel_bias":
                a = a.reshape(N_Q_HEADS, N_BUCKETS).T
            else:
                a = a[:, :shape[1]].reshape(shape)
            vals.append(a)
        res[n] = tuple(vals)

    order = ["w_ada", "b_ada", "w_in", "rel_bias", "sinks", "conv_w", "conv_b", "conv_ln_g", "conv_ln_b", "w_pw",
             "b_pw", "w_out", "ln_g", "ln_b"]
    out = [loss, grad_x[None]]
    for j in range(4):
        out += [res[n][j] for n in order]
    return tuple(out)
```

```python
import functools
import math

import jax
import jax.numpy as jnp
import numpy as np
from jax import lax
from jax.experimental import pallas as pl
from jax.experimental.pallas import tpu as pltpu

F32, BF16, I32 = jnp.float32, jnp.bfloat16, jnp.int32

D_MODEL = 2048
D_ATTN = 1024
D_CONV = 1024
D_KV = 256
HEAD_DIM = 64
N_Q_HEADS = 16
N_KV_HEADS = 4
GQA = 4
BLOCK = 128
CONV_WIDTH = 31
CONV_ROWS = 32
HALO = 32
N_BUCKETS = 32
MAX_DISTANCE = 128
LN_EPS = 1e-5
ALPHA = 2.0 ** 0.25
D_IN = 5632
N_DEV = 8
NEG = -1e30

ADAM_LR, ADAM_B1, ADAM_B2, ADAM_EPS, ADAM_WD, ADAM_STEP = 0.001, 0.9, 0.999, 1e-08, 0.01, 10

NT_DIMS = (((1,), (1,)), ((), ()))
TN_DIMS = (((0,), (0,)), ((), ()))
MIB = 1 << 20


def _pc(body, name, **kw):
    return pl.pallas_call(body, name=name, **kw)


def _cp(vmem_mib=None, sem=None):
    kw = {}
    if vmem_mib is not None:
        kw["vmem_limit_bytes"] = vmem_mib * MIB
    if sem is not None:
        kw["dimension_semantics"] = sem
    return pltpu.CompilerParams(**kw)


def _sig(x):
    return 1.0 / (1.0 + jnp.exp(-x))


def _dsilu(x, s):
    return s * (1.0 + x * (1.0 - s))


def _me():
    return lax.axis_index("x"), lax.axis_index("y"), lax.axis_index("c")


def _peer(k):
    x, y, c = _me()
    px = 1 - x if k & 4 else x
    py = 1 - y if k & 2 else y
    pc = 1 - c if k & 1 else c
    return (px, py, pc), 4 * px + 2 * py + pc


def _remote(src, dst, send_sem, recv_sem, dev):
    return pltpu.make_async_remote_copy(src_ref=src, dst_ref=dst, send_sem=send_sem, recv_sem=recv_sem,
                                        device_id=dev, device_id_type=pl.DeviceIdType.MESH)


HBM_SPEC = pl.BlockSpec(memory_space=pl.ANY)
VMEM_SPEC = pl.BlockSpec(memory_space=pltpu.VMEM)


def _comm_scratch(nt):
    return [pltpu.SemaphoreType.DMA((nt, N_DEV - 1)), pltpu.SemaphoreType.DMA((nt, N_DEV - 1)),
            pltpu.SemaphoreType.DMA((nt,))]


def _gather_shapes(shards):
    return [jax.ShapeDtypeStruct((N_DEV * s.shape[0], s.shape[1]), s.dtype) for s in shards]


def _block(ref, blk, rows):
    return ref.at[pl.ds(pl.multiple_of(blk * rows, 8), rows), :]


def _gather_copies(src, dst, sems):
    send_sems, recv_sems, local_sems = sems
    x, y, c = _me()
    me = 4 * x + 2 * y + c
    nt = len(src)
    local = [pltpu.make_async_copy(src[t], _block(dst[t], me, src[t].shape[0]), local_sems.at[t]) for t in range(nt)]
    sends, arrivals = [], []
    for k in range(1, N_DEV):
        dev, blk = _peer(k)
        for t in range(nt):
            r = src[t].shape[0]
            pair = (send_sems.at[t, k - 1], recv_sems.at[t, k - 1], dev)
            sends.append(_remote(src[t], _block(dst[t], me, r), *pair))
            arrivals.append(_remote(src[t], _block(dst[t], blk, r), *pair))
    return local, sends, arrivals


def _scatter_copies(src, dst, sems):
    send_sems, recv_sems, local_sems = sems
    x, y, c = _me()
    me = 4 * x + 2 * y + c
    nt = len(src)
    rows = [s.shape[0] // N_DEV for s in src]
    local = [pltpu.make_async_copy(_block(src[t], me, rows[t]), _block(dst[t], me, rows[t]), local_sems.at[t])
             for t in range(nt)]
    sends, arrivals = [], []
    for k in range(1, N_DEV):
        dev, blk = _peer(k)
        for t in range(nt):
            pair = (send_sems.at[t, k - 1], recv_sems.at[t, k - 1], dev)
            sends.append(_remote(_block(src[t], blk, rows[t]), _block(dst[t], me, rows[t]), *pair))
            arrivals.append(_remote(_block(src[t], me, rows[t]), _block(dst[t], blk, rows[t]), *pair))
    return local, sends, arrivals


def _comm_start(cps):
    local, sends, _ = cps
    for cp in local + sends:
        cp.start()


def _comm_wait(cps):
    local, sends, arrivals = cps
    for cp in arrivals:
        cp.wait_recv()
    for cp in sends:
        cp.wait_send()
    for cp in local:
        cp.wait()


IN_PIECES = ((0, 1536, BF16), (1536, 1024, F32), (2560, 2048, F32), (4608, 1024, F32))


def _inproj(x, mod, wt, gather):
    S = x.shape[0]
    tm = min(256, S)
    ni = S // tm
    ng = len(gather)
    npc = len(IN_PIECES)

    def body(*refs):
        x_ref, sh_ref, sc_ref, w_ref = refs[:4]
        outs = refs[4 + ng:]
        h_ref, piece_refs = outs[0], outs[1:1 + npc]
        i = pl.program_id(0)
        if ng:
            cps = _gather_copies(refs[4:4 + ng], outs[1 + npc:1 + npc + ng], outs[1 + npc + ng:])

            @pl.when(i == 0)
            def _():
                _comm_start(cps)

        h = (x_ref[...] * (1.0 + sc_ref[...]) + sh_ref[...]).astype(BF16)
        h_ref[...] = h
        for (r0, n, dt), o_ref in zip(IN_PIECES, piece_refs):
            o_ref[...] = lax.dot_general(h, w_ref[r0:r0 + n, :], NT_DIMS, preferred_element_type=F32).astype(dt)
        if ng:
            @pl.when(i == ni - 1)
            def _():
                _comm_wait(cps)

    tile = lambda n: pl.BlockSpec((tm, n), lambda i: (i, 0))
    res = _pc(
        body, "inproj", grid=(ni,),
        in_specs=[tile(D_MODEL),
                  pl.BlockSpec((1, D_MODEL), lambda i: (0, 0)),
                  pl.BlockSpec((1, D_MODEL), lambda i: (0, 1)),
                  pl.BlockSpec((D_IN, D_MODEL), lambda i: (0, 0), pipeline_mode=pl.Buffered(1))] + [HBM_SPEC] * ng,
        out_specs=[tile(D_MODEL)] + [tile(n) for _, n, _ in IN_PIECES] + [HBM_SPEC] * ng,
        out_shape=[jax.ShapeDtypeStruct((S, D_MODEL), BF16)]
        + [jax.ShapeDtypeStruct((S, n), dt) for _, n, dt in IN_PIECES] + _gather_shapes(gather),
        scratch_shapes=_comm_scratch(ng) if ng else [],
        compiler_params=_cp(56, ("arbitrary",)),
    )(x, mod, mod, wt, *gather)
    return res[:1 + npc], res[1 + npc:]


def _matmul_tn(a, b, out_prev, row_off, m_total, name):
    S, M = a.shape
    N = b.shape[1]
    tm, ts = 512, min(2048, S)
    assert row_off % tm == 0 and M % tm == 0
    ob = row_off // tm
    ns = S // ts

    def body(*refs):
        a_ref, b_ref = refs[0], refs[1]
        o_ref, acc = refs[-2], refs[-1]
        s = pl.program_id(1)

        @pl.when(s == 0)
        def _():
            acc[...] = jnp.zeros_like(acc)

        acc[...] += lax.dot_general(a_ref[...], b_ref[...], TN_DIMS, preferred_element_type=F32)

        @pl.when(s == ns - 1)
        def _():
            o_ref[...] = acc[...].astype(BF16)

    in_specs = [pl.BlockSpec((ts, tm), lambda i, s: (s, i)),
                pl.BlockSpec((ts, N), lambda i, s: (s, 0))]
    args = [a, b]
    aliases = {}
    if out_prev is not None:
        in_specs.append(pl.BlockSpec(memory_space=pl.ANY))
        args.append(out_prev)
        aliases = {2: 0}
    return _pc(
        body, name, grid=(M // tm, ns),
        in_specs=in_specs,
        out_specs=pl.BlockSpec((tm, N), lambda i, s: (ob + i, 0)),
        out_shape=jax.ShapeDtypeStruct((m_total, N), BF16),
        scratch_shapes=[pltpu.VMEM((tm, N), F32)],
        input_output_aliases=aliases,
        compiler_params=_cp(48, ("parallel", "arbitrary")),
    )(*args)


def _bucket_map():
    qi = jnp.arange(BLOCK, dtype=I32)[:, None]
    kj = jnp.arange(2 * BLOCK, dtype=I32)[None, :]
    dist = qi + BLOCK - kj
    in_window = (dist >= 0) & (dist < BLOCK)
    d0 = jnp.maximum(dist, 0)
    max_exact = N_BUCKETS // 2
    d = jnp.maximum(d0, 1).astype(F32)
    large = max_exact + (jnp.log(d / max_exact) / math.log(MAX_DISTANCE / max_exact)
                         * (N_BUCKETS - max_exact)).astype(I32)
    large = jnp.minimum(large, N_BUCKETS - 1)
    bucket = jnp.where(d0 < max_exact, d0, large)
    return jnp.where(in_window, bucket, -1).astype(I32)


def _bias_table(rel_bias, bmap):
    def body(rb_ref, bm_ref, o_ref):
        h = pl.program_id(0)
        bm = bm_ref[...]
        acc = jnp.full((BLOCK, 2 * BLOCK), NEG, F32)
        for b in range(N_BUCKETS):
            acc = jnp.where(bm == b, rb_ref[b, h], acc)
        o_ref[0] = acc

    return _pc(
        body, "bias_table", grid=(N_Q_HEADS,),
        in_specs=[pl.BlockSpec(memory_space=pltpu.SMEM),
                  pl.BlockSpec((BLOCK, 2 * BLOCK), lambda h: (0, 0))],
        out_specs=pl.BlockSpec((1, BLOCK, 2 * BLOCK), lambda h: (h, 0, 0)),
        out_shape=jax.ShapeDtypeStruct((N_Q_HEADS, BLOCK, 2 * BLOCK), F32),
    )(rel_bias, bmap)


def _relbias_grad(dbias, bmap):
    def body(db_ref, bm_ref, o_ref):
        bm = bm_ref[...]
        x = db_ref[0]
        lane = lax.broadcasted_iota(I32, (1, 128), 1)
        row = jnp.zeros((1, 128), F32)
        for b in range(N_BUCKETS):
            row = jnp.where(lane == b, jnp.sum(jnp.where(bm == b, x, 0.0)), row)
        o_ref[0] = jnp.broadcast_to(row, (8, 128))

    out = _pc(
        body, "relbias_grad", grid=(N_Q_HEADS,),
        in_specs=[pl.BlockSpec((1, BLOCK, 2 * BLOCK), lambda h: (h, 0, 0)),
                  pl.BlockSpec((BLOCK, 2 * BLOCK), lambda h: (0, 0))],
        out_specs=pl.BlockSpec((1, 8, 128), lambda h: (h, 0, 0)),
        out_shape=jax.ShapeDtypeStruct((N_Q_HEADS, 8, 128), F32),
    )(dbias, bmap)
    return out[:, 0, :]


def _stack_heads(ref, hk):
    return jnp.concatenate([ref[:, pl.ds(256 * hk + 64 * g, 64)] for g in range(GQA)], axis=0)


def _sink_col(sink_ref, hk):
    row = lax.broadcasted_iota(I32, (GQA * BLOCK, 1), 0)
    s = jnp.full((GQA * BLOCK, 1), sink_ref[0, 4 * hk + 3], F32)
    for g in (2, 1, 0):
        s = jnp.where(row < (g + 1) * BLOCK, sink_ref[0, 4 * hk + g], s)
    return s


def _attn_probs(q4, kw, bias_ref, sink_ref, hk, first_mask):
    s = lax.dot_general(q4, kw, NT_DIMS, preferred_element_type=F32) * (HEAD_DIM ** -0.5)
    s = s + bias_ref[4 * hk:4 * hk + 4].reshape(GQA * BLOCK, 2 * BLOCK)
    s = jnp.where(first_mask, s, NEG)
    sink = _sink_col(sink_ref, hk)
    m = jnp.maximum(jnp.max(s, axis=1, keepdims=True), sink)
    e = jnp.exp(s - m)
    es = jnp.exp(sink - m)
    inv = 1.0 / (jnp.sum(e, axis=1, keepdims=True) + es)
    return e * inv, es * inv


def _attn_fwd(qkv, g_attn, bias, sinks):
    S = qkv.shape[0]
    nb = S // BLOCK

    def body(q_ref, kc_ref, kp_ref, vc_ref, vp_ref, g_ref, bias_ref, sink_ref, y_ref, a_ref):
        n = pl.program_id(0)
        kj = lax.broadcasted_iota(I32, (GQA * BLOCK, 2 * BLOCK), 1)
        first_mask = (n > 0) | (kj >= BLOCK)
        for hk in range(N_KV_HEADS):
            q4 = _stack_heads(q_ref, hk)
            ks = pl.ds(64 * hk, 64)
            kw = jnp.concatenate([kp_ref[:, ks], kc_ref[:, ks]], axis=0)
            vw = jnp.concatenate([vp_ref[:, ks], vc_ref[:, ks]], axis=0)
            p, _ = _attn_probs(q4, kw, bias_ref, sink_ref, hk, first_mask)
            o4 = jnp.dot(p.astype(BF16), vw, preferred_element_type=F32)
            for g in range(GQA):
                sl = pl.ds(256 * hk + 64 * g, 64)
                og = o4[BLOCK * g:BLOCK * (g + 1)]
                gg = g_ref[:, sl]
                a_ref[:, sl] = og
                y_ref[:, sl] = (og * (gg * _sig(gg))).astype(BF16)

    prev = lambda n: jnp.maximum(n - 1, 0)
    return _pc(
        body, "attn_fwd", grid=(nb,),
        in_specs=[pl.BlockSpec((BLOCK, D_ATTN), lambda n: (n, 0)),
                  pl.BlockSpec((BLOCK, D_KV), lambda n: (n, 4)),
                  pl.BlockSpec((BLOCK, D_KV), lambda n: (prev(n), 4)),
                  pl.BlockSpec((BLOCK, D_KV), lambda n: (n, 5)),
                  pl.BlockSpec((BLOCK, D_KV), lambda n: (prev(n), 5)),
                  pl.BlockSpec((BLOCK, D_ATTN), lambda n: (n, 0)),
                  pl.BlockSpec((N_Q_HEADS, BLOCK, 2 * BLOCK), lambda n: (0, 0, 0)),
                  pl.BlockSpec(memory_space=pltpu.SMEM)],
        out_specs=[pl.BlockSpec((BLOCK, D_ATTN), lambda n: (n, 0)),
                   pl.BlockSpec((BLOCK, D_ATTN), lambda n: (n, 0))],
        out_shape=[jax.ShapeDtypeStruct((S, 2 * D_ATTN), BF16),
                   jax.ShapeDtypeStruct((S, D_ATTN), F32)],
        compiler_params=_cp(40, ("parallel",)),
    )(qkv, qkv, qkv, qkv, qkv, g_attn, bias, sinks)


def _attn_bwd(qkv, g_attn, a_out, dycat, bias, sinks, scatter=()):
    S = qkv.shape[0]
    nb = S // BLOCK
    R = GQA * BLOCK
    ns = len(scatter)

    def body(*refs):
        (q_ref, kc_ref, kp_ref, vc_ref, vp_ref, g_ref, a_ref, dy_ref, bias_ref, sink_ref) = refs[:10]
        dqkv_ref, dg_ref, dbias_ref, dsink_ref = refs[10 + ns:14 + ns]
        dq_scr, dq_new, dk_scr, dv_scr, dkw_scr, dvw_scr, ds_scr = refs[14 + 2 * ns:21 + 2 * ns]
        n = pl.program_id(0)
        if ns:
            cps = _scatter_copies(refs[10:10 + ns], refs[14 + ns:14 + 2 * ns], refs[21 + 2 * ns:])

            @pl.when(n == 0)
            def _():
                _comm_start(cps)

        @pl.when(n == 0)
        def _():
            dbias_ref[...] = jnp.zeros_like(dbias_ref)
            ds_scr[...] = jnp.zeros_like(ds_scr)
            dq_scr[...] = jnp.zeros_like(dq_scr)
            dk_scr[...] = jnp.zeros_like(dk_scr)
            dv_scr[...] = jnp.zeros_like(dv_scr)

        @pl.when(n < nb)
        def _():
            kj = lax.broadcasted_iota(I32, (R, 2 * BLOCK), 1)
            first_mask = (n > 0) | (kj >= BLOCK)
            for hk in range(N_KV_HEADS):
                q4 = _stack_heads(q_ref, hk)
                ks = pl.ds(64 * hk, 64)
                kw = jnp.concatenate([kp_ref[:, ks], kc_ref[:, ks]], axis=0)
                vw = jnp.concatenate([vp_ref[:, ks], vc_ref[:, ks]], axis=0)
                p, psink = _attn_probs(q4, kw, bias_ref, sink_ref, hk, first_mask)
                da_parts, a_parts = [], []
                for g in range(GQA):
                    sl = pl.ds(256 * hk + 64 * g, 64)
                    gg = g_ref[:, sl]
                    sg = _sig(gg)
                    dyg = dy_ref[:, sl]
                    ag = a_ref[:, sl]
                    da_parts.append(dyg * (gg * sg))
                    a_parts.append(ag)
                    dg_ref[:, sl] = (dyg * ag * _dsilu(gg, sg)).astype(BF16)
                da4 = jnp.concatenate(da_parts, axis=0)
                a4 = jnp.concatenate(a_parts, axis=0)
                delta = jnp.sum(da4 * a4, axis=1, keepdims=True)
                da4b = da4.astype(BF16)
                dp = lax.dot_general(da4b, vw, NT_DIMS, preferred_element_type=F32)
                ds = p * (dp - delta)
                ds_scr[hk] += -psink * delta
                dbias_ref[4 * hk:4 * hk + 4] += ds.reshape(GQA, BLOCK, 2 * BLOCK)
                dsb = ds.astype(BF16)
                dq4 = jnp.dot(dsb, kw, preferred_element_type=F32) * (HEAD_DIM ** -0.5)
                for g in range(GQA):
                    dq_new[:, pl.ds(256 * hk + 64 * g, 64)] = dq4[BLOCK * g:BLOCK * (g + 1)].astype(BF16)
                dkw_scr[:, ks] = lax.dot_general(dsb, q4, TN_DIMS, preferred_element_type=F32) * (HEAD_DIM ** -0.5)
                dvw_scr[:, ks] = lax.dot_general(p.astype(BF16), da4b, TN_DIMS, preferred_element_type=F32)

        @pl.when(n == nb)
        def _():
            dkw_scr[0:BLOCK, :] = jnp.zeros((BLOCK, D_KV), F32)
            dvw_scr[0:BLOCK, :] = jnp.zeros((BLOCK, D_KV), F32)

        dqkv_ref[:, 0:D_ATTN] = dq_scr[...]
        dqkv_ref[:, D_ATTN:D_ATTN + D_KV] = (dk_scr[...] + dkw_scr[0:BLOCK, :]).astype(BF16)
        dqkv_ref[:, D_ATTN + D_KV:D_ATTN + 2 * D_KV] = (dv_scr[...] + dvw_scr[0:BLOCK, :]).astype(BF16)
        dq_scr[...] = dq_new[...]
        dk_scr[...] = dkw_scr[BLOCK:2 * BLOCK, :]
        dv_scr[...] = dvw_scr[BLOCK:2 * BLOCK, :]

        @pl.when(n == nb)
        def _():
            lane = lax.broadcasted_iota(I32, (1, 128), 1)
            row = jnp.zeros((1, 128), F32)
            for hk in range(N_KV_HEADS):
                col = ds_scr[hk]
                for g in range(GQA):
                    row = jnp.where(lane == 4 * hk + g, jnp.sum(col[BLOCK * g:BLOCK * (g + 1)]), row)
            dsink_ref[...] = jnp.broadcast_to(row, (8, 128))
            if ns:
                _comm_wait(cps)

    cur = lambda n: jnp.minimum(n, nb - 1)
    prev = lambda n: jnp.clip(n - 1, 0, nb - 1)
    res = _pc(
        body, "attn_bwd", grid=(nb + 1,),
        in_specs=[pl.BlockSpec((BLOCK, D_ATTN), lambda n: (cur(n), 0)),
                  pl.BlockSpec((BLOCK, D_KV), lambda n: (cur(n), 4)),
                  pl.BlockSpec((BLOCK, D_KV), lambda n: (prev(n), 4)),
                  pl.BlockSpec((BLOCK, D_KV), lambda n: (cur(n), 5)),
                  pl.BlockSpec((BLOCK, D_KV), lambda n: (prev(n), 5)),
                  pl.BlockSpec((BLOCK, D_ATTN), lambda n: (cur(n), 0)),
                  pl.BlockSpec((BLOCK, D_ATTN), lambda n: (cur(n), 0)),
                  pl.BlockSpec((BLOCK, D_ATTN), lambda n: (cur(n), 0)),
                  pl.BlockSpec((N_Q_HEADS, BLOCK, 2 * BLOCK), lambda n: (0, 0, 0)),
                  pl.BlockSpec(memory_space=pltpu.SMEM)] + [HBM_SPEC] * ns,
        out_specs=[pl.BlockSpec((BLOCK, D_ATTN + 2 * D_KV), lambda n: (prev(n), 0)),
                   pl.BlockSpec((BLOCK, D_ATTN), lambda n: (cur(n), 0)),
                   pl.BlockSpec((N_Q_HEADS, BLOCK, 2 * BLOCK), lambda n: (0, 0, 0)),
                   pl.BlockSpec((8, 128), lambda n: (0, 0))] + [HBM_SPEC] * ns,
        out_shape=[jax.ShapeDtypeStruct((S, D_ATTN + 2 * D_KV), BF16),
                   jax.ShapeDtypeStruct((S, D_ATTN), BF16),
                   jax.ShapeDtypeStruct((N_Q_HEADS, BLOCK, 2 * BLOCK), F32),
                   jax.ShapeDtypeStruct((8, 128), F32)] + [jax.ShapeDtypeStruct(f.shape, f.dtype) for f in scatter],
        scratch_shapes=[pltpu.VMEM((BLOCK, D_ATTN), BF16), pltpu.VMEM((BLOCK, D_ATTN), BF16),
                        pltpu.VMEM((BLOCK, D_KV), F32), pltpu.VMEM((BLOCK, D_KV), F32),
                        pltpu.VMEM((2 * BLOCK, D_KV), F32), pltpu.VMEM((2 * BLOCK, D_KV), F32),
                        pltpu.VMEM((N_KV_HEADS, R, 1), F32)] + (_comm_scratch(ns) if ns else []),
        compiler_params=_cp(48, ("arbitrary",)),
    )(qkv, qkv, qkv, qkv, qkv, g_attn, a_out, dycat, bias, sinks, *scatter)
    return res[:4], res[4:]


def _conv_tile(S):
    return min(256, S)


def _shifted(win, s):
    return win if s == 0 else pltpu.roll(win, win.shape[0] - s, axis=0)


def _conv_fwd(glu, g_conv, ycat_in, conv_w, conv_b, ln_g, ln_b, w_pw, b_pw):
    S = glu.shape[0]
    T = _conv_tile(S)
    hb = T // HALO

    def body(a_ref, ah_ref, b_ref, bh_ref, gc_ref, cw_ref, cb_ref, lg_ref, lb_ref, wpw_ref, bpw_ref, ycat_in_ref,
             y_ref, u1_ref, u3_ref, p_ref):
        i = pl.program_id(0)
        u0c = a_ref[...] * _sig(b_ref[...])
        u0h = jnp.where(i > 0, ah_ref[...] * _sig(bh_ref[...]), 0.0)
        win = jnp.concatenate([u0h, u0c], axis=0)
        acc = jnp.zeros((T, D_CONV), F32)
        for s in range(8):
            ws = _shifted(win, s)
            for aa in range(5):
                j = 8 * aa + s - 2
                if 0 <= j < CONV_WIDTH:
                    acc = acc + ws[8 * aa:8 * aa + T] * cw_ref[j:j + 1, :]
        u1 = acc + cb_ref[...]
        u1_ref[...] = u1
        mu = jnp.mean(u1, axis=1, keepdims=True)
        uc = u1 - mu
        rstd = lax.rsqrt(jnp.mean(uc * uc, axis=1, keepdims=True) + LN_EPS)
        u2 = uc * rstd * lg_ref[...] + lb_ref[...]
        u3 = (u2 * _sig(u2)).astype(BF16)
        u3_ref[...] = u3
        p = jnp.dot(u3, wpw_ref[...], preferred_element_type=F32) + bpw_ref[...]
        p_ref[...] = p
        gc = gc_ref[...]
        y_ref[...] = (p * (gc * _sig(gc))).astype(BF16)

    halo = lambda i: jnp.maximum(i * hb - 1, 0)
    vec = pl.BlockSpec((1, D_CONV), lambda i: (0, 0))
    return _pc(
        body, "conv_fwd", grid=(S // T,),
        in_specs=[pl.BlockSpec((T, D_CONV), lambda i: (i, 0)),
                  pl.BlockSpec((HALO, D_CONV), lambda i: (halo(i), 0)),
                  pl.BlockSpec((T, D_CONV), lambda i: (i, 1)),
                  pl.BlockSpec((HALO, D_CONV), lambda i: (halo(i), 1)),
                  pl.BlockSpec((T, D_CONV), lambda i: (i, 0)),
                  pl.BlockSpec((CONV_ROWS, D_CONV), lambda i: (0, 0)),
                  vec, vec, vec,
                  pl.BlockSpec((D_CONV, D_CONV), lambda i: (0, 0)),
                  vec,
                  pl.BlockSpec(memory_space=pl.ANY)],
        out_specs=[pl.BlockSpec((T, D_CONV), lambda i: (i, 1)),
                   pl.BlockSpec((T, D_CONV), lambda i: (i, 0)),
                   pl.BlockSpec((T, D_CONV), lambda i: (i, 0)),
                   pl.BlockSpec((T, D_CONV), lambda i: (i, 0))],
        out_shape=[jax.ShapeDtypeStruct((S, 2 * D_CONV), BF16),
                   jax.ShapeDtypeStruct((S, D_CONV), F32),
                   jax.ShapeDtypeStruct((S, D_CONV), BF16),
                   jax.ShapeDtypeStruct((S, D_CONV), F32)],
        input_output_aliases={11: 0},
        compiler_params=_cp(48, ("parallel",)),
    )(glu, glu, glu, glu, g_conv, conv_w, conv_b, ln_g, ln_b, w_pw, b_pw, ycat_in)


def _conv_taps(win, cw_ref, lanes, rows):
    acc = jnp.zeros((rows, win.shape[1]), F32)
    for s in range(8):
        ws = _shifted(win, s)
        for aa in range(5):
            j = 8 * aa + s - 2
            if 0 <= j < CONV_WIDTH:
                acc = acc + ws[8 * aa:8 * aa + rows] * cw_ref[j:j + 1, lanes]
    return acc


def _mixer_fwd(qkv, g_attn, glu, g_conv, bias, sinks, conv_w, conv_b, ln_g, ln_b, w_pw, b_pw):
    S = qkv.shape[0]
    nb = S // BLOCK
    hb = BLOCK // HALO
    QC = D_CONV // N_KV_HEADS

    def body(q_ref, kc_ref, kp_ref, vc_ref, vp_ref, g_ref, bias_ref, sink_ref,
             a_in, ah_in, b_in, bh_in, gc_ref, cw_ref, cb_ref, lg_ref, lb_ref, wpw_ref, bpw_ref,
             y_ref, a_ref, u1_ref, u3_ref, p_ref, win):
        n = pl.program_id(0)
        kj = lax.broadcasted_iota(I32, (GQA * BLOCK, 2 * BLOCK), 1)
        first_mask = (n > 0) | (kj >= BLOCK)
        win[0:HALO, :] = jnp.where(n > 0, ah_in[...] * _sig(bh_in[...]), 0.0)
        win[HALO:HALO + BLOCK, :] = a_in[...] * _sig(b_in[...])
        for hk in range(N_KV_HEADS):
            q4 = _stack_heads(q_ref, hk)
            ks = pl.ds(64 * hk, 64)
            kw = jnp.concatenate([kp_ref[:, ks], kc_ref[:, ks]], axis=0)
            vw = jnp.concatenate([vp_ref[:, ks], vc_ref[:, ks]], axis=0)
            p, _ = _attn_probs(q4, kw, bias_ref, sink_ref, hk, first_mask)
            o4 = jnp.dot(p.astype(BF16), vw, preferred_element_type=F32)
            for g in range(GQA):
                sl = pl.ds(256 * hk + 64 * g, 64)
                og = o4[BLOCK * g:BLOCK * (g + 1)]
                gg = g_ref[:, sl]
                a_ref[:, sl] = og
                y_ref[:, sl] = (og * (gg * _sig(gg))).astype(BF16)
            lanes = pl.ds(QC * hk, QC)
            u1_ref[:, lanes] = _conv_taps(win[:, lanes], cw_ref, lanes, BLOCK) + cb_ref[:, lanes]
        u1 = u1_ref[...]
        mu = jnp.mean(u1, axis=1, keepdims=True)
        uc = u1 - mu
        rstd = lax.rsqrt(jnp.mean(uc * uc, axis=1, keepdims=True) + LN_EPS)
        u2 = uc * rstd * lg_ref[...] + lb_ref[...]
        u3 = (u2 * _sig(u2)).astype(BF16)
        u3_ref[...] = u3
        pw = jnp.dot(u3, wpw_ref[...], preferred_element_type=F32) + bpw_ref[...]
        p_ref[...] = pw
        gc = gc_ref[...]
        y_ref[:, D_ATTN:] = (pw * (gc * _sig(gc))).astype(BF16)

    prev = lambda n: jnp.maximum(n - 1, 0)
    halo = lambda n: jnp.maximum(n * hb - 1, 0)
    vec = pl.BlockSpec((1, D_CONV), lambda n: (0, 0))
    blk = lambda w, j: pl.BlockSpec((BLOCK, w), lambda n: (n, j))
    return _pc(
        body, "mixer_fwd", grid=(nb,),
        in_specs=[blk(D_ATTN, 0),
                  blk(D_KV, 4), pl.BlockSpec((BLOCK, D_KV), lambda n: (prev(n), 4)),
                  blk(D_KV, 5), pl.BlockSpec((BLOCK, D_KV), lambda n: (prev(n), 5)),
                  blk(D_ATTN, 0),
                  pl.BlockSpec((N_Q_HEADS, BLOCK, 2 * BLOCK), lambda n: (0, 0, 0)),
                  pl.BlockSpec(memory_space=pltpu.SMEM),
                  blk(D_CONV, 0), pl.BlockSpec((HALO, D_CONV), lambda n: (halo(n), 0)),
                  blk(D_CONV, 1), pl.BlockSpec((HALO, D_CONV), lambda n: (halo(n), 1)),
                  blk(D_CONV, 0),
                  pl.BlockSpec((CONV_ROWS, D_CONV), lambda n: (0, 0)),
                  vec, vec, vec,
                  pl.BlockSpec((D_CONV, D_CONV), lambda n: (0, 0)),
                  vec],
        out_specs=[blk(2 * D_ATTN, 0), blk(D_ATTN, 0), blk(D_CONV, 0), blk(D_CONV, 0), blk(D_CONV, 0)],
        out_shape=[jax.ShapeDtypeStruct((S, 2 * D_ATTN), BF16),
                   jax.ShapeDtypeStruct((S, D_ATTN), F32),
                   jax.ShapeDtypeStruct((S, D_CONV), F32),
                   jax.ShapeDtypeStruct((S, D_CONV), BF16),
                   jax.ShapeDtypeStruct((S, D_CONV), F32)],
        scratch_shapes=[pltpu.VMEM((BLOCK + HALO, D_CONV), F32)],
        compiler_params=_cp(48, ("parallel",)),
    )(qkv, qkv, qkv, qkv, qkv, g_attn, bias, sinks, glu, glu, glu, glu, g_conv, conv_w, conv_b, ln_g, ln_b, w_pw,
      b_pw)


def _conv_bwd_a(dycat, g_conv, p_out, u1, ln_g, ln_b, w_pw):
    S = u1.shape[0]
    T = _conv_tile(S)

    def body(dy_ref, gc_ref, p_ref, u1_ref, lg_ref, lb_ref, wpw_ref,
             dp_ref, dgc_ref, du1_ref, gbpw_ref, glg_ref, glb_ref, gcb_ref):
        i = pl.program_id(0)

        @pl.when(i == 0)
        def _():
            for r in (gbpw_ref, glg_ref, glb_ref, gcb_ref):
                r[...] = jnp.zeros_like(r)

        dy = dy_ref[...]
        gc = gc_ref[...]
        sg = _sig(gc)
        dp = dy * (gc * sg)
        dgc_ref[...] = (dy * p_ref[...] * _dsilu(gc, sg)).astype(BF16)
        gbpw_ref[...] += jnp.sum(dp, axis=0, keepdims=True)
        dpb = dp.astype(BF16)
        dp_ref[...] = dpb
        du3 = lax.dot_general(dpb, wpw_ref[...], NT_DIMS, preferred_element_type=F32)
        u1 = u1_ref[...]
        mu = jnp.mean(u1, axis=1, keepdims=True)
        uc = u1 - mu
        rstd = lax.rsqrt(jnp.mean(uc * uc, axis=1, keepdims=True) + LN_EPS)
        uh = uc * rstd
        lg = lg_ref[...]
        u2 = uh * lg + lb_ref[...]
        s2 = _sig(u2)
        du2 = du3 * _dsilu(u2, s2)
        glg_ref[...] += jnp.sum(du2 * uh, axis=0, keepdims=True)
        glb_ref[...] += jnp.sum(du2, axis=0, keepdims=True)
        duh = du2 * lg
        du1 = rstd * (duh - jnp.mean(duh, axis=1, keepdims=True) - uh * jnp.mean(duh * uh, axis=1, keepdims=True))
        du1_ref[...] = du1
        gcb_ref[...] += jnp.sum(du1, axis=0, keepdims=True)

    vec = pl.BlockSpec((1, D_CONV), lambda i: (0, 0))
    tile = pl.BlockSpec((T, D_CONV), lambda i: (i, 0))
    vshape = jax.ShapeDtypeStruct((1, D_CONV), F32)
    return _pc(
        body, "conv_bwd_a", grid=(S // T,),
        in_specs=[pl.BlockSpec((T, D_CONV), lambda i: (i, 1)), tile, tile, tile, vec, vec,
                  pl.BlockSpec((D_CONV, D_CONV), lambda i: (0, 0))],
        out_specs=[tile, tile, tile, vec, vec, vec, vec],
        out_shape=[jax.ShapeDtypeStruct((S, D_CONV), BF16), jax.ShapeDtypeStruct((S, D_CONV), BF16),
                   jax.ShapeDtypeStruct((S, D_CONV), F32), vshape, vshape, vshape, vshape],
        compiler_params=_cp(48, ("arbitrary",)),
    )(dycat, g_conv, p_out, u1, ln_g, ln_b, w_pw)


def _conv_bwd_b(du1, glu, conv_w):
    S = du1.shape[0]
    T = _conv_tile(S)
    hb = T // HALO
    nt = S // T
    last_h = S // HALO - 1

    def body(du_ref, dun_ref, a_ref, b_ref, cw_ref, dab_ref, gw_ref):
        i = pl.program_id(0)

        @pl.when(i == 0)
        def _():
            gw_ref[...] = jnp.zeros_like(gw_ref)

        for lg in range(D_CONV // 128):
            lanes = pl.ds(128 * lg, 128)
            a = a_ref[:, lanes]
            sb = _sig(b_ref[:, lanes])
            u0 = a * sb
            du1n = jnp.where(i < nt - 1, dun_ref[:, lanes], 0.0)
            win2 = jnp.concatenate([du_ref[:, lanes], du1n], axis=0)
            acc = jnp.zeros((T, 128), F32)
            for s in range(8):
                w2 = _shifted(win2, s)
                for aa in range(4):
                    j = CONV_WIDTH - 1 - (8 * aa + s)
                    if 0 <= j < CONV_WIDTH:
                        xo = w2[8 * aa:8 * aa + T]
                        acc = acc + xo * cw_ref[j:j + 1, lanes]
                        gw_ref[j:j + 1, lanes] += jnp.sum(xo * u0, axis=0, keepdims=True)
            dab_ref[:, lanes] = (acc * sb).astype(BF16)
            dab_ref[:, pl.ds(D_CONV + 128 * lg, 128)] = (acc * a * sb * (1.0 - sb)).astype(BF16)

    nxt = lambda i: jnp.minimum((i + 1) * hb, last_h)
    return _pc(
        body, "conv_bwd_b", grid=(nt,),
        in_specs=[pl.BlockSpec((T, D_CONV), lambda i: (i, 0)),
                  pl.BlockSpec((HALO, D_CONV), lambda i: (nxt(i), 0)),
                  pl.BlockSpec((T, D_CONV), lambda i: (i, 0)),
                  pl.BlockSpec((T, D_CONV), lambda i: (i, 1)),
                  pl.BlockSpec((CONV_ROWS, D_CONV), lambda i: (0, 0))],
        out_specs=[pl.BlockSpec((T, 2 * D_CONV), lambda i: (i, 0)),
                   pl.BlockSpec((CONV_ROWS, D_CONV), lambda i: (0, 0))],
        out_shape=[jax.ShapeDtypeStruct((S, 2 * D_CONV), BF16),
                   jax.ShapeDtypeStruct((CONV_ROWS, D_CONV), F32)],
        compiler_params=_cp(48, ("arbitrary",)),
    )(du1, du1, glu, glu, conv_w)


def _outproj_ln(ycat, w_out, x, target, mod, ln_g, ln_b):
    S = x.shape[0]
    tm = min(256, S)

    def body(yc_ref, w_ref, x_ref, t_ref, gate_ref, lg_ref, lb_ref,
             dz_ref, dy_ref, dyc_ref, loss_ref, glg_ref, glb_ref, dgate_ref):
        i = pl.program_id(0)

        @pl.when(i == 0)
        def _():
            for r in (loss_ref, glg_ref, glb_ref, dgate_ref):
                r[...] = jnp.zeros_like(r)

        w = w_ref[...]
        y = jnp.dot(yc_ref[...], w, preferred_element_type=F32)
        gate = gate_ref[...]
        z = ALPHA * x_ref[...] + gate * y
        mu = jnp.mean(z, axis=1, keepdims=True)
        zc = z - mu
        rstd = lax.rsqrt(jnp.mean(zc * zc, axis=1, keepdims=True) + LN_EPS)
        zh = zc * rstd
        lg = lg_ref[...]
        err = zh * lg + lb_ref[...] - t_ref[...]
        loss_ref[...] += 0.5 * jnp.sum(jnp.sum(err * err, axis=1, keepdims=True)) / D_MODEL
        dout = err * (1.0 / D_MODEL)
        glg_ref[...] += jnp.sum(dout * zh, axis=0, keepdims=True)
        glb_ref[...] += jnp.sum(dout, axis=0, keepdims=True)
        dzh = dout * lg
        dz = rstd * (dzh - jnp.mean(dzh, axis=1, keepdims=True) - zh * jnp.mean(dzh * zh, axis=1, keepdims=True))
        dz_ref[...] = dz
        dgate_ref[...] += jnp.sum(dz * y, axis=0, keepdims=True)
        dy = (dz * gate).astype(BF16)
        dy_ref[...] = dy
        dyc_ref[...] = lax.dot_general(dy, w, NT_DIMS, preferred_element_type=F32)

    vec = pl.BlockSpec((1, D_MODEL), lambda i: (0, 0))
    tile = pl.BlockSpec((tm, D_MODEL), lambda i: (i, 0))
    vshape = jax.ShapeDtypeStruct((1, D_MODEL), F32)
    return _pc(
        body, "outproj_ln", grid=(S // tm,),
        in_specs=[tile, pl.BlockSpec((D_MODEL, D_MODEL), lambda i: (0, 0)), tile, tile,
                  pl.BlockSpec((1, D_MODEL), lambda i: (0, 2)), vec, vec],
        out_specs=[tile, tile, tile, pl.BlockSpec((8, 128), lambda i: (0, 0)), vec, vec, vec],
        out_shape=[jax.ShapeDtypeStruct((S, D_MODEL), F32), jax.ShapeDtypeStruct((S, D_MODEL), BF16),
                   jax.ShapeDtypeStruct((S, D_MODEL), F32), jax.ShapeDtypeStruct((8, 128), F32),
                   vshape, vshape, vshape],
        compiler_params=_cp(56, ("arbitrary",)),
    )(ycat, w_out, x, target, mod, ln_g, ln_b)


def _dh_kernel(segs, wt, dz, x, mod, scatter=()):
    S = x.shape[0]
    tm = min(256, S)
    row0 = [0]
    for a in segs:
        row0.append(row0[-1] + a.shape[1])
    assert row0[-1] == wt.shape[0]
    nseg = len(segs)
    ns = len(scatter)
    ni = S // tm

    def body(*refs):
        seg_refs = refs[:nseg]
        w_ref, dz_ref, x_ref, sc_ref = refs[nseg:nseg + 4]
        outs = refs[nseg + 4 + ns:]
        gx_ref, dsh_ref, dsc_ref = outs[:3]
        i = pl.program_id(0)
        if ns:
            cps = _scatter_copies(refs[nseg + 4:nseg + 4 + ns], outs[3:3 + ns], outs[3 + ns:])

        @pl.when(i == 0)
        def _():
            dsh_ref[...] = jnp.zeros_like(dsh_ref)
            dsc_ref[...] = jnp.zeros_like(dsc_ref)
            if ns:
                _comm_start(cps)

        dh = jnp.dot(seg_refs[0][...], w_ref[row0[0]:row0[1], :], preferred_element_type=F32)
        for t in range(1, nseg):
            dh = dh + jnp.dot(seg_refs[t][...], w_ref[row0[t]:row0[t + 1], :], preferred_element_type=F32)
        gx_ref[...] = ALPHA * dz_ref[...] + dh * (1.0 + sc_ref[...])
        dsh_ref[...] += jnp.sum(dh, axis=0, keepdims=True)
        dsc_ref[...] += jnp.sum(dh * x_ref[...], axis=0, keepdims=True)

        if ns:
            @pl.when(i == ni - 1)
            def _():
                _comm_wait(cps)

    tile = pl.BlockSpec((tm, D_MODEL), lambda i: (i, 0))
    vec = pl.BlockSpec((1, D_MODEL), lambda i: (0, 0))
    vshape = jax.ShapeDtypeStruct((1, D_MODEL), F32)
    res = _pc(
        body, "dh_gradx", grid=(ni,),
        in_specs=[pl.BlockSpec((tm, a.shape[1]), lambda i: (i, 0)) for a in segs] + [
            pl.BlockSpec(wt.shape, lambda i: (0, 0), pipeline_mode=pl.Buffered(1)), tile, tile,
            pl.BlockSpec((1, D_MODEL), lambda i: (0, 1))] + [HBM_SPEC] * ns,
        out_specs=[tile, vec, vec] + [HBM_SPEC] * ns,
        out_shape=[jax.ShapeDtypeStruct((S, D_MODEL), F32), vshape, vshape]
        + [jax.ShapeDtypeStruct(f.shape, f.dtype) for f in scatter],
        scratch_shapes=_comm_scratch(ns) if ns else [],
        compiler_params=_cp(58, ("arbitrary",)),
    )(*segs, wt, dz, x, mod, *scatter)
    return res[:3], res[3:]


def _row_tile(rows, cols):
    if rows * cols * 4 <= 2 * MIB or rows % 8:
        return rows
    tr = max(8, (2 * MIB // (cols * 4)) // 8 * 8)
    while rows % tr:
        tr -= 8
    return tr


def _sum8(recv, name):
    _, R, C = recv.shape
    tr = _row_tile(R, C)

    def body(r_ref, o_ref):
        acc = r_ref[0].astype(F32)
        for d in range(1, N_DEV):
            acc = acc + r_ref[d].astype(F32)
        o_ref[...] = acc

    return _pc(
        body, name, grid=(R // tr,),
        in_specs=[pl.BlockSpec((N_DEV, tr, C), lambda i: (0, i, 0))],
        out_specs=pl.BlockSpec((tr, C), lambda i: (i, 0)),
        out_shape=jax.ShapeDtypeStruct((R, C), F32),
        compiler_params=_cp(40, ("parallel",)),
    )(recv)


def _adamw(w, g, m, v, name):
    R, C = w.shape
    tr = _row_tile(R, C)

    def body(w_ref, g_ref, m_ref, v_ref, d_ref, nm_ref, nv_ref):
        g_ = g_ref[...]
        m_ = ADAM_B1 * m_ref[...] + (1.0 - ADAM_B1) * g_
        v_ = ADAM_B2 * v_ref[...] + (1.0 - ADAM_B2) * (g_ * g_)
        m_hat = m_ / (1.0 - ADAM_B1 ** ADAM_STEP)
        v_hat = v_ / (1.0 - ADAM_B2 ** ADAM_STEP)
        d_ref[...] = -ADAM_LR * (m_hat / (jnp.sqrt(v_hat) + ADAM_EPS) + ADAM_WD * w_ref[...])
        nm_ref[...] = m_
        nv_ref[...] = v_

    spec = pl.BlockSpec((tr, C), lambda i: (i, 0))
    shape = jax.ShapeDtypeStruct((R, C), F32)
    return _pc(
        body, name, grid=(R // tr,),
        in_specs=[spec] * 4, out_specs=[spec] * 3, out_shape=[shape] * 3,
        compiler_params=_cp(40, ("parallel",)),
    )(w, g, m, v)


def _small_mm(a, b, name):
    def body(a_ref, b_ref, o_ref):
        o_ref[...] = jnp.dot(a_ref[...], b_ref[...], preferred_element_type=F32)

    return _pc(body, name, out_shape=jax.ShapeDtypeStruct((a.shape[0], b.shape[1]), F32),
               compiler_params=_cp(40))(a, b)


def _all_gather_two_level(shards, name):
    nt = len(shards)

    def body(*refs):
        src, dst = refs[:nt], refs[nt:2 * nt]
        send_sems, recv_sems, local_sems = refs[2 * nt:]
        x, y, c = _me()
        sibling = (x, y, 1 - c)
        chips = [(1 - x, y), (x, 1 - y), (1 - x, 1 - y)]

        def blk(t, px, py, pc):
            return _block(dst[t], 4 * px + 2 * py + pc, src[t].shape[0])

        def copy(t, k, to, block, from_src):
            return _remote(src[t] if from_src else blk(t, *block), blk(t, *block),
                           send_sems.at[t, k], recv_sems.at[t, k], to)

        local = [pltpu.make_async_copy(src[t], blk(t, x, y, c), local_sems.at[t]) for t in range(nt)]
        for cp in local:
            cp.start()
        sends = []
        for t in range(nt):
            sends.append(copy(t, 0, sibling, (x, y, c), True))
            sends += [copy(t, 1 + j, (*chip, c), (x, y, c), True) for j, chip in enumerate(chips)]
        for cp in sends:
            cp.start()
        for j, chip in enumerate(chips):
            for t in range(nt):
                copy(t, 1 + j, (x, y, c), (*chip, c), False).wait_recv()
                fwd = copy(t, 4 + j, sibling, (*chip, c), False)
                fwd.start()
                sends.append(fwd)
        for t in range(nt):
            copy(t, 0, (x, y, c), (x, y, 1 - c), False).wait_recv()
            for j, chip in enumerate(chips):
                copy(t, 4 + j, (x, y, c), (*chip, 1 - c), False).wait_recv()
        for cp in sends:
            cp.wait_send()
        for cp in local:
            cp.wait()

    return _pc(
        body, name,
        in_specs=[HBM_SPEC] * nt, out_specs=[HBM_SPEC] * nt,
        out_shape=_gather_shapes(shards), scratch_shapes=_comm_scratch(nt),
        compiler_params=pltpu.CompilerParams(has_side_effects=True),
    )(*shards)


def _ada_fwd(c, w_ada, b_ada_cols):
    ncol = w_ada.shape[1]

    def body(c_ref, w_ref, b_ref, mod_ref, call_ref, cact, cmat, mloc, send1, recv1, send2, recv2):
        x, y, z = _me()
        me = 4 * x + 2 * y + z
        cv = c_ref[...]
        cact[...] = cv * _sig(cv)
        call_ref[me] = cact[...]
        sends = []
        for k in range(1, N_DEV):
            dev, _ = _peer(k)
            cp = _remote(cact, call_ref.at[me], send1.at[k - 1], recv1.at[k - 1], dev)
            cp.start()
            sends.append(cp)
        for k in range(1, N_DEV):
            dev, blk = _peer(k)
            _remote(cact, call_ref.at[blk], send1.at[k - 1], recv1.at[k - 1], dev).wait_recv()
        cmat[...] = jnp.zeros_like(cmat)
        for b in range(N_DEV):
            cmat[b:b + 1, :] = call_ref[b]
        m = jnp.dot(cmat[...].astype(BF16), w_ref[...].astype(BF16), preferred_element_type=F32) + b_ref[...]
        for b in range(N_DEV):
            mloc[b] = m[b:b + 1, :]
        mod_ref[me] = mloc[me]
        for k in range(1, N_DEV):
            dev, blk = _peer(k)
            cp = _remote(mloc.at[blk], mod_ref.at[me], send2.at[k - 1], recv2.at[k - 1], dev)
            cp.start()
            sends.append(cp)
        for k in range(1, N_DEV):
            dev, blk = _peer(k)
            _remote(mloc.at[me], mod_ref.at[blk], send2.at[k - 1], recv2.at[k - 1], dev).wait_recv()
        for cp in sends:
            cp.wait_send()

    return _pc(
        body, "ada_fwd",
        in_specs=[VMEM_SPEC] * 3, out_specs=[VMEM_SPEC] * 2,
        out_shape=[jax.ShapeDtypeStruct((N_DEV, 1, ncol), F32), jax.ShapeDtypeStruct((N_DEV, 1, D_MODEL), F32)],
        scratch_shapes=[pltpu.VMEM((1, D_MODEL), F32), pltpu.VMEM((16, D_MODEL), F32),
                        pltpu.VMEM((N_DEV, 1, ncol), F32)] + [pltpu.SemaphoreType.DMA((N_DEV - 1,))] * 4,
        compiler_params=pltpu.CompilerParams(has_side_effects=True, vmem_limit_bytes=40 * MIB),
    )(c, w_ada, b_ada_cols)


def _small_gather(vec):
    n = vec.shape[1]

    def body(v_ref, all_ref, sum_ref, send, recv):
        x, y, z = _me()
        me = 4 * x + 2 * y + z
        all_ref[me] = v_ref[...]
        sends = []
        for k in range(1, N_DEV):
            dev, _ = _peer(k)
            cp = _remote(v_ref, all_ref.at[me], send.at[k - 1], recv.at[k - 1], dev)
            cp.start()
            sends.append(cp)
        for k in range(1, N_DEV):
            dev, blk = _peer(k)
            _remote(v_ref, all_ref.at[blk], send.at[k - 1], recv.at[k - 1], dev).wait_recv()
        acc = all_ref[0]
        for d in range(1, N_DEV):
            acc = acc + all_ref[d]
        sum_ref[...] = acc
        for cp in sends:
            cp.wait_send()

    return _pc(
        body, "small_gather",
        in_specs=[VMEM_SPEC], out_specs=[VMEM_SPEC] * 2,
        out_shape=[jax.ShapeDtypeStruct((N_DEV, 1, n), F32), jax.ShapeDtypeStruct((1, n), F32)],
        scratch_shapes=[pltpu.SemaphoreType.DMA((N_DEV - 1,))] * 2,
        compiler_params=pltpu.CompilerParams(has_side_effects=True),
    )(vec)


def _local_step(x, target, mod, wt_in, w_out_loc, w_pw_loc, conv_w_loc, rel_bias, sinks, conv_b, conv_ln_g,
                conv_ln_b, b_pw, ln_g, ln_b):
    bmap = _bucket_map()
    bias = _bias_table(rel_bias, bmap)
    (h, qkv, g_attn, glu, g_conv), (w_out, w_pw, conv_w_blocks) = _inproj(
        x, mod, wt_in, (w_out_loc, w_pw_loc, conv_w_loc))
    conv_w = conv_w_blocks.reshape(N_DEV, CONV_ROWS, 128).transpose(1, 0, 2).reshape(CONV_ROWS, D_CONV)
    ycat, a_out, u1, u3, p_out = _mixer_fwd(qkv, g_attn, glu, g_conv, bias, sinks, conv_w, conv_b, conv_ln_g,
                                            conv_ln_b, w_pw, b_pw)
    dz, dy, dycat, loss, g_ln_g, g_ln_b, dgate = _outproj_ln(ycat, w_out, x, target, mod, ln_g, ln_b)
    gw_out = _matmul_tn(ycat, dy, None, 0, D_MODEL, "grad_w_out")
    dp, dgc, du1, g_bpw, g_clg, g_clb, g_cb = _conv_bwd_a(dycat, g_conv, p_out, u1, conv_ln_g, conv_ln_b, w_pw)
    gw_pw = _matmul_tn(u3, dp, None, 0, D_CONV, "grad_w_pw")
    dab, g_cw = _conv_bwd_b(du1, glu, conv_w)
    (dqkv, dga, dbias, dsink), (r_out, r_pw) = _attn_bwd(qkv, g_attn, a_out, dycat, bias, sinks,
                                                         scatter=(gw_out, gw_pw))
    g_rb = _relbias_grad(dbias, bmap)
    gwt_in = _matmul_tn(dqkv, h, None, 0, D_IN, "grad_w_in_qkv")
    gwt_in = _matmul_tn(dga, h, gwt_in, 1536, D_IN, "grad_w_in_gattn")
    gwt_in = _matmul_tn(dab, h, gwt_in, 2560, D_IN, "grad_w_in_glu")
    gwt_in = _matmul_tn(dgc, h, gwt_in, 4608, D_IN, "grad_w_in_gconv")
    g_cw_blocks = g_cw.reshape(CONV_ROWS, N_DEV, 128).transpose(1, 0, 2).reshape(N_DEV * CONV_ROWS, 128)
    (grad_x, dshift, dscale), (r_in, r_cw) = _dh_kernel([dqkv, dga, dab, dgc], wt_in, dz, x, mod,
                                                        scatter=(gwt_in, g_cw_blocks))
    dmod = jnp.concatenate([dshift, dscale, dgate], axis=1)
    small = dict(dmod=dmod, b_pw=g_bpw, conv_ln_g=g_clg, conv_ln_b=g_clb, conv_b=g_cb, ln_g=g_ln_g, ln_b=g_ln_b,
                 rel_bias=g_rb[:, :N_BUCKETS].reshape(1, N_BUCKETS * N_Q_HEADS),
                 sinks=dsink[0:1, :], loss=loss[0:1, :])
    return grad_x, r_in, r_out, r_pw, r_cw, small


SMALL_FIELDS = (("dmod", 3 * D_MODEL), ("b_pw", D_CONV), ("conv_ln_g", D_CONV), ("conv_ln_b", D_CONV),
                ("conv_b", D_CONV), ("ln_g", D_MODEL), ("ln_b", D_MODEL), ("rel_bias", N_BUCKETS * N_Q_HEADS),
                ("sinks", 128), ("loss", 128))


def _pack(fields):
    parts = []
    for name, width in SMALL_FIELDS:
        v = fields[name].reshape(1, -1).astype(F32)
        if v.shape[1] < width:
            v = jnp.pad(v, ((0, 0), (0, width - v.shape[1])))
        parts.append(v)
    return jnp.concatenate(parts, axis=1)


def _unpack(vec):
    out, off = {}, 0
    for name, width in SMALL_FIELDS:
        out[name] = vec[:, off:off + width]
        off += width
    return out


def kernel(x, c, w_ada, b_ada, w_in, rel_bias, sinks, conv_w, conv_b, conv_ln_g, conv_ln_b, w_pw, b_pw, w_out, ln_g, ln_b, loss_target, m_w_ada, m_b_ada, m_w_in, m_rel_bias, m_sinks, m_conv_w, m_conv_b, m_conv_ln_g, m_conv_ln_b, m_w_pw, m_b_pw, m_w_out, m_ln_g, m_ln_b, v_w_ada, v_b_ada, v_w_in, v_rel_bias, v_sinks, v_conv_w, v_conv_b, v_conv_ln_g, v_conv_ln_b, v_w_pw, v_b_pw, v_w_out, v_ln_g, v_ln_b):
    xi, yi, ci = _me()
    me = 4 * xi + 2 * yi + ci
    ncol = w_ada.shape[2]

    wt_in_loc = w_in[0].T.astype(BF16)
    conv_w_loc = jnp.pad(conv_w[0], ((0, CONV_ROWS - CONV_WIDTH), (0, 0)))
    (wt_in_full,) = _all_gather_two_level([wt_in_loc], "gather_w_in")

    b_ada_cols = lax.dynamic_slice(b_ada, (0, me * ncol), (1, ncol))
    mod_blocks, c_all = _ada_fwd(c, w_ada[0], b_ada_cols)
    mod = mod_blocks.reshape(1, 3 * D_MODEL)

    grad_x, r_in, r_out, r_pw, r_cw, small = _local_step(
        x[0], loss_target[0], mod, wt_in_full, w_out[0].astype(BF16), w_pw[0].astype(BF16), conv_w_loc, rel_bias,
        sinks, conv_b, conv_ln_g, conv_ln_b, b_pw, ln_g, ln_b)

    gathered, summed = _small_gather(_pack(small))
    tot = _unpack(summed)
    dmod_all = gathered[:, 0, :3 * D_MODEL]
    loss = tot["loss"][0, 0]

    ct = jnp.zeros((D_MODEL, 128), BF16).at[:, :N_DEV].set(c_all[:, 0, :].T.astype(BF16))
    dm = jnp.zeros((128, ncol), BF16).at[:N_DEV, :].set(
        lax.dynamic_slice(dmod_all, (0, me * ncol), (N_DEV, ncol)).astype(BF16))
    g_w_ada = _small_mm(ct, dm, "grad_w_ada")

    g_w_in = _sum8(r_in.reshape(N_DEV, D_IN // N_DEV, D_MODEL), "sum_w_in").T
    g_w_out = _sum8(r_out.reshape(N_DEV, D_MODEL // N_DEV, D_MODEL), "sum_w_out")
    g_w_pw = _sum8(r_pw.reshape(N_DEV, D_CONV // N_DEV, D_CONV), "sum_w_pw")
    g_conv_w = _sum8(r_cw.reshape(N_DEV, CONV_ROWS, 128), "sum_conv_w")[:CONV_WIDTH]

    grads = {"w_ada": g_w_ada, "w_in": g_w_in, "conv_w": g_conv_w, "w_pw": g_w_pw, "w_out": g_w_out}
    params = {"w_ada": (w_ada, m_w_ada, v_w_ada), "w_in": (w_in, m_w_in, v_w_in),
              "conv_w": (conv_w, m_conv_w, v_conv_w), "w_pw": (w_pw, m_w_pw, v_w_pw),
              "w_out": (w_out, m_w_out, v_w_out)}
    res = {}
    for name, g in grads.items():
        w_, m_, v_ = params[name]
        d_, nm_, nv_ = _adamw(w_[0], g, m_[0], v_[0], "adamw_" + name)
        res[name] = (g[None], d_[None], nm_[None], nv_[None])

    small_params = {"b_ada": (b_ada, m_b_ada, v_b_ada), "b_pw": (b_pw, m_b_pw, v_b_pw),
                    "conv_ln_g": (conv_ln_g, m_conv_ln_g, v_conv_ln_g),
                    "conv_ln_b": (conv_ln_b, m_conv_ln_b, v_conv_ln_b), "conv_b": (conv_b, m_conv_b, v_conv_b),
                    "ln_g": (ln_g, m_ln_g, v_ln_g), "ln_b": (ln_b, m_ln_b, v_ln_b),
                    "rel_bias": (rel_bias, m_rel_bias, v_rel_bias), "sinks": (sinks, m_sinks, v_sinks)}
    to_small = lambda n, a: a.T if n == "rel_bias" else a
    key_of = {"b_ada": "dmod"}
    packs = []
    for j in range(3):
        fields = {key_of.get(n, n): to_small(n, t[j]) for n, t in small_params.items()}
        fields["loss"] = jnp.zeros((1, 1), F32)
        packs.append(_pack(fields))
    gsum = summed
    d_s, nm_s, nv_s = _adamw(packs[0], gsum, packs[1], packs[2], "adamw_small")
    outs_small = [_unpack(a) for a in (gsum, d_s, nm_s, nv_s)]
    for n, t in small_params.items():
        shape = t[0].shape
        vals = []
        for o in outs_small:
            a = o[key_of.get(n, n)]
            if n == "rel_bias":
                a = a.reshape(N_Q_HEADS, N_BUCKETS).T
            else:
                a = a[:, :shape[1]].reshape(shape)
            vals.append(a)
        res[n] = tuple(vals)

    order = ["w_ada", "b_ada", "w_in", "rel_bias", "sinks", "conv_w", "conv_b", "conv_ln_g", "conv_ln_b", "w_pw",
             "b_pw", "w_out", "ln_g", "ln_b"]
    out = [loss, grad_x[None]]
    for j in range(4):
        out += [res[n][j] for n in order]
    return tuple(out)
```

```python
import functools
import math

import jax
import jax.numpy as jnp
import numpy as np
from jax import lax
from jax.experimental import pallas as pl
from jax.experimental.pallas import tpu as pltpu

F32, BF16, I32 = jnp.float32, jnp.bfloat16, jnp.int32

D_MODEL = 2048
D_ATTN = 1024
D_CONV = 1024
D_KV = 256
HEAD_DIM = 64
N_Q_HEADS = 16
N_KV_HEADS = 4
GQA = 4
BLOCK = 128
CONV_WIDTH = 31
CONV_ROWS = 32
HALO = 32
N_BUCKETS = 32
MAX_DISTANCE = 128
LN_EPS = 1e-5
ALPHA = 2.0 ** 0.25
D_IN = 5632
N_DEV = 8
NEG = -1e30

ADAM_LR, ADAM_B1, ADAM_B2, ADAM_EPS, ADAM_WD, ADAM_STEP = 0.001, 0.9, 0.999, 1e-08, 0.01, 10

NT_DIMS = (((1,), (1,)), ((), ()))
TN_DIMS = (((0,), (0,)), ((), ()))
MIB = 1 << 20


def _pc(body, name, **kw):
    return pl.pallas_call(body, name=name, **kw)


def _cp(vmem_mib=None, sem=None):
    kw = {}
    if vmem_mib is not None:
        kw["vmem_limit_bytes"] = vmem_mib * MIB
    if sem is not None:
        kw["dimension_semantics"] = sem
    return pltpu.CompilerParams(**kw)


def _sig(x):
    return 1.0 / (1.0 + jnp.exp(-x))


def _dsilu(x, s):
    return s * (1.0 + x * (1.0 - s))


def _me():
    return lax.axis_index("x"), lax.axis_index("y"), lax.axis_index("c")


def _peer(k):
    x, y, c = _me()
    px = 1 - x if k & 4 else x
    py = 1 - y if k & 2 else y
    pc = 1 - c if k & 1 else c
    return (px, py, pc), 4 * px + 2 * py + pc


def _remote(src, dst, send_sem, recv_sem, dev):
    return pltpu.make_async_remote_copy(src_ref=src, dst_ref=dst, send_sem=send_sem, recv_sem=recv_sem,
                                        device_id=dev, device_id_type=pl.DeviceIdType.MESH)


HBM_SPEC = pl.BlockSpec(memory_space=pl.ANY)
VMEM_SPEC = pl.BlockSpec(memory_space=pltpu.VMEM)


def _comm_scratch(nt):
    return [pltpu.SemaphoreType.DMA((nt, N_DEV - 1)), pltpu.SemaphoreType.DMA((nt, N_DEV - 1)),
            pltpu.SemaphoreType.DMA((nt,))]


def _gather_shapes(shards):
    return [jax.ShapeDtypeStruct((N_DEV * s.shape[0], s.shape[1]), s.dtype) for s in shards]


def _block(ref, blk, rows):
    return ref.at[pl.ds(pl.multiple_of(blk * rows, 8), rows), :]


def _gather_copies(src, dst, sems):
    send_sems, recv_sems, local_sems = sems
    x, y, c = _me()
    me = 4 * x + 2 * y + c
    nt = len(src)
    local = [pltpu.make_async_copy(src[t], _block(dst[t], me, src[t].shape[0]), local_sems.at[t]) for t in range(nt)]
    sends, arrivals = [], []
    for k in range(1, N_DEV):
        dev, blk = _peer(k)
        for t in range(nt):
            r = src[t].shape[0]
            pair = (send_sems.at[t, k - 1], recv_sems.at[t, k - 1], dev)
            sends.append(_remote(src[t], _block(dst[t], me, r), *pair))
            arrivals.append(_remote(src[t], _block(dst[t], blk, r), *pair))
    return local, sends, arrivals


def _scatter_copies(src, dst, sems):
    send_sems, recv_sems, local_sems = sems
    x, y, c = _me()
    me = 4 * x + 2 * y + c
    nt = len(src)
    rows = [s.shape[0] // N_DEV for s in src]
    local = [pltpu.make_async_copy(_block(src[t], me, rows[t]), _block(dst[t], me, rows[t]), local_sems.at[t])
             for t in range(nt)]
    sends, arrivals = [], []
    for k in range(1, N_DEV):
        dev, blk = _peer(k)
        for t in range(nt):
            pair = (send_sems.at[t, k - 1], recv_sems.at[t, k - 1], dev)
            sends.append(_remote(_block(src[t], blk, rows[t]), _block(dst[t], me, rows[t]), *pair))
            arrivals.append(_remote(_block(src[t], me, rows[t]), _block(dst[t], blk, rows[t]), *pair))
    return local, sends, arrivals


def _comm_start(cps):
    local, sends, _ = cps
    for cp in local + sends:
        cp.start()


def _comm_wait(cps):
    local, sends, arrivals = cps
    for cp in arrivals:
        cp.wait_recv()
    for cp in sends:
        cp.wait_send()
    for cp in local:
        cp.wait()


IN_PIECES = ((0, 1536, BF16), (1536, 1024, F32), (2560, 2048, F32), (4608, 1024, F32))


def _inproj(x, mod, wt, gather):
    S = x.shape[0]
    tm = min(256, S)
    ni = S // tm
    ng = len(gather)
    npc = len(IN_PIECES)

    def body(*refs):
        x_ref, sh_ref, sc_ref, w_ref = refs[:4]
        outs = refs[4 + ng:]
        h_ref, piece_refs = outs[0], outs[1:1 + npc]
        i = pl.program_id(0)
        if ng:
            cps = _gather_copies(refs[4:4 + ng], outs[1 + npc:1 + npc + ng], outs[1 + npc + ng:])

            @pl.when(i == 0)
            def _():
                _comm_start(cps)

        h = (x_ref[...] * (1.0 + sc_ref[...]) + sh_ref[...]).astype(BF16)
        h_ref[...] = h
        for (r0, n, dt), o_ref in zip(IN_PIECES, piece_refs):
            o_ref[...] = lax.dot_general(h, w_ref[r0:r0 + n, :], NT_DIMS, preferred_element_type=F32).astype(dt)
        if ng:
            @pl.when(i == ni - 1)
            def _():
                _comm_wait(cps)

    tile = lambda n: pl.BlockSpec((tm, n), lambda i: (i, 0))
    res = _pc(
        body, "inproj", grid=(ni,),
        in_specs=[tile(D_MODEL),
                  pl.BlockSpec((1, D_MODEL), lambda i: (0, 0)),
                  pl.BlockSpec((1, D_MODEL), lambda i: (0, 1)),
                  pl.BlockSpec((D_IN, D_MODEL), lambda i: (0, 0), pipeline_mode=pl.Buffered(1))] + [HBM_SPEC] * ng,
        out_specs=[tile(D_MODEL)] + [tile(n) for _, n, _ in IN_PIECES] + [HBM_SPEC] * ng,
        out_shape=[jax.ShapeDtypeStruct((S, D_MODEL), BF16)]
        + [jax.ShapeDtypeStruct((S, n), dt) for _, n, dt in IN_PIECES] + _gather_shapes(gather),
        scratch_shapes=_comm_scratch(ng) if ng else [],
        compiler_params=_cp(56, ("arbitrary",)),
    )(x, mod, mod, wt, *gather)
    return res[:1 + npc], res[1 + npc:]


def _matmul_tn(a, b, out_prev, row_off, m_total, name):
    S, M = a.shape
    N = b.shape[1]
    tm, ts = 512, min(2048, S)
    assert row_off % tm == 0 and M % tm == 0
    ob = row_off // tm
    ns = S // ts

    def body(*refs):
        a_ref, b_ref = refs[0], refs[1]
        o_ref, acc = refs[-2], refs[-1]
        s = pl.program_id(1)

        @pl.when(s == 0)
        def _():
            acc[...] = jnp.zeros_like(acc)

        acc[...] += lax.dot_general(a_ref[...], b_ref[...], TN_DIMS, preferred_element_type=F32)

        @pl.when(s == ns - 1)
        def _():
            o_ref[...] = acc[...].astype(BF16)

    in_specs = [pl.BlockSpec((ts, tm), lambda i, s: (s, i)),
                pl.BlockSpec((ts, N), lambda i, s: (s, 0))]
    args = [a, b]
    aliases = {}
    if out_prev is not None:
        in_specs.append(pl.BlockSpec(memory_space=pl.ANY))
        args.append(out_prev)
        aliases = {2: 0}
    return _pc(
        body, name, grid=(M // tm, ns),
        in_specs=in_specs,
        out_specs=pl.BlockSpec((tm, N), lambda i, s: (ob + i, 0)),
        out_shape=jax.ShapeDtypeStruct((m_total, N), BF16),
        scratch_shapes=[pltpu.VMEM((tm, N), F32)],
        input_output_aliases=aliases,
        compiler_params=_cp(48, ("parallel", "arbitrary")),
    )(*args)


def _bucket_map():
    qi = jnp.arange(BLOCK, dtype=I32)[:, None]
    kj = jnp.arange(2 * BLOCK, dtype=I32)[None, :]
    dist = qi + BLOCK - kj
    in_window = (dist >= 0) & (dist < BLOCK)
    d0 = jnp.maximum(dist, 0)
    max_exact = N_BUCKETS // 2
    d = jnp.maximum(d0, 1).astype(F32)
    large = max_exact + (jnp.log(d / max_exact) / math.log(MAX_DISTANCE / max_exact)
                         * (N_BUCKETS - max_exact)).astype(I32)
    large = jnp.minimum(large, N_BUCKETS - 1)
    bucket = jnp.where(d0 < max_exact, d0, large)
    return jnp.where(in_window, bucket, -1).astype(I32)


def _bias_table(rel_bias, bmap):
    def body(rb_ref, bm_ref, o_ref):
        t, h = pl.program_id(0), pl.program_id(1)
        bm = bm_ref[...]
        acc = jnp.full((BLOCK, 2 * BLOCK), NEG, F32)
        for b in range(N_BUCKETS):
            acc = jnp.where(bm == b, rb_ref[b, h], acc)
        kj = lax.broadcasted_iota(I32, (BLOCK, 2 * BLOCK), 1)
        o_ref[0, 0] = jnp.where((t > 0) | (kj >= BLOCK), acc, NEG)

    return _pc(
        body, "bias_table", grid=(2, N_Q_HEADS),
        in_specs=[pl.BlockSpec(memory_space=pltpu.SMEM),
                  pl.BlockSpec((BLOCK, 2 * BLOCK), lambda t, h: (0, 0))],
        out_specs=pl.BlockSpec((1, 1, BLOCK, 2 * BLOCK), lambda t, h: (t, h, 0, 0)),
        out_shape=jax.ShapeDtypeStruct((2, N_Q_HEADS, BLOCK, 2 * BLOCK), F32),
    )(rel_bias, bmap)


def _bias_spec():
    return pl.BlockSpec((1, N_Q_HEADS, BLOCK, 2 * BLOCK), lambda n: (jnp.minimum(n, 1), 0, 0, 0))


def _relbias_grad(dbias, bmap):
    def body(db_ref, bm_ref, o_ref):
        bm = bm_ref[...]
        x = db_ref[0]
        lane = lax.broadcasted_iota(I32, (1, 128), 1)
        row = jnp.zeros((1, 128), F32)
        for b in range(N_BUCKETS):
            row = jnp.where(lane == b, jnp.sum(jnp.where(bm == b, x, 0.0)), row)
        o_ref[0] = jnp.broadcast_to(row, (8, 128))

    out = _pc(
        body, "relbias_grad", grid=(N_Q_HEADS,),
        in_specs=[pl.BlockSpec((1, BLOCK, 2 * BLOCK), lambda h: (h, 0, 0)),
                  pl.BlockSpec((BLOCK, 2 * BLOCK), lambda h: (0, 0))],
        out_specs=pl.BlockSpec((1, 8, 128), lambda h: (h, 0, 0)),
        out_shape=jax.ShapeDtypeStruct((N_Q_HEADS, 8, 128), F32),
    )(dbias, bmap)
    return out[:, 0, :]


Q_SCALE = HEAD_DIM ** -0.5
assert math.frexp(Q_SCALE)[0] == 0.5


def _stack_heads(ref, hk):
    return jnp.concatenate([ref[:, pl.ds(256 * hk + 64 * g, 64)] for g in range(GQA)], axis=0) * Q_SCALE


def _sink_col(sink_ref, hk):
    row = lax.broadcasted_iota(I32, (GQA * BLOCK, 1), 0)
    s = jnp.full((GQA * BLOCK, 1), sink_ref[0, 4 * hk + 3], F32)
    for g in (2, 1, 0):
        s = jnp.where(row < (g + 1) * BLOCK, sink_ref[0, 4 * hk + g], s)
    return s


def _attn_probs(q4s, kw, bias_ref, sink_ref, hk):
    s = lax.dot_general(q4s, kw, NT_DIMS, preferred_element_type=F32)
    s = s + bias_ref[0, 4 * hk:4 * hk + 4].reshape(GQA * BLOCK, 2 * BLOCK)
    sink = _sink_col(sink_ref, hk)
    m = jnp.maximum(jnp.max(s, axis=1, keepdims=True), sink)
    e = jnp.exp(s - m)
    es = jnp.exp(sink - m)
    inv = 1.0 / (jnp.sum(e, axis=1, keepdims=True) + es)
    return e * inv, es * inv


def _attn_bwd(qkv, g_attn, a_out, dycat, bias, sinks, scatter=()):
    S = qkv.shape[0]
    nb = S // BLOCK
    R = GQA * BLOCK
    ns = len(scatter)

    def body(*refs):
        (q_ref, kc_ref, kp_ref, vc_ref, vp_ref, g_ref, a_ref, dy_ref, bias_ref, sink_ref) = refs[:10]
        dqkv_ref, dg_ref, dbias_ref, dsink_ref = refs[10 + ns:14 + ns]
        dq_scr, dq_new, dk_scr, dv_scr, dkw_scr, dvw_scr, ds_scr = refs[14 + 2 * ns:21 + 2 * ns]
        n = pl.program_id(0)
        if ns:
            cps = _scatter_copies(refs[10:10 + ns], refs[14 + ns:14 + 2 * ns], refs[21 + 2 * ns:])

            @pl.when(n == 0)
            def _():
                _comm_start(cps)

        @pl.when(n == 0)
        def _():
            dbias_ref[...] = jnp.zeros_like(dbias_ref)
            ds_scr[...] = jnp.zeros_like(ds_scr)
            dq_scr[...] = jnp.zeros_like(dq_scr)
            dk_scr[...] = jnp.zeros_like(dk_scr)
            dv_scr[...] = jnp.zeros_like(dv_scr)

        @pl.when(n < nb)
        def _():
            for hk in range(N_KV_HEADS):
                q4 = _stack_heads(q_ref, hk)
                ks = pl.ds(64 * hk, 64)
                kw = jnp.concatenate([kp_ref[:, ks], kc_ref[:, ks]], axis=0)
                vw = jnp.concatenate([vp_ref[:, ks], vc_ref[:, ks]], axis=0)
                p, psink = _attn_probs(q4, kw, bias_ref, sink_ref, hk)
                da_parts, a_parts = [], []
                for g in range(GQA):
                    sl = pl.ds(256 * hk + 64 * g, 64)
                    gg = g_ref[:, sl]
                    sg = _sig(gg)
                    dyg = dy_ref[:, sl]
                    ag = a_ref[:, sl]
                    da_parts.append(dyg * (gg * sg))
                    a_parts.append(ag)
                    dg_ref[:, sl] = (dyg * ag * _dsilu(gg, sg)).astype(BF16)
                da4 = jnp.concatenate(da_parts, axis=0)
                a4 = jnp.concatenate(a_parts, axis=0)
                delta = jnp.sum(da4 * a4, axis=1, keepdims=True)
                da4b = da4.astype(BF16)
                dp = lax.dot_general(da4b, vw, NT_DIMS, preferred_element_type=F32)
                ds = p * (dp - delta)
                ds_scr[hk] += -psink * delta
                dbias_ref[4 * hk:4 * hk + 4] += ds.reshape(GQA, BLOCK, 2 * BLOCK)
                dsb = ds.astype(BF16)
                dq4 = jnp.dot(dsb, kw, preferred_element_type=F32) * Q_SCALE
                for g in range(GQA):
                    dq_new[:, pl.ds(256 * hk + 64 * g, 64)] = dq4[BLOCK * g:BLOCK * (g + 1)].astype(BF16)
                dkw_scr[:, ks] = lax.dot_general(dsb, q4, TN_DIMS, preferred_element_type=F32)
                dvw_scr[:, ks] = lax.dot_general(p.astype(BF16), da4b, TN_DIMS, preferred_element_type=F32)

        @pl.when(n == nb)
        def _():
            dkw_scr[0:BLOCK, :] = jnp.zeros((BLOCK, D_KV), F32)
            dvw_scr[0:BLOCK, :] = jnp.zeros((BLOCK, D_KV), F32)

        dqkv_ref[:, 0:D_ATTN] = dq_scr[...]
        dqkv_ref[:, D_ATTN:D_ATTN + D_KV] = (dk_scr[...] + dkw_scr[0:BLOCK, :]).astype(BF16)
        dqkv_ref[:, D_ATTN + D_KV:D_ATTN + 2 * D_KV] = (dv_scr[...] + dvw_scr[0:BLOCK, :]).astype(BF16)
        dq_scr[...] = dq_new[...]
        dk_scr[...] = dkw_scr[BLOCK:2 * BLOCK, :]
        dv_scr[...] = dvw_scr[BLOCK:2 * BLOCK, :]

        @pl.when(n == nb)
        def _():
            lane = lax.broadcasted_iota(I32, (1, 128), 1)
            row = jnp.zeros((1, 128), F32)
            for hk in range(N_KV_HEADS):
                col = ds_scr[hk]
                for g in range(GQA):
                    row = jnp.where(lane == 4 * hk + g, jnp.sum(col[BLOCK * g:BLOCK * (g + 1)]), row)
            dsink_ref[...] = jnp.broadcast_to(row, (8, 128))
            if ns:
                _comm_wait(cps)

    cur = lambda n: jnp.minimum(n, nb - 1)
    prev = lambda n: jnp.clip(n - 1, 0, nb - 1)
    res = _pc(
        body, "attn_bwd", grid=(nb + 1,),
        in_specs=[pl.BlockSpec((BLOCK, D_ATTN), lambda n: (cur(n), 0)),
                  pl.BlockSpec((BLOCK, D_KV), lambda n: (cur(n), 4)),
                  pl.BlockSpec((BLOCK, D_KV), lambda n: (prev(n), 4)),
                  pl.BlockSpec((BLOCK, D_KV), lambda n: (cur(n), 5)),
                  pl.BlockSpec((BLOCK, D_KV), lambda n: (prev(n), 5)),
                  pl.BlockSpec((BLOCK, D_ATTN), lambda n: (cur(n), 0)),
                  pl.BlockSpec((BLOCK, D_ATTN), lambda n: (cur(n), 0)),
                  pl.BlockSpec((BLOCK, D_ATTN), lambda n: (cur(n), 0)),
                  _bias_spec(),
                  pl.BlockSpec(memory_space=pltpu.SMEM)] + [HBM_SPEC] * ns,
        out_specs=[pl.BlockSpec((BLOCK, D_ATTN + 2 * D_KV), lambda n: (prev(n), 0)),
                   pl.BlockSpec((BLOCK, D_ATTN), lambda n: (cur(n), 0)),
                   pl.BlockSpec((N_Q_HEADS, BLOCK, 2 * BLOCK), lambda n: (0, 0, 0)),
                   pl.BlockSpec((8, 128), lambda n: (0, 0))] + [HBM_SPEC] * ns,
        out_shape=[jax.ShapeDtypeStruct((S, D_ATTN + 2 * D_KV), BF16),
                   jax.ShapeDtypeStruct((S, D_ATTN), BF16),
                   jax.ShapeDtypeStruct((N_Q_HEADS, BLOCK, 2 * BLOCK), F32),
                   jax.ShapeDtypeStruct((8, 128), F32)] + [jax.ShapeDtypeStruct(f.shape, f.dtype) for f in scatter],
        scratch_shapes=[pltpu.VMEM((BLOCK, D_ATTN), BF16), pltpu.VMEM((BLOCK, D_ATTN), BF16),
                        pltpu.VMEM((BLOCK, D_KV), F32), pltpu.VMEM((BLOCK, D_KV), F32),
                        pltpu.VMEM((2 * BLOCK, D_KV), F32), pltpu.VMEM((2 * BLOCK, D_KV), F32),
                        pltpu.VMEM((N_KV_HEADS, R, 1), F32)] + (_comm_scratch(ns) if ns else []),
        compiler_params=_cp(48, ("arbitrary",)),
    )(qkv, qkv, qkv, qkv, qkv, g_attn, a_out, dycat, bias, sinks, *scatter)
    return res[:4], res[4:]


def _conv_tile(S):
    return min(256, S)


def _shifted(win, s):
    return win if s == 0 else pltpu.roll(win, win.shape[0] - s, axis=0)


def _conv_taps(win, cw_ref, lanes, rows):
    acc = jnp.zeros((rows, win.shape[1]), F32)
    for s in range(8):
        ws = _shifted(win, s)
        for aa in range(5):
            j = 8 * aa + s - 2
            if 0 <= j < CONV_WIDTH:
                acc = acc + ws[8 * aa:8 * aa + rows] * cw_ref[j:j + 1, lanes]
    return acc


def _mixer_fwd(qkv, g_attn, glu, g_conv, bias, sinks, conv_w, conv_b, ln_g, ln_b, w_pw, b_pw):
    S = qkv.shape[0]
    nb = S // BLOCK
    hb = BLOCK // HALO
    QC = D_CONV // N_KV_HEADS

    def body(q_ref, kc_ref, kp_ref, vc_ref, vp_ref, g_ref, bias_ref, sink_ref,
             a_in, ah_in, b_in, bh_in, gc_ref, cw_ref, cb_ref, lg_ref, lb_ref, wpw_ref, bpw_ref,
             y_ref, a_ref, u1_ref, u3_ref, p_ref, win):
        n = pl.program_id(0)
        win[0:HALO, :] = jnp.where(n > 0, ah_in[...] * _sig(bh_in[...]), 0.0)
        win[HALO:HALO + BLOCK, :] = a_in[...] * _sig(b_in[...])
        for hk in range(N_KV_HEADS):
            q4 = _stack_heads(q_ref, hk)
            ks = pl.ds(64 * hk, 64)
            kw = jnp.concatenate([kp_ref[:, ks], kc_ref[:, ks]], axis=0)
            vw = jnp.concatenate([vp_ref[:, ks], vc_ref[:, ks]], axis=0)
            p, _ = _attn_probs(q4, kw, bias_ref, sink_ref, hk)
            o4 = jnp.dot(p.astype(BF16), vw, preferred_element_type=F32)
            for g in range(GQA):
                sl = pl.ds(256 * hk + 64 * g, 64)
                og = o4[BLOCK * g:BLOCK * (g + 1)]
                gg = g_ref[:, sl]
                a_ref[:, sl] = og
                y_ref[:, sl] = (og * (gg * _sig(gg))).astype(BF16)
            lanes = pl.ds(QC * hk, QC)
            u1_ref[:, lanes] = _conv_taps(win[:, lanes], cw_ref, lanes, BLOCK) + cb_ref[:, lanes]
        u1 = u1_ref[...]
        mu = jnp.mean(u1, axis=1, keepdims=True)
        uc = u1 - mu
        rstd = lax.rsqrt(jnp.mean(uc * uc, axis=1, keepdims=True) + LN_EPS)
        u2 = uc * rstd * lg_ref[...] + lb_ref[...]
        u3 = (u2 * _sig(u2)).astype(BF16)
        u3_ref[...] = u3
        pw = jnp.dot(u3, wpw_ref[...], preferred_element_type=F32) + bpw_ref[...]
        p_ref[...] = pw
        gc = gc_ref[...]
        y_ref[:, D_ATTN:] = (pw * (gc * _sig(gc))).astype(BF16)

    prev = lambda n: jnp.maximum(n - 1, 0)
    halo = lambda n: jnp.maximum(n * hb - 1, 0)
    vec = pl.BlockSpec((1, D_CONV), lambda n: (0, 0))
    blk = lambda w, j: pl.BlockSpec((BLOCK, w), lambda n: (n, j))
    return _pc(
        body, "mixer_fwd", grid=(nb,),
        in_specs=[blk(D_ATTN, 0),
                  blk(D_KV, 4), pl.BlockSpec((BLOCK, D_KV), lambda n: (prev(n), 4)),
                  blk(D_KV, 5), pl.BlockSpec((BLOCK, D_KV), lambda n: (prev(n), 5)),
                  blk(D_ATTN, 0),
                  _bias_spec(),
                  pl.BlockSpec(memory_space=pltpu.SMEM),
                  blk(D_CONV, 0), pl.BlockSpec((HALO, D_CONV), lambda n: (halo(n), 0)),
                  blk(D_CONV, 1), pl.BlockSpec((HALO, D_CONV), lambda n: (halo(n), 1)),
                  blk(D_CONV, 0),
                  pl.BlockSpec((CONV_ROWS, D_CONV), lambda n: (0, 0)),
                  vec, vec, vec,
                  pl.BlockSpec((D_CONV, D_CONV), lambda n: (0, 0)),
                  vec],
        out_specs=[blk(2 * D_ATTN, 0), blk(D_ATTN, 0), blk(D_CONV, 0), blk(D_CONV, 0), blk(D_CONV, 0)],
        out_shape=[jax.ShapeDtypeStruct((S, 2 * D_ATTN), BF16),
                   jax.ShapeDtypeStruct((S, D_ATTN), F32),
                   jax.ShapeDtypeStruct((S, D_CONV), F32),
                   jax.ShapeDtypeStruct((S, D_CONV), BF16),
                   jax.ShapeDtypeStruct((S, D_CONV), F32)],
        scratch_shapes=[pltpu.VMEM((BLOCK + HALO, D_CONV), F32)],
        compiler_params=_cp(48, ("parallel",)),
    )(qkv, qkv, qkv, qkv, qkv, g_attn, bias, sinks, glu, glu, glu, glu, g_conv, conv_w, conv_b, ln_g, ln_b, w_pw,
      b_pw)


def _conv_bwd_a(dycat, g_conv, p_out, u1, ln_g, ln_b, w_pw):
    S = u1.shape[0]
    T = _conv_tile(S)

    def body(dy_ref, gc_ref, p_ref, u1_ref, lg_ref, lb_ref, wpw_ref,
             dp_ref, dgc_ref, du1_ref, gbpw_ref, glg_ref, glb_ref, gcb_ref):
        i = pl.program_id(0)

        @pl.when(i == 0)
        def _():
            for r in (gbpw_ref, glg_ref, glb_ref, gcb_ref):
                r[...] = jnp.zeros_like(r)

        dy = dy_ref[...]
        gc = gc_ref[...]
        sg = _sig(gc)
        dp = dy * (gc * sg)
        dgc_ref[...] = (dy * p_ref[...] * _dsilu(gc, sg)).astype(BF16)
        gbpw_ref[...] += jnp.sum(dp, axis=0, keepdims=True)
        dpb = dp.astype(BF16)
        dp_ref[...] = dpb
        du3 = lax.dot_general(dpb, wpw_ref[...], NT_DIMS, preferred_element_type=F32)
        u1 = u1_ref[...]
        mu = jnp.mean(u1, axis=1, keepdims=True)
        uc = u1 - mu
        rstd = lax.rsqrt(jnp.mean(uc * uc, axis=1, keepdims=True) + LN_EPS)
        uh = uc * rstd
        lg = lg_ref[...]
        u2 = uh * lg + lb_ref[...]
        s2 = _sig(u2)
        du2 = du3 * _dsilu(u2, s2)
        glg_ref[...] += jnp.sum(du2 * uh, axis=0, keepdims=True)
        glb_ref[...] += jnp.sum(du2, axis=0, keepdims=True)
        duh = du2 * lg
        du1 = rstd * (duh - jnp.mean(duh, axis=1, keepdims=True) - uh * jnp.mean(duh * uh, axis=1, keepdims=True))
        du1_ref[...] = du1
        gcb_ref[...] += jnp.sum(du1, axis=0, keepdims=True)

    vec = pl.BlockSpec((1, D_CONV), lambda i: (0, 0))
    tile = pl.BlockSpec((T, D_CONV), lambda i: (i, 0))
    vshape = jax.ShapeDtypeStruct((1, D_CONV), F32)
    return _pc(
        body, "conv_bwd_a", grid=(S // T,),
        in_specs=[pl.BlockSpec((T, D_CONV), lambda i: (i, 1)), tile, tile, tile, vec, vec,
                  pl.BlockSpec((D_CONV, D_CONV), lambda i: (0, 0))],
        out_specs=[tile, tile, tile, vec, vec, vec, vec],
        out_shape=[jax.ShapeDtypeStruct((S, D_CONV), BF16), jax.ShapeDtypeStruct((S, D_CONV), BF16),
                   jax.ShapeDtypeStruct((S, D_CONV), F32), vshape, vshape, vshape, vshape],
        compiler_params=_cp(48, ("arbitrary",)),
    )(dycat, g_conv, p_out, u1, ln_g, ln_b, w_pw)


def _conv_bwd_b(du1, glu, conv_w):
    S = du1.shape[0]
    T = _conv_tile(S)
    hb = T // HALO
    nt = S // T
    last_h = S // HALO - 1

    def body(du_ref, dun_ref, a_ref, b_ref, cw_ref, dab_ref, gw_ref):
        i = pl.program_id(0)

        @pl.when(i == 0)
        def _():
            gw_ref[...] = jnp.zeros_like(gw_ref)

        for lg in range(D_CONV // 128):
            lanes = pl.ds(128 * lg, 128)
            a = a_ref[:, lanes]
            sb = _sig(b_ref[:, lanes])
            u0 = a * sb
            du1n = jnp.where(i < nt - 1, dun_ref[:, lanes], 0.0)
            win2 = jnp.concatenate([du_ref[:, lanes], du1n], axis=0)
            acc = jnp.zeros((T, 128), F32)
            for s in range(8):
                w2 = _shifted(win2, s)
                for aa in range(4):
                    j = CONV_WIDTH - 1 - (8 * aa + s)
                    if 0 <= j < CONV_WIDTH:
                        xo = w2[8 * aa:8 * aa + T]
                        acc = acc + xo * cw_ref[j:j + 1, lanes]
                        gw_ref[j:j + 1, lanes] += jnp.sum(xo * u0, axis=0, keepdims=True)
            dab_ref[:, lanes] = (acc * sb).astype(BF16)
            dab_ref[:, pl.ds(D_CONV + 128 * lg, 128)] = (acc * a * sb * (1.0 - sb)).astype(BF16)

    nxt = lambda i: jnp.minimum((i + 1) * hb, last_h)
    return _pc(
        body, "conv_bwd_b", grid=(nt,),
        in_specs=[pl.BlockSpec((T, D_CONV), lambda i: (i, 0)),
                  pl.BlockSpec((HALO, D_CONV), lambda i: (nxt(i), 0)),
                  pl.BlockSpec((T, D_CONV), lambda i: (i, 0)),
                  pl.BlockSpec((T, D_CONV), lambda i: (i, 1)),
                  pl.BlockSpec((CONV_ROWS, D_CONV), lambda i: (0, 0))],
        out_specs=[pl.BlockSpec((T, 2 * D_CONV), lambda i: (i, 0)),
                   pl.BlockSpec((CONV_ROWS, D_CONV), lambda i: (0, 0))],
        out_shape=[jax.ShapeDtypeStruct((S, 2 * D_CONV), BF16),
                   jax.ShapeDtypeStruct((CONV_ROWS, D_CONV), F32)],
        compiler_params=_cp(48, ("arbitrary",)),
    )(du1, du1, glu, glu, conv_w)


def _outproj_ln(ycat, w_out, x, target, mod, ln_g, ln_b):
    S = x.shape[0]
    tm = min(256, S)

    def body(yc_ref, w_ref, x_ref, t_ref, gate_ref, lg_ref, lb_ref,
             dz_ref, dy_ref, dyc_ref, loss_ref, glg_ref, glb_ref, dgate_ref):
        i = pl.program_id(0)

        @pl.when(i == 0)
        def _():
            for r in (loss_ref, glg_ref, glb_ref, dgate_ref):
                r[...] = jnp.zeros_like(r)

        w = w_ref[...]
        y = jnp.dot(yc_ref[...], w, preferred_element_type=F32)
        gate = gate_ref[...]
        z = ALPHA * x_ref[...] + gate * y
        mu = jnp.mean(z, axis=1, keepdims=True)
        zc = z - mu
        rstd = lax.rsqrt(jnp.mean(zc * zc, axis=1, keepdims=True) + LN_EPS)
        zh = zc * rstd
        lg = lg_ref[...]
        err = zh * lg + lb_ref[...] - t_ref[...]
        loss_ref[...] += 0.5 * jnp.sum(jnp.sum(err * err, axis=1, keepdims=True)) / D_MODEL
        dout = err * (1.0 / D_MODEL)
        glg_ref[...] += jnp.sum(dout * zh, axis=0, keepdims=True)
        glb_ref[...] += jnp.sum(dout, axis=0, keepdims=True)
        dzh = dout * lg
        dz = rstd * (dzh - jnp.mean(dzh, axis=1, keepdims=True) - zh * jnp.mean(dzh * zh, axis=1, keepdims=True))
        dz_ref[...] = dz
        dgate_ref[...] += jnp.sum(dz * y, axis=0, keepdims=True)
        dy = (dz * gate).astype(BF16)
        dy_ref[...] = dy
        dyc_ref[...] = lax.dot_general(dy, w, NT_DIMS, preferred_element_type=F32)

    vec = pl.BlockSpec((1, D_MODEL), lambda i: (0, 0))
    tile = pl.BlockSpec((tm, D_MODEL), lambda i: (i, 0))
    vshape = jax.ShapeDtypeStruct((1, D_MODEL), F32)
    return _pc(
        body, "outproj_ln", grid=(S // tm,),
        in_specs=[tile, pl.BlockSpec((D_MODEL, D_MODEL), lambda i: (0, 0)), tile, tile,
                  pl.BlockSpec((1, D_MODEL), lambda i: (0, 2)), vec, vec],
        out_specs=[tile, tile, tile, pl.BlockSpec((8, 128), lambda i: (0, 0)), vec, vec, vec],
        out_shape=[jax.ShapeDtypeStruct((S, D_MODEL), F32), jax.ShapeDtypeStruct((S, D_MODEL), BF16),
                   jax.ShapeDtypeStruct((S, D_MODEL), F32), jax.ShapeDtypeStruct((8, 128), F32),
                   vshape, vshape, vshape],
        compiler_params=_cp(56, ("arbitrary",)),
    )(ycat, w_out, x, target, mod, ln_g, ln_b)


def _dh_kernel(segs, wt, dz, x, mod, scatter=()):
    S = x.shape[0]
    tm = min(256, S)
    row0 = [0]
    for a in segs:
        row0.append(row0[-1] + a.shape[1])
    assert row0[-1] == wt.shape[0]
    nseg = len(segs)
    ns = len(scatter)
    ni = S // tm

    def body(*refs):
        seg_refs = refs[:nseg]
        w_ref, dz_ref, x_ref, sc_ref = refs[nseg:nseg + 4]
        outs = refs[nseg + 4 + ns:]
        gx_ref, dsh_ref, dsc_ref = outs[:3]
        i = pl.program_id(0)
        if ns:
            cps = _scatter_copies(refs[nseg + 4:nseg + 4 + ns], outs[3:3 + ns], outs[3 + ns:])

        @pl.when(i == 0)
        def _():
            dsh_ref[...] = jnp.zeros_like(dsh_ref)
            dsc_ref[...] = jnp.zeros_like(dsc_ref)
            if ns:
                _comm_start(cps)

        dh = jnp.dot(seg_refs[0][...], w_ref[row0[0]:row0[1], :], preferred_element_type=F32)
        for t in range(1, nseg):
            dh = dh + jnp.dot(seg_refs[t][...], w_ref[row0[t]:row0[t + 1], :], preferred_element_type=F32)
        gx_ref[...] = ALPHA * dz_ref[...] + dh * (1.0 + sc_ref[...])
        dsh_ref[...] += jnp.sum(dh, axis=0, keepdims=True)
        dsc_ref[...] += jnp.sum(dh * x_ref[...], axis=0, keepdims=True)

        if ns:
            @pl.when(i == ni - 1)
            def _():
                _comm_wait(cps)

    tile = pl.BlockSpec((tm, D_MODEL), lambda i: (i, 0))
    vec = pl.BlockSpec((1, D_MODEL), lambda i: (0, 0))
    vshape = jax.ShapeDtypeStruct((1, D_MODEL), F32)
    res = _pc(
        body, "dh_gradx", grid=(ni,),
        in_specs=[pl.BlockSpec((tm, a.shape[1]), lambda i: (i, 0)) for a in segs] + [
            pl.BlockSpec(wt.shape, lambda i: (0, 0), pipeline_mode=pl.Buffered(1)), tile, tile,
            pl.BlockSpec((1, D_MODEL), lambda i: (0, 1))] + [HBM_SPEC] * ns,
        out_specs=[tile, vec, vec] + [HBM_SPEC] * ns,
        out_shape=[jax.ShapeDtypeStruct((S, D_MODEL), F32), vshape, vshape]
        + [jax.ShapeDtypeStruct(f.shape, f.dtype) for f in scatter],
        scratch_shapes=_comm_scratch(ns) if ns else [],
        compiler_params=_cp(58, ("arbitrary",)),
    )(*segs, wt, dz, x, mod, *scatter)
    return res[:3], res[3:]


def _row_tile(rows, cols):
    if rows * cols * 4 <= 2 * MIB or rows % 8:
        return rows
    tr = max(8, (2 * MIB // (cols * 4)) // 8 * 8)
    while rows % tr:
        tr -= 8
    return tr


def _sum8(recv, name):
    _, R, C = recv.shape
    tr = _row_tile(R, C)

    def body(r_ref, o_ref):
        acc = r_ref[0].astype(F32)
        for d in range(1, N_DEV):
            acc = acc + r_ref[d].astype(F32)
        o_ref[...] = acc

    return _pc(
        body, name, grid=(R // tr,),
        in_specs=[pl.BlockSpec((N_DEV, tr, C), lambda i: (0, i, 0))],
        out_specs=pl.BlockSpec((tr, C), lambda i: (i, 0)),
        out_shape=jax.ShapeDtypeStruct((R, C), F32),
        compiler_params=_cp(40, ("parallel",)),
    )(recv)


def _adamw(w, g, m, v, name):
    R, C = w.shape
    tr = _row_tile(R, C)

    def body(w_ref, g_ref, m_ref, v_ref, d_ref, nm_ref, nv_ref):
        g_ = g_ref[...]
        m_ = ADAM_B1 * m_ref[...] + (1.0 - ADAM_B1) * g_
        v_ = ADAM_B2 * v_ref[...] + (1.0 - ADAM_B2) * (g_ * g_)
        m_hat = m_ / (1.0 - ADAM_B1 ** ADAM_STEP)
        v_hat = v_ / (1.0 - ADAM_B2 ** ADAM_STEP)
        d_ref[...] = -ADAM_LR * (m_hat / (jnp.sqrt(v_hat) + ADAM_EPS) + ADAM_WD * w_ref[...])
        nm_ref[...] = m_
        nv_ref[...] = v_

    spec = pl.BlockSpec((tr, C), lambda i: (i, 0))
    shape = jax.ShapeDtypeStruct((R, C), F32)
    return _pc(
        body, name, grid=(R // tr,),
        in_specs=[spec] * 4, out_specs=[spec] * 3, out_shape=[shape] * 3,
        compiler_params=_cp(40, ("parallel",)),
    )(w, g, m, v)


def _small_mm(a, b, name):
    def body(a_ref, b_ref, o_ref):
        o_ref[...] = jnp.dot(a_ref[...], b_ref[...], preferred_element_type=F32)

    return _pc(body, name, out_shape=jax.ShapeDtypeStruct((a.shape[0], b.shape[1]), F32),
               compiler_params=_cp(40))(a, b)


def _all_gather_two_level(shards, name):
    nt = len(shards)

    def body(*refs):
        src, dst = refs[:nt], refs[nt:2 * nt]
        send_sems, recv_sems, local_sems = refs[2 * nt:]
        x, y, c = _me()
        sibling = (x, y, 1 - c)
        chips = [(1 - x, y), (x, 1 - y), (1 - x, 1 - y)]

        def blk(t, px, py, pc):
            return _block(dst[t], 4 * px + 2 * py + pc, src[t].shape[0])

        def copy(t, k, to, block, from_src):
            return _remote(src[t] if from_src else blk(t, *block), blk(t, *block),
                           send_sems.at[t, k], recv_sems.at[t, k], to)

        local = [pltpu.make_async_copy(src[t], blk(t, x, y, c), local_sems.at[t]) for t in range(nt)]
        for cp in local:
            cp.start()
        sends = []
        for t in range(nt):
            sends.append(copy(t, 0, sibling, (x, y, c), True))
            sends += [copy(t, 1 + j, (*chip, c), (x, y, c), True) for j, chip in enumerate(chips)]
        for cp in sends:
            cp.start()
        for j, chip in enumerate(chips):
            for t in range(nt):
                copy(t, 1 + j, (x, y, c), (*chip, c), False).wait_recv()
                fwd = copy(t, 4 + j, sibling, (*chip, c), False)
                fwd.start()
                sends.append(fwd)
        for t in range(nt):
            copy(t, 0, (x, y, c), (x, y, 1 - c), False).wait_recv()
            for j, chip in enumerate(chips):
                copy(t, 4 + j, (x, y, c), (*chip, 1 - c), False).wait_recv()
        for cp in sends:
            cp.wait_send()
        for cp in local:
            cp.wait()

    return _pc(
        body, name,
        in_specs=[HBM_SPEC] * nt, out_specs=[HBM_SPEC] * nt,
        out_shape=_gather_shapes(shards), scratch_shapes=_comm_scratch(nt),
        compiler_params=pltpu.CompilerParams(has_side_effects=True),
    )(*shards)


def _ada_fwd(c, w_ada, b_ada_cols):
    ncol = w_ada.shape[1]

    def body(c_ref, w_ref, b_ref, mod_ref, call_ref, cact, cmat, mloc, send1, recv1, send2, recv2):
        x, y, z = _me()
        me = 4 * x + 2 * y + z
        cv = c_ref[...]
        cact[...] = cv * _sig(cv)
        call_ref[me] = cact[...]
        sends = []
        for k in range(1, N_DEV):
            dev, _ = _peer(k)
            cp = _remote(cact, call_ref.at[me], send1.at[k - 1], recv1.at[k - 1], dev)
            cp.start()
            sends.append(cp)
        for k in range(1, N_DEV):
            dev, blk = _peer(k)
            _remote(cact, call_ref.at[blk], send1.at[k - 1], recv1.at[k - 1], dev).wait_recv()
        cmat[...] = jnp.zeros_like(cmat)
        for b in range(N_DEV):
            cmat[b:b + 1, :] = call_ref[b]
        m = jnp.dot(cmat[...].astype(BF16), w_ref[...].astype(BF16), preferred_element_type=F32) + b_ref[...]
        for b in range(N_DEV):
            mloc[b] = m[b:b + 1, :]
        mod_ref[me] = mloc[me]
        for k in range(1, N_DEV):
            dev, blk = _peer(k)
            cp = _remote(mloc.at[blk], mod_ref.at[me], send2.at[k - 1], recv2.at[k - 1], dev)
            cp.start()
            sends.append(cp)
        for k in range(1, N_DEV):
            dev, blk = _peer(k)
            _remote(mloc.at[me], mod_ref.at[blk], send2.at[k - 1], recv2.at[k - 1], dev).wait_recv()
        for cp in sends:
            cp.wait_send()

    return _pc(
        body, "ada_fwd",
        in_specs=[VMEM_SPEC] * 3, out_specs=[VMEM_SPEC] * 2,
        out_shape=[jax.ShapeDtypeStruct((N_DEV, 1, ncol), F32), jax.ShapeDtypeStruct((N_DEV, 1, D_MODEL), F32)],
        scratch_shapes=[pltpu.VMEM((1, D_MODEL), F32), pltpu.VMEM((16, D_MODEL), F32),
                        pltpu.VMEM((N_DEV, 1, ncol), F32)] + [pltpu.SemaphoreType.DMA((N_DEV - 1,))] * 4,
        compiler_params=pltpu.CompilerParams(has_side_effects=True, vmem_limit_bytes=40 * MIB),
    )(c, w_ada, b_ada_cols)


def _small_gather(vec):
    n = vec.shape[1]

    def body(v_ref, all_ref, sum_ref, send, recv):
        x, y, z = _me()
        me = 4 * x + 2 * y + z
        all_ref[me] = v_ref[...]
        sends = []
        for k in range(1, N_DEV):
            dev, _ = _peer(k)
            cp = _remote(v_ref, all_ref.at[me], send.at[k - 1], recv.at[k - 1], dev)
            cp.start()
            sends.append(cp)
        for k in range(1, N_DEV):
            dev, blk = _peer(k)
            _remote(v_ref, all_ref.at[blk], send.at[k - 1], recv.at[k - 1], dev).wait_recv()
        acc = all_ref[0]
        for d in range(1, N_DEV):
            acc = acc + all_ref[d]
        sum_ref[...] = acc
        for cp in sends:
            cp.wait_send()

    return _pc(
        body, "small_gather",
        in_specs=[VMEM_SPEC], out_specs=[VMEM_SPEC] * 2,
        out_shape=[jax.ShapeDtypeStruct((N_DEV, 1, n), F32), jax.ShapeDtypeStruct((1, n), F32)],
        scratch_shapes=[pltpu.SemaphoreType.DMA((N_DEV - 1,))] * 2,
        compiler_params=pltpu.CompilerParams(has_side_effects=True),
    )(vec)


def _local_step(x, target, mod, wt_in, w_out_loc, w_pw_loc, conv_w_loc, rel_bias, sinks, conv_b, conv_ln_g,
                conv_ln_b, b_pw, ln_g, ln_b):
    bmap = _bucket_map()
    bias = _bias_table(rel_bias, bmap)
    (h, qkv, g_attn, glu, g_conv), (w_out, w_pw, conv_w_blocks) = _inproj(
        x, mod, wt_in, (w_out_loc, w_pw_loc, conv_w_loc))
    conv_w = conv_w_blocks.reshape(N_DEV, CONV_ROWS, 128).transpose(1, 0, 2).reshape(CONV_ROWS, D_CONV)
    ycat, a_out, u1, u3, p_out = _mixer_fwd(qkv, g_attn, glu, g_conv, bias, sinks, conv_w, conv_b, conv_ln_g,
                                            conv_ln_b, w_pw, b_pw)
    dz, dy, dycat, loss, g_ln_g, g_ln_b, dgate = _outproj_ln(ycat, w_out, x, target, mod, ln_g, ln_b)
    gw_out = _matmul_tn(ycat, dy, None, 0, D_MODEL, "grad_w_out")
    dp, dgc, du1, g_bpw, g_clg, g_clb, g_cb = _conv_bwd_a(dycat, g_conv, p_out, u1, conv_ln_g, conv_ln_b, w_pw)
    gw_pw = _matmul_tn(u3, dp, None, 0, D_CONV, "grad_w_pw")
    dab, g_cw = _conv_bwd_b(du1, glu, conv_w)
    (dqkv, dga, dbias, dsink), (r_out, r_pw) = _attn_bwd(qkv, g_attn, a_out, dycat, bias, sinks,
                                                         scatter=(gw_out, gw_pw))
    g_rb = _relbias_grad(dbias, bmap)
    gwt_in = _matmul_tn(dqkv, h, None, 0, D_IN, "grad_w_in_qkv")
    gwt_in = _matmul_tn(dga, h, gwt_in, 1536, D_IN, "grad_w_in_gattn")
    gwt_in = _matmul_tn(dab, h, gwt_in, 2560, D_IN, "grad_w_in_glu")
    gwt_in = _matmul_tn(dgc, h, gwt_in, 4608, D_IN, "grad_w_in_gconv")
    g_cw_blocks = g_cw.reshape(CONV_ROWS, N_DEV, 128).transpose(1, 0, 2).reshape(N_DEV * CONV_ROWS, 128)
    (grad_x, dshift, dscale), (r_in, r_cw) = _dh_kernel([dqkv, dga, dab, dgc], wt_in, dz, x, mod,
                                                        scatter=(gwt_in, g_cw_blocks))
    dmod = jnp.concatenate([dshift, dscale, dgate], axis=1)
    small = dict(dmod=dmod, b_pw=g_bpw, conv_ln_g=g_clg, conv_ln_b=g_clb, conv_b=g_cb, ln_g=g_ln_g, ln_b=g_ln_b,
                 rel_bias=g_rb[:, :N_BUCKETS].reshape(1, N_BUCKETS * N_Q_HEADS),
                 sinks=dsink[0:1, :], loss=loss[0:1, :])
    return grad_x, r_in, r_out, r_pw, r_cw, small


SMALL_FIELDS = (("dmod", 3 * D_MODEL), ("b_pw", D_CONV), ("conv_ln_g", D_CONV), ("conv_ln_b", D_CONV),
                ("conv_b", D_CONV), ("ln_g", D_MODEL), ("ln_b", D_MODEL), ("rel_bias", N_BUCKETS * N_Q_HEADS),
                ("sinks", 128), ("loss", 128))


def _pack(fields):
    parts = []
    for name, width in SMALL_FIELDS:
        v = fields[name].reshape(1, -1).astype(F32)
        if v.shape[1] < width:
            v = jnp.pad(v, ((0, 0), (0, width - v.shape[1])))
        parts.append(v)
    return jnp.concatenate(parts, axis=1)


def _unpack(vec):
    out, off = {}, 0
    for name, width in SMALL_FIELDS:
        out[name] = vec[:, off:off + width]
        off += width
    return out


def kernel(x, c, w_ada, b_ada, w_in, rel_bias, sinks, conv_w, conv_b, conv_ln_g, conv_ln_b, w_pw, b_pw, w_out, ln_g, ln_b, loss_target, m_w_ada, m_b_ada, m_w_in, m_rel_bias, m_sinks, m_conv_w, m_conv_b, m_conv_ln_g, m_conv_ln_b, m_w_pw, m_b_pw, m_w_out, m_ln_g, m_ln_b, v_w_ada, v_b_ada, v_w_in, v_rel_bias, v_sinks, v_conv_w, v_conv_b, v_conv_ln_g, v_conv_ln_b, v_w_pw, v_b_pw, v_w_out, v_ln_g, v_ln_b):
    xi, yi, ci = _me()
    me = 4 * xi + 2 * yi + ci
    ncol = w_ada.shape[2]

    wt_in_loc = w_in[0].T.astype(BF16)
    conv_w_loc = jnp.pad(conv_w[0], ((0, CONV_ROWS - CONV_WIDTH), (0, 0)))
    (wt_in_full,) = _all_gather_two_level([wt_in_loc], "gather_w_in")

    b_ada_cols = lax.dynamic_slice(b_ada, (0, me * ncol), (1, ncol))
    mod_blocks, c_all = _ada_fwd(c, w_ada[0], b_ada_cols)
    mod = mod_blocks.reshape(1, 3 * D_MODEL)

    grad_x, r_in, r_out, r_pw, r_cw, small = _local_step(
        x[0], loss_target[0], mod, wt_in_full, w_out[0].astype(BF16), w_pw[0].astype(BF16), conv_w_loc, rel_bias,
        sinks, conv_b, conv_ln_g, conv_ln_b, b_pw, ln_g, ln_b)

    gathered, summed = _small_gather(_pack(small))
    tot = _unpack(summed)
    dmod_all = gathered[:, 0, :3 * D_MODEL]
    loss = tot["loss"][0, 0]

    ct = jnp.zeros((D_MODEL, 128), BF16).at[:, :N_DEV].set(c_all[:, 0, :].T.astype(BF16))
    dm = jnp.zeros((128, ncol), BF16).at[:N_DEV, :].set(
        lax.dynamic_slice(dmod_all, (0, me * ncol), (N_DEV, ncol)).astype(BF16))
    g_w_ada = _small_mm(ct, dm, "grad_w_ada")

    gt_w_in = _sum8(r_in.reshape(N_DEV, D_IN // N_DEV, D_MODEL), "sum_w_in")
    g_w_out = _sum8(r_out.reshape(N_DEV, D_MODEL // N_DEV, D_MODEL), "sum_w_out")
    g_w_pw = _sum8(r_pw.reshape(N_DEV, D_CONV // N_DEV, D_CONV), "sum_w_pw")
    g_conv_w = _sum8(r_cw.reshape(N_DEV, CONV_ROWS, 128), "sum_conv_w")[:CONV_WIDTH]

    grads = {"w_ada": g_w_ada, "conv_w": g_conv_w, "w_pw": g_w_pw, "w_out": g_w_out}
    params = {"w_ada": (w_ada, m_w_ada, v_w_ada), "conv_w": (conv_w, m_conv_w, v_conv_w),
              "w_pw": (w_pw, m_w_pw, v_w_pw), "w_out": (w_out, m_w_out, v_w_out)}
    res = {}
    for name, g in grads.items():
        w_, m_, v_ = params[name]
        d_, nm_, nv_ = _adamw(w_[0], g, m_[0], v_[0], "adamw_" + name)
        res[name] = (g[None], d_[None], nm_[None], nv_[None])
    upd = _adamw(w_in[0].T, gt_w_in, m_w_in[0].T, v_w_in[0].T, "adamw_w_in")
    res["w_in"] = tuple(a.T[None] for a in (gt_w_in, *upd))

    small_params = {"b_ada": (b_ada, m_b_ada, v_b_ada), "b_pw": (b_pw, m_b_pw, v_b_pw),
                    "conv_ln_g": (conv_ln_g, m_conv_ln_g, v_conv_ln_g),
                    "conv_ln_b": (conv_ln_b, m_conv_ln_b, v_conv_ln_b), "conv_b": (conv_b, m_conv_b, v_conv_b),
                    "ln_g": (ln_g, m_ln_g, v_ln_g), "ln_b": (ln_b, m_ln_b, v_ln_b),
                    "rel_bias": (rel_bias, m_rel_bias, v_rel_bias), "sinks": (sinks, m_sinks, v_sinks)}
    to_small = lambda n, a: a.T if n == "rel_bias" else a
    key_of = {"b_ada": "dmod"}
    packs = []
    for j in range(3):
        fields = {key_of.get(n, n): to_small(n, t[j]) for n, t in small_params.items()}
        fields["loss"] = jnp.zeros((1, 1), F32)
        packs.append(_pack(fields))
    gsum = summed
    d_s, nm_s, nv_s = _adamw(packs[0], gsum, packs[1], packs[2], "adamw_small")
    outs_small = [_unpack(a) for a in (gsum, d_s, nm_s, nv_s)]
    for n, t in small_params.items():
        shape = t[0].shape
        vals = []
        for o in outs_small:
            a = o[key_of.get(n, n)]
            if n == "rel_bias":
                a = a.reshape(N_Q_HEADS, N_BUCKETS).T
            else:
                a = a[:, :shape[1]].reshape(shape)
            vals.append(a)
        res[n] = tuple(vals)

    order = ["w_ada", "b_ada", "w_in", "rel_bias", "sinks", "conv_w", "conv_b", "conv_ln_g", "conv_ln_b", "w_pw",
             "b_pw", "w_out", "ln_g", "ln_b"]
    out = [loss, grad_x[None]]
    for j in range(4):
        out += [res[n][j] for n in order]
    return tuple(out)
```

```python
import functools
import math

import jax
import jax.numpy as jnp
import numpy as np
from jax import lax
from jax.experimental import pallas as pl
from jax.experimental.pallas import tpu as pltpu

F32, BF16, I32 = jnp.float32, jnp.bfloat16, jnp.int32

D_MODEL = 2048
D_ATTN = 1024
D_CONV = 1024
D_KV = 256
HEAD_DIM = 64
N_Q_HEADS = 16
N_KV_HEADS = 4
GQA = 4
BLOCK = 128
CONV_WIDTH = 31
CONV_ROWS = 32
HALO = 32
N_BUCKETS = 32
MAX_DISTANCE = 128
LN_EPS = 1e-5
ALPHA = 2.0 ** 0.25
D_IN = 5632
N_DEV = 8
NEG = -1e30

ADAM_LR, ADAM_B1, ADAM_B2, ADAM_EPS, ADAM_WD, ADAM_STEP = 0.001, 0.9, 0.999, 1e-08, 0.01, 10

NT_DIMS = (((1,), (1,)), ((), ()))
TN_DIMS = (((0,), (0,)), ((), ()))
MIB = 1 << 20


def _pc(body, name, **kw):
    return pl.pallas_call(body, name=name, **kw)


def _cp(vmem_mib=None, sem=None):
    kw = {}
    if vmem_mib is not None:
        kw["vmem_limit_bytes"] = vmem_mib * MIB
    if sem is not None:
        kw["dimension_semantics"] = sem
    return pltpu.CompilerParams(**kw)


def _sig(x):
    return 1.0 / (1.0 + jnp.exp(-x))


def _dsilu(x, s):
    return s * (1.0 + x * (1.0 - s))


def _me():
    return lax.axis_index("x"), lax.axis_index("y"), lax.axis_index("c")


def _peer(k):
    x, y, c = _me()
    px = 1 - x if k & 4 else x
    py = 1 - y if k & 2 else y
    pc = 1 - c if k & 1 else c
    return (px, py, pc), 4 * px + 2 * py + pc


def _remote(src, dst, send_sem, recv_sem, dev):
    return pltpu.make_async_remote_copy(src_ref=src, dst_ref=dst, send_sem=send_sem, recv_sem=recv_sem,
                                        device_id=dev, device_id_type=pl.DeviceIdType.MESH)


HBM_SPEC = pl.BlockSpec(memory_space=pl.ANY)
VMEM_SPEC = pl.BlockSpec(memory_space=pltpu.VMEM)


def _comm_scratch(nt):
    return [pltpu.SemaphoreType.DMA((nt, N_DEV - 1)), pltpu.SemaphoreType.DMA((nt, N_DEV - 1)),
            pltpu.SemaphoreType.DMA((nt,))]


def _gather_shapes(shards):
    return [jax.ShapeDtypeStruct((N_DEV * s.shape[0], s.shape[1]), s.dtype) for s in shards]


def _block(ref, blk, rows):
    return ref.at[pl.ds(pl.multiple_of(blk * rows, 8), rows), :]


def _gather_copies(src, dst, sems):
    send_sems, recv_sems, local_sems = sems
    x, y, c = _me()
    me = 4 * x + 2 * y + c
    nt = len(src)
    local = [pltpu.make_async_copy(src[t], _block(dst[t], me, src[t].shape[0]), local_sems.at[t]) for t in range(nt)]
    sends, arrivals = [], []
    for k in range(1, N_DEV):
        dev, blk = _peer(k)
        for t in range(nt):
            r = src[t].shape[0]
            pair = (send_sems.at[t, k - 1], recv_sems.at[t, k - 1], dev)
            sends.append(_remote(src[t], _block(dst[t], me, r), *pair))
            arrivals.append(_remote(src[t], _block(dst[t], blk, r), *pair))
    return local, sends, arrivals


def _scatter_copies(src, dst, sems):
    send_sems, recv_sems, local_sems = sems
    x, y, c = _me()
    me = 4 * x + 2 * y + c
    nt = len(src)
    rows = [s.shape[0] // N_DEV for s in src]
    local = [pltpu.make_async_copy(_block(src[t], me, rows[t]), _block(dst[t], me, rows[t]), local_sems.at[t])
             for t in range(nt)]
    sends, arrivals = [], []
    for k in range(1, N_DEV):
        dev, blk = _peer(k)
        for t in range(nt):
            pair = (send_sems.at[t, k - 1], recv_sems.at[t, k - 1], dev)
            sends.append(_remote(_block(src[t], blk, rows[t]), _block(dst[t], me, rows[t]), *pair))
            arrivals.append(_remote(_block(src[t], me, rows[t]), _block(dst[t], blk, rows[t]), *pair))
    return local, sends, arrivals


def _comm_start(cps):
    local, sends, _ = cps
    for cp in local + sends:
        cp.start()


def _comm_wait(cps):
    local, sends, arrivals = cps
    for cp in arrivals:
        cp.wait_recv()
    for cp in sends:
        cp.wait_send()
    for cp in local:
        cp.wait()


IN_PIECES = ((0, 1536, BF16), (1536, 1024, F32), (2560, 2048, F32), (4608, 1024, F32))


def _inproj(x, mod, wt, gather):
    S = x.shape[0]
    tm = min(256, S)
    ni = S // tm
    ng = len(gather)
    npc = len(IN_PIECES)

    def body(*refs):
        x_ref, sh_ref, sc_ref, w_ref = refs[:4]
        outs = refs[4 + ng:]
        h_ref, piece_refs = outs[0], outs[1:1 + npc]
        i = pl.program_id(0)
        if ng:
            cps = _gather_copies(refs[4:4 + ng], outs[1 + npc:1 + npc + ng], outs[1 + npc + ng:])

            @pl.when(i == 0)
            def _():
                _comm_start(cps)

        h = (x_ref[...] * (1.0 + sc_ref[...]) + sh_ref[...]).astype(BF16)
        h_ref[...] = h
        for (r0, n, dt), o_ref in zip(IN_PIECES, piece_refs):
            o_ref[...] = lax.dot_general(h, w_ref[r0:r0 + n, :], NT_DIMS, preferred_element_type=F32).astype(dt)
        if ng:
            @pl.when(i == ni - 1)
            def _():
                _comm_wait(cps)

    tile = lambda n: pl.BlockSpec((tm, n), lambda i: (i, 0))
    res = _pc(
        body, "inproj", grid=(ni,),
        in_specs=[tile(D_MODEL),
                  pl.BlockSpec((1, D_MODEL), lambda i: (0, 0)),
                  pl.BlockSpec((1, D_MODEL), lambda i: (0, 1)),
                  pl.BlockSpec((D_IN, D_MODEL), lambda i: (0, 0), pipeline_mode=pl.Buffered(1))] + [HBM_SPEC] * ng,
        out_specs=[tile(D_MODEL)] + [tile(n) for _, n, _ in IN_PIECES] + [HBM_SPEC] * ng,
        out_shape=[jax.ShapeDtypeStruct((S, D_MODEL), BF16)]
        + [jax.ShapeDtypeStruct((S, n), dt) for _, n, dt in IN_PIECES] + _gather_shapes(gather),
        scratch_shapes=_comm_scratch(ng) if ng else [],
        compiler_params=_cp(56, ("arbitrary",)),
    )(x, mod, mod, wt, *gather)
    return res[:1 + npc], res[1 + npc:]


def _matmul_tn(a, b, out_prev, row_off, m_total, name):
    S, M = a.shape
    N = b.shape[1]
    tm, ts = 512, min(2048, S)
    assert row_off % tm == 0 and M % tm == 0
    ob = row_off // tm
    ns = S // ts

    def body(*refs):
        a_ref, b_ref = refs[0], refs[1]
        o_ref, acc = refs[-2], refs[-1]
        s = pl.program_id(1)

        @pl.when(s == 0)
        def _():
            acc[...] = jnp.zeros_like(acc)

        acc[...] += lax.dot_general(a_ref[...], b_ref[...], TN_DIMS, preferred_element_type=F32)

        @pl.when(s == ns - 1)
        def _():
            o_ref[...] = acc[...].astype(BF16)

    in_specs = [pl.BlockSpec((ts, tm), lambda i, s: (s, i)),
                pl.BlockSpec((ts, N), lambda i, s: (s, 0))]
    args = [a, b]
    aliases = {}
    if out_prev is not None:
        in_specs.append(pl.BlockSpec(memory_space=pl.ANY))
        args.append(out_prev)
        aliases = {2: 0}
    return _pc(
        body, name, grid=(M // tm, ns),
        in_specs=in_specs,
        out_specs=pl.BlockSpec((tm, N), lambda i, s: (ob + i, 0)),
        out_shape=jax.ShapeDtypeStruct((m_total, N), BF16),
        scratch_shapes=[pltpu.VMEM((tm, N), F32)],
        input_output_aliases=aliases,
        compiler_params=_cp(48, ("parallel", "arbitrary")),
    )(*args)


def _bucket_map():
    qi = jnp.arange(BLOCK, dtype=I32)[:, None]
    kj = jnp.arange(2 * BLOCK, dtype=I32)[None, :]
    dist = qi + BLOCK - kj
    in_window = (dist >= 0) & (dist < BLOCK)
    d0 = jnp.maximum(dist, 0)
    max_exact = N_BUCKETS // 2
    d = jnp.maximum(d0, 1).astype(F32)
    large = max_exact + (jnp.log(d / max_exact) / math.log(MAX_DISTANCE / max_exact)
                         * (N_BUCKETS - max_exact)).astype(I32)
    large = jnp.minimum(large, N_BUCKETS - 1)
    bucket = jnp.where(d0 < max_exact, d0, large)
    return jnp.where(in_window, bucket, -1).astype(I32)


def _bias_table(rel_bias, bmap):
    def body(rb_ref, bm_ref, o_ref):
        h = pl.program_id(0)
        bm = bm_ref[...]
        acc = jnp.full((BLOCK, 2 * BLOCK), NEG, F32)
        for b in range(N_BUCKETS):
            acc = jnp.where(bm == b, rb_ref[b, h], acc)
        kj = lax.broadcasted_iota(I32, (BLOCK, 2 * BLOCK), 1)
        o_ref[0, 0] = jnp.where(kj >= BLOCK, acc, NEG)
        o_ref[1, 0] = acc

    return _pc(
        body, "bias_table", grid=(N_Q_HEADS,),
        in_specs=[pl.BlockSpec(memory_space=pltpu.SMEM),
                  pl.BlockSpec((BLOCK, 2 * BLOCK), lambda h: (0, 0))],
        out_specs=pl.BlockSpec((2, 1, BLOCK, 2 * BLOCK), lambda h: (0, h, 0, 0)),
        out_shape=jax.ShapeDtypeStruct((2, N_Q_HEADS, BLOCK, 2 * BLOCK), F32),
    )(rel_bias, bmap)


def _bias_spec():
    return pl.BlockSpec((2, N_Q_HEADS, BLOCK, 2 * BLOCK), lambda n: (0, 0, 0, 0))


def _relbias_grad(dbias, bmap):
    def body(db_ref, bm_ref, o_ref):
        bm = bm_ref[...]
        x = db_ref[0]
        lane = lax.broadcasted_iota(I32, (1, 128), 1)
        row = jnp.zeros((1, 128), F32)
        for b in range(N_BUCKETS):
            row = jnp.where(lane == b, jnp.sum(jnp.where(bm == b, x, 0.0)), row)
        o_ref[0] = jnp.broadcast_to(row, (8, 128))

    out = _pc(
        body, "relbias_grad", grid=(N_Q_HEADS,),
        in_specs=[pl.BlockSpec((1, BLOCK, 2 * BLOCK), lambda h: (h, 0, 0)),
                  pl.BlockSpec((BLOCK, 2 * BLOCK), lambda h: (0, 0))],
        out_specs=pl.BlockSpec((1, 8, 128), lambda h: (h, 0, 0)),
        out_shape=jax.ShapeDtypeStruct((N_Q_HEADS, 8, 128), F32),
    )(dbias, bmap)
    return out[:, 0, :]


Q_SCALE = HEAD_DIM ** -0.5
assert math.frexp(Q_SCALE)[0] == 0.5


def _stack_heads(ref, hk):
    return jnp.concatenate([ref[:, pl.ds(256 * hk + 64 * g, 64)] for g in range(GQA)], axis=0) * Q_SCALE


def _sink_col(sink_ref, hk):
    row = lax.broadcasted_iota(I32, (GQA * BLOCK, 1), 0)
    s = jnp.full((GQA * BLOCK, 1), sink_ref[0, 4 * hk + 3], F32)
    for g in (2, 1, 0):
        s = jnp.where(row < (g + 1) * BLOCK, sink_ref[0, 4 * hk + g], s)
    return s


def _attn_probs(q4s, kw, bias_ref, table, sink_ref, hk):
    s = lax.dot_general(q4s, kw, NT_DIMS, preferred_element_type=F32)
    s = s + bias_ref[table, 4 * hk:4 * hk + 4].reshape(GQA * BLOCK, 2 * BLOCK)
    sink = _sink_col(sink_ref, hk)
    m = jnp.maximum(jnp.max(s, axis=1, keepdims=True), sink)
    e = jnp.exp(s - m)
    es = jnp.exp(sink - m)
    inv = 1.0 / (jnp.sum(e, axis=1, keepdims=True) + es)
    return e * inv, es * inv


def _attn_bwd(qkv, g_attn, a_out, dycat, bias, sinks, scatter=()):
    S = qkv.shape[0]
    nb = S // BLOCK
    R = GQA * BLOCK
    ns = len(scatter)

    def body(*refs):
        (q_ref, kc_ref, kp_ref, vc_ref, vp_ref, g_ref, a_ref, dy_ref, bias_ref, sink_ref) = refs[:10]
        dqkv_ref, dg_ref, dbias_ref, dsink_ref = refs[10 + ns:14 + ns]
        dq_scr, dq_new, dk_scr, dv_scr, dkw_scr, dvw_scr, ds_scr = refs[14 + 2 * ns:21 + 2 * ns]
        n = pl.program_id(0)
        table = jnp.minimum(n, 1)
        if ns:
            cps = _scatter_copies(refs[10:10 + ns], refs[14 + ns:14 + 2 * ns], refs[21 + 2 * ns:])

            @pl.when(n == 0)
            def _():
                _comm_start(cps)

        @pl.when(n == 0)
        def _():
            dbias_ref[...] = jnp.zeros_like(dbias_ref)
            ds_scr[...] = jnp.zeros_like(ds_scr)
            dq_scr[...] = jnp.zeros_like(dq_scr)
            dk_scr[...] = jnp.zeros_like(dk_scr)
            dv_scr[...] = jnp.zeros_like(dv_scr)

        @pl.when(n < nb)
        def _():
            for hk in range(N_KV_HEADS):
                q4 = _stack_heads(q_ref, hk)
                ks = pl.ds(64 * hk, 64)
                kw = jnp.concatenate([kp_ref[:, ks], kc_ref[:, ks]], axis=0)
                vw = jnp.concatenate([vp_ref[:, ks], vc_ref[:, ks]], axis=0)
                p, psink = _attn_probs(q4, kw, bias_ref, table, sink_ref, hk)
                da_parts, a_parts = [], []
                for g in range(GQA):
                    sl = pl.ds(256 * hk + 64 * g, 64)
                    gg = g_ref[:, sl]
                    sg = _sig(gg)
                    dyg = dy_ref[:, sl]
                    ag = a_ref[:, sl]
                    da_parts.append(dyg * (gg * sg))
                    a_parts.append(ag)
                    dg_ref[:, sl] = (dyg * ag * _dsilu(gg, sg)).astype(BF16)
                da4 = jnp.concatenate(da_parts, axis=0)
                a4 = jnp.concatenate(a_parts, axis=0)
                delta = jnp.sum(da4 * a4, axis=1, keepdims=True)
                da4b = da4.astype(BF16)
                dp = lax.dot_general(da4b, vw, NT_DIMS, preferred_element_type=F32)
                ds = p * (dp - delta)
                ds_scr[hk] += -psink * delta
                dbias_ref[4 * hk:4 * hk + 4] += ds.reshape(GQA, BLOCK, 2 * BLOCK)
                dsb = ds.astype(BF16)
                dq4 = jnp.dot(dsb, kw, preferred_element_type=F32) * Q_SCALE
                for g in range(GQA):
                    dq_new[:, pl.ds(256 * hk + 64 * g, 64)] = dq4[BLOCK * g:BLOCK * (g + 1)].astype(BF16)
                dkw_scr[:, ks] = lax.dot_general(dsb, q4, TN_DIMS, preferred_element_type=F32)
                dvw_scr[:, ks] = lax.dot_general(p.astype(BF16), da4b, TN_DIMS, preferred_element_type=F32)

        @pl.when(n == nb)
        def _():
            dkw_scr[0:BLOCK, :] = jnp.zeros((BLOCK, D_KV), F32)
            dvw_scr[0:BLOCK, :] = jnp.zeros((BLOCK, D_KV), F32)

        dqkv_ref[:, 0:D_ATTN] = dq_scr[...]
        dqkv_ref[:, D_ATTN:D_ATTN + D_KV] = (dk_scr[...] + dkw_scr[0:BLOCK, :]).astype(BF16)
        dqkv_ref[:, D_ATTN + D_KV:D_ATTN + 2 * D_KV] = (dv_scr[...] + dvw_scr[0:BLOCK, :]).astype(BF16)
        dq_scr[...] = dq_new[...]
        dk_scr[...] = dkw_scr[BLOCK:2 * BLOCK, :]
        dv_scr[...] = dvw_scr[BLOCK:2 * BLOCK, :]

        @pl.when(n == nb)
        def _():
            lane = lax.broadcasted_iota(I32, (1, 128), 1)
            row = jnp.zeros((1, 128), F32)
            for hk in range(N_KV_HEADS):
                col = ds_scr[hk]
                for g in range(GQA):
                    row = jnp.where(lane == 4 * hk + g, jnp.sum(col[BLOCK * g:BLOCK * (g + 1)]), row)
            dsink_ref[...] = jnp.broadcast_to(row, (8, 128))
            if ns:
                _comm_wait(cps)

    cur = lambda n: jnp.minimum(n, nb - 1)
    prev = lambda n: jnp.clip(n - 1, 0, nb - 1)
    res = _pc(
        body, "attn_bwd", grid=(nb + 1,),
        in_specs=[pl.BlockSpec((BLOCK, D_ATTN), lambda n: (cur(n), 0)),
                  pl.BlockSpec((BLOCK, D_KV), lambda n: (cur(n), 4)),
                  pl.BlockSpec((BLOCK, D_KV), lambda n: (prev(n), 4)),
                  pl.BlockSpec((BLOCK, D_KV), lambda n: (cur(n), 5)),
                  pl.BlockSpec((BLOCK, D_KV), lambda n: (prev(n), 5)),
                  pl.BlockSpec((BLOCK, D_ATTN), lambda n: (cur(n), 0)),
                  pl.BlockSpec((BLOCK, D_ATTN), lambda n: (cur(n), 0)),
                  pl.BlockSpec((BLOCK, D_ATTN), lambda n: (cur(n), 0)),
                  _bias_spec(),
                  pl.BlockSpec(memory_space=pltpu.SMEM)] + [HBM_SPEC] * ns,
        out_specs=[pl.BlockSpec((BLOCK, D_ATTN + 2 * D_KV), lambda n: (prev(n), 0)),
                   pl.BlockSpec((BLOCK, D_ATTN), lambda n: (cur(n), 0)),
                   pl.BlockSpec((N_Q_HEADS, BLOCK, 2 * BLOCK), lambda n: (0, 0, 0)),
                   pl.BlockSpec((8, 128), lambda n: (0, 0))] + [HBM_SPEC] * ns,
        out_shape=[jax.ShapeDtypeStruct((S, D_ATTN + 2 * D_KV), BF16),
                   jax.ShapeDtypeStruct((S, D_ATTN), BF16),
                   jax.ShapeDtypeStruct((N_Q_HEADS, BLOCK, 2 * BLOCK), F32),
                   jax.ShapeDtypeStruct((8, 128), F32)] + [jax.ShapeDtypeStruct(f.shape, f.dtype) for f in scatter],
        scratch_shapes=[pltpu.VMEM((BLOCK, D_ATTN), BF16), pltpu.VMEM((BLOCK, D_ATTN), BF16),
                        pltpu.VMEM((BLOCK, D_KV), F32), pltpu.VMEM((BLOCK, D_KV), F32),
                        pltpu.VMEM((2 * BLOCK, D_KV), F32), pltpu.VMEM((2 * BLOCK, D_KV), F32),
                        pltpu.VMEM((N_KV_HEADS, R, 1), F32)] + (_comm_scratch(ns) if ns else []),
        compiler_params=_cp(48, ("arbitrary",)),
    )(qkv, qkv, qkv, qkv, qkv, g_attn, a_out, dycat, bias, sinks, *scatter)
    return res[:4], res[4:]


def _conv_tile(S):
    return min(256, S)


def _shifted(win, s):
    return win if s == 0 else pltpu.roll(win, win.shape[0] - s, axis=0)


def _conv_taps(win, cw_ref, lanes, rows):
    acc = jnp.zeros((rows, win.shape[1]), F32)
    for s in range(8):
        ws = _shifted(win, s)
        for aa in range(5):
            j = 8 * aa + s - 2
            if 0 <= j < CONV_WIDTH:
                acc = acc + ws[8 * aa:8 * aa + rows] * cw_ref[j:j + 1, lanes]
    return acc


def _mixer_fwd(qkv, g_attn, glu, g_conv, bias, sinks, conv_w, conv_b, ln_g, ln_b, w_pw, b_pw):
    S = qkv.shape[0]
    nb = S // BLOCK
    hb = BLOCK // HALO
    QC = D_CONV // N_KV_HEADS

    def body(q_ref, kc_ref, kp_ref, vc_ref, vp_ref, g_ref, bias_ref, sink_ref,
             a_in, ah_in, b_in, bh_in, gc_ref, cw_ref, cb_ref, lg_ref, lb_ref, wpw_ref, bpw_ref,
             y_ref, a_ref, u1_ref, u3_ref, p_ref, win):
        n = pl.program_id(0)
        table = jnp.minimum(n, 1)
        win[0:HALO, :] = jnp.where(n > 0, ah_in[...] * _sig(bh_in[...]), 0.0)
        win[HALO:HALO + BLOCK, :] = a_in[...] * _sig(b_in[...])
        for hk in range(N_KV_HEADS):
            q4 = _stack_heads(q_ref, hk)
            ks = pl.ds(64 * hk, 64)
            kw = jnp.concatenate([kp_ref[:, ks], kc_ref[:, ks]], axis=0)
            vw = jnp.concatenate([vp_ref[:, ks], vc_ref[:, ks]], axis=0)
            p, _ = _attn_probs(q4, kw, bias_ref, table, sink_ref, hk)
            o4 = jnp.dot(p.astype(BF16), vw, preferred_element_type=F32)
            for g in range(GQA):
                sl = pl.ds(256 * hk + 64 * g, 64)
                og = o4[BLOCK * g:BLOCK * (g + 1)]
                gg = g_ref[:, sl]
                a_ref[:, sl] = og
                y_ref[:, sl] = (og * (gg * _sig(gg))).astype(BF16)
            lanes = pl.ds(QC * hk, QC)
            u1_ref[:, lanes] = _conv_taps(win[:, lanes], cw_ref, lanes, BLOCK) + cb_ref[:, lanes]
        u1 = u1_ref[...]
        mu = jnp.mean(u1, axis=1, keepdims=True)
        uc = u1 - mu
        rstd = lax.rsqrt(jnp.mean(uc * uc, axis=1, keepdims=True) + LN_EPS)
        u2 = uc * rstd * lg_ref[...] + lb_ref[...]
        u3 = (u2 * _sig(u2)).astype(BF16)
        u3_ref[...] = u3
        pw = jnp.dot(u3, wpw_ref[...], preferred_element_type=F32) + bpw_ref[...]
        p_ref[...] = pw
        gc = gc_ref[...]
        y_ref[:, D_ATTN:] = (pw * (gc * _sig(gc))).astype(BF16)

    prev = lambda n: jnp.maximum(n - 1, 0)
    halo = lambda n: jnp.maximum(n * hb - 1, 0)
    vec = pl.BlockSpec((1, D_CONV), lambda n: (0, 0))
    blk = lambda w, j: pl.BlockSpec((BLOCK, w), lambda n: (n, j))
    return _pc(
        body, "mixer_fwd", grid=(nb,),
        in_specs=[blk(D_ATTN, 0),
                  blk(D_KV, 4), pl.BlockSpec((BLOCK, D_KV), lambda n: (prev(n), 4)),
                  blk(D_KV, 5), pl.BlockSpec((BLOCK, D_KV), lambda n: (prev(n), 5)),
                  blk(D_ATTN, 0),
                  _bias_spec(),
                  pl.BlockSpec(memory_space=pltpu.SMEM),
                  blk(D_CONV, 0), pl.BlockSpec((HALO, D_CONV), lambda n: (halo(n), 0)),
                  blk(D_CONV, 1), pl.BlockSpec((HALO, D_CONV), lambda n: (halo(n), 1)),
                  blk(D_CONV, 0),
                  pl.BlockSpec((CONV_ROWS, D_CONV), lambda n: (0, 0)),
                  vec, vec, vec,
                  pl.BlockSpec((D_CONV, D_CONV), lambda n: (0, 0)),
                  vec],
        out_specs=[blk(2 * D_ATTN, 0), blk(D_ATTN, 0), blk(D_CONV, 0), blk(D_CONV, 0), blk(D_CONV, 0)],
        out_shape=[jax.ShapeDtypeStruct((S, 2 * D_ATTN), BF16),
                   jax.ShapeDtypeStruct((S, D_ATTN), F32),
                   jax.ShapeDtypeStruct((S, D_CONV), F32),
                   jax.ShapeDtypeStruct((S, D_CONV), BF16),
                   jax.ShapeDtypeStruct((S, D_CONV), F32)],
        scratch_shapes=[pltpu.VMEM((BLOCK + HALO, D_CONV), F32)],
        compiler_params=_cp(48, ("parallel",)),
    )(qkv, qkv, qkv, qkv, qkv, g_attn, bias, sinks, glu, glu, glu, glu, g_conv, conv_w, conv_b, ln_g, ln_b, w_pw,
      b_pw)


def _conv_bwd_a(dycat, g_conv, p_out, u1, ln_g, ln_b, w_pw):
    S = u1.shape[0]
    T = _conv_tile(S)

    def body(dy_ref, gc_ref, p_ref, u1_ref, lg_ref, lb_ref, wpw_ref,
             dp_ref, dgc_ref, du1_ref, gbpw_ref, glg_ref, glb_ref, gcb_ref):
        i = pl.program_id(0)

        @pl.when(i == 0)
        def _():
            for r in (gbpw_ref, glg_ref, glb_ref, gcb_ref):
                r[...] = jnp.zeros_like(r)

        dy = dy_ref[...]
        gc = gc_ref[...]
        sg = _sig(gc)
        dp = dy * (gc * sg)
        dgc_ref[...] = (dy * p_ref[...] * _dsilu(gc, sg)).astype(BF16)
        gbpw_ref[...] += jnp.sum(dp, axis=0, keepdims=True)
        dpb = dp.astype(BF16)
        dp_ref[...] = dpb
        du3 = lax.dot_general(dpb, wpw_ref[...], NT_DIMS, preferred_element_type=F32)
        u1 = u1_ref[...]
        mu = jnp.mean(u1, axis=1, keepdims=True)
        uc = u1 - mu
        rstd = lax.rsqrt(jnp.mean(uc * uc, axis=1, keepdims=True) + LN_EPS)
        uh = uc * rstd
        lg = lg_ref[...]
        u2 = uh * lg + lb_ref[...]
        s2 = _sig(u2)
        du2 = du3 * _dsilu(u2, s2)
        glg_ref[...] += jnp.sum(du2 * uh, axis=0, keepdims=True)
        glb_ref[...] += jnp.sum(du2, axis=0, keepdims=True)
        duh = du2 * lg
        du1 = rstd * (duh - jnp.mean(duh, axis=1, keepdims=True) - uh * jnp.mean(duh * uh, axis=1, keepdims=True))
        du1_ref[...] = du1
        gcb_ref[...] += jnp.sum(du1, axis=0, keepdims=True)

    vec = pl.BlockSpec((1, D_CONV), lambda i: (0, 0))
    tile = pl.BlockSpec((T, D_CONV), lambda i: (i, 0))
    vshape = jax.ShapeDtypeStruct((1, D_CONV), F32)
    return _pc(
        body, "conv_bwd_a", grid=(S // T,),
        in_specs=[pl.BlockSpec((T, D_CONV), lambda i: (i, 1)), tile, tile, tile, vec, vec,
                  pl.BlockSpec((D_CONV, D_CONV), lambda i: (0, 0))],
        out_specs=[tile, tile, tile, vec, vec, vec, vec],
        out_shape=[jax.ShapeDtypeStruct((S, D_CONV), BF16), jax.ShapeDtypeStruct((S, D_CONV), BF16),
                   jax.ShapeDtypeStruct((S, D_CONV), F32), vshape, vshape, vshape, vshape],
        compiler_params=_cp(48, ("arbitrary",)),
    )(dycat, g_conv, p_out, u1, ln_g, ln_b, w_pw)


def _conv_bwd_b(du1, glu, conv_w):
    S = du1.shape[0]
    T = _conv_tile(S)
    hb = T // HALO
    nt = S // T
    last_h = S // HALO - 1

    def body(du_ref, dun_ref, a_ref, b_ref, cw_ref, dab_ref, gw_ref):
        i = pl.program_id(0)

        @pl.when(i == 0)
        def _():
            gw_ref[...] = jnp.zeros_like(gw_ref)

        for lg in range(D_CONV // 128):
            lanes = pl.ds(128 * lg, 128)
            a = a_ref[:, lanes]
            sb = _sig(b_ref[:, lanes])
            u0 = a * sb
            du1n = jnp.where(i < nt - 1, dun_ref[:, lanes], 0.0)
            win2 = jnp.concatenate([du_ref[:, lanes], du1n], axis=0)
            acc = jnp.zeros((T, 128), F32)
            for s in range(8):
                w2 = _shifted(win2, s)
                for aa in range(4):
                    j = CONV_WIDTH - 1 - (8 * aa + s)
                    if 0 <= j < CONV_WIDTH:
                        xo = w2[8 * aa:8 * aa + T]
                        acc = acc + xo * cw_ref[j:j + 1, lanes]
                        gw_ref[j:j + 1, lanes] += jnp.sum(xo * u0, axis=0, keepdims=True)
            dab_ref[:, lanes] = (acc * sb).astype(BF16)
            dab_ref[:, pl.ds(D_CONV + 128 * lg, 128)] = (acc * a * sb * (1.0 - sb)).astype(BF16)

    nxt = lambda i: jnp.minimum((i + 1) * hb, last_h)
    return _pc(
        body, "conv_bwd_b", grid=(nt,),
        in_specs=[pl.BlockSpec((T, D_CONV), lambda i: (i, 0)),
                  pl.BlockSpec((HALO, D_CONV), lambda i: (nxt(i), 0)),
                  pl.BlockSpec((T, D_CONV), lambda i: (i, 0)),
                  pl.BlockSpec((T, D_CONV), lambda i: (i, 1)),
                  pl.BlockSpec((CONV_ROWS, D_CONV), lambda i: (0, 0))],
        out_specs=[pl.BlockSpec((T, 2 * D_CONV), lambda i: (i, 0)),
                   pl.BlockSpec((CONV_ROWS, D_CONV), lambda i: (0, 0))],
        out_shape=[jax.ShapeDtypeStruct((S, 2 * D_CONV), BF16),
                   jax.ShapeDtypeStruct((CONV_ROWS, D_CONV), F32)],
        compiler_params=_cp(48, ("arbitrary",)),
    )(du1, du1, glu, glu, conv_w)


def _outproj_ln(ycat, w_out, x, target, mod, ln_g, ln_b):
    S = x.shape[0]
    tm = min(256, S)

    def body(yc_ref, w_ref, x_ref, t_ref, gate_ref, lg_ref, lb_ref,
             dz_ref, dy_ref, dyc_ref, loss_ref, glg_ref, glb_ref, dgate_ref):
        i = pl.program_id(0)

        @pl.when(i == 0)
        def _():
            for r in (loss_ref, glg_ref, glb_ref, dgate_ref):
                r[...] = jnp.zeros_like(r)

        w = w_ref[...]
        y = jnp.dot(yc_ref[...], w, preferred_element_type=F32)
        gate = gate_ref[...]
        z = ALPHA * x_ref[...] + gate * y
        mu = jnp.mean(z, axis=1, keepdims=True)
        zc = z - mu
        rstd = lax.rsqrt(jnp.mean(zc * zc, axis=1, keepdims=True) + LN_EPS)
        zh = zc * rstd
        lg = lg_ref[...]
        err = zh * lg + lb_ref[...] - t_ref[...]
        loss_ref[...] += 0.5 * jnp.sum(jnp.sum(err * err, axis=1, keepdims=True)) / D_MODEL
        dout = err * (1.0 / D_MODEL)
        glg_ref[...] += jnp.sum(dout * zh, axis=0, keepdims=True)
        glb_ref[...] += jnp.sum(dout, axis=0, keepdims=True)
        dzh = dout * lg
        dz = rstd * (dzh - jnp.mean(dzh, axis=1, keepdims=True) - zh * jnp.mean(dzh * zh, axis=1, keepdims=True))
        dz_ref[...] = dz
        dgate_ref[...] += jnp.sum(dz * y, axis=0, keepdims=True)
        dy = (dz * gate).astype(BF16)
        dy_ref[...] = dy
        dyc_ref[...] = lax.dot_general(dy, w, NT_DIMS, preferred_element_type=F32).astype(BF16)

    vec = pl.BlockSpec((1, D_MODEL), lambda i: (0, 0))
    tile = pl.BlockSpec((tm, D_MODEL), lambda i: (i, 0))
    vshape = jax.ShapeDtypeStruct((1, D_MODEL), F32)
    return _pc(
        body, "outproj_ln", grid=(S // tm,),
        in_specs=[tile, pl.BlockSpec((D_MODEL, D_MODEL), lambda i: (0, 0)), tile, tile,
                  pl.BlockSpec((1, D_MODEL), lambda i: (0, 2)), vec, vec],
        out_specs=[tile, tile, tile, pl.BlockSpec((8, 128), lambda i: (0, 0)), vec, vec, vec],
        out_shape=[jax.ShapeDtypeStruct((S, D_MODEL), F32), jax.ShapeDtypeStruct((S, D_MODEL), BF16),
                   jax.ShapeDtypeStruct((S, D_MODEL), BF16), jax.ShapeDtypeStruct((8, 128), F32),
                   vshape, vshape, vshape],
        compiler_params=_cp(56, ("arbitrary",)),
    )(ycat, w_out, x, target, mod, ln_g, ln_b)


def _dh_kernel(segs, wt, dz, x, mod, scatter=()):
    S = x.shape[0]
    tm = min(256, S)
    row0 = [0]
    for a in segs:
        row0.append(row0[-1] + a.shape[1])
    assert row0[-1] == wt.shape[0]
    nseg = len(segs)
    ns = len(scatter)
    ni = S // tm

    def body(*refs):
        seg_refs = refs[:nseg]
        w_ref, dz_ref, x_ref, sc_ref = refs[nseg:nseg + 4]
        outs = refs[nseg + 4 + ns:]
        gx_ref, dsh_ref, dsc_ref = outs[:3]
        i = pl.program_id(0)
        if ns:
            cps = _scatter_copies(refs[nseg + 4:nseg + 4 + ns], outs[3:3 + ns], outs[3 + ns:])

        @pl.when(i == 0)
        def _():
            dsh_ref[...] = jnp.zeros_like(dsh_ref)
            dsc_ref[...] = jnp.zeros_like(dsc_ref)
            if ns:
                _comm_start(cps)

        dh = jnp.dot(seg_refs[0][...], w_ref[row0[0]:row0[1], :], preferred_element_type=F32)
        for t in range(1, nseg):
            dh = dh + jnp.dot(seg_refs[t][...], w_ref[row0[t]:row0[t + 1], :], preferred_element_type=F32)
        gx_ref[...] = ALPHA * dz_ref[...] + dh * (1.0 + sc_ref[...])
        dsh_ref[...] += jnp.sum(dh, axis=0, keepdims=True)
        dsc_ref[...] += jnp.sum(dh * x_ref[...], axis=0, keepdims=True)

        if ns:
            @pl.when(i == ni - 1)
            def _():
                _comm_wait(cps)

    tile = pl.BlockSpec((tm, D_MODEL), lambda i: (i, 0))
    vec = pl.BlockSpec((1, D_MODEL), lambda i: (0, 0))
    vshape = jax.ShapeDtypeStruct((1, D_MODEL), F32)
    res = _pc(
        body, "dh_gradx", grid=(ni,),
        in_specs=[pl.BlockSpec((tm, a.shape[1]), lambda i: (i, 0)) for a in segs] + [
            pl.BlockSpec(wt.shape, lambda i: (0, 0), pipeline_mode=pl.Buffered(1)), tile, tile,
            pl.BlockSpec((1, D_MODEL), lambda i: (0, 1))] + [HBM_SPEC] * ns,
        out_specs=[tile, vec, vec] + [HBM_SPEC] * ns,
        out_shape=[jax.ShapeDtypeStruct((S, D_MODEL), F32), vshape, vshape]
        + [jax.ShapeDtypeStruct(f.shape, f.dtype) for f in scatter],
        scratch_shapes=_comm_scratch(ns) if ns else [],
        compiler_params=_cp(58, ("arbitrary",)),
    )(*segs, wt, dz, x, mod, *scatter)
    return res[:3], res[3:]


def _row_tile(rows, cols):
    if rows * cols * 4 <= 2 * MIB or rows % 8:
        return rows
    tr = max(8, (2 * MIB // (cols * 4)) // 8 * 8)
    while rows % tr:
        tr -= 8
    return tr


def _sum8(recv, name):
    _, R, C = recv.shape
    tr = _row_tile(R, C)

    def body(r_ref, o_ref):
        acc = r_ref[0].astype(F32)
        for d in range(1, N_DEV):
            acc = acc + r_ref[d].astype(F32)
        o_ref[...] = acc

    return _pc(
        body, name, grid=(R // tr,),
        in_specs=[pl.BlockSpec((N_DEV, tr, C), lambda i: (0, i, 0))],
        out_specs=pl.BlockSpec((tr, C), lambda i: (i, 0)),
        out_shape=jax.ShapeDtypeStruct((R, C), F32),
        compiler_params=_cp(40, ("parallel",)),
    )(recv)


def _adamw(w, g, m, v, name):
    R, C = w.shape
    tr = _row_tile(R, C)

    def body(w_ref, g_ref, m_ref, v_ref, d_ref, nm_ref, nv_ref):
        g_ = g_ref[...]
        m_ = ADAM_B1 * m_ref[...] + (1.0 - ADAM_B1) * g_
        v_ = ADAM_B2 * v_ref[...] + (1.0 - ADAM_B2) * (g_ * g_)
        m_hat = m_ / (1.0 - ADAM_B1 ** ADAM_STEP)
        v_hat = v_ / (1.0 - ADAM_B2 ** ADAM_STEP)
        d_ref[...] = -ADAM_LR * (m_hat / (jnp.sqrt(v_hat) + ADAM_EPS) + ADAM_WD * w_ref[...])
        nm_ref[...] = m_
        nv_ref[...] = v_

    spec = pl.BlockSpec((tr, C), lambda i: (i, 0))
    shape = jax.ShapeDtypeStruct((R, C), F32)
    return _pc(
        body, name, grid=(R // tr,),
        in_specs=[spec] * 4, out_specs=[spec] * 3, out_shape=[shape] * 3,
        compiler_params=_cp(40, ("parallel",)),
    )(w, g, m, v)


def _small_mm(a, b, name):
    def body(a_ref, b_ref, o_ref):
        o_ref[...] = jnp.dot(a_ref[...], b_ref[...], preferred_element_type=F32)

    return _pc(body, name, out_shape=jax.ShapeDtypeStruct((a.shape[0], b.shape[1]), F32),
               compiler_params=_cp(40))(a, b)


def _all_gather_two_level(shards, name):
    nt = len(shards)

    def body(*refs):
        src, dst = refs[:nt], refs[nt:2 * nt]
        send_sems, recv_sems, local_sems = refs[2 * nt:]
        x, y, c = _me()
        sibling = (x, y, 1 - c)
        chips = [(1 - x, y), (x, 1 - y), (1 - x, 1 - y)]

        def blk(t, px, py, pc):
            return _block(dst[t], 4 * px + 2 * py + pc, src[t].shape[0])

        def copy(t, k, to, block, from_src):
            return _remote(src[t] if from_src else blk(t, *block), blk(t, *block),
                           send_sems.at[t, k], recv_sems.at[t, k], to)

        local = [pltpu.make_async_copy(src[t], blk(t, x, y, c), local_sems.at[t]) for t in range(nt)]
        for cp in local:
            cp.start()
        sends = []
        for t in range(nt):
            sends.append(copy(t, 0, sibling, (x, y, c), True))
            sends += [copy(t, 1 + j, (*chip, c), (x, y, c), True) for j, chip in enumerate(chips)]
        for cp in sends:
            cp.start()
        for j, chip in enumerate(chips):
            for t in range(nt):
                copy(t, 1 + j, (x, y, c), (*chip, c), False).wait_recv()
                fwd = copy(t, 4 + j, sibling, (*chip, c), False)
                fwd.start()
                sends.append(fwd)
        for t in range(nt):
            copy(t, 0, (x, y, c), (x, y, 1 - c), False).wait_recv()
            for j, chip in enumerate(chips):
                copy(t, 4 + j, (x, y, c), (*chip, 1 - c), False).wait_recv()
        for cp in sends:
            cp.wait_send()
        for cp in local:
            cp.wait()

    return _pc(
        body, name,
        in_specs=[HBM_SPEC] * nt, out_specs=[HBM_SPEC] * nt,
        out_shape=_gather_shapes(shards), scratch_shapes=_comm_scratch(nt),
        compiler_params=pltpu.CompilerParams(has_side_effects=True),
    )(*shards)


def _ada_fwd(c, w_ada, b_ada_cols):
    ncol = w_ada.shape[1]

    def body(c_ref, w_ref, b_ref, mod_ref, call_ref, cact, cmat, mloc, send1, recv1, send2, recv2):
        x, y, z = _me()
        me = 4 * x + 2 * y + z
        cv = c_ref[...]
        cact[...] = cv * _sig(cv)
        call_ref[me] = cact[...]
        sends = []
        for k in range(1, N_DEV):
            dev, _ = _peer(k)
            cp = _remote(cact, call_ref.at[me], send1.at[k - 1], recv1.at[k - 1], dev)
            cp.start()
            sends.append(cp)
        for k in range(1, N_DEV):
            dev, blk = _peer(k)
            _remote(cact, call_ref.at[blk], send1.at[k - 1], recv1.at[k - 1], dev).wait_recv()
        cmat[...] = jnp.zeros_like(cmat)
        for b in range(N_DEV):
            cmat[b:b + 1, :] = call_ref[b]
        m = jnp.dot(cmat[...].astype(BF16), w_ref[...].astype(BF16), preferred_element_type=F32) + b_ref[...]
        for b in range(N_DEV):
            mloc[b] = m[b:b + 1, :]
        mod_ref[me] = mloc[me]
        for k in range(1, N_DEV):
            dev, blk = _peer(k)
            cp = _remote(mloc.at[blk], mod_ref.at[me], send2.at[k - 1], recv2.at[k - 1], dev)
            cp.start()
            sends.append(cp)
        for k in range(1, N_DEV):
            dev, blk = _peer(k)
            _remote(mloc.at[me], mod_ref.at[blk], send2.at[k - 1], recv2.at[k - 1], dev).wait_recv()
        for cp in sends:
            cp.wait_send()

    return _pc(
        body, "ada_fwd",
        in_specs=[VMEM_SPEC] * 3, out_specs=[VMEM_SPEC] * 2,
        out_shape=[jax.ShapeDtypeStruct((N_DEV, 1, ncol), F32), jax.ShapeDtypeStruct((N_DEV, 1, D_MODEL), F32)],
        scratch_shapes=[pltpu.VMEM((1, D_MODEL), F32), pltpu.VMEM((16, D_MODEL), F32),
                        pltpu.VMEM((N_DEV, 1, ncol), F32)] + [pltpu.SemaphoreType.DMA((N_DEV - 1,))] * 4,
        compiler_params=pltpu.CompilerParams(has_side_effects=True, vmem_limit_bytes=40 * MIB),
    )(c, w_ada, b_ada_cols)


def _small_gather(vec):
    n = vec.shape[1]

    def body(v_ref, all_ref, sum_ref, send, recv):
        x, y, z = _me()
        me = 4 * x + 2 * y + z
        all_ref[me] = v_ref[...]
        sends = []
        for k in range(1, N_DEV):
            dev, _ = _peer(k)
            cp = _remote(v_ref, all_ref.at[me], send.at[k - 1], recv.at[k - 1], dev)
            cp.start()
            sends.append(cp)
        for k in range(1, N_DEV):
            dev, blk = _peer(k)
            _remote(v_ref, all_ref.at[blk], send.at[k - 1], recv.at[k - 1], dev).wait_recv()
        acc = all_ref[0]
        for d in range(1, N_DEV):
            acc = acc + all_ref[d]
        sum_ref[...] = acc
        for cp in sends:
            cp.wait_send()

    return _pc(
        body, "small_gather",
        in_specs=[VMEM_SPEC], out_specs=[VMEM_SPEC] * 2,
        out_shape=[jax.ShapeDtypeStruct((N_DEV, 1, n), F32), jax.ShapeDtypeStruct((1, n), F32)],
        scratch_shapes=[pltpu.SemaphoreType.DMA((N_DEV - 1,))] * 2,
        compiler_params=pltpu.CompilerParams(has_side_effects=True),
    )(vec)


def _local_step(x, target, mod, wt_in, w_out_loc, w_pw_loc, conv_w_loc, rel_bias, sinks, conv_b, conv_ln_g,
                conv_ln_b, b_pw, ln_g, ln_b):
    bmap = _bucket_map()
    bias = _bias_table(rel_bias, bmap)
    (h, qkv, g_attn, glu, g_conv), (w_out, w_pw, conv_w_blocks) = _inproj(
        x, mod, wt_in, (w_out_loc, w_pw_loc, conv_w_loc))
    conv_w = conv_w_blocks.reshape(N_DEV, CONV_ROWS, 128).transpose(1, 0, 2).reshape(CONV_ROWS, D_CONV)
    ycat, a_out, u1, u3, p_out = _mixer_fwd(qkv, g_attn, glu, g_conv, bias, sinks, conv_w, conv_b, conv_ln_g,
                                            conv_ln_b, w_pw, b_pw)
    dz, dy, dycat, loss, g_ln_g, g_ln_b, dgate = _outproj_ln(ycat, w_out, x, target, mod, ln_g, ln_b)
    gw_out = _matmul_tn(ycat, dy, None, 0, D_MODEL, "grad_w_out")
    dp, dgc, du1, g_bpw, g_clg, g_clb, g_cb = _conv_bwd_a(dycat, g_conv, p_out, u1, conv_ln_g, conv_ln_b, w_pw)
    gw_pw = _matmul_tn(u3, dp, None, 0, D_CONV, "grad_w_pw")
    dab, g_cw = _conv_bwd_b(du1, glu, conv_w)
    (dqkv, dga, dbias, dsink), (r_out, r_pw) = _attn_bwd(qkv, g_attn, a_out, dycat, bias, sinks,
                                                         scatter=(gw_out, gw_pw))
    g_rb = _relbias_grad(dbias, bmap)
    gwt_in = _matmul_tn(dqkv, h, None, 0, D_IN, "grad_w_in_qkv")
    gwt_in = _matmul_tn(dga, h, gwt_in, 1536, D_IN, "grad_w_in_gattn")
    gwt_in = _matmul_tn(dab, h, gwt_in, 2560, D_IN, "grad_w_in_glu")
    gwt_in = _matmul_tn(dgc, h, gwt_in, 4608, D_IN, "grad_w_in_gconv")
    g_cw_blocks = g_cw.reshape(CONV_ROWS, N_DEV, 128).transpose(1, 0, 2).reshape(N_DEV * CONV_ROWS, 128)
    (grad_x, dshift, dscale), (r_in, r_cw) = _dh_kernel([dqkv, dga, dab, dgc], wt_in, dz, x, mod,
                                                        scatter=(gwt_in, g_cw_blocks))
    dmod = jnp.concatenate([dshift, dscale, dgate], axis=1)
    small = dict(dmod=dmod, b_pw=g_bpw, conv_ln_g=g_clg, conv_ln_b=g_clb, conv_b=g_cb, ln_g=g_ln_g, ln_b=g_ln_b,
                 rel_bias=g_rb[:, :N_BUCKETS].reshape(1, N_BUCKETS * N_Q_HEADS),
                 sinks=dsink[0:1, :], loss=loss[0:1, :])
    return grad_x, r_in, r_out, r_pw, r_cw, small


SMALL_FIELDS = (("dmod", 3 * D_MODEL), ("b_pw", D_CONV), ("conv_ln_g", D_CONV), ("conv_ln_b", D_CONV),
                ("conv_b", D_CONV), ("ln_g", D_MODEL), ("ln_b", D_MODEL), ("rel_bias", N_BUCKETS * N_Q_HEADS),
                ("sinks", 128), ("loss", 128))


def _pack(fields):
    parts = []
    for name, width in SMALL_FIELDS:
        v = fields[name].reshape(1, -1).astype(F32)
        if v.shape[1] < width:
            v = jnp.pad(v, ((0, 0), (0, width - v.shape[1])))
        parts.append(v)
    return jnp.concatenate(parts, axis=1)


def _unpack(vec):
    out, off = {}, 0
    for name, width in SMALL_FIELDS:
        out[name] = vec[:, off:off + width]
        off += width
    return out


def kernel(x, c, w_ada, b_ada, w_in, rel_bias, sinks, conv_w, conv_b, conv_ln_g, conv_ln_b, w_pw, b_pw, w_out, ln_g, ln_b, loss_target, m_w_ada, m_b_ada, m_w_in, m_rel_bias, m_sinks, m_conv_w, m_conv_b, m_conv_ln_g, m_conv_ln_b, m_w_pw, m_b_pw, m_w_out, m_ln_g, m_ln_b, v_w_ada, v_b_ada, v_w_in, v_rel_bias, v_sinks, v_conv_w, v_conv_b, v_conv_ln_g, v_conv_ln_b, v_w_pw, v_b_pw, v_w_out, v_ln_g, v_ln_b):
    xi, yi, ci = _me()
    me = 4 * xi + 2 * yi + ci
    ncol = w_ada.shape[2]

    wt_in_loc = w_in[0].T.astype(BF16)
    conv_w_loc = jnp.pad(conv_w[0], ((0, CONV_ROWS - CONV_WIDTH), (0, 0)))
    (wt_in_full,) = _all_gather_two_level([wt_in_loc], "gather_w_in")

    b_ada_cols = lax.dynamic_slice(b_ada, (0, me * ncol), (1, ncol))
    mod_blocks, c_all = _ada_fwd(c, w_ada[0], b_ada_cols)
    mod = mod_blocks.reshape(1, 3 * D_MODEL)

    grad_x, r_in, r_out, r_pw, r_cw, small = _local_step(
        x[0], loss_target[0], mod, wt_in_full, w_out[0].astype(BF16), w_pw[0].astype(BF16), conv_w_loc, rel_bias,
        sinks, conv_b, conv_ln_g, conv_ln_b, b_pw, ln_g, ln_b)

    gathered, summed = _small_gather(_pack(small))
    tot = _unpack(summed)
    dmod_all = gathered[:, 0, :3 * D_MODEL]
    loss = tot["loss"][0, 0]

    ct = jnp.zeros((D_MODEL, 128), BF16).at[:, :N_DEV].set(c_all[:, 0, :].T.astype(BF16))
    dm = jnp.zeros((128, ncol), BF16).at[:N_DEV, :].set(
        lax.dynamic_slice(dmod_all, (0, me * ncol), (N_DEV, ncol)).astype(BF16))
    g_w_ada = _small_mm(ct, dm, "grad_w_ada")

    gt_w_in = _sum8(r_in.reshape(N_DEV, D_IN // N_DEV, D_MODEL), "sum_w_in")
    g_w_out = _sum8(r_out.reshape(N_DEV, D_MODEL // N_DEV, D_MODEL), "sum_w_out")
    g_w_pw = _sum8(r_pw.reshape(N_DEV, D_CONV // N_DEV, D_CONV), "sum_w_pw")
    g_conv_w = _sum8(r_cw.reshape(N_DEV, CONV_ROWS, 128), "sum_conv_w")[:CONV_WIDTH]

    grads = {"w_ada": g_w_ada, "conv_w": g_conv_w, "w_pw": g_w_pw, "w_out": g_w_out}
    params = {"w_ada": (w_ada, m_w_ada, v_w_ada), "conv_w": (conv_w, m_conv_w, v_conv_w),
              "w_pw": (w_pw, m_w_pw, v_w_pw), "w_out": (w_out, m_w_out, v_w_out)}
    res = {}
    for name, g in grads.items():
        w_, m_, v_ = params[name]
        d_, nm_, nv_ = _adamw(w_[0], g, m_[0], v_[0], "adamw_" + name)
        res[name] = (g[None], d_[None], nm_[None], nv_[None])
    upd = _adamw(w_in[0].T, gt_w_in, m_w_in[0].T, v_w_in[0].T, "adamw_w_in")
    res["w_in"] = tuple(a.T[None] for a in (gt_w_in, *upd))

    small_params = {"b_ada": (b_ada, m_b_ada, v_b_ada), "b_pw": (b_pw, m_b_pw, v_b_pw),
                    "conv_ln_g": (conv_ln_g, m_conv_ln_g, v_conv_ln_g),
                    "conv_ln_b": (conv_ln_b, m_conv_ln_b, v_conv_ln_b), "conv_b": (conv_b, m_conv_b, v_conv_b),
                    "ln_g": (ln_g, m_ln_g, v_ln_g), "ln_b": (ln_b, m_ln_b, v_ln_b),
                    "rel_bias": (rel_bias, m_rel_bias, v_rel_bias), "sinks": (sinks, m_sinks, v_sinks)}
    to_small = lambda n, a: a.T if n == "rel_bias" else a
    key_of = {"b_ada": "dmod"}
    packs = []
    for j in range(3):
        fields = {key_of.get(n, n): to_small(n, t[j]) for n, t in small_params.items()}
        fields["loss"] = jnp.zeros((1, 1), F32)
        packs.append(_pack(fields))
    gsum = summed
    d_s, nm_s, nv_s = _adamw(packs[0], gsum, packs[1], packs[2], "adamw_small")
    outs_small = [_unpack(a) for a in (gsum, d_s, nm_s, nv_s)]
    for n, t in small_params.items():
        shape = t[0].shape
        vals = []
        for o in outs_small:
            a = o[key_of.get(n, n)]
            if n == "rel_bias":
                a = a.reshape(N_Q_HEADS, N_BUCKETS).T
            else:
                a = a[:, :shape[1]].reshape(shape)
            vals.append(a)
        res[n] = tuple(vals)

    order = ["w_ada", "b_ada", "w_in", "rel_bias", "sinks", "conv_w", "conv_b", "conv_ln_g", "conv_ln_b", "w_pw",
             "b_pw", "w_out", "ln_g", "ln_b"]
    out = [loss, grad_x[None]]
    for j in range(4):
        out += [res[n][j] for n in order]
    return tuple(out)
```

```python
import functools
import math

import jax
import jax.numpy as jnp
import numpy as np
from jax import lax
from jax.experimental import pallas as pl
from jax.experimental.pallas import tpu as pltpu

F32, BF16, I32 = jnp.float32, jnp.bfloat16, jnp.int32

D_MODEL = 2048
D_ATTN = 1024
D_CONV = 1024
D_KV = 256
HEAD_DIM = 64
N_Q_HEADS = 16
N_KV_HEADS = 4
GQA = 4
BLOCK = 128
CONV_WIDTH = 31
CONV_ROWS = 32
HALO = 32
N_BUCKETS = 32
MAX_DISTANCE = 128
LN_EPS = 1e-5
ALPHA = 2.0 ** 0.25
D_IN = 5632
N_DEV = 8
NEG = -1e30

ADAM_LR, ADAM_B1, ADAM_B2, ADAM_EPS, ADAM_WD, ADAM_STEP = 0.001, 0.9, 0.999, 1e-08, 0.01, 10

NT_DIMS = (((1,), (1,)), ((), ()))
TN_DIMS = (((0,), (0,)), ((), ()))
MIB = 1 << 20


def _pc(body, name, **kw):
    return pl.pallas_call(body, name=name, **kw)


def _cp(vmem_mib=None, sem=None):
    kw = {}
    if vmem_mib is not None:
        kw["vmem_limit_bytes"] = vmem_mib * MIB
    if sem is not None:
        kw["dimension_semantics"] = sem
    return pltpu.CompilerParams(**kw)


def _sig(x):
    return 1.0 / (1.0 + jnp.exp(-x))


def _dsilu(x, s):
    return s * (1.0 + x * (1.0 - s))


def _me():
    return lax.axis_index("x"), lax.axis_index("y"), lax.axis_index("c")


def _peer(k):
    x, y, c = _me()
    px = 1 - x if k & 4 else x
    py = 1 - y if k & 2 else y
    pc = 1 - c if k & 1 else c
    return (px, py, pc), 4 * px + 2 * py + pc


def _remote(src, dst, send_sem, recv_sem, dev):
    return pltpu.make_async_remote_copy(src_ref=src, dst_ref=dst, send_sem=send_sem, recv_sem=recv_sem,
                                        device_id=dev, device_id_type=pl.DeviceIdType.MESH)


HBM_SPEC = pl.BlockSpec(memory_space=pl.ANY)
VMEM_SPEC = pl.BlockSpec(memory_space=pltpu.VMEM)


def _comm_scratch(nt):
    return [pltpu.SemaphoreType.DMA((nt, N_DEV - 1)), pltpu.SemaphoreType.DMA((nt, N_DEV - 1)),
            pltpu.SemaphoreType.DMA((nt,))]


def _gather_shapes(shards):
    return [jax.ShapeDtypeStruct((N_DEV * s.shape[0], s.shape[1]), s.dtype) for s in shards]


def _block(ref, blk, rows):
    return ref.at[pl.ds(pl.multiple_of(blk * rows, 8), rows), :]


def _gather_copies(src, dst, sems):
    send_sems, recv_sems, local_sems = sems
    x, y, c = _me()
    me = 4 * x + 2 * y + c
    nt = len(src)
    local = [pltpu.make_async_copy(src[t], _block(dst[t], me, src[t].shape[0]), local_sems.at[t]) for t in range(nt)]
    sends, arrivals = [], []
    for k in range(1, N_DEV):
        dev, blk = _peer(k)
        for t in range(nt):
            r = src[t].shape[0]
            pair = (send_sems.at[t, k - 1], recv_sems.at[t, k - 1], dev)
            sends.append(_remote(src[t], _block(dst[t], me, r), *pair))
            arrivals.append(_remote(src[t], _block(dst[t], blk, r), *pair))
    return local, sends, arrivals


def _scatter_copies(src, dst, sems):
    send_sems, recv_sems, local_sems = sems
    x, y, c = _me()
    me = 4 * x + 2 * y + c
    nt = len(src)
    rows = [s.shape[0] // N_DEV for s in src]
    local = [pltpu.make_async_copy(_block(src[t], me, rows[t]), _block(dst[t], me, rows[t]), local_sems.at[t])
             for t in range(nt)]
    sends, arrivals = [], []
    for k in range(1, N_DEV):
        dev, blk = _peer(k)
        for t in range(nt):
            pair = (send_sems.at[t, k - 1], recv_sems.at[t, k - 1], dev)
            sends.append(_remote(_block(src[t], blk, rows[t]), _block(dst[t], me, rows[t]), *pair))
            arrivals.append(_remote(_block(src[t], me, rows[t]), _block(dst[t], blk, rows[t]), *pair))
    return local, sends, arrivals


def _comm_start(cps):
    local, sends, _ = cps
    for cp in local + sends:
        cp.start()


def _comm_wait(cps):
    local, sends, arrivals = cps
    for cp in arrivals:
        cp.wait_recv()
    for cp in sends:
        cp.wait_send()
    for cp in local:
        cp.wait()


IN_PIECES = ((0, 1536, BF16), (1536, 1024, F32), (2560, 2048, F32), (4608, 1024, F32))


def _chip_pieces(m, cb_cols):
    lo, hi = m * cb_cols, (m + 1) * cb_cols
    out = []
    for k, (p0, pn, dt) in enumerate(IN_PIECES):
        a, b = max(lo, p0), min(hi, p0 + pn)
        if a < b:
            out.append((k, a - lo, b - a, a - p0, dt))
    return out


def _inproj_gather(x, mod, wt_loc, gather):
    S = x.shape[0]
    tm = min(512, S // 2)
    ni = S // tm
    R = wt_loc.shape[0]
    CB = 2 * R
    ng = len(gather)
    npc = len(IN_PIECES)
    MAXC = 3

    def body(*refs):
        x_ref, sh_ref, sc_ref, wl_ref = refs[:4]
        g_src = refs[4:4 + ng]
        outs = refs[4 + ng:]
        h_out, piece_out, wf_ref = outs[0], outs[1:1 + npc], outs[1 + npc]
        g_dst = outs[2 + npc:2 + npc + ng]
        (wbuf, res, resb, hst, out_sems, h_sems, w_sem, wsend, wrecv, wlocal) = outs[2 + npc + ng:12 + npc + ng]
        g_sems = outs[12 + npc + ng:]
        cb, i = pl.program_id(0), pl.program_id(1)
        slot = i % 2
        mx, my, mc = _me()
        sibling = (mx, my, 1 - mc)
        chips = [(mx, my), (1 - mx, my), (mx, 1 - my), (1 - mx, 1 - my)]

        def blk(px, py, pc):
            return _block(wf_ref, 4 * px + 2 * py + pc, R)

        def wcopy(k, to, block, from_src):
            return _remote(wl_ref if from_src else blk(*block), blk(*block), wsend.at[k], wrecv.at[k], to)

        local = pltpu.make_async_copy(wl_ref, blk(mx, my, mc), wlocal)
        first = [wcopy(0, sibling, (mx, my, mc), True)]
        first += [wcopy(1 + j, (*chip, mc), (mx, my, mc), True) for j, chip in enumerate(chips[1:])]
        fwd = [wcopy(4 + j, sibling, (*chip, mc), False) for j, chip in enumerate(chips[1:])]
        if ng:
            gcp = _gather_copies(g_src, g_dst, g_sems)

        def load_w(chip):
            m = 2 * chip[0] + chip[1]
            cp = pltpu.make_async_copy(wf_ref.at[pl.ds(pl.multiple_of(m * CB, 16), CB), :], wbuf, w_sem)
            cp.start()
            cp.wait()

        @pl.when((cb == 0) & (i == 0))
        def _():
            local.start()
            for cp in first:
                cp.start()
            if ng:
                _comm_start(gcp)
            local.wait()
            wcopy(0, (mx, my, mc), (mx, my, 1 - mc), False).wait_recv()
            load_w(chips[0])

        for j, chip in enumerate(chips[1:]):
            @pl.when((cb == 1 + j) & (i == 0))
            def _(j=j, chip=chip):
                wcopy(1 + j, (mx, my, mc), (*chip, mc), False).wait_recv()
                fwd[j].start()
                wcopy(4 + j, (mx, my, mc), (*chip, 1 - mc), False).wait_recv()
                load_w(chip)

        row0 = pl.multiple_of(i * tm, tm)

        def piece_copies(m, s, r0):
            return [pltpu.make_async_copy((resb if dt == BF16 else res).at[s, :, c0:c0 + w],
                                          piece_out[k].at[pl.ds(r0, tm), d0:d0 + w], out_sems.at[s, n])
                    for n, (k, c0, w, d0, dt) in enumerate(_chip_pieces(m, CB))]

        def h_copy(s, r0):
            return pltpu.make_async_copy(hst.at[s], h_out.at[pl.ds(r0, tm), :], h_sems.at[s])

        m_now = 2 * chips[0][0] + chips[0][1]
        for k in range(1, 4):
            m_now = jnp.where(cb == k, 2 * chips[k][0] + chips[k][1], m_now)

        for m in range(4):
            @pl.when((m_now == m) & (i >= 2))
            def _(m=m):
                for cp in piece_copies(m, slot, row0):
                    cp.wait()

        @pl.when((cb == 0) & (i >= 2))
        def _():
            h_copy(slot, row0).wait()

        h = (x_ref[...] * (1.0 + sc_ref[...]) + sh_ref[...]).astype(BF16)
        r = lax.dot_general(h, wbuf[...], NT_DIMS, preferred_element_type=F32)
        res[slot] = r
        resb[slot] = r.astype(BF16)

        @pl.when(cb == 0)
        def _():
            hst[slot] = h
            h_copy(slot, row0).start()

        for m in range(4):
            @pl.when(m_now == m)
            def _(m=m):
                for cp in piece_copies(m, slot, row0):
                    cp.start()

                @pl.when(i == ni - 1)
                def _():
                    for s in (1 - slot, slot):
                        for cp in piece_copies(m, s, row0):
                            cp.wait()

        @pl.when((cb == 0) & (i == ni - 1))
        def _():
            h_copy(1 - slot, row0).wait()
            h_copy(slot, row0).wait()

        @pl.when((cb == 3) & (i == ni - 1))
        def _():
            for cp in first + fwd:
                cp.wait_send()
            if ng:
                _comm_wait(gcp)

    res = _pc(
        body, "inproj_gather", grid=(4, ni),
        in_specs=[pl.BlockSpec((tm, D_MODEL), lambda cb, i: (i, 0)),
                  pl.BlockSpec((1, D_MODEL), lambda cb, i: (0, 0)),
                  pl.BlockSpec((1, D_MODEL), lambda cb, i: (0, 1))] + [HBM_SPEC] * (1 + ng),
        out_specs=[HBM_SPEC] * (2 + npc + ng),
        out_shape=[jax.ShapeDtypeStruct((S, D_MODEL), BF16)]
        + [jax.ShapeDtypeStruct((S, n), dt) for _, n, dt in IN_PIECES]
        + _gather_shapes([wt_loc]) + _gather_shapes(gather),
        scratch_shapes=[pltpu.VMEM((CB, D_MODEL), BF16),
                        pltpu.VMEM((2, tm, CB), F32), pltpu.VMEM((2, tm, CB), BF16),
                        pltpu.VMEM((2, tm, D_MODEL), BF16),
                        pltpu.SemaphoreType.DMA((2, MAXC)), pltpu.SemaphoreType.DMA((2,)),
                        pltpu.SemaphoreType.DMA(()),
                        pltpu.SemaphoreType.DMA((N_DEV - 1,)), pltpu.SemaphoreType.DMA((N_DEV - 1,)),
                        pltpu.SemaphoreType.DMA(())] + (_comm_scratch(ng) if ng else []),
        compiler_params=_cp(56, ("arbitrary", "arbitrary")),
    )(x, mod, mod, wt_loc, *gather)
    return res[:2 + npc], res[2 + npc:]


def _inproj(x, mod, wt, gather):
    S = x.shape[0]
    tm = min(256, S)
    ni = S // tm
    ng = len(gather)
    npc = len(IN_PIECES)

    def body(*refs):
        x_ref, sh_ref, sc_ref, w_ref = refs[:4]
        outs = refs[4 + ng:]
        h_ref, piece_refs = outs[0], outs[1:1 + npc]
        i = pl.program_id(0)
        if ng:
            cps = _gather_copies(refs[4:4 + ng], outs[1 + npc:1 + npc + ng], outs[1 + npc + ng:])

            @pl.when(i == 0)
            def _():
                _comm_start(cps)

        h = (x_ref[...] * (1.0 + sc_ref[...]) + sh_ref[...]).astype(BF16)
        h_ref[...] = h
        for (r0, n, dt), o_ref in zip(IN_PIECES, piece_refs):
            o_ref[...] = lax.dot_general(h, w_ref[r0:r0 + n, :], NT_DIMS, preferred_element_type=F32).astype(dt)
        if ng:
            @pl.when(i == ni - 1)
            def _():
                _comm_wait(cps)

    tile = lambda n: pl.BlockSpec((tm, n), lambda i: (i, 0))
    res = _pc(
        body, "inproj", grid=(ni,),
        in_specs=[tile(D_MODEL),
                  pl.BlockSpec((1, D_MODEL), lambda i: (0, 0)),
                  pl.BlockSpec((1, D_MODEL), lambda i: (0, 1)),
                  pl.BlockSpec((D_IN, D_MODEL), lambda i: (0, 0), pipeline_mode=pl.Buffered(1))] + [HBM_SPEC] * ng,
        out_specs=[tile(D_MODEL)] + [tile(n) for _, n, _ in IN_PIECES] + [HBM_SPEC] * ng,
        out_shape=[jax.ShapeDtypeStruct((S, D_MODEL), BF16)]
        + [jax.ShapeDtypeStruct((S, n), dt) for _, n, dt in IN_PIECES] + _gather_shapes(gather),
        scratch_shapes=_comm_scratch(ng) if ng else [],
        compiler_params=_cp(56, ("arbitrary",)),
    )(x, mod, mod, wt, *gather)
    return res[:1 + npc], res[1 + npc:]


def _matmul_tn(a, b, out_prev, row_off, m_total, name):
    S, M = a.shape
    N = b.shape[1]
    tm, ts = 512, min(2048, S)
    assert row_off % tm == 0 and M % tm == 0
    ob = row_off // tm
    ns = S // ts

    def body(*refs):
        a_ref, b_ref = refs[0], refs[1]
        o_ref, acc = refs[-2], refs[-1]
        s = pl.program_id(1)

        @pl.when(s == 0)
        def _():
            acc[...] = jnp.zeros_like(acc)

        acc[...] += lax.dot_general(a_ref[...], b_ref[...], TN_DIMS, preferred_element_type=F32)

        @pl.when(s == ns - 1)
        def _():
            o_ref[...] = acc[...].astype(BF16)

    in_specs = [pl.BlockSpec((ts, tm), lambda i, s: (s, i)),
                pl.BlockSpec((ts, N), lambda i, s: (s, 0))]
    args = [a, b]
    aliases = {}
    if out_prev is not None:
        in_specs.append(pl.BlockSpec(memory_space=pl.ANY))
        args.append(out_prev)
        aliases = {2: 0}
    return _pc(
        body, name, grid=(M // tm, ns),
        in_specs=in_specs,
        out_specs=pl.BlockSpec((tm, N), lambda i, s: (ob + i, 0)),
        out_shape=jax.ShapeDtypeStruct((m_total, N), BF16),
        scratch_shapes=[pltpu.VMEM((tm, N), F32)],
        input_output_aliases=aliases,
        compiler_params=_cp(48, ("parallel", "arbitrary")),
    )(*args)


def _bucket_map():
    qi = jnp.arange(BLOCK, dtype=I32)[:, None]
    kj = jnp.arange(2 * BLOCK, dtype=I32)[None, :]
    dist = qi + BLOCK - kj
    in_window = (dist >= 0) & (dist < BLOCK)
    d0 = jnp.maximum(dist, 0)
    max_exact = N_BUCKETS // 2
    d = jnp.maximum(d0, 1).astype(F32)
    large = max_exact + (jnp.log(d / max_exact) / math.log(MAX_DISTANCE / max_exact)
                         * (N_BUCKETS - max_exact)).astype(I32)
    large = jnp.minimum(large, N_BUCKETS - 1)
    bucket = jnp.where(d0 < max_exact, d0, large)
    return jnp.where(in_window, bucket, -1).astype(I32)


def _bias_table(rel_bias, bmap):
    def body(rb_ref, bm_ref, o_ref):
        h = pl.program_id(0)
        bm = bm_ref[...]
        acc = jnp.full((BLOCK, 2 * BLOCK), NEG, F32)
        for b in range(N_BUCKETS):
            acc = jnp.where(bm == b, rb_ref[b, h], acc)
        kj = lax.broadcasted_iota(I32, (BLOCK, 2 * BLOCK), 1)
        o_ref[0, 0] = jnp.where(kj >= BLOCK, acc, NEG)
        o_ref[1, 0] = acc

    return _pc(
        body, "bias_table", grid=(N_Q_HEADS,),
        in_specs=[pl.BlockSpec(memory_space=pltpu.SMEM),
                  pl.BlockSpec((BLOCK, 2 * BLOCK), lambda h: (0, 0))],
        out_specs=pl.BlockSpec((2, 1, BLOCK, 2 * BLOCK), lambda h: (0, h, 0, 0)),
        out_shape=jax.ShapeDtypeStruct((2, N_Q_HEADS, BLOCK, 2 * BLOCK), F32),
    )(rel_bias, bmap)


def _bias_spec():
    return pl.BlockSpec((2, N_Q_HEADS, BLOCK, 2 * BLOCK), lambda n: (0, 0, 0, 0))


def _relbias_grad(dbias, bmap):
    def body(db_ref, bm_ref, o_ref):
        bm = bm_ref[...]
        x = db_ref[0]
        lane = lax.broadcasted_iota(I32, (1, 128), 1)
        row = jnp.zeros((1, 128), F32)
        for b in range(N_BUCKETS):
            row = jnp.where(lane == b, jnp.sum(jnp.where(bm == b, x, 0.0)), row)
        o_ref[0] = jnp.broadcast_to(row, (8, 128))

    out = _pc(
        body, "relbias_grad", grid=(N_Q_HEADS,),
        in_specs=[pl.BlockSpec((1, BLOCK, 2 * BLOCK), lambda h: (h, 0, 0)),
                  pl.BlockSpec((BLOCK, 2 * BLOCK), lambda h: (0, 0))],
        out_specs=pl.BlockSpec((1, 8, 128), lambda h: (h, 0, 0)),
        out_shape=jax.ShapeDtypeStruct((N_Q_HEADS, 8, 128), F32),
    )(dbias, bmap)
    return out[:, 0, :]


Q_SCALE = HEAD_DIM ** -0.5
assert math.frexp(Q_SCALE)[0] == 0.5


def _stack_heads(ref, hk):
    return jnp.concatenate([ref[:, pl.ds(256 * hk + 64 * g, 64)] for g in range(GQA)], axis=0) * Q_SCALE


def _sink_col(sink_ref, hk):
    row = lax.broadcasted_iota(I32, (GQA * BLOCK, 1), 0)
    s = jnp.full((GQA * BLOCK, 1), sink_ref[0, 4 * hk + 3], F32)
    for g in (2, 1, 0):
        s = jnp.where(row < (g + 1) * BLOCK, sink_ref[0, 4 * hk + g], s)
    return s


def _attn_probs(q4s, kw, bias_ref, table, sink_ref, hk):
    s = lax.dot_general(q4s, kw, NT_DIMS, preferred_element_type=F32)
    s = s + bias_ref[table, 4 * hk:4 * hk + 4].reshape(GQA * BLOCK, 2 * BLOCK)
    sink = _sink_col(sink_ref, hk)
    m = jnp.maximum(jnp.max(s, axis=1, keepdims=True), sink)
    e = jnp.exp(s - m)
    es = jnp.exp(sink - m)
    inv = 1.0 / (jnp.sum(e, axis=1, keepdims=True) + es)
    return e * inv, es * inv


def _attn_bwd(qkv, g_attn, a_out, dycat, bias, sinks, scatter=()):
    S = qkv.shape[0]
    nb = S // BLOCK
    R = GQA * BLOCK
    ns = len(scatter)

    def body(*refs):
        (q_ref, kc_ref, kp_ref, vc_ref, vp_ref, g_ref, a_ref, dy_ref, bias_ref, sink_ref) = refs[:10]
        dqkv_ref, dg_ref, dbias_ref, dsink_ref = refs[10 + ns:14 + ns]
        dq_scr, dq_new, dk_scr, dv_scr, dkw_scr, dvw_scr, ds_scr = refs[14 + 2 * ns:21 + 2 * ns]
        n = pl.program_id(0)
        table = jnp.minimum(n, 1)
        if ns:
            cps = _scatter_copies(refs[10:10 + ns], refs[14 + ns:14 + 2 * ns], refs[21 + 2 * ns:])

            @pl.when(n == 0)
            def _():
                _comm_start(cps)

        @pl.when(n == 0)
        def _():
            dbias_ref[...] = jnp.zeros_like(dbias_ref)
            ds_scr[...] = jnp.zeros_like(ds_scr)
            dq_scr[...] = jnp.zeros_like(dq_scr)
            dk_scr[...] = jnp.zeros_like(dk_scr)
            dv_scr[...] = jnp.zeros_like(dv_scr)

        @pl.when(n < nb)
        def _():
            for hk in range(N_KV_HEADS):
                q4 = _stack_heads(q_ref, hk)
                ks = pl.ds(64 * hk, 64)
                kw = jnp.concatenate([kp_ref[:, ks], kc_ref[:, ks]], axis=0)
                vw = jnp.concatenate([vp_ref[:, ks], vc_ref[:, ks]], axis=0)
                p, psink = _attn_probs(q4, kw, bias_ref, table, sink_ref, hk)
                da_parts, a_parts = [], []
                for g in range(GQA):
                    sl = pl.ds(256 * hk + 64 * g, 64)
                    gg = g_ref[:, sl]
                    sg = _sig(gg)
                    dyg = dy_ref[:, sl]
                    ag = a_ref[:, sl]
                    da_parts.append(dyg * (gg * sg))
                    a_parts.append(ag)
                    dg_ref[:, sl] = (dyg * ag * _dsilu(gg, sg)).astype(BF16)
                da4 = jnp.concatenate(da_parts, axis=0)
                a4 = jnp.concatenate(a_parts, axis=0)
                delta = jnp.sum(da4 * a4, axis=1, keepdims=True)
                da4b = da4.astype(BF16)
                dp = lax.dot_general(da4b, vw, NT_DIMS, preferred_element_type=F32)
                ds = p * (dp - delta)
                ds_scr[hk] += -psink * delta
                dbias_ref[4 * hk:4 * hk + 4] += ds.reshape(GQA, BLOCK, 2 * BLOCK)
                dsb = ds.astype(BF16)
                dq4 = jnp.dot(dsb, kw, preferred_element_type=F32) * Q_SCALE
                for g in range(GQA):
                    dq_new[:, pl.ds(256 * hk + 64 * g, 64)] = dq4[BLOCK * g:BLOCK * (g + 1)].astype(BF16)
                dkw_scr[:, ks] = lax.dot_general(dsb, q4, TN_DIMS, preferred_element_type=F32)
                dvw_scr[:, ks] = lax.dot_general(p.astype(BF16), da4b, TN_DIMS, preferred_element_type=F32)

        @pl.when(n == nb)
        def _():
            dkw_scr[0:BLOCK, :] = jnp.zeros((BLOCK, D_KV), F32)
            dvw_scr[0:BLOCK, :] = jnp.zeros((BLOCK, D_KV), F32)

        dqkv_ref[:, 0:D_ATTN] = dq_scr[...]
        dqkv_ref[:, D_ATTN:D_ATTN + D_KV] = (dk_scr[...] + dkw_scr[0:BLOCK, :]).astype(BF16)
        dqkv_ref[:, D_ATTN + D_KV:D_ATTN + 2 * D_KV] = (dv_scr[...] + dvw_scr[0:BLOCK, :]).astype(BF16)
        dq_scr[...] = dq_new[...]
        dk_scr[...] = dkw_scr[BLOCK:2 * BLOCK, :]
        dv_scr[...] = dvw_scr[BLOCK:2 * BLOCK, :]

        @pl.when(n == nb)
        def _():
            lane = lax.broadcasted_iota(I32, (1, 128), 1)
            row = jnp.zeros((1, 128), F32)
            for hk in range(N_KV_HEADS):
                col = ds_scr[hk]
                for g in range(GQA):
                    row = jnp.where(lane == 4 * hk + g, jnp.sum(col[BLOCK * g:BLOCK * (g + 1)]), row)
            dsink_ref[...] = jnp.broadcast_to(row, (8, 128))
            if ns:
                _comm_wait(cps)

    cur = lambda n: jnp.minimum(n, nb - 1)
    prev = lambda n: jnp.clip(n - 1, 0, nb - 1)
    res = _pc(
        body, "attn_bwd", grid=(nb + 1,),
        in_specs=[pl.BlockSpec((BLOCK, D_ATTN), lambda n: (cur(n), 0)),
                  pl.BlockSpec((BLOCK, D_KV), lambda n: (cur(n), 4)),
                  pl.BlockSpec((BLOCK, D_KV), lambda n: (prev(n), 4)),
                  pl.BlockSpec((BLOCK, D_KV), lambda n: (cur(n), 5)),
                  pl.BlockSpec((BLOCK, D_KV), lambda n: (prev(n), 5)),
                  pl.BlockSpec((BLOCK, D_ATTN), lambda n: (cur(n), 0)),
                  pl.BlockSpec((BLOCK, D_ATTN), lambda n: (cur(n), 0)),
                  pl.BlockSpec((BLOCK, D_ATTN), lambda n: (cur(n), 0)),
                  _bias_spec(),
                  pl.BlockSpec(memory_space=pltpu.SMEM)] + [HBM_SPEC] * ns,
        out_specs=[pl.BlockSpec((BLOCK, D_ATTN + 2 * D_KV), lambda n: (prev(n), 0)),
                   pl.BlockSpec((BLOCK, D_ATTN), lambda n: (cur(n), 0)),
                   pl.BlockSpec((N_Q_HEADS, BLOCK, 2 * BLOCK), lambda n: (0, 0, 0)),
                   pl.BlockSpec((8, 128), lambda n: (0, 0))] + [HBM_SPEC] * ns,
        out_shape=[jax.ShapeDtypeStruct((S, D_ATTN + 2 * D_KV), BF16),
                   jax.ShapeDtypeStruct((S, D_ATTN), BF16),
                   jax.ShapeDtypeStruct((N_Q_HEADS, BLOCK, 2 * BLOCK), F32),
                   jax.ShapeDtypeStruct((8, 128), F32)] + [jax.ShapeDtypeStruct(f.shape, f.dtype) for f in scatter],
        scratch_shapes=[pltpu.VMEM((BLOCK, D_ATTN), BF16), pltpu.VMEM((BLOCK, D_ATTN), BF16),
                        pltpu.VMEM((BLOCK, D_KV), F32), pltpu.VMEM((BLOCK, D_KV), F32),
                        pltpu.VMEM((2 * BLOCK, D_KV), F32), pltpu.VMEM((2 * BLOCK, D_KV), F32),
                        pltpu.VMEM((N_KV_HEADS, R, 1), F32)] + (_comm_scratch(ns) if ns else []),
        compiler_params=_cp(48, ("arbitrary",)),
    )(qkv, qkv, qkv, qkv, qkv, g_attn, a_out, dycat, bias, sinks, *scatter)
    return res[:4], res[4:]


def _conv_tile(S):
    return min(256, S)


def _shifted(win, s):
    return win if s == 0 else pltpu.roll(win, win.shape[0] - s, axis=0)


def _conv_taps(win, cw_ref, lanes, rows):
    acc = jnp.zeros((rows, win.shape[1]), F32)
    for s in range(8):
        ws = _shifted(win, s)
        for aa in range(5):
            j = 8 * aa + s - 2
            if 0 <= j < CONV_WIDTH:
                acc = acc + ws[8 * aa:8 * aa + rows] * cw_ref[j:j + 1, lanes]
    return acc


def _mixer_fwd(qkv, g_attn, glu, g_conv, bias, sinks, conv_w, conv_b, ln_g, ln_b, w_pw, b_pw):
    S = qkv.shape[0]
    nb = S // BLOCK
    hb = BLOCK // HALO
    QC = D_CONV // N_KV_HEADS

    def body(q_ref, kc_ref, kp_ref, vc_ref, vp_ref, g_ref, bias_ref, sink_ref,
             a_in, ah_in, b_in, bh_in, gc_ref, cw_ref, cb_ref, lg_ref, lb_ref, wpw_ref, bpw_ref,
             y_ref, a_ref, u1_ref, u3_ref, p_ref, win):
        n = pl.program_id(0)
        table = jnp.minimum(n, 1)
        win[0:HALO, :] = jnp.where(n > 0, ah_in[...] * _sig(bh_in[...]), 0.0)
        win[HALO:HALO + BLOCK, :] = a_in[...] * _sig(b_in[...])
        for hk in range(N_KV_HEADS):
            q4 = _stack_heads(q_ref, hk)
            ks = pl.ds(64 * hk, 64)
            kw = jnp.concatenate([kp_ref[:, ks], kc_ref[:, ks]], axis=0)
            vw = jnp.concatenate([vp_ref[:, ks], vc_ref[:, ks]], axis=0)
            p, _ = _attn_probs(q4, kw, bias_ref, table, sink_ref, hk)
            o4 = jnp.dot(p.astype(BF16), vw, preferred_element_type=F32)
            for g in range(GQA):
                sl = pl.ds(256 * hk + 64 * g, 64)
                og = o4[BLOCK * g:BLOCK * (g + 1)]
                gg = g_ref[:, sl]
                a_ref[:, sl] = og
                y_ref[:, sl] = (og * (gg * _sig(gg))).astype(BF16)
            lanes = pl.ds(QC * hk, QC)
            u1_ref[:, lanes] = _conv_taps(win[:, lanes], cw_ref, lanes, BLOCK) + cb_ref[:, lanes]
        u1 = u1_ref[...]
        mu = jnp.mean(u1, axis=1, keepdims=True)
        uc = u1 - mu
        rstd = lax.rsqrt(jnp.mean(uc * uc, axis=1, keepdims=True) + LN_EPS)
        u2 = uc * rstd * lg_ref[...] + lb_ref[...]
        u3 = (u2 * _sig(u2)).astype(BF16)
        u3_ref[...] = u3
        pw = jnp.dot(u3, wpw_ref[...], preferred_element_type=F32) + bpw_ref[...]
        p_ref[...] = pw
        gc = gc_ref[...]
        y_ref[:, D_ATTN:] = (pw * (gc * _sig(gc))).astype(BF16)

    prev = lambda n: jnp.maximum(n - 1, 0)
    halo = lambda n: jnp.maximum(n * hb - 1, 0)
    vec = pl.BlockSpec((1, D_CONV), lambda n: (0, 0))
    blk = lambda w, j: pl.BlockSpec((BLOCK, w), lambda n: (n, j))
    return _pc(
        body, "mixer_fwd", grid=(nb,),
        in_specs=[blk(D_ATTN, 0),
                  blk(D_KV, 4), pl.BlockSpec((BLOCK, D_KV), lambda n: (prev(n), 4)),
                  blk(D_KV, 5), pl.BlockSpec((BLOCK, D_KV), lambda n: (prev(n), 5)),
                  blk(D_ATTN, 0),
                  _bias_spec(),
                  pl.BlockSpec(memory_space=pltpu.SMEM),
                  blk(D_CONV, 0), pl.BlockSpec((HALO, D_CONV), lambda n: (halo(n), 0)),
                  blk(D_CONV, 1), pl.BlockSpec((HALO, D_CONV), lambda n: (halo(n), 1)),
                  blk(D_CONV, 0),
                  pl.BlockSpec((CONV_ROWS, D_CONV), lambda n: (0, 0)),
                  vec, vec, vec,
                  pl.BlockSpec((D_CONV, D_CONV), lambda n: (0, 0)),
                  vec],
        out_specs=[blk(2 * D_ATTN, 0), blk(D_ATTN, 0), blk(D_CONV, 0), blk(D_CONV, 0), blk(D_CONV, 0)],
        out_shape=[jax.ShapeDtypeStruct((S, 2 * D_ATTN), BF16),
                   jax.ShapeDtypeStruct((S, D_ATTN), F32),
                   jax.ShapeDtypeStruct((S, D_CONV), F32),
                   jax.ShapeDtypeStruct((S, D_CONV), BF16),
                   jax.ShapeDtypeStruct((S, D_CONV), F32)],
        scratch_shapes=[pltpu.VMEM((BLOCK + HALO, D_CONV), F32)],
        compiler_params=_cp(48, ("parallel",)),
    )(qkv, qkv, qkv, qkv, qkv, g_attn, bias, sinks, glu, glu, glu, glu, g_conv, conv_w, conv_b, ln_g, ln_b, w_pw,
      b_pw)


def _conv_bwd_a(dycat, g_conv, p_out, u1, ln_g, ln_b, w_pw):
    S = u1.shape[0]
    T = _conv_tile(S)

    def body(dy_ref, gc_ref, p_ref, u1_ref, lg_ref, lb_ref, wpw_ref,
             dp_ref, dgc_ref, du1_ref, gbpw_ref, glg_ref, glb_ref, gcb_ref):
        i = pl.program_id(0)

        @pl.when(i == 0)
        def _():
            for r in (gbpw_ref, glg_ref, glb_ref, gcb_ref):
                r[...] = jnp.zeros_like(r)

        dy = dy_ref[...]
        gc = gc_ref[...]
        sg = _sig(gc)
        dp = dy * (gc * sg)
        dgc_ref[...] = (dy * p_ref[...] * _dsilu(gc, sg)).astype(BF16)
        gbpw_ref[...] += jnp.sum(dp, axis=0, keepdims=True)
        dpb = dp.astype(BF16)
        dp_ref[...] = dpb
        du3 = lax.dot_general(dpb, wpw_ref[...], NT_DIMS, preferred_element_type=F32)
        u1 = u1_ref[...]
        mu = jnp.mean(u1, axis=1, keepdims=True)
        uc = u1 - mu
        rstd = lax.rsqrt(jnp.mean(uc * uc, axis=1, keepdims=True) + LN_EPS)
        uh = uc * rstd
        lg = lg_ref[...]
        u2 = uh * lg + lb_ref[...]
        s2 = _sig(u2)
        du2 = du3 * _dsilu(u2, s2)
        glg_ref[...] += jnp.sum(du2 * uh, axis=0, keepdims=True)
        glb_ref[...] += jnp.sum(du2, axis=0, keepdims=True)
        duh = du2 * lg
        du1 = rstd * (duh - jnp.mean(duh, axis=1, keepdims=True) - uh * jnp.mean(duh * uh, axis=1, keepdims=True))
        du1_ref[...] = du1
        gcb_ref[...] += jnp.sum(du1, axis=0, keepdims=True)

    vec = pl.BlockSpec((1, D_CONV), lambda i: (0, 0))
    tile = pl.BlockSpec((T, D_CONV), lambda i: (i, 0))
    vshape = jax.ShapeDtypeStruct((1, D_CONV), F32)
    return _pc(
        body, "conv_bwd_a", grid=(S // T,),
        in_specs=[pl.BlockSpec((T, D_CONV), lambda i: (i, 1)), tile, tile, tile, vec, vec,
                  pl.BlockSpec((D_CONV, D_CONV), lambda i: (0, 0))],
        out_specs=[tile, tile, tile, vec, vec, vec, vec],
        out_shape=[jax.ShapeDtypeStruct((S, D_CONV), BF16), jax.ShapeDtypeStruct((S, D_CONV), BF16),
                   jax.ShapeDtypeStruct((S, D_CONV), F32), vshape, vshape, vshape, vshape],
        compiler_params=_cp(48, ("arbitrary",)),
    )(dycat, g_conv, p_out, u1, ln_g, ln_b, w_pw)


def _conv_bwd_b(du1, glu, conv_w):
    S = du1.shape[0]
    T = _conv_tile(S)
    hb = T // HALO
    nt = S // T
    last_h = S // HALO - 1

    def body(du_ref, dun_ref, a_ref, b_ref, cw_ref, dab_ref, gw_ref):
        i = pl.program_id(0)

        @pl.when(i == 0)
        def _():
            gw_ref[...] = jnp.zeros_like(gw_ref)

        for lg in range(D_CONV // 128):
            lanes = pl.ds(128 * lg, 128)
            a = a_ref[:, lanes]
            sb = _sig(b_ref[:, lanes])
            u0 = a * sb
            du1n = jnp.where(i < nt - 1, dun_ref[:, lanes], 0.0)
            win2 = jnp.concatenate([du_ref[:, lanes], du1n], axis=0)
            acc = jnp.zeros((T, 128), F32)
            for s in range(8):
                w2 = _shifted(win2, s)
                for aa in range(4):
                    j = CONV_WIDTH - 1 - (8 * aa + s)
                    if 0 <= j < CONV_WIDTH:
                        xo = w2[8 * aa:8 * aa + T]
                        acc = acc + xo * cw_ref[j:j + 1, lanes]
                        gw_ref[j:j + 1, lanes] += jnp.sum(xo * u0, axis=0, keepdims=True)
            dab_ref[:, lanes] = (acc * sb).astype(BF16)
            dab_ref[:, pl.ds(D_CONV + 128 * lg, 128)] = (acc * a * sb * (1.0 - sb)).astype(BF16)

    nxt = lambda i: jnp.minimum((i + 1) * hb, last_h)
    return _pc(
        body, "conv_bwd_b", grid=(nt,),
        in_specs=[pl.BlockSpec((T, D_CONV), lambda i: (i, 0)),
                  pl.BlockSpec((HALO, D_CONV), lambda i: (nxt(i), 0)),
                  pl.BlockSpec((T, D_CONV), lambda i: (i, 0)),
                  pl.BlockSpec((T, D_CONV), lambda i: (i, 1)),
                  pl.BlockSpec((CONV_ROWS, D_CONV), lambda i: (0, 0))],
        out_specs=[pl.BlockSpec((T, 2 * D_CONV), lambda i: (i, 0)),
                   pl.BlockSpec((CONV_ROWS, D_CONV), lambda i: (0, 0))],
        out_shape=[jax.ShapeDtypeStruct((S, 2 * D_CONV), BF16),
                   jax.ShapeDtypeStruct((CONV_ROWS, D_CONV), F32)],
        compiler_params=_cp(48, ("arbitrary",)),
    )(du1, du1, glu, glu, conv_w)


def _outproj_ln(ycat, w_out, x, target, mod, ln_g, ln_b):
    S = x.shape[0]
    tm = min(256, S)

    def body(yc_ref, w_ref, x_ref, t_ref, gate_ref, lg_ref, lb_ref,
             dz_ref, dy_ref, dyc_ref, loss_ref, glg_ref, glb_ref, dgate_ref):
        i = pl.program_id(0)

        @pl.when(i == 0)
        def _():
            for r in (loss_ref, glg_ref, glb_ref, dgate_ref):
                r[...] = jnp.zeros_like(r)

        w = w_ref[...]
        y = jnp.dot(yc_ref[...], w, preferred_element_type=F32)
        gate = gate_ref[...]
        z = ALPHA * x_ref[...] + gate * y
        mu = jnp.mean(z, axis=1, keepdims=True)
        zc = z - mu
        rstd = lax.rsqrt(jnp.mean(zc * zc, axis=1, keepdims=True) + LN_EPS)
        zh = zc * rstd
        lg = lg_ref[...]
        err = zh * lg + lb_ref[...] - t_ref[...]
        loss_ref[...] += 0.5 * jnp.sum(jnp.sum(err * err, axis=1, keepdims=True)) / D_MODEL
        dout = err * (1.0 / D_MODEL)
        glg_ref[...] += jnp.sum(dout * zh, axis=0, keepdims=True)
        glb_ref[...] += jnp.sum(dout, axis=0, keepdims=True)
        dzh = dout * lg
        dz = rstd * (dzh - jnp.mean(dzh, axis=1, keepdims=True) - zh * jnp.mean(dzh * zh, axis=1, keepdims=True))
        dz_ref[...] = dz
        dgate_ref[...] += jnp.sum(dz * y, axis=0, keepdims=True)
        dy = (dz * gate).astype(BF16)
        dy_ref[...] = dy
        dyc_ref[...] = lax.dot_general(dy, w, NT_DIMS, preferred_element_type=F32).astype(BF16)

    vec = pl.BlockSpec((1, D_MODEL), lambda i: (0, 0))
    tile = pl.BlockSpec((tm, D_MODEL), lambda i: (i, 0))
    vshape = jax.ShapeDtypeStruct((1, D_MODEL), F32)
    return _pc(
        body, "outproj_ln", grid=(S // tm,),
        in_specs=[tile, pl.BlockSpec((D_MODEL, D_MODEL), lambda i: (0, 0)), tile, tile,
                  pl.BlockSpec((1, D_MODEL), lambda i: (0, 2)), vec, vec],
        out_specs=[tile, tile, tile, pl.BlockSpec((8, 128), lambda i: (0, 0)), vec, vec, vec],
        out_shape=[jax.ShapeDtypeStruct((S, D_MODEL), F32), jax.ShapeDtypeStruct((S, D_MODEL), BF16),
                   jax.ShapeDtypeStruct((S, D_MODEL), BF16), jax.ShapeDtypeStruct((8, 128), F32),
                   vshape, vshape, vshape],
        compiler_params=_cp(56, ("arbitrary",)),
    )(ycat, w_out, x, target, mod, ln_g, ln_b)


def _dh_kernel(segs, wt, dz, x, mod, scatter=()):
    S = x.shape[0]
    tm = min(256, S)
    row0 = [0]
    for a in segs:
        row0.append(row0[-1] + a.shape[1])
    assert row0[-1] == wt.shape[0]
    nseg = len(segs)
    ns = len(scatter)
    ni = S // tm

    def body(*refs):
        seg_refs = refs[:nseg]
        w_ref, dz_ref, x_ref, sc_ref = refs[nseg:nseg + 4]
        outs = refs[nseg + 4 + ns:]
        gx_ref, dsh_ref, dsc_ref = outs[:3]
        i = pl.program_id(0)
        if ns:
            cps = _scatter_copies(refs[nseg + 4:nseg + 4 + ns], outs[3:3 + ns], outs[3 + ns:])

        @pl.when(i == 0)
        def _():
            dsh_ref[...] = jnp.zeros_like(dsh_ref)
            dsc_ref[...] = jnp.zeros_like(dsc_ref)
            if ns:
                _comm_start(cps)

        dh = jnp.dot(seg_refs[0][...], w_ref[row0[0]:row0[1], :], preferred_element_type=F32)
        for t in range(1, nseg):
            dh = dh + jnp.dot(seg_refs[t][...], w_ref[row0[t]:row0[t + 1], :], preferred_element_type=F32)
        gx_ref[...] = ALPHA * dz_ref[...] + dh * (1.0 + sc_ref[...])
        dsh_ref[...] += jnp.sum(dh, axis=0, keepdims=True)
        dsc_ref[...] += jnp.sum(dh * x_ref[...], axis=0, keepdims=True)

        if ns:
            @pl.when(i == ni - 1)
            def _():
                _comm_wait(cps)

    tile = pl.BlockSpec((tm, D_MODEL), lambda i: (i, 0))
    vec = pl.BlockSpec((1, D_MODEL), lambda i: (0, 0))
    vshape = jax.ShapeDtypeStruct((1, D_MODEL), F32)
    res = _pc(
        body, "dh_gradx", grid=(ni,),
        in_specs=[pl.BlockSpec((tm, a.shape[1]), lambda i: (i, 0)) for a in segs] + [
            pl.BlockSpec(wt.shape, lambda i: (0, 0), pipeline_mode=pl.Buffered(1)), tile, tile,
            pl.BlockSpec((1, D_MODEL), lambda i: (0, 1))] + [HBM_SPEC] * ns,
        out_specs=[tile, vec, vec] + [HBM_SPEC] * ns,
        out_shape=[jax.ShapeDtypeStruct((S, D_MODEL), F32), vshape, vshape]
        + [jax.ShapeDtypeStruct(f.shape, f.dtype) for f in scatter],
        scratch_shapes=_comm_scratch(ns) if ns else [],
        compiler_params=_cp(58, ("arbitrary",)),
    )(*segs, wt, dz, x, mod, *scatter)
    return res[:3], res[3:]


def _row_tile(rows, cols):
    if rows * cols * 4 <= 2 * MIB or rows % 8:
        return rows
    tr = max(8, (2 * MIB // (cols * 4)) // 8 * 8)
    while rows % tr:
        tr -= 8
    return tr


def _sum8(recv, name):
    _, R, C = recv.shape
    tr = _row_tile(R, C)

    def body(r_ref, o_ref):
        acc = r_ref[0].astype(F32)
        for d in range(1, N_DEV):
            acc = acc + r_ref[d].astype(F32)
        o_ref[...] = acc

    return _pc(
        body, name, grid=(R // tr,),
        in_specs=[pl.BlockSpec((N_DEV, tr, C), lambda i: (0, i, 0))],
        out_specs=pl.BlockSpec((tr, C), lambda i: (i, 0)),
        out_shape=jax.ShapeDtypeStruct((R, C), F32),
        compiler_params=_cp(40, ("parallel",)),
    )(recv)


def _adamw(w, g, m, v, name):
    R, C = w.shape
    tr = _row_tile(R, C)

    def body(w_ref, g_ref, m_ref, v_ref, d_ref, nm_ref, nv_ref):
        g_ = g_ref[...]
        m_ = ADAM_B1 * m_ref[...] + (1.0 - ADAM_B1) * g_
        v_ = ADAM_B2 * v_ref[...] + (1.0 - ADAM_B2) * (g_ * g_)
        m_hat = m_ / (1.0 - ADAM_B1 ** ADAM_STEP)
        v_hat = v_ / (1.0 - ADAM_B2 ** ADAM_STEP)
        d_ref[...] = -ADAM_LR * (m_hat / (jnp.sqrt(v_hat) + ADAM_EPS) + ADAM_WD * w_ref[...])
        nm_ref[...] = m_
        nv_ref[...] = v_

    spec = pl.BlockSpec((tr, C), lambda i: (i, 0))
    shape = jax.ShapeDtypeStruct((R, C), F32)
    return _pc(
        body, name, grid=(R // tr,),
        in_specs=[spec] * 4, out_specs=[spec] * 3, out_shape=[shape] * 3,
        compiler_params=_cp(40, ("parallel",)),
    )(w, g, m, v)


def _small_mm(a, b, name):
    def body(a_ref, b_ref, o_ref):
        o_ref[...] = jnp.dot(a_ref[...], b_ref[...], preferred_element_type=F32)

    return _pc(body, name, out_shape=jax.ShapeDtypeStruct((a.shape[0], b.shape[1]), F32),
               compiler_params=_cp(40))(a, b)


def _all_gather_two_level(shards, name):
    nt = len(shards)

    def body(*refs):
        src, dst = refs[:nt], refs[nt:2 * nt]
        send_sems, recv_sems, local_sems = refs[2 * nt:]
        x, y, c = _me()
        sibling = (x, y, 1 - c)
        chips = [(1 - x, y), (x, 1 - y), (1 - x, 1 - y)]

        def blk(t, px, py, pc):
            return _block(dst[t], 4 * px + 2 * py + pc, src[t].shape[0])

        def copy(t, k, to, block, from_src):
            return _remote(src[t] if from_src else blk(t, *block), blk(t, *block),
                           send_sems.at[t, k], recv_sems.at[t, k], to)

        local = [pltpu.make_async_copy(src[t], blk(t, x, y, c), local_sems.at[t]) for t in range(nt)]
        for cp in local:
            cp.start()
        sends = []
        for t in range(nt):
            sends.append(copy(t, 0, sibling, (x, y, c), True))
            sends += [copy(t, 1 + j, (*chip, c), (x, y, c), True) for j, chip in enumerate(chips)]
        for cp in sends:
            cp.start()
        for j, chip in enumerate(chips):
            for t in range(nt):
                copy(t, 1 + j, (x, y, c), (*chip, c), False).wait_recv()
                fwd = copy(t, 4 + j, sibling, (*chip, c), False)
                fwd.start()
                sends.append(fwd)
        for t in range(nt):
            copy(t, 0, (x, y, c), (x, y, 1 - c), False).wait_recv()
            for j, chip in enumerate(chips):
                copy(t, 4 + j, (x, y, c), (*chip, 1 - c), False).wait_recv()
        for cp in sends:
            cp.wait_send()
        for cp in local:
            cp.wait()

    return _pc(
        body, name,
        in_specs=[HBM_SPEC] * nt, out_specs=[HBM_SPEC] * nt,
        out_shape=_gather_shapes(shards), scratch_shapes=_comm_scratch(nt),
        compiler_params=pltpu.CompilerParams(has_side_effects=True),
    )(*shards)


def _ada_fwd(c, w_ada, b_ada_cols):
    ncol = w_ada.shape[1]

    def body(c_ref, w_ref, b_ref, mod_ref, call_ref, cact, cmat, mloc, send1, recv1, send2, recv2):
        x, y, z = _me()
        me = 4 * x + 2 * y + z
        cv = c_ref[...]
        cact[...] = cv * _sig(cv)
        call_ref[me] = cact[...]
        sends = []
        for k in range(1, N_DEV):
            dev, _ = _peer(k)
            cp = _remote(cact, call_ref.at[me], send1.at[k - 1], recv1.at[k - 1], dev)
            cp.start()
            sends.append(cp)
        for k in range(1, N_DEV):
            dev, blk = _peer(k)
            _remote(cact, call_ref.at[blk], send1.at[k - 1], recv1.at[k - 1], dev).wait_recv()
        cmat[...] = jnp.zeros_like(cmat)
        for b in range(N_DEV):
            cmat[b:b + 1, :] = call_ref[b]
        m = jnp.dot(cmat[...].astype(BF16), w_ref[...].astype(BF16), preferred_element_type=F32) + b_ref[...]
        for b in range(N_DEV):
            mloc[b] = m[b:b + 1, :]
        mod_ref[me] = mloc[me]
        for k in range(1, N_DEV):
            dev, blk = _peer(k)
            cp = _remote(mloc.at[blk], mod_ref.at[me], send2.at[k - 1], recv2.at[k - 1], dev)
            cp.start()
            sends.append(cp)
        for k in range(1, N_DEV):
            dev, blk = _peer(k)
            _remote(mloc.at[me], mod_ref.at[blk], send2.at[k - 1], recv2.at[k - 1], dev).wait_recv()
        for cp in sends:
            cp.wait_send()

    return _pc(
        body, "ada_fwd",
        in_specs=[VMEM_SPEC] * 3, out_specs=[VMEM_SPEC] * 2,
        out_shape=[jax.ShapeDtypeStruct((N_DEV, 1, ncol), F32), jax.ShapeDtypeStruct((N_DEV, 1, D_MODEL), F32)],
        scratch_shapes=[pltpu.VMEM((1, D_MODEL), F32), pltpu.VMEM((16, D_MODEL), F32),
                        pltpu.VMEM((N_DEV, 1, ncol), F32)] + [pltpu.SemaphoreType.DMA((N_DEV - 1,))] * 4,
        compiler_params=pltpu.CompilerParams(has_side_effects=True, vmem_limit_bytes=40 * MIB),
    )(c, w_ada, b_ada_cols)


def _small_gather(vec):
    n = vec.shape[1]

    def body(v_ref, all_ref, sum_ref, send, recv):
        x, y, z = _me()
        me = 4 * x + 2 * y + z
        all_ref[me] = v_ref[...]
        sends = []
        for k in range(1, N_DEV):
            dev, _ = _peer(k)
            cp = _remote(v_ref, all_ref.at[me], send.at[k - 1], recv.at[k - 1], dev)
            cp.start()
            sends.append(cp)
        for k in range(1, N_DEV):
            dev, blk = _peer(k)
            _remote(v_ref, all_ref.at[blk], send.at[k - 1], recv.at[k - 1], dev).wait_recv()
        acc = all_ref[0]
        for d in range(1, N_DEV):
            acc = acc + all_ref[d]
        sum_ref[...] = acc
        for cp in sends:
            cp.wait_send()

    return _pc(
        body, "small_gather",
        in_specs=[VMEM_SPEC], out_specs=[VMEM_SPEC] * 2,
        out_shape=[jax.ShapeDtypeStruct((N_DEV, 1, n), F32), jax.ShapeDtypeStruct((1, n), F32)],
        scratch_shapes=[pltpu.SemaphoreType.DMA((N_DEV - 1,))] * 2,
        compiler_params=pltpu.CompilerParams(has_side_effects=True),
    )(vec)


def _local_step(x, target, mod, wt_in_loc, w_out_loc, w_pw_loc, conv_w_loc, rel_bias, sinks, conv_b, conv_ln_g,
                conv_ln_b, b_pw, ln_g, ln_b):
    bmap = _bucket_map()
    bias = _bias_table(rel_bias, bmap)
    (h, qkv, g_attn, glu, g_conv, wt_in), (w_out, w_pw, conv_w_blocks) = _inproj_gather(
        x, mod, wt_in_loc, (w_out_loc, w_pw_loc, conv_w_loc))
    conv_w = conv_w_blocks.reshape(N_DEV, CONV_ROWS, 128).transpose(1, 0, 2).reshape(CONV_ROWS, D_CONV)
    ycat, a_out, u1, u3, p_out = _mixer_fwd(qkv, g_attn, glu, g_conv, bias, sinks, conv_w, conv_b, conv_ln_g,
                                            conv_ln_b, w_pw, b_pw)
    dz, dy, dycat, loss, g_ln_g, g_ln_b, dgate = _outproj_ln(ycat, w_out, x, target, mod, ln_g, ln_b)
    gw_out = _matmul_tn(ycat, dy, None, 0, D_MODEL, "grad_w_out")
    dp, dgc, du1, g_bpw, g_clg, g_clb, g_cb = _conv_bwd_a(dycat, g_conv, p_out, u1, conv_ln_g, conv_ln_b, w_pw)
    gw_pw = _matmul_tn(u3, dp, None, 0, D_CONV, "grad_w_pw")
    dab, g_cw = _conv_bwd_b(du1, glu, conv_w)
    (dqkv, dga, dbias, dsink), (r_out, r_pw) = _attn_bwd(qkv, g_attn, a_out, dycat, bias, sinks,
                                                         scatter=(gw_out, gw_pw))
    g_rb = _relbias_grad(dbias, bmap)
    gwt_in = _matmul_tn(dqkv, h, None, 0, D_IN, "grad_w_in_qkv")
    gwt_in = _matmul_tn(dga, h, gwt_in, 1536, D_IN, "grad_w_in_gattn")
    gwt_in = _matmul_tn(dab, h, gwt_in, 2560, D_IN, "grad_w_in_glu")
    gwt_in = _matmul_tn(dgc, h, gwt_in, 4608, D_IN, "grad_w_in_gconv")
    g_cw_blocks = g_cw.reshape(CONV_ROWS, N_DEV, 128).transpose(1, 0, 2).reshape(N_DEV * CONV_ROWS, 128)
    (grad_x, dshift, dscale), (r_in, r_cw) = _dh_kernel([dqkv, dga, dab, dgc], wt_in, dz, x, mod,
                                                        scatter=(gwt_in, g_cw_blocks))
    dmod = jnp.concatenate([dshift, dscale, dgate], axis=1)
    small = dict(dmod=dmod, b_pw=g_bpw, conv_ln_g=g_clg, conv_ln_b=g_clb, conv_b=g_cb, ln_g=g_ln_g, ln_b=g_ln_b,
                 rel_bias=g_rb[:, :N_BUCKETS].reshape(1, N_BUCKETS * N_Q_HEADS),
                 sinks=dsink[0:1, :], loss=loss[0:1, :])
    return grad_x, r_in, r_out, r_pw, r_cw, small


SMALL_FIELDS = (("dmod", 3 * D_MODEL), ("b_pw", D_CONV), ("conv_ln_g", D_CONV), ("conv_ln_b", D_CONV),
                ("conv_b", D_CONV), ("ln_g", D_MODEL), ("ln_b", D_MODEL), ("rel_bias", N_BUCKETS * N_Q_HEADS),
                ("sinks", 128), ("loss", 128))


def _pack(fields):
    parts = []
    for name, width in SMALL_FIELDS:
        v = fields[name].reshape(1, -1).astype(F32)
        if v.shape[1] < width:
            v = jnp.pad(v, ((0, 0), (0, width - v.shape[1])))
        parts.append(v)
    return jnp.concatenate(parts, axis=1)


def _unpack(vec):
    out, off = {}, 0
    for name, width in SMALL_FIELDS:
        out[name] = vec[:, off:off + width]
        off += width
    return out


def kernel(x, c, w_ada, b_ada, w_in, rel_bias, sinks, conv_w, conv_b, conv_ln_g, conv_ln_b, w_pw, b_pw, w_out, ln_g, ln_b, loss_target, m_w_ada, m_b_ada, m_w_in, m_rel_bias, m_sinks, m_conv_w, m_conv_b, m_conv_ln_g, m_conv_ln_b, m_w_pw, m_b_pw, m_w_out, m_ln_g, m_ln_b, v_w_ada, v_b_ada, v_w_in, v_rel_bias, v_sinks, v_conv_w, v_conv_b, v_conv_ln_g, v_conv_ln_b, v_w_pw, v_b_pw, v_w_out, v_ln_g, v_ln_b):
    xi, yi, ci = _me()
    me = 4 * xi + 2 * yi + ci
    ncol = w_ada.shape[2]

    wt_in_loc = w_in[0].T.astype(BF16)
    conv_w_loc = jnp.pad(conv_w[0], ((0, CONV_ROWS - CONV_WIDTH), (0, 0)))

    b_ada_cols = lax.dynamic_slice(b_ada, (0, me * ncol), (1, ncol))
    mod_blocks, c_all = _ada_fwd(c, w_ada[0], b_ada_cols)
    mod = mod_blocks.reshape(1, 3 * D_MODEL)

    grad_x, r_in, r_out, r_pw, r_cw, small = _local_step(
        x[0], loss_target[0], mod, wt_in_loc, w_out[0].astype(BF16), w_pw[0].astype(BF16), conv_w_loc, rel_bias,
        sinks, conv_b, conv_ln_g, conv_ln_b, b_pw, ln_g, ln_b)

    gathered, summed = _small_gather(_pack(small))
    tot = _unpack(summed)
    dmod_all = gathered[:, 0, :3 * D_MODEL]
    loss = tot["loss"][0, 0]

    ct = jnp.zeros((D_MODEL, 128), BF16).at[:, :N_DEV].set(c_all[:, 0, :].T.astype(BF16))
    dm = jnp.zeros((128, ncol), BF16).at[:N_DEV, :].set(
        lax.dynamic_slice(dmod_all, (0, me * ncol), (N_DEV, ncol)).astype(BF16))
    g_w_ada = _small_mm(ct, dm, "grad_w_ada")

    gt_w_in = _sum8(r_in.reshape(N_DEV, D_IN // N_DEV, D_MODEL), "sum_w_in")
    g_w_out = _sum8(r_out.reshape(N_DEV, D_MODEL // N_DEV, D_MODEL), "sum_w_out")
    g_w_pw = _sum8(r_pw.reshape(N_DEV, D_CONV // N_DEV, D_CONV), "sum_w_pw")
    g_conv_w = _sum8(r_cw.reshape(N_DEV, CONV_ROWS, 128), "sum_conv_w")[:CONV_WIDTH]

    grads = {"w_ada": g_w_ada, "conv_w": g_conv_w, "w_pw": g_w_pw, "w_out": g_w_out}
    params = {"w_ada": (w_ada, m_w_ada, v_w_ada), "conv_w": (conv_w, m_conv_w, v_conv_w),
              "w_pw": (w_pw, m_w_pw, v_w_pw), "w_out": (w_out, m_w_out, v_w_out)}
    res = {}
    for name, g in grads.items():
        w_, m_, v_ = params[name]
        d_, nm_, nv_ = _adamw(w_[0], g, m_[0], v_[0], "adamw_" + name)
        res[name] = (g[None], d_[None], nm_[None], nv_[None])
    upd = _adamw(w_in[0].T, gt_w_in, m_w_in[0].T, v_w_in[0].T, "adamw_w_in")
    res["w_in"] = tuple(a.T[None] for a in (gt_w_in, *upd))

    small_params = {"b_ada": (b_ada, m_b_ada, v_b_ada), "b_pw": (b_pw, m_b_pw, v_b_pw),
                    "conv_ln_g": (conv_ln_g, m_conv_ln_g, v_conv_ln_g),
                    "conv_ln_b": (conv_ln_b, m_conv_ln_b, v_conv_ln_b), "conv_b": (conv_b, m_conv_b, v_conv_b),
                    "ln_g": (ln_g, m_ln_g, v_ln_g), "ln_b": (ln_b, m_ln_b, v_ln_b),
                    "rel_bias": (rel_bias, m_rel_bias, v_rel_bias), "sinks": (sinks, m_sinks, v_sinks)}
    to_small = lambda n, a: a.T if n == "rel_bias" else a
    key_of = {"b_ada": "dmod"}
    packs = []
    for j in range(3):
        fields = {key_of.get(n, n): to_small(n, t[j]) for n, t in small_params.items()}
        fields["loss"] = jnp.zeros((1, 1), F32)
        packs.append(_pack(fields))
    gsum = summed
    d_s, nm_s, nv_s = _adamw(packs[0], gsum, packs[1], packs[2], "adamw_small")
    outs_small = [_unpack(a) for a in (gsum, d_s, nm_s, nv_s)]
    for n, t in small_params.items():
        shape = t[0].shape
        vals = []
        for o in outs_small:
            a = o[key_of.get(n, n)]
            if n == "rel_bias":
                a = a.reshape(N_Q_HEADS, N_BUCKETS).T
            else:
                a = a[:, :shape[1]].reshape(shape)
            vals.append(a)
        res[n] = tuple(vals)

    order = ["w_ada", "b_ada", "w_in", "rel_bias", "sinks", "conv_w", "conv_b", "conv_ln_g", "conv_ln_b", "w_pw",
             "b_pw", "w_out", "ln_g", "ln_b"]
    out = [loss, grad_x[None]]
    for j in range(4):
        out += [res[n][j] for n in order]
    return tuple(out)
```

```python
import functools
import math

import jax
import jax.numpy as jnp
import numpy as np
from jax import lax
from jax.experimental import pallas as pl
from jax.experimental.pallas import tpu as pltpu

F32, BF16, I32 = jnp.float32, jnp.bfloat16, jnp.int32

D_MODEL = 2048
D_ATTN = 1024
D_CONV = 1024
D_KV = 256
HEAD_DIM = 64
N_Q_HEADS = 16
N_KV_HEADS = 4
GQA = 4
BLOCK = 128
CONV_WIDTH = 31
CONV_ROWS = 32
HALO = 32
N_BUCKETS = 32
MAX_DISTANCE = 128
LN_EPS = 1e-5
ALPHA = 2.0 ** 0.25
D_IN = 5632
N_DEV = 8
NEG = -1e30

ADAM_LR, ADAM_B1, ADAM_B2, ADAM_EPS, ADAM_WD, ADAM_STEP = 0.001, 0.9, 0.999, 1e-08, 0.01, 10

NT_DIMS = (((1,), (1,)), ((), ()))
TN_DIMS = (((0,), (0,)), ((), ()))
MIB = 1 << 20


def _pc(body, name, **kw):
    return pl.pallas_call(body, name=name, **kw)


def _cp(vmem_mib=None, sem=None):
    kw = {}
    if vmem_mib is not None:
        kw["vmem_limit_bytes"] = vmem_mib * MIB
    if sem is not None:
        kw["dimension_semantics"] = sem
    return pltpu.CompilerParams(**kw)


def _sig(x):
    return 1.0 / (1.0 + jnp.exp(-x))


def _dsilu(x, s):
    return s * (1.0 + x * (1.0 - s))


def _me():
    return lax.axis_index("x"), lax.axis_index("y"), lax.axis_index("c")


def _peer(k):
    x, y, c = _me()
    px = 1 - x if k & 4 else x
    py = 1 - y if k & 2 else y
    pc = 1 - c if k & 1 else c
    return (px, py, pc), 4 * px + 2 * py + pc


def _remote(src, dst, send_sem, recv_sem, dev):
    return pltpu.make_async_remote_copy(src_ref=src, dst_ref=dst, send_sem=send_sem, recv_sem=recv_sem,
                                        device_id=dev, device_id_type=pl.DeviceIdType.MESH)


HBM_SPEC = pl.BlockSpec(memory_space=pl.ANY)
VMEM_SPEC = pl.BlockSpec(memory_space=pltpu.VMEM)


def _comm_scratch(nt):
    return [pltpu.SemaphoreType.DMA((nt, N_DEV - 1)), pltpu.SemaphoreType.DMA((nt, N_DEV - 1)),
            pltpu.SemaphoreType.DMA((nt,))]


def _gather_shapes(shards):
    return [jax.ShapeDtypeStruct((N_DEV * s.shape[0], s.shape[1]), s.dtype) for s in shards]


def _block(ref, blk, rows):
    return ref.at[pl.ds(pl.multiple_of(blk * rows, 8), rows), :]


def _gather_copies(src, dst, sems):
    send_sems, recv_sems, local_sems = sems
    x, y, c = _me()
    me = 4 * x + 2 * y + c
    nt = len(src)
    local = [pltpu.make_async_copy(src[t], _block(dst[t], me, src[t].shape[0]), local_sems.at[t]) for t in range(nt)]
    sends, arrivals = [], []
    for k in range(1, N_DEV):
        dev, blk = _peer(k)
        for t in range(nt):
            r = src[t].shape[0]
            pair = (send_sems.at[t, k - 1], recv_sems.at[t, k - 1], dev)
            sends.append(_remote(src[t], _block(dst[t], me, r), *pair))
            arrivals.append(_remote(src[t], _block(dst[t], blk, r), *pair))
    return local, sends, arrivals


def _scatter_copies(src, dst, sems):
    send_sems, recv_sems, local_sems = sems
    x, y, c = _me()
    me = 4 * x + 2 * y + c
    nt = len(src)
    rows = [s.shape[0] // N_DEV for s in src]
    local = [pltpu.make_async_copy(_block(src[t], me, rows[t]), _block(dst[t], me, rows[t]), local_sems.at[t])
             for t in range(nt)]
    sends, arrivals = [], []
    for k in range(1, N_DEV):
        dev, blk = _peer(k)
        for t in range(nt):
            pair = (send_sems.at[t, k - 1], recv_sems.at[t, k - 1], dev)
            sends.append(_remote(_block(src[t], blk, rows[t]), _block(dst[t], me, rows[t]), *pair))
            arrivals.append(_remote(_block(src[t], me, rows[t]), _block(dst[t], blk, rows[t]), *pair))
    return local, sends, arrivals


def _comm_start(cps):
    local, sends, _ = cps
    for cp in local + sends:
        cp.start()


def _comm_wait(cps):
    local, sends, arrivals = cps
    for cp in arrivals:
        cp.wait_recv()
    for cp in sends:
        cp.wait_send()
    for cp in local:
        cp.wait()


IN_PIECES = ((0, 1536, BF16), (1536, 1024, F32), (2560, 2048, F32), (4608, 1024, F32))


def _inproj(x, mod, wt, gather):
    S = x.shape[0]
    tm = min(256, S)
    ni = S // tm
    ng = len(gather)
    npc = len(IN_PIECES)

    def body(*refs):
        x_ref, sh_ref, sc_ref, w_ref = refs[:4]
        outs = refs[4 + ng:]
        h_ref, piece_refs = outs[0], outs[1:1 + npc]
        i = pl.program_id(0)
        if ng:
            cps = _gather_copies(refs[4:4 + ng], outs[1 + npc:1 + npc + ng], outs[1 + npc + ng:])

            @pl.when(i == 0)
            def _():
                _comm_start(cps)

        h = (x_ref[...] * (1.0 + sc_ref[...]) + sh_ref[...]).astype(BF16)
        h_ref[...] = h
        for (r0, n, dt), o_ref in zip(IN_PIECES, piece_refs):
            o_ref[...] = lax.dot_general(h, w_ref[r0:r0 + n, :], NT_DIMS, preferred_element_type=F32).astype(dt)
        if ng:
            @pl.when(i == ni - 1)
            def _():
                _comm_wait(cps)

    tile = lambda n: pl.BlockSpec((tm, n), lambda i: (i, 0))
    res = _pc(
        body, "inproj", grid=(ni,),
        in_specs=[tile(D_MODEL),
                  pl.BlockSpec((1, D_MODEL), lambda i: (0, 0)),
                  pl.BlockSpec((1, D_MODEL), lambda i: (0, 1)),
                  pl.BlockSpec((D_IN, D_MODEL), lambda i: (0, 0), pipeline_mode=pl.Buffered(1))] + [HBM_SPEC] * ng,
        out_specs=[tile(D_MODEL)] + [tile(n) for _, n, _ in IN_PIECES] + [HBM_SPEC] * ng,
        out_shape=[jax.ShapeDtypeStruct((S, D_MODEL), BF16)]
        + [jax.ShapeDtypeStruct((S, n), dt) for _, n, dt in IN_PIECES] + _gather_shapes(gather),
        scratch_shapes=_comm_scratch(ng) if ng else [],
        compiler_params=_cp(56, ("arbitrary",)),
    )(x, mod, mod, wt, *gather)
    return res[:1 + npc], res[1 + npc:]


def _matmul_tn(a, b, out_prev, row_off, m_total, name):
    S, M = a.shape
    N = b.shape[1]
    tm, ts = 512, min(2048, S)
    assert row_off % tm == 0 and M % tm == 0
    ob = row_off // tm
    ns = S // ts

    def body(*refs):
        a_ref, b_ref = refs[0], refs[1]
        o_ref, acc = refs[-2], refs[-1]
        s = pl.program_id(1)

        @pl.when(s == 0)
        def _():
            acc[...] = jnp.zeros_like(acc)

        acc[...] += lax.dot_general(a_ref[...], b_ref[...], TN_DIMS, preferred_element_type=F32)

        @pl.when(s == ns - 1)
        def _():
            o_ref[...] = acc[...].astype(BF16)

    in_specs = [pl.BlockSpec((ts, tm), lambda i, s: (s, i)),
                pl.BlockSpec((ts, N), lambda i, s: (s, 0))]
    args = [a, b]
    aliases = {}
    if out_prev is not None:
        in_specs.append(pl.BlockSpec(memory_space=pl.ANY))
        args.append(out_prev)
        aliases = {2: 0}
    return _pc(
        body, name, grid=(M // tm, ns),
        in_specs=in_specs,
        out_specs=pl.BlockSpec((tm, N), lambda i, s: (ob + i, 0)),
        out_shape=jax.ShapeDtypeStruct((m_total, N), BF16),
        scratch_shapes=[pltpu.VMEM((tm, N), F32)],
        input_output_aliases=aliases,
        compiler_params=_cp(48, ("parallel", "arbitrary")),
    )(*args)


def _bucket_map():
    qi = jnp.arange(BLOCK, dtype=I32)[:, None]
    kj = jnp.arange(2 * BLOCK, dtype=I32)[None, :]
    dist = qi + BLOCK - kj
    in_window = (dist >= 0) & (dist < BLOCK)
    d0 = jnp.maximum(dist, 0)
    max_exact = N_BUCKETS // 2
    d = jnp.maximum(d0, 1).astype(F32)
    large = max_exact + (jnp.log(d / max_exact) / math.log(MAX_DISTANCE / max_exact)
                         * (N_BUCKETS - max_exact)).astype(I32)
    large = jnp.minimum(large, N_BUCKETS - 1)
    bucket = jnp.where(d0 < max_exact, d0, large)
    return jnp.where(in_window, bucket, -1).astype(I32)


def _bias_table(rel_bias, bmap):
    def body(rb_ref, bm_ref, o_ref):
        h = pl.program_id(0)
        bm = bm_ref[...]
        acc = jnp.full((BLOCK, 2 * BLOCK), NEG, F32)
        for b in range(N_BUCKETS):
            acc = jnp.where(bm == b, rb_ref[b, h], acc)
        o_ref[0] = acc

    return _pc(
        body, "bias_table", grid=(N_Q_HEADS,),
        in_specs=[pl.BlockSpec(memory_space=pltpu.SMEM),
                  pl.BlockSpec((BLOCK, 2 * BLOCK), lambda h: (0, 0))],
        out_specs=pl.BlockSpec((1, BLOCK, 2 * BLOCK), lambda h: (h, 0, 0)),
        out_shape=jax.ShapeDtypeStruct((N_Q_HEADS, BLOCK, 2 * BLOCK), F32),
    )(rel_bias, bmap)


def _bias_spec():
    return pl.BlockSpec((N_Q_HEADS, BLOCK, 2 * BLOCK), lambda n: (0, 0, 0))


def _relbias_grad(dbias, bmap):
    def body(db_ref, bm_ref, o_ref):
        bm = bm_ref[...]
        x = db_ref[0]
        lane = lax.broadcasted_iota(I32, (1, 128), 1)
        row = jnp.zeros((1, 128), F32)
        for b in range(N_BUCKETS):
            row = jnp.where(lane == b, jnp.sum(jnp.where(bm == b, x, 0.0)), row)
        o_ref[0] = jnp.broadcast_to(row, (8, 128))

    out = _pc(
        body, "relbias_grad", grid=(N_Q_HEADS,),
        in_specs=[pl.BlockSpec((1, BLOCK, 2 * BLOCK), lambda h: (h, 0, 0)),
                  pl.BlockSpec((BLOCK, 2 * BLOCK), lambda h: (0, 0))],
        out_specs=pl.BlockSpec((1, 8, 128), lambda h: (h, 0, 0)),
        out_shape=jax.ShapeDtypeStruct((N_Q_HEADS, 8, 128), F32),
    )(dbias, bmap)
    return out[:, 0, :]


Q_SCALE = HEAD_DIM ** -0.5
assert math.frexp(Q_SCALE)[0] == 0.5


def _stack_heads(ref, hk, rows=slice(None)):
    return jnp.concatenate([ref[rows, pl.ds(256 * hk + 64 * g, 64)] for g in range(GQA)], axis=0) * Q_SCALE


def _sink_col(sink_ref, hk):
    row = lax.broadcasted_iota(I32, (GQA * BLOCK, 1), 0)
    s = jnp.full((GQA * BLOCK, 1), sink_ref[0, 4 * hk + 3], F32)
    for g in (2, 1, 0):
        s = jnp.where(row < (g + 1) * BLOCK, sink_ref[0, 4 * hk + g], s)
    return s


def _attn_probs(q4, kw, bias_ref, sink_ref, hk, first_mask):
    s = lax.dot_general(q4, kw, NT_DIMS, preferred_element_type=F32)
    s = s + bias_ref[4 * hk:4 * hk + 4].reshape(GQA * BLOCK, 2 * BLOCK)
    if first_mask is not None:
        s = jnp.where(first_mask, s, NEG)
    sink = _sink_col(sink_ref, hk)
    m = jnp.maximum(jnp.max(s, axis=1, keepdims=True), sink)
    e = jnp.exp(s - m)
    es = jnp.exp(sink - m)
    inv = 1.0 / (jnp.sum(e, axis=1, keepdims=True) + es)
    return e * inv, es * inv


def _attn_bwd(qkv, g_attn, a_out, dycat, bias, sinks, scatter=()):
    S = qkv.shape[0]
    nb = S // BLOCK
    R = GQA * BLOCK
    ns = len(scatter)

    def body(*refs):
        (q_ref, kc_ref, kp_ref, vc_ref, vp_ref, g_ref, a_ref, dy_ref, bias_ref, sink_ref) = refs[:10]
        dqkv_ref, dg_ref, dbias_ref, dsink_ref = refs[10 + ns:14 + ns]
        dq_scr, dq_new, dk_scr, dv_scr, dkw_scr, dvw_scr, ds_scr = refs[14 + 2 * ns:21 + 2 * ns]
        n = pl.program_id(0)
        if ns:
            cps = _scatter_copies(refs[10:10 + ns], refs[14 + ns:14 + 2 * ns], refs[21 + 2 * ns:])

            @pl.when(n == 0)
            def _():
                _comm_start(cps)

        @pl.when(n == 0)
        def _():
            dbias_ref[...] = jnp.zeros_like(dbias_ref)
            ds_scr[...] = jnp.zeros_like(ds_scr)
            dq_scr[...] = jnp.zeros_like(dq_scr)
            dk_scr[...] = jnp.zeros_like(dk_scr)
            dv_scr[...] = jnp.zeros_like(dv_scr)

        @pl.when(n < nb)
        def _():
            kj = lax.broadcasted_iota(I32, (R, 2 * BLOCK), 1)
            first_mask = (n > 0) | (kj >= BLOCK)
            for hk in range(N_KV_HEADS):
                q4 = _stack_heads(q_ref, hk)
                ks = pl.ds(64 * hk, 64)
                kw = jnp.concatenate([kp_ref[:, ks], kc_ref[:, ks]], axis=0)
                vw = jnp.concatenate([vp_ref[:, ks], vc_ref[:, ks]], axis=0)
                p, psink = _attn_probs(q4, kw, bias_ref, sink_ref, hk, first_mask)
                da_parts, a_parts = [], []
                for g in range(GQA):
                    sl = pl.ds(256 * hk + 64 * g, 64)
                    gg = g_ref[:, sl]
                    sg = _sig(gg)
                    dyg = dy_ref[:, sl]
                    ag = a_ref[:, sl]
                    da_parts.append(dyg * (gg * sg))
                    a_parts.append(ag)
                    dg_ref[:, sl] = (dyg * ag * _dsilu(gg, sg)).astype(BF16)
                da4 = jnp.concatenate(da_parts, axis=0)
                a4 = jnp.concatenate(a_parts, axis=0)
                delta = jnp.sum(da4 * a4, axis=1, keepdims=True)
                da4b = da4.astype(BF16)
                dp = lax.dot_general(da4b, vw, NT_DIMS, preferred_element_type=F32)
                ds = p * (dp - delta)
                ds_scr[hk] += -psink * delta
                dbias_ref[4 * hk:4 * hk + 4] += ds.reshape(GQA, BLOCK, 2 * BLOCK)
                dsb = ds.astype(BF16)
                dq4 = jnp.dot(dsb, kw, preferred_element_type=F32) * Q_SCALE
                for g in range(GQA):
                    dq_new[:, pl.ds(256 * hk + 64 * g, 64)] = dq4[BLOCK * g:BLOCK * (g + 1)].astype(BF16)
                dkw_scr[:, ks] = lax.dot_general(dsb, q4, TN_DIMS, preferred_element_type=F32)
                dvw_scr[:, ks] = lax.dot_general(p.astype(BF16), da4b, TN_DIMS, preferred_element_type=F32)

        @pl.when(n == nb)
        def _():
            dkw_scr[0:BLOCK, :] = jnp.zeros((BLOCK, D_KV), F32)
            dvw_scr[0:BLOCK, :] = jnp.zeros((BLOCK, D_KV), F32)

        dqkv_ref[:, 0:D_ATTN] = dq_scr[...]
        dqkv_ref[:, D_ATTN:D_ATTN + D_KV] = (dk_scr[...] + dkw_scr[0:BLOCK, :]).astype(BF16)
        dqkv_ref[:, D_ATTN + D_KV:D_ATTN + 2 * D_KV] = (dv_scr[...] + dvw_scr[0:BLOCK, :]).astype(BF16)
        dq_scr[...] = dq_new[...]
        dk_scr[...] = dkw_scr[BLOCK:2 * BLOCK, :]
        dv_scr[...] = dvw_scr[BLOCK:2 * BLOCK, :]

        @pl.when(n == nb)
        def _():
            lane = lax.broadcasted_iota(I32, (1, 128), 1)
            row = jnp.zeros((1, 128), F32)
            for hk in range(N_KV_HEADS):
                col = ds_scr[hk]
                for g in range(GQA):
                    row = jnp.where(lane == 4 * hk + g, jnp.sum(col[BLOCK * g:BLOCK * (g + 1)]), row)
            dsink_ref[...] = jnp.broadcast_to(row, (8, 128))
            if ns:
                _comm_wait(cps)

    cur = lambda n: jnp.minimum(n, nb - 1)
    prev = lambda n: jnp.clip(n - 1, 0, nb - 1)
    res = _pc(
        body, "attn_bwd", grid=(nb + 1,),
        in_specs=[pl.BlockSpec((BLOCK, D_ATTN), lambda n: (cur(n), 0)),
                  pl.BlockSpec((BLOCK, D_KV), lambda n: (cur(n), 4)),
                  pl.BlockSpec((BLOCK, D_KV), lambda n: (prev(n), 4)),
                  pl.BlockSpec((BLOCK, D_KV), lambda n: (cur(n), 5)),
                  pl.BlockSpec((BLOCK, D_KV), lambda n: (prev(n), 5)),
                  pl.BlockSpec((BLOCK, D_ATTN), lambda n: (cur(n), 0)),
                  pl.BlockSpec((BLOCK, D_ATTN), lambda n: (cur(n), 0)),
                  pl.BlockSpec((BLOCK, D_ATTN), lambda n: (cur(n), 0)),
                  _bias_spec(),
                  pl.BlockSpec(memory_space=pltpu.SMEM)] + [HBM_SPEC] * ns,
        out_specs=[pl.BlockSpec((BLOCK, D_ATTN + 2 * D_KV), lambda n: (prev(n), 0)),
                   pl.BlockSpec((BLOCK, D_ATTN), lambda n: (cur(n), 0)),
                   pl.BlockSpec((N_Q_HEADS, BLOCK, 2 * BLOCK), lambda n: (0, 0, 0)),
                   pl.BlockSpec((8, 128), lambda n: (0, 0))] + [HBM_SPEC] * ns,
        out_shape=[jax.ShapeDtypeStruct((S, D_ATTN + 2 * D_KV), BF16),
                   jax.ShapeDtypeStruct((S, D_ATTN), BF16),
                   jax.ShapeDtypeStruct((N_Q_HEADS, BLOCK, 2 * BLOCK), F32),
                   jax.ShapeDtypeStruct((8, 128), F32)] + [jax.ShapeDtypeStruct(f.shape, f.dtype) for f in scatter],
        scratch_shapes=[pltpu.VMEM((BLOCK, D_ATTN), BF16), pltpu.VMEM((BLOCK, D_ATTN), BF16),
                        pltpu.VMEM((BLOCK, D_KV), F32), pltpu.VMEM((BLOCK, D_KV), F32),
                        pltpu.VMEM((2 * BLOCK, D_KV), F32), pltpu.VMEM((2 * BLOCK, D_KV), F32),
                        pltpu.VMEM((N_KV_HEADS, R, 1), F32)] + (_comm_scratch(ns) if ns else []),
        compiler_params=_cp(48, ("arbitrary",)),
    )(qkv, qkv, qkv, qkv, qkv, g_attn, a_out, dycat, bias, sinks, *scatter)
    return res[:4], res[4:]


def _conv_tile(S):
    return min(256, S)


def _shifted(win, s):
    return win if s == 0 else pltpu.roll(win, win.shape[0] - s, axis=0)


def _conv_taps(win, cw_ref, lanes, rows):
    acc = jnp.zeros((rows, win.shape[1]), F32)
    for s in range(8):
        ws = _shifted(win, s)
        for aa in range(5):
            j = 8 * aa + s - 2
            if 0 <= j < CONV_WIDTH:
                acc = acc + ws[8 * aa:8 * aa + rows] * cw_ref[j:j + 1, lanes]
    return acc


def _mixer_fwd(qkv, g_attn, glu, g_conv, bias, sinks, conv_w, conv_b, ln_g, ln_b, w_pw, b_pw, gather=()):
    S = qkv.shape[0]
    RB = 2
    TR = RB * BLOCK
    nb = S // TR
    hb = TR // HALO
    LG = D_CONV // (RB * N_KV_HEADS)
    ng = len(gather)

    def body(*refs):
        (q_ref, kc_ref, kp_ref, vc_ref, vp_ref, g_ref, bias_ref, sink_ref,
         a_in, ah_in, b_in, bh_in, gc_ref, cw_ref, cb_ref, lg_ref, lb_ref, wpw_ref, bpw_ref) = refs[:19]
        y_ref, a_ref, u1_ref, u3_ref, p_ref = refs[19 + ng:24 + ng]
        win = refs[24 + 2 * ng]
        n = pl.program_id(0)
        if ng:
            cps = _gather_copies(refs[19:19 + ng], refs[24 + ng:24 + 2 * ng], refs[25 + 2 * ng:])

            @pl.when(n == 0)
            def _():
                _comm_start(cps)

        kj = lax.broadcasted_iota(I32, (GQA * BLOCK, 2 * BLOCK), 1)
        first_mask = (n > 0) | (kj >= BLOCK)
        win[0:HALO, :] = jnp.where(n > 0, ah_in[...] * _sig(bh_in[...]), 0.0)
        win[HALO:HALO + TR, :] = a_in[...] * _sig(b_in[...])
        piece = 0
        for s in range(RB):
            rows = slice(BLOCK * s, BLOCK * (s + 1))
            before = slice(BLOCK * (s - 1), BLOCK * s)
            for hk in range(N_KV_HEADS):
                q4 = _stack_heads(q_ref, hk, rows)
                ks = pl.ds(64 * hk, 64)
                k_prev, v_prev = (kp_ref[:, ks], vp_ref[:, ks]) if s == 0 else (kc_ref[before, ks], vc_ref[before, ks])
                kw = jnp.concatenate([k_prev, kc_ref[rows, ks]], axis=0)
                vw = jnp.concatenate([v_prev, vc_ref[rows, ks]], axis=0)
                p, _ = _attn_probs(q4, kw, bias_ref, sink_ref, hk, first_mask if s == 0 else None)
                o4 = jnp.dot(p.astype(BF16), vw, preferred_element_type=F32)
                for g in range(GQA):
                    sl = pl.ds(256 * hk + 64 * g, 64)
                    og = o4[BLOCK * g:BLOCK * (g + 1)]
                    gg = g_ref[rows, sl]
                    a_ref[rows, sl] = og
                    y_ref[rows, sl] = (og * (gg * _sig(gg))).astype(BF16)
                lanes = pl.ds(LG * piece, LG)
                u1_ref[:, lanes] = _conv_taps(win[:, lanes], cw_ref, lanes, TR) + cb_ref[:, lanes]
                piece += 1
        u1 = u1_ref[...]
        mu = jnp.mean(u1, axis=1, keepdims=True)
        uc = u1 - mu
        rstd = lax.rsqrt(jnp.mean(uc * uc, axis=1, keepdims=True) + LN_EPS)
        u2 = uc * rstd * lg_ref[...] + lb_ref[...]
        u3 = (u2 * _sig(u2)).astype(BF16)
        u3_ref[...] = u3
        pw = jnp.dot(u3, wpw_ref[...], preferred_element_type=F32) + bpw_ref[...]
        p_ref[...] = pw
        gc = gc_ref[...]
        y_ref[:, D_ATTN:] = (pw * (gc * _sig(gc))).astype(BF16)
        if ng:
            @pl.when(n == nb - 1)
            def _():
                _comm_wait(cps)

    prev = lambda n: jnp.maximum(RB * n - 1, 0)
    halo = lambda n: jnp.maximum(n * hb - 1, 0)
    vec = pl.BlockSpec((1, D_CONV), lambda n: (0, 0))
    blk = lambda w, j: pl.BlockSpec((TR, w), lambda n: (n, j))
    res = _pc(
        body, "mixer_fwd", grid=(nb,),
        in_specs=[blk(D_ATTN, 0),
                  blk(D_KV, 4), pl.BlockSpec((BLOCK, D_KV), lambda n: (prev(n), 4)),
                  blk(D_KV, 5), pl.BlockSpec((BLOCK, D_KV), lambda n: (prev(n), 5)),
                  blk(D_ATTN, 0),
                  _bias_spec(),
                  pl.BlockSpec(memory_space=pltpu.SMEM),
                  blk(D_CONV, 0), pl.BlockSpec((HALO, D_CONV), lambda n: (halo(n), 0)),
                  blk(D_CONV, 1), pl.BlockSpec((HALO, D_CONV), lambda n: (halo(n), 1)),
                  blk(D_CONV, 0),
                  pl.BlockSpec((CONV_ROWS, D_CONV), lambda n: (0, 0)),
                  vec, vec, vec,
                  pl.BlockSpec((D_CONV, D_CONV), lambda n: (0, 0)),
                  vec] + [HBM_SPEC] * ng,
        out_specs=[blk(2 * D_ATTN, 0), blk(D_ATTN, 0), blk(D_CONV, 0), blk(D_CONV, 0), blk(D_CONV, 0)]
        + [HBM_SPEC] * ng,
        out_shape=[jax.ShapeDtypeStruct((S, 2 * D_ATTN), BF16),
                   jax.ShapeDtypeStruct((S, D_ATTN), F32),
                   jax.ShapeDtypeStruct((S, D_CONV), F32),
                   jax.ShapeDtypeStruct((S, D_CONV), BF16),
                   jax.ShapeDtypeStruct((S, D_CONV), F32)] + _gather_shapes(gather),
        scratch_shapes=[pltpu.VMEM((TR + HALO, D_CONV), F32)] + (_comm_scratch(ng) if ng else []),
        compiler_params=_cp(56, ("arbitrary",)),
    )(qkv, qkv, qkv, qkv, qkv, g_attn, bias, sinks, glu, glu, glu, glu, g_conv, conv_w, conv_b, ln_g, ln_b, w_pw,
      b_pw, *gather)
    return res[:5], res[5:]


def _conv_bwd_a(dycat, g_conv, p_out, u1, ln_g, ln_b, w_pw):
    S = u1.shape[0]
    T = _conv_tile(S)

    def body(dy_ref, gc_ref, p_ref, u1_ref, lg_ref, lb_ref, wpw_ref,
             dp_ref, dgc_ref, du1_ref, gbpw_ref, glg_ref, glb_ref, gcb_ref):
        i = pl.program_id(0)

        @pl.when(i == 0)
        def _():
            for r in (gbpw_ref, glg_ref, glb_ref, gcb_ref):
                r[...] = jnp.zeros_like(r)

        dy = dy_ref[...]
        gc = gc_ref[...]
        sg = _sig(gc)
        dp = dy * (gc * sg)
        dgc_ref[...] = (dy * p_ref[...] * _dsilu(gc, sg)).astype(BF16)
        gbpw_ref[...] += jnp.sum(dp, axis=0, keepdims=True)
        dpb = dp.astype(BF16)
        dp_ref[...] = dpb
        du3 = lax.dot_general(dpb, wpw_ref[...], NT_DIMS, preferred_element_type=F32)
        u1 = u1_ref[...]
        mu = jnp.mean(u1, axis=1, keepdims=True)
        uc = u1 - mu
        rstd = lax.rsqrt(jnp.mean(uc * uc, axis=1, keepdims=True) + LN_EPS)
        uh = uc * rstd
        lg = lg_ref[...]
        u2 = uh * lg + lb_ref[...]
        s2 = _sig(u2)
        du2 = du3 * _dsilu(u2, s2)
        glg_ref[...] += jnp.sum(du2 * uh, axis=0, keepdims=True)
        glb_ref[...] += jnp.sum(du2, axis=0, keepdims=True)
        duh = du2 * lg
        du1 = rstd * (duh - jnp.mean(duh, axis=1, keepdims=True) - uh * jnp.mean(duh * uh, axis=1, keepdims=True))
        du1_ref[...] = du1
        gcb_ref[...] += jnp.sum(du1, axis=0, keepdims=True)

    vec = pl.BlockSpec((1, D_CONV), lambda i: (0, 0))
    tile = pl.BlockSpec((T, D_CONV), lambda i: (i, 0))
    vshape = jax.ShapeDtypeStruct((1, D_CONV), F32)
    return _pc(
        body, "conv_bwd_a", grid=(S // T,),
        in_specs=[pl.BlockSpec((T, D_CONV), lambda i: (i, 1)), tile, tile, tile, vec, vec,
                  pl.BlockSpec((D_CONV, D_CONV), lambda i: (0, 0))],
        out_specs=[tile, tile, tile, vec, vec, vec, vec],
        out_shape=[jax.ShapeDtypeStruct((S, D_CONV), BF16), jax.ShapeDtypeStruct((S, D_CONV), BF16),
                   jax.ShapeDtypeStruct((S, D_CONV), F32), vshape, vshape, vshape, vshape],
        compiler_params=_cp(48, ("arbitrary",)),
    )(dycat, g_conv, p_out, u1, ln_g, ln_b, w_pw)


def _conv_bwd_b(du1, glu, conv_w):
    S = du1.shape[0]
    T = _conv_tile(S)
    hb = T // HALO
    nt = S // T
    last_h = S // HALO - 1

    def body(du_ref, dun_ref, a_ref, b_ref, cw_ref, dab_ref, gw_ref):
        i = pl.program_id(0)

        @pl.when(i == 0)
        def _():
            gw_ref[...] = jnp.zeros_like(gw_ref)

        for lg in range(D_CONV // 128):
            lanes = pl.ds(128 * lg, 128)
            a = a_ref[:, lanes]
            sb = _sig(b_ref[:, lanes])
            u0 = a * sb
            du1n = jnp.where(i < nt - 1, dun_ref[:, lanes], 0.0)
            win2 = jnp.concatenate([du_ref[:, lanes], du1n], axis=0)
            acc = jnp.zeros((T, 128), F32)
            for s in range(8):
                w2 = _shifted(win2, s)
                for aa in range(4):
                    j = CONV_WIDTH - 1 - (8 * aa + s)
                    if 0 <= j < CONV_WIDTH:
                        xo = w2[8 * aa:8 * aa + T]
                        acc = acc + xo * cw_ref[j:j + 1, lanes]
                        gw_ref[j:j + 1, lanes] += jnp.sum(xo * u0, axis=0, keepdims=True)
            dab_ref[:, lanes] = (acc * sb).astype(BF16)
            dab_ref[:, pl.ds(D_CONV + 128 * lg, 128)] = (acc * a * sb * (1.0 - sb)).astype(BF16)

    nxt = lambda i: jnp.minimum((i + 1) * hb, last_h)
    return _pc(
        body, "conv_bwd_b", grid=(nt,),
        in_specs=[pl.BlockSpec((T, D_CONV), lambda i: (i, 0)),
                  pl.BlockSpec((HALO, D_CONV), lambda i: (nxt(i), 0)),
                  pl.BlockSpec((T, D_CONV), lambda i: (i, 0)),
                  pl.BlockSpec((T, D_CONV), lambda i: (i, 1)),
                  pl.BlockSpec((CONV_ROWS, D_CONV), lambda i: (0, 0))],
        out_specs=[pl.BlockSpec((T, 2 * D_CONV), lambda i: (i, 0)),
                   pl.BlockSpec((CONV_ROWS, D_CONV), lambda i: (0, 0))],
        out_shape=[jax.ShapeDtypeStruct((S, 2 * D_CONV), BF16),
                   jax.ShapeDtypeStruct((CONV_ROWS, D_CONV), F32)],
        compiler_params=_cp(48, ("arbitrary",)),
    )(du1, du1, glu, glu, conv_w)


def _outproj_ln(ycat, w_out, x, target, mod, ln_g, ln_b):
    S = x.shape[0]
    tm = min(256, S)

    def body(yc_ref, w_ref, x_ref, t_ref, gate_ref, lg_ref, lb_ref,
             dz_ref, dy_ref, dyc_ref, loss_ref, glg_ref, glb_ref, dgate_ref):
        i = pl.program_id(0)

        @pl.when(i == 0)
        def _():
            for r in (loss_ref, glg_ref, glb_ref, dgate_ref):
                r[...] = jnp.zeros_like(r)

        w = w_ref[...]
        y = jnp.dot(yc_ref[...], w, preferred_element_type=F32)
        gate = gate_ref[...]
        z = ALPHA * x_ref[...] + gate * y
        mu = jnp.mean(z, axis=1, keepdims=True)
        zc = z - mu
        rstd = lax.rsqrt(jnp.mean(zc * zc, axis=1, keepdims=True) + LN_EPS)
        zh = zc * rstd
        lg = lg_ref[...]
        err = zh * lg + lb_ref[...] - t_ref[...]
        loss_ref[...] += 0.5 * jnp.sum(jnp.sum(err * err, axis=1, keepdims=True)) / D_MODEL
        dout = err * (1.0 / D_MODEL)
        glg_ref[...] += jnp.sum(dout * zh, axis=0, keepdims=True)
        glb_ref[...] += jnp.sum(dout, axis=0, keepdims=True)
        dzh = dout * lg
        dz = rstd * (dzh - jnp.mean(dzh, axis=1, keepdims=True) - zh * jnp.mean(dzh * zh, axis=1, keepdims=True))
        dz_ref[...] = dz
        dgate_ref[...] += jnp.sum(dz * y, axis=0, keepdims=True)
        dy = (dz * gate).astype(BF16)
        dy_ref[...] = dy
        dyc_ref[...] = lax.dot_general(dy, w, NT_DIMS, preferred_element_type=F32).astype(BF16)

    vec = pl.BlockSpec((1, D_MODEL), lambda i: (0, 0))
    tile = pl.BlockSpec((tm, D_MODEL), lambda i: (i, 0))
    vshape = jax.ShapeDtypeStruct((1, D_MODEL), F32)
    return _pc(
        body, "outproj_ln", grid=(S // tm,),
        in_specs=[tile, pl.BlockSpec((D_MODEL, D_MODEL), lambda i: (0, 0), pipeline_mode=pl.Buffered(1)), tile, tile,
                  pl.BlockSpec((1, D_MODEL), lambda i: (0, 2)), vec, vec],
        out_specs=[tile, tile, tile, pl.BlockSpec((8, 128), lambda i: (0, 0)), vec, vec, vec],
        out_shape=[jax.ShapeDtypeStruct((S, D_MODEL), F32), jax.ShapeDtypeStruct((S, D_MODEL), BF16),
                   jax.ShapeDtypeStruct((S, D_MODEL), BF16), jax.ShapeDtypeStruct((8, 128), F32),
                   vshape, vshape, vshape],
        compiler_params=_cp(56, ("arbitrary",)),
    )(ycat, w_out, x, target, mod, ln_g, ln_b)


def _dh_kernel(segs, wt, dz, x, mod, scatter=()):
    S = x.shape[0]
    tm = min(256, S)
    row0 = [0]
    for a in segs:
        row0.append(row0[-1] + a.shape[1])
    assert row0[-1] == wt.shape[0]
    nseg = len(segs)
    ns = len(scatter)
    ni = S // tm

    def body(*refs):
        seg_refs = refs[:nseg]
        w_ref, dz_ref, x_ref, sc_ref = refs[nseg:nseg + 4]
        outs = refs[nseg + 4 + ns:]
        gx_ref, dsh_ref, dsc_ref = outs[:3]
        i = pl.program_id(0)
        if ns:
            cps = _scatter_copies(refs[nseg + 4:nseg + 4 + ns], outs[3:3 + ns], outs[3 + ns:])

        @pl.when(i == 0)
        def _():
            dsh_ref[...] = jnp.zeros_like(dsh_ref)
            dsc_ref[...] = jnp.zeros_like(dsc_ref)
            if ns:
                _comm_start(cps)

        dh = jnp.dot(seg_refs[0][...], w_ref[row0[0]:row0[1], :], preferred_element_type=F32)
        for t in range(1, nseg):
            dh = dh + jnp.dot(seg_refs[t][...], w_ref[row0[t]:row0[t + 1], :], preferred_element_type=F32)
        gx_ref[...] = ALPHA * dz_ref[...] + dh * (1.0 + sc_ref[...])
        dsh_ref[...] += jnp.sum(dh, axis=0, keepdims=True)
        dsc_ref[...] += jnp.sum(dh * x_ref[...], axis=0, keepdims=True)

        if ns:
            @pl.when(i == ni - 1)
            def _():
                _comm_wait(cps)

    tile = pl.BlockSpec((tm, D_MODEL), lambda i: (i, 0))
    vec = pl.BlockSpec((1, D_MODEL), lambda i: (0, 0))
    vshape = jax.ShapeDtypeStruct((1, D_MODEL), F32)
    res = _pc(
        body, "dh_gradx", grid=(ni,),
        in_specs=[pl.BlockSpec((tm, a.shape[1]), lambda i: (i, 0)) for a in segs] + [
            pl.BlockSpec(wt.shape, lambda i: (0, 0), pipeline_mode=pl.Buffered(1)), tile, tile,
            pl.BlockSpec((1, D_MODEL), lambda i: (0, 1))] + [HBM_SPEC] * ns,
        out_specs=[tile, vec, vec] + [HBM_SPEC] * ns,
        out_shape=[jax.ShapeDtypeStruct((S, D_MODEL), F32), vshape, vshape]
        + [jax.ShapeDtypeStruct(f.shape, f.dtype) for f in scatter],
        scratch_shapes=_comm_scratch(ns) if ns else [],
        compiler_params=_cp(58, ("arbitrary",)),
    )(*segs, wt, dz, x, mod, *scatter)
    return res[:3], res[3:]


def _row_tile(rows, cols):
    if rows * cols * 4 <= 2 * MIB or rows % 8:
        return rows
    tr = max(8, (2 * MIB // (cols * 4)) // 8 * 8)
    while rows % tr:
        tr -= 8
    return tr


def _sum8(recv, name):
    _, R, C = recv.shape
    tr = _row_tile(R, C)

    def body(r_ref, o_ref):
        acc = r_ref[0].astype(F32)
        for d in range(1, N_DEV):
            acc = acc + r_ref[d].astype(F32)
        o_ref[...] = acc

    return _pc(
        body, name, grid=(R // tr,),
        in_specs=[pl.BlockSpec((N_DEV, tr, C), lambda i: (0, i, 0))],
        out_specs=pl.BlockSpec((tr, C), lambda i: (i, 0)),
        out_shape=jax.ShapeDtypeStruct((R, C), F32),
        compiler_params=_cp(40, ("parallel",)),
    )(recv)


def _adamw(w, g, m, v, name):
    R, C = w.shape
    tr = _row_tile(R, C)
    parts = g.ndim == 3

    def body(w_ref, g_ref, m_ref, v_ref, go_ref, d_ref, nm_ref, nv_ref):
        if parts:
            g_ = g_ref[0].astype(F32)
            for d in range(1, N_DEV):
                g_ = g_ + g_ref[d].astype(F32)
        else:
            g_ = g_ref[...]
        go_ref[...] = g_
        m_ = ADAM_B1 * m_ref[...] + (1.0 - ADAM_B1) * g_
        v_ = ADAM_B2 * v_ref[...] + (1.0 - ADAM_B2) * (g_ * g_)
        m_hat = m_ / (1.0 - ADAM_B1 ** ADAM_STEP)
        v_hat = v_ / (1.0 - ADAM_B2 ** ADAM_STEP)
        d_ref[...] = -ADAM_LR * (m_hat / (jnp.sqrt(v_hat) + ADAM_EPS) + ADAM_WD * w_ref[...])
        nm_ref[...] = m_
        nv_ref[...] = v_

    spec = pl.BlockSpec((tr, C), lambda i: (i, 0))
    gspec = pl.BlockSpec((N_DEV, tr, C), lambda i: (0, i, 0)) if parts else spec
    shape = jax.ShapeDtypeStruct((R, C), F32)
    return _pc(
        body, name, grid=(R // tr,),
        in_specs=[spec, gspec, spec, spec], out_specs=[spec] * 4, out_shape=[shape] * 4,
        compiler_params=_cp(40, ("parallel",)),
    )(w, g, m, v)


def _small_mm(a, b, name):
    def body(a_ref, b_ref, o_ref):
        o_ref[...] = jnp.dot(a_ref[...], b_ref[...], preferred_element_type=F32)

    return _pc(body, name, out_shape=jax.ShapeDtypeStruct((a.shape[0], b.shape[1]), F32),
               compiler_params=_cp(40))(a, b)


def _all_gather_two_level(shards, name):
    nt = len(shards)

    def body(*refs):
        src, dst = refs[:nt], refs[nt:2 * nt]
        send_sems, recv_sems, local_sems = refs[2 * nt:]
        x, y, c = _me()
        sibling = (x, y, 1 - c)
        chips = [(1 - x, y), (x, 1 - y), (1 - x, 1 - y)]

        def blk(t, px, py, pc):
            return _block(dst[t], 4 * px + 2 * py + pc, src[t].shape[0])

        def copy(t, k, to, block, from_src):
            return _remote(src[t] if from_src else blk(t, *block), blk(t, *block),
                           send_sems.at[t, k], recv_sems.at[t, k], to)

        local = [pltpu.make_async_copy(src[t], blk(t, x, y, c), local_sems.at[t]) for t in range(nt)]
        for cp in local:
            cp.start()
        sends = []
        for t in range(nt):
            sends.append(copy(t, 0, sibling, (x, y, c), True))
            sends += [copy(t, 1 + j, (*chip, c), (x, y, c), True) for j, chip in enumerate(chips)]
        for cp in sends:
            cp.start()
        for j, chip in enumerate(chips):
            for t in range(nt):
                copy(t, 1 + j, (x, y, c), (*chip, c), False).wait_recv()
                fwd = copy(t, 4 + j, sibling, (*chip, c), False)
                fwd.start()
                sends.append(fwd)
        for t in range(nt):
            copy(t, 0, (x, y, c), (x, y, 1 - c), False).wait_recv()
            for j, chip in enumerate(chips):
                copy(t, 4 + j, (x, y, c), (*chip, 1 - c), False).wait_recv()
        for cp in sends:
            cp.wait_send()
        for cp in local:
            cp.wait()

    return _pc(
        body, name,
        in_specs=[HBM_SPEC] * nt, out_specs=[HBM_SPEC] * nt,
        out_shape=_gather_shapes(shards), scratch_shapes=_comm_scratch(nt),
        compiler_params=pltpu.CompilerParams(has_side_effects=True),
    )(*shards)


def _ada_fwd(c, w_ada, b_ada_cols):
    ncol = w_ada.shape[1]

    def body(c_ref, w_ref, b_ref, mod_ref, call_ref, cact, cmat, mloc, send1, recv1, send2, recv2):
        x, y, z = _me()
        me = 4 * x + 2 * y + z
        cv = c_ref[...]
        cact[...] = cv * _sig(cv)
        call_ref[me] = cact[...]
        sends = []
        for k in range(1, N_DEV):
            dev, _ = _peer(k)
            cp = _remote(cact, call_ref.at[me], send1.at[k - 1], recv1.at[k - 1], dev)
            cp.start()
            sends.append(cp)
        for k in range(1, N_DEV):
            dev, blk = _peer(k)
            _remote(cact, call_ref.at[blk], send1.at[k - 1], recv1.at[k - 1], dev).wait_recv()
        cmat[...] = jnp.zeros_like(cmat)
        for b in range(N_DEV):
            cmat[b:b + 1, :] = call_ref[b]
        m = jnp.dot(cmat[...].astype(BF16), w_ref[...].astype(BF16), preferred_element_type=F32) + b_ref[...]
        for b in range(N_DEV):
            mloc[b] = m[b:b + 1, :]
        mod_ref[me] = mloc[me]
        for k in range(1, N_DEV):
            dev, blk = _peer(k)
            cp = _remote(mloc.at[blk], mod_ref.at[me], send2.at[k - 1], recv2.at[k - 1], dev)
            cp.start()
            sends.append(cp)
        for k in range(1, N_DEV):
            dev, blk = _peer(k)
            _remote(mloc.at[me], mod_ref.at[blk], send2.at[k - 1], recv2.at[k - 1], dev).wait_recv()
        for cp in sends:
            cp.wait_send()

    return _pc(
        body, "ada_fwd",
        in_specs=[VMEM_SPEC] * 3, out_specs=[VMEM_SPEC] * 2,
        out_shape=[jax.ShapeDtypeStruct((N_DEV, 1, ncol), F32), jax.ShapeDtypeStruct((N_DEV, 1, D_MODEL), F32)],
        scratch_shapes=[pltpu.VMEM((1, D_MODEL), F32), pltpu.VMEM((16, D_MODEL), F32),
                        pltpu.VMEM((N_DEV, 1, ncol), F32)] + [pltpu.SemaphoreType.DMA((N_DEV - 1,))] * 4,
        compiler_params=pltpu.CompilerParams(has_side_effects=True, vmem_limit_bytes=40 * MIB),
    )(c, w_ada, b_ada_cols)


def _small_gather(vec):
    n = vec.shape[1]

    def body(v_ref, all_ref, sum_ref, send, recv):
        x, y, z = _me()
        me = 4 * x + 2 * y + z
        all_ref[me] = v_ref[...]
        sends = []
        for k in range(1, N_DEV):
            dev, _ = _peer(k)
            cp = _remote(v_ref, all_ref.at[me], send.at[k - 1], recv.at[k - 1], dev)
            cp.start()
            sends.append(cp)
        for k in range(1, N_DEV):
            dev, blk = _peer(k)
            _remote(v_ref, all_ref.at[blk], send.at[k - 1], recv.at[k - 1], dev).wait_recv()
        acc = all_ref[0]
        for d in range(1, N_DEV):
            acc = acc + all_ref[d]
        sum_ref[...] = acc
        for cp in sends:
            cp.wait_send()

    return _pc(
        body, "small_gather",
        in_specs=[VMEM_SPEC], out_specs=[VMEM_SPEC] * 2,
        out_shape=[jax.ShapeDtypeStruct((N_DEV, 1, n), F32), jax.ShapeDtypeStruct((1, n), F32)],
        scratch_shapes=[pltpu.SemaphoreType.DMA((N_DEV - 1,))] * 2,
        compiler_params=pltpu.CompilerParams(has_side_effects=True),
    )(vec)


def _local_step(x, target, mod, wt_in_loc, w_out_loc, w_pw_loc, conv_w_loc, rel_bias, sinks, conv_b, conv_ln_g,
                conv_ln_b, b_pw, ln_g, ln_b):
    bmap = _bucket_map()
    bias = _bias_table(rel_bias, bmap)
    (wt_in,) = _all_gather_two_level([wt_in_loc], "gather_w_in")
    (h, qkv, g_attn, glu, g_conv), (w_out, w_pw, conv_w_blocks) = _inproj(
        x, mod, wt_in, (w_out_loc, w_pw_loc, conv_w_loc))
    conv_w = conv_w_blocks.reshape(N_DEV, CONV_ROWS, 128).transpose(1, 0, 2).reshape(CONV_ROWS, D_CONV)
    (ycat, a_out, u1, u3, p_out), _ = _mixer_fwd(qkv, g_attn, glu, g_conv, bias, sinks, conv_w, conv_b,
                                                 conv_ln_g, conv_ln_b, w_pw, b_pw)
    dz, dy, dycat, loss, g_ln_g, g_ln_b, dgate = _outproj_ln(ycat, w_out, x, target, mod, ln_g, ln_b)
    gw_out = _matmul_tn(ycat, dy, None, 0, D_MODEL, "grad_w_out")
    dp, dgc, du1, g_bpw, g_clg, g_clb, g_cb = _conv_bwd_a(dycat, g_conv, p_out, u1, conv_ln_g, conv_ln_b, w_pw)
    gw_pw = _matmul_tn(u3, dp, None, 0, D_CONV, "grad_w_pw")
    dab, g_cw = _conv_bwd_b(du1, glu, conv_w)
    (dqkv, dga, dbias, dsink), (r_out, r_pw) = _attn_bwd(qkv, g_attn, a_out, dycat, bias, sinks,
                                                         scatter=(gw_out, gw_pw))
    g_rb = _relbias_grad(dbias, bmap)
    gwt_in = _matmul_tn(dqkv, h, None, 0, D_IN, "grad_w_in_qkv")
    gwt_in = _matmul_tn(dga, h, gwt_in, 1536, D_IN, "grad_w_in_gattn")
    gwt_in = _matmul_tn(dab, h, gwt_in, 2560, D_IN, "grad_w_in_glu")
    gwt_in = _matmul_tn(dgc, h, gwt_in, 4608, D_IN, "grad_w_in_gconv")
    g_cw_blocks = g_cw.reshape(CONV_ROWS, N_DEV, 128).transpose(1, 0, 2).reshape(N_DEV * CONV_ROWS, 128)
    (grad_x, dshift, dscale), (r_in, r_cw) = _dh_kernel([dqkv, dga, dab, dgc], wt_in, dz, x, mod,
                                                        scatter=(gwt_in, g_cw_blocks))
    dmod = jnp.concatenate([dshift, dscale, dgate], axis=1)
    small = dict(dmod=dmod, b_pw=g_bpw, conv_ln_g=g_clg, conv_ln_b=g_clb, conv_b=g_cb, ln_g=g_ln_g, ln_b=g_ln_b,
                 rel_bias=g_rb[:, :N_BUCKETS].reshape(1, N_BUCKETS * N_Q_HEADS),
                 sinks=dsink[0:1, :], loss=loss[0:1, :])
    return grad_x, r_in, r_out, r_pw, r_cw, small


SMALL_FIELDS = (("dmod", 3 * D_MODEL), ("b_pw", D_CONV), ("conv_ln_g", D_CONV), ("conv_ln_b", D_CONV),
                ("conv_b", D_CONV), ("ln_g", D_MODEL), ("ln_b", D_MODEL), ("rel_bias", N_BUCKETS * N_Q_HEADS),
                ("sinks", 128), ("loss", 128))


def _pack(fields):
    parts = []
    for name, width in SMALL_FIELDS:
        v = fields[name].reshape(1, -1).astype(F32)
        if v.shape[1] < width:
            v = jnp.pad(v, ((0, 0), (0, width - v.shape[1])))
        parts.append(v)
    return jnp.concatenate(parts, axis=1)


def _unpack(vec):
    out, off = {}, 0
    for name, width in SMALL_FIELDS:
        out[name] = vec[:, off:off + width]
        off += width
    return out


def kernel(x, c, w_ada, b_ada, w_in, rel_bias, sinks, conv_w, conv_b, conv_ln_g, conv_ln_b, w_pw, b_pw, w_out, ln_g, ln_b, loss_target, m_w_ada, m_b_ada, m_w_in, m_rel_bias, m_sinks, m_conv_w, m_conv_b, m_conv_ln_g, m_conv_ln_b, m_w_pw, m_b_pw, m_w_out, m_ln_g, m_ln_b, v_w_ada, v_b_ada, v_w_in, v_rel_bias, v_sinks, v_conv_w, v_conv_b, v_conv_ln_g, v_conv_ln_b, v_w_pw, v_b_pw, v_w_out, v_ln_g, v_ln_b):
    xi, yi, ci = _me()
    me = 4 * xi + 2 * yi + ci
    ncol = w_ada.shape[2]

    wt_in_loc = w_in[0].T.astype(BF16)
    conv_w_loc = jnp.pad(conv_w[0], ((0, CONV_ROWS - CONV_WIDTH), (0, 0)))

    b_ada_cols = lax.dynamic_slice(b_ada, (0, me * ncol), (1, ncol))
    mod_blocks, c_all = _ada_fwd(c, w_ada[0], b_ada_cols)
    mod = mod_blocks.reshape(1, 3 * D_MODEL)

    grad_x, r_in, r_out, r_pw, r_cw, small = _local_step(
        x[0], loss_target[0], mod, wt_in_loc, w_out[0].astype(BF16), w_pw[0].astype(BF16), conv_w_loc, rel_bias,
        sinks, conv_b, conv_ln_g, conv_ln_b, b_pw, ln_g, ln_b)

    gathered, summed = _small_gather(_pack(small))
    tot = _unpack(summed)
    dmod_all = gathered[:, 0, :3 * D_MODEL]
    loss = tot["loss"][0, 0]

    ct = jnp.zeros((D_MODEL, 128), BF16).at[:, :N_DEV].set(c_all[:, 0, :].T.astype(BF16))
    dm = jnp.zeros((128, ncol), BF16).at[:N_DEV, :].set(
        lax.dynamic_slice(dmod_all, (0, me * ncol), (N_DEV, ncol)).astype(BF16))
    g_w_ada = _small_mm(ct, dm, "grad_w_ada")

    g_conv_w = _sum8(r_cw.reshape(N_DEV, CONV_ROWS, 128), "sum_conv_w")[:CONV_WIDTH]

    grads = {"w_ada": g_w_ada, "conv_w": g_conv_w,
             "w_pw": r_pw.reshape(N_DEV, D_CONV // N_DEV, D_CONV),
             "w_out": r_out.reshape(N_DEV, D_MODEL // N_DEV, D_MODEL)}
    params = {"w_ada": (w_ada, m_w_ada, v_w_ada), "conv_w": (conv_w, m_conv_w, v_conv_w),
              "w_pw": (w_pw, m_w_pw, v_w_pw), "w_out": (w_out, m_w_out, v_w_out)}
    res = {}
    for name, g in grads.items():
        w_, m_, v_ = params[name]
        res[name] = tuple(a[None] for a in _adamw(w_[0], g, m_[0], v_[0], "adamw_" + name))
    upd = _adamw(w_in[0].T, r_in.reshape(N_DEV, D_IN // N_DEV, D_MODEL), m_w_in[0].T, v_w_in[0].T, "adamw_w_in")
    res["w_in"] = tuple(a.T[None] for a in upd)

    small_params = {"b_ada": (b_ada, m_b_ada, v_b_ada), "b_pw": (b_pw, m_b_pw, v_b_pw),
                    "conv_ln_g": (conv_ln_g, m_conv_ln_g, v_conv_ln_g),
                    "conv_ln_b": (conv_ln_b, m_conv_ln_b, v_conv_ln_b), "conv_b": (conv_b, m_conv_b, v_conv_b),
                    "ln_g": (ln_g, m_ln_g, v_ln_g), "ln_b": (ln_b, m_ln_b, v_ln_b),
                    "rel_bias": (rel_bias, m_rel_bias, v_rel_bias), "sinks": (sinks, m_sinks, v_sinks)}
    to_small = lambda n, a: a.T if n == "rel_bias" else a
    key_of = {"b_ada": "dmod"}
    packs = []
    for j in range(3):
        fields = {key_of.get(n, n): to_small(n, t[j]) for n, t in small_params.items()}
        fields["loss"] = jnp.zeros((1, 1), F32)
        packs.append(_pack(fields))
    gsum = summed
    _, d_s, nm_s, nv_s = _adamw(packs[0], gsum, packs[1], packs[2], "adamw_small")
    outs_small = [_unpack(a) for a in (gsum, d_s, nm_s, nv_s)]
    for n, t in small_params.items():
        shape = t[0].shape
        vals = []
        for o in outs_small:
            a = o[key_of.get(n, n)]
            if n == "rel_bias":
                a = a.reshape(N_Q_HEADS, N_BUCKETS).T
            else:
                a = a[:, :shape[1]].reshape(shape)
            vals.append(a)
        res[n] = tuple(vals)

    order = ["w_ada", "b_ada", "w_in", "rel_bias", "sinks", "conv_w", "conv_b", "conv_ln_g", "conv_ln_b", "w_pw",
             "b_pw", "w_out", "ln_g", "ln_b"]
    out = [loss, grad_x[None]]
    for j in range(4):
        out += [res[n][j] for n in order]
    return tuple(out)
```

```python
import math

import jax
import jax.numpy as jnp
from jax import lax
from jax.experimental import pallas as pl
from jax.experimental.pallas import tpu as pltpu

F32, BF16, I32 = jnp.float32, jnp.bfloat16, jnp.int32

D_MODEL = 2048
D_ATTN = 1024
D_CONV = 1024
D_KV = 256
HEAD_DIM = 64
N_Q_HEADS = 16
N_KV_HEADS = 4
GQA = 4
BLOCK = 128
CONV_WIDTH = 31
CONV_ROWS = 32
HALO = 32
N_BUCKETS = 32
MAX_DISTANCE = 128
LN_EPS = 1e-5
ALPHA = 2.0 ** 0.25
D_IN = 5632
N_DEV = 8
NEG = -1e30

ADAM_LR, ADAM_B1, ADAM_B2, ADAM_EPS, ADAM_WD, ADAM_STEP = 0.001, 0.9, 0.999, 1e-08, 0.01, 10

NT_DIMS = (((1,), (1,)), ((), ()))
TN_DIMS = (((0,), (0,)), ((), ()))
MIB = 1 << 20


def _pc(body, name, **kw):
    return pl.pallas_call(body, name=name, **kw)


def _cp(vmem_mib=None, sem=None):
    kw = {}
    if vmem_mib is not None:
        kw["vmem_limit_bytes"] = vmem_mib * MIB
    if sem is not None:
        kw["dimension_semantics"] = sem
    return pltpu.CompilerParams(**kw)


def _sig(x):
    return 1.0 / (1.0 + jnp.exp(-x))


def _dsilu(x, s):
    return s * (1.0 + x * (1.0 - s))


def _me():
    return lax.axis_index("x"), lax.axis_index("y"), lax.axis_index("c")


def _peer(k):
    x, y, c = _me()
    px = 1 - x if k & 4 else x
    py = 1 - y if k & 2 else y
    pc = 1 - c if k & 1 else c
    return (px, py, pc), 4 * px + 2 * py + pc


def _remote(src, dst, send_sem, recv_sem, dev):
    return pltpu.make_async_remote_copy(src_ref=src, dst_ref=dst, send_sem=send_sem, recv_sem=recv_sem,
                                        device_id=dev, device_id_type=pl.DeviceIdType.MESH)


HBM_SPEC = pl.BlockSpec(memory_space=pl.ANY)
VMEM_SPEC = pl.BlockSpec(memory_space=pltpu.VMEM)


def _comm_scratch(nt):
    return [pltpu.SemaphoreType.DMA((nt, N_DEV - 1)), pltpu.SemaphoreType.DMA((nt, N_DEV - 1)),
            pltpu.SemaphoreType.DMA((nt,))]


def _gather_shapes(shards):
    return [jax.ShapeDtypeStruct((N_DEV * s.shape[0], s.shape[1]), s.dtype) for s in shards]


def _block(ref, blk, rows):
    return ref.at[pl.ds(pl.multiple_of(blk * rows, 8), rows), :]


def _gather_copies(src, dst, sems):
    send_sems, recv_sems, local_sems = sems
    x, y, c = _me()
    me = 4 * x + 2 * y + c
    nt = len(src)
    local = [pltpu.make_async_copy(src[t], _block(dst[t], me, src[t].shape[0]), local_sems.at[t]) for t in range(nt)]
    sends, arrivals = [], []
    for k in range(1, N_DEV):
        dev, blk = _peer(k)
        for t in range(nt):
            r = src[t].shape[0]
            pair = (send_sems.at[t, k - 1], recv_sems.at[t, k - 1], dev)
            sends.append(_remote(src[t], _block(dst[t], me, r), *pair))
            arrivals.append(_remote(src[t], _block(dst[t], blk, r), *pair))
    return local, sends, arrivals


def _scatter_copies(src, dst, sems):
    send_sems, recv_sems, local_sems = sems
    x, y, c = _me()
    me = 4 * x + 2 * y + c
    nt = len(src)
    rows = [s.shape[0] // N_DEV for s in src]
    local = [pltpu.make_async_copy(_block(src[t], me, rows[t]), _block(dst[t], me, rows[t]), local_sems.at[t])
             for t in range(nt)]
    sends, arrivals = [], []
    for k in range(1, N_DEV):
        dev, blk = _peer(k)
        for t in range(nt):
            pair = (send_sems.at[t, k - 1], recv_sems.at[t, k - 1], dev)
            sends.append(_remote(_block(src[t], blk, rows[t]), _block(dst[t], me, rows[t]), *pair))
            arrivals.append(_remote(_block(src[t], me, rows[t]), _block(dst[t], blk, rows[t]), *pair))
    return local, sends, arrivals


def _comm_start(cps):
    local, sends, _ = cps
    for cp in local + sends:
        cp.start()


def _comm_wait(cps):
    local, sends, arrivals = cps
    for cp in arrivals:
        cp.wait_recv()
    for cp in sends:
        cp.wait_send()
    for cp in local:
        cp.wait()


IN_PIECES = ((0, 1536, BF16), (1536, 1024, F32), (2560, 2048, F32), (4608, 1024, F32))


def _inproj(x, mod, wt, gather):
    S = x.shape[0]
    tm = min(256, S)
    ni = S // tm
    ng = len(gather)
    npc = len(IN_PIECES)

    def body(*refs):
        x_ref, sh_ref, sc_ref, w_ref = refs[:4]
        outs = refs[4 + ng:]
        h_ref, piece_refs = outs[0], outs[1:1 + npc]
        i = pl.program_id(0)
        if ng:
            cps = _gather_copies(refs[4:4 + ng], outs[1 + npc:1 + npc + ng], outs[1 + npc + ng:])

            @pl.when(i == 0)
            def _():
                _comm_start(cps)

        h = (x_ref[...] * (1.0 + sc_ref[...]) + sh_ref[...]).astype(BF16)
        h_ref[...] = h
        for (r0, n, dt), o_ref in zip(IN_PIECES, piece_refs):
            o_ref[...] = lax.dot_general(h, w_ref[r0:r0 + n, :], NT_DIMS, preferred_element_type=F32).astype(dt)
        if ng:
            @pl.when(i == ni - 1)
            def _():
                _comm_wait(cps)

    tile = lambda n: pl.BlockSpec((tm, n), lambda i: (i, 0))
    res = _pc(
        body, "inproj", grid=(ni,),
        in_specs=[tile(D_MODEL),
                  pl.BlockSpec((1, D_MODEL), lambda i: (0, 0)),
                  pl.BlockSpec((1, D_MODEL), lambda i: (0, 1)),
                  pl.BlockSpec((D_IN, D_MODEL), lambda i: (0, 0), pipeline_mode=pl.Buffered(1))] + [HBM_SPEC] * ng,
        out_specs=[tile(D_MODEL)] + [tile(n) for _, n, _ in IN_PIECES] + [HBM_SPEC] * ng,
        out_shape=[jax.ShapeDtypeStruct((S, D_MODEL), BF16)]
        + [jax.ShapeDtypeStruct((S, n), dt) for _, n, dt in IN_PIECES] + _gather_shapes(gather),
        scratch_shapes=_comm_scratch(ng) if ng else [],
        compiler_params=_cp(56, ("arbitrary",)),
    )(x, mod, mod, wt, *gather)
    return res[:1 + npc], res[1 + npc:]


def _matmul_tn(segs, b, name):
    S, N = b.shape
    tm, ts = 512, min(2048, S)
    ns = S // ts
    bounds = []
    t0 = 0
    for a in segs:
        assert a.shape[1] % tm == 0
        bounds.append((t0, a.shape[1] // tm))
        t0 += a.shape[1] // tm
    nseg = len(segs)

    def body(*refs):
        seg_refs, b_ref = refs[:nseg], refs[nseg]
        o_ref, acc = refs[-2], refs[-1]
        t, s = pl.program_id(0), pl.program_id(1)

        @pl.when(s == 0)
        def _():
            acc[...] = jnp.zeros_like(acc)

        for (s0, sn), r in zip(bounds, seg_refs):
            @pl.when((t >= s0) & (t < s0 + sn))
            def _(r=r):
                acc[...] += lax.dot_general(r[...], b_ref[...], TN_DIMS, preferred_element_type=F32)

        @pl.when(s == ns - 1)
        def _():
            o_ref[...] = acc[...].astype(BF16)

    def seg_spec(s0, sn):
        def index(t, s):
            mine = (t >= s0) & (t < s0 + sn)
            return jnp.where(mine, s, 0), jnp.clip(t - s0, 0, sn - 1)
        return pl.BlockSpec((ts, tm), index)

    return _pc(
        body, name, grid=(t0, ns),
        in_specs=[seg_spec(*bd) for bd in bounds] + [pl.BlockSpec((ts, N), lambda t, s: (s, 0))],
        out_specs=pl.BlockSpec((tm, N), lambda t, s: (t, 0)),
        out_shape=jax.ShapeDtypeStruct((t0 * tm, N), BF16),
        scratch_shapes=[pltpu.VMEM((tm, N), F32)],
        compiler_params=_cp(48, ("parallel", "arbitrary")),
    )(*segs, b)


def _bucket_map():
    qi = jnp.arange(BLOCK, dtype=I32)[:, None]
    kj = jnp.arange(2 * BLOCK, dtype=I32)[None, :]
    dist = qi + BLOCK - kj
    in_window = (dist >= 0) & (dist < BLOCK)
    d0 = jnp.maximum(dist, 0)
    max_exact = N_BUCKETS // 2
    d = jnp.maximum(d0, 1).astype(F32)
    large = max_exact + (jnp.log(d / max_exact) / math.log(MAX_DISTANCE / max_exact)
                         * (N_BUCKETS - max_exact)).astype(I32)
    large = jnp.minimum(large, N_BUCKETS - 1)
    bucket = jnp.where(d0 < max_exact, d0, large)
    return jnp.where(in_window, bucket, -1).astype(I32)


def _bias_table(rel_bias, bmap):
    def body(rb_ref, bm_ref, o_ref):
        h = pl.program_id(0)
        bm = bm_ref[...]
        acc = jnp.full((BLOCK, 2 * BLOCK), NEG, F32)
        for b in range(N_BUCKETS):
            acc = jnp.where(bm == b, rb_ref[b, h], acc)
        o_ref[0] = acc

    return _pc(
        body, "bias_table", grid=(N_Q_HEADS,),
        in_specs=[pl.BlockSpec(memory_space=pltpu.SMEM),
                  pl.BlockSpec((BLOCK, 2 * BLOCK), lambda h: (0, 0))],
        out_specs=pl.BlockSpec((1, BLOCK, 2 * BLOCK), lambda h: (h, 0, 0)),
        out_shape=jax.ShapeDtypeStruct((N_Q_HEADS, BLOCK, 2 * BLOCK), F32),
    )(rel_bias, bmap)


def _bias_spec():
    return pl.BlockSpec((N_Q_HEADS, BLOCK, 2 * BLOCK), lambda n: (0, 0, 0))


def _relbias_grad(dbias, bmap):
    def body(db_ref, bm_ref, o_ref):
        bm = bm_ref[...]
        x = db_ref[0]
        lane = lax.broadcasted_iota(I32, (1, 128), 1)
        row = jnp.zeros((1, 128), F32)
        for b in range(N_BUCKETS):
            row = jnp.where(lane == b, jnp.sum(jnp.where(bm == b, x, 0.0)), row)
        o_ref[0] = jnp.broadcast_to(row, (8, 128))

    out = _pc(
        body, "relbias_grad", grid=(N_Q_HEADS,),
        in_specs=[pl.BlockSpec((1, BLOCK, 2 * BLOCK), lambda h: (h, 0, 0)),
                  pl.BlockSpec((BLOCK, 2 * BLOCK), lambda h: (0, 0))],
        out_specs=pl.BlockSpec((1, 8, 128), lambda h: (h, 0, 0)),
        out_shape=jax.ShapeDtypeStruct((N_Q_HEADS, 8, 128), F32),
    )(dbias, bmap)
    return out[:, 0, :]


Q_SCALE = HEAD_DIM ** -0.5
assert math.frexp(Q_SCALE)[0] == 0.5


def _stack_heads(ref, hk, rows=slice(None)):
    return jnp.concatenate([ref[rows, pl.ds(256 * hk + 64 * g, 64)] for g in range(GQA)], axis=0) * Q_SCALE


def _sink_col(sink_ref, hk):
    row = lax.broadcasted_iota(I32, (GQA * BLOCK, 1), 0)
    s = jnp.full((GQA * BLOCK, 1), sink_ref[0, 4 * hk + 3], F32)
    for g in (2, 1, 0):
        s = jnp.where(row < (g + 1) * BLOCK, sink_ref[0, 4 * hk + g], s)
    return s


def _attn_probs(q4, kw, bias_ref, sink_ref, hk, first_mask):
    s = lax.dot_general(q4, kw, NT_DIMS, preferred_element_type=F32)
    s = s + bias_ref[4 * hk:4 * hk + 4].reshape(GQA * BLOCK, 2 * BLOCK)
    if first_mask is not None:
        s = jnp.where(first_mask, s, NEG)
    sink = _sink_col(sink_ref, hk)
    m = jnp.maximum(jnp.max(s, axis=1, keepdims=True), sink)
    e = jnp.exp(s - m)
    es = jnp.exp(sink - m)
    inv = 1.0 / (jnp.sum(e, axis=1, keepdims=True) + es)
    return e * inv, es * inv


def _attn_bwd(qkv, g_attn, a_out, dycat, bias, sinks, scatter=()):
    S = qkv.shape[0]
    nb = S // BLOCK
    R = GQA * BLOCK
    ns = len(scatter)

    def body(*refs):
        (q_ref, kc_ref, kp_ref, vc_ref, vp_ref, g_ref, a_ref, dy_ref, bias_ref, sink_ref) = refs[:10]
        dqkv_ref, dg_ref, dbias_ref, dsink_ref = refs[10 + ns:14 + ns]
        dq_scr, dq_new, dk_scr, dv_scr, dkw_scr, dvw_scr, ds_scr = refs[14 + 2 * ns:21 + 2 * ns]
        n = pl.program_id(0)
        if ns:
            cps = _scatter_copies(refs[10:10 + ns], refs[14 + ns:14 + 2 * ns], refs[21 + 2 * ns:])

            @pl.when(n == 0)
            def _():
                _comm_start(cps)

        @pl.when(n == 0)
        def _():
            dbias_ref[...] = jnp.zeros_like(dbias_ref)
            ds_scr[...] = jnp.zeros_like(ds_scr)
            dq_scr[...] = jnp.zeros_like(dq_scr)
            dk_scr[...] = jnp.zeros_like(dk_scr)
            dv_scr[...] = jnp.zeros_like(dv_scr)

        @pl.when(n < nb)
        def _():
            kj = lax.broadcasted_iota(I32, (R, 2 * BLOCK), 1)
            first_mask = (n > 0) | (kj >= BLOCK)
            for hk in range(N_KV_HEADS):
                q4 = _stack_heads(q_ref, hk)
                ks = pl.ds(64 * hk, 64)
                kw = jnp.concatenate([kp_ref[:, ks], kc_ref[:, ks]], axis=0)
                vw = jnp.concatenate([vp_ref[:, ks], vc_ref[:, ks]], axis=0)
                p, psink = _attn_probs(q4, kw, bias_ref, sink_ref, hk, first_mask)
                da_parts, a_parts = [], []
                for g in range(GQA):
                    sl = pl.ds(256 * hk + 64 * g, 64)
                    gg = g_ref[:, sl]
                    sg = _sig(gg)
                    dyg = dy_ref[:, sl]
                    ag = a_ref[:, sl]
                    da_parts.append(dyg * (gg * sg))
                    a_parts.append(ag)
                    dg_ref[:, sl] = (dyg * ag * _dsilu(gg, sg)).astype(BF16)
                da4 = jnp.concatenate(da_parts, axis=0)
                a4 = jnp.concatenate(a_parts, axis=0)
                delta = jnp.sum(da4 * a4, axis=1, keepdims=True)
                da4b = da4.astype(BF16)
                dp = lax.dot_general(da4b, vw, NT_DIMS, preferred_element_type=F32)
                ds = p * (dp - delta)
                ds_scr[hk] += -psink * delta
                dbias_ref[4 * hk:4 * hk + 4] += ds.reshape(GQA, BLOCK, 2 * BLOCK)
                dsb = ds.astype(BF16)
                dq4 = jnp.dot(dsb, kw, preferred_element_type=F32) * Q_SCALE
                for g in range(GQA):
                    dq_new[:, pl.ds(256 * hk + 64 * g, 64)] = dq4[BLOCK * g:BLOCK * (g + 1)].astype(BF16)
                dkw_scr[:, ks] = lax.dot_general(dsb, q4, TN_DIMS, preferred_element_type=F32)
                dvw_scr[:, ks] = lax.dot_general(p.astype(BF16), da4b, TN_DIMS, preferred_element_type=F32)

        @pl.when(n == nb)
        def _():
            dkw_scr[0:BLOCK, :] = jnp.zeros((BLOCK, D_KV), F32)
            dvw_scr[0:BLOCK, :] = jnp.zeros((BLOCK, D_KV), F32)

        dqkv_ref[:, 0:D_ATTN] = dq_scr[...]
        dqkv_ref[:, D_ATTN:D_ATTN + D_KV] = (dk_scr[...] + dkw_scr[0:BLOCK, :]).astype(BF16)
        dqkv_ref[:, D_ATTN + D_KV:D_ATTN + 2 * D_KV] = (dv_scr[...] + dvw_scr[0:BLOCK, :]).astype(BF16)
        dq_scr[...] = dq_new[...]
        dk_scr[...] = dkw_scr[BLOCK:2 * BLOCK, :]
        dv_scr[...] = dvw_scr[BLOCK:2 * BLOCK, :]

        @pl.when(n == nb)
        def _():
            lane = lax.broadcasted_iota(I32, (1, 128), 1)
            row = jnp.zeros((1, 128), F32)
            for hk in range(N_KV_HEADS):
                col = ds_scr[hk]
                for g in range(GQA):
                    row = jnp.where(lane == 4 * hk + g, jnp.sum(col[BLOCK * g:BLOCK * (g + 1)]), row)
            dsink_ref[...] = jnp.broadcast_to(row, (8, 128))
            if ns:
                _comm_wait(cps)

    cur = lambda n: jnp.minimum(n, nb - 1)
    prev = lambda n: jnp.clip(n - 1, 0, nb - 1)
    res = _pc(
        body, "attn_bwd", grid=(nb + 1,),
        in_specs=[pl.BlockSpec((BLOCK, D_ATTN), lambda n: (cur(n), 0)),
                  pl.BlockSpec((BLOCK, D_KV), lambda n: (cur(n), 4)),
                  pl.BlockSpec((BLOCK, D_KV), lambda n: (prev(n), 4)),
                  pl.BlockSpec((BLOCK, D_KV), lambda n: (cur(n), 5)),
                  pl.BlockSpec((BLOCK, D_KV), lambda n: (prev(n), 5)),
                  pl.BlockSpec((BLOCK, D_ATTN), lambda n: (cur(n), 0)),
                  pl.BlockSpec((BLOCK, D_ATTN), lambda n: (cur(n), 0)),
                  pl.BlockSpec((BLOCK, D_ATTN), lambda n: (cur(n), 0)),
                  _bias_spec(),
                  pl.BlockSpec(memory_space=pltpu.SMEM)] + [HBM_SPEC] * ns,
        out_specs=[pl.BlockSpec((BLOCK, D_ATTN + 2 * D_KV), lambda n: (prev(n), 0)),
                   pl.BlockSpec((BLOCK, D_ATTN), lambda n: (cur(n), 0)),
                   pl.BlockSpec((N_Q_HEADS, BLOCK, 2 * BLOCK), lambda n: (0, 0, 0)),
                   pl.BlockSpec((8, 128), lambda n: (0, 0))] + [HBM_SPEC] * ns,
        out_shape=[jax.ShapeDtypeStruct((S, D_ATTN + 2 * D_KV), BF16),
                   jax.ShapeDtypeStruct((S, D_ATTN), BF16),
                   jax.ShapeDtypeStruct((N_Q_HEADS, BLOCK, 2 * BLOCK), F32),
                   jax.ShapeDtypeStruct((8, 128), F32)] + [jax.ShapeDtypeStruct(f.shape, f.dtype) for f in scatter],
        scratch_shapes=[pltpu.VMEM((BLOCK, D_ATTN), BF16), pltpu.VMEM((BLOCK, D_ATTN), BF16),
                        pltpu.VMEM((BLOCK, D_KV), F32), pltpu.VMEM((BLOCK, D_KV), F32),
                        pltpu.VMEM((2 * BLOCK, D_KV), F32), pltpu.VMEM((2 * BLOCK, D_KV), F32),
                        pltpu.VMEM((N_KV_HEADS, R, 1), F32)] + (_comm_scratch(ns) if ns else []),
        compiler_params=_cp(48, ("arbitrary",)),
    )(qkv, qkv, qkv, qkv, qkv, g_attn, a_out, dycat, bias, sinks, *scatter)
    return res[:4], res[4:]


def _conv_tile(S):
    return min(256, S)


def _shifted(win, s):
    return win if s == 0 else pltpu.roll(win, win.shape[0] - s, axis=0)


def _conv_taps(win, cw_ref, lanes, rows):
    acc = jnp.zeros((rows, win.shape[1]), F32)
    for s in range(8):
        ws = _shifted(win, s)
        for aa in range(5):
            j = 8 * aa + s - 2
            if 0 <= j < CONV_WIDTH:
                acc = acc + ws[8 * aa:8 * aa + rows] * cw_ref[j:j + 1, lanes]
    return acc


def _mixer_fwd(qkv, g_attn, glu, g_conv, bias, sinks, conv_w, conv_b, ln_g, ln_b, w_pw, b_pw):
    S = qkv.shape[0]
    RB = 2
    TR = RB * BLOCK
    nb = S // TR
    hb = TR // HALO
    LG = D_CONV // (RB * N_KV_HEADS)

    def body(q_ref, kc_ref, kp_ref, vc_ref, vp_ref, g_ref, bias_ref, sink_ref,
             a_in, ah_in, b_in, bh_in, gc_ref, cw_ref, cb_ref, lg_ref, lb_ref, wpw_ref, bpw_ref,
             y_ref, a_ref, u1_ref, u3_ref, p_ref, win):
        n = pl.program_id(0)
        kj = lax.broadcasted_iota(I32, (GQA * BLOCK, 2 * BLOCK), 1)
        first_mask = (n > 0) | (kj >= BLOCK)
        win[0:HALO, :] = jnp.where(n > 0, ah_in[...] * _sig(bh_in[...]), 0.0)
        win[HALO:HALO + TR, :] = a_in[...] * _sig(b_in[...])
        piece = 0
        for s in range(RB):
            rows = slice(BLOCK * s, BLOCK * (s + 1))
            before = slice(BLOCK * (s - 1), BLOCK * s)
            for hk in range(N_KV_HEADS):
                q4 = _stack_heads(q_ref, hk, rows)
                ks = pl.ds(64 * hk, 64)
                k_prev, v_prev = (kp_ref[:, ks], vp_ref[:, ks]) if s == 0 else (kc_ref[before, ks], vc_ref[before, ks])
                kw = jnp.concatenate([k_prev, kc_ref[rows, ks]], axis=0)
                vw = jnp.concatenate([v_prev, vc_ref[rows, ks]], axis=0)
                p, _ = _attn_probs(q4, kw, bias_ref, sink_ref, hk, first_mask if s == 0 else None)
                o4 = jnp.dot(p.astype(BF16), vw, preferred_element_type=F32)
                for g in range(GQA):
                    sl = pl.ds(256 * hk + 64 * g, 64)
                    og = o4[BLOCK * g:BLOCK * (g + 1)]
                    gg = g_ref[rows, sl]
                    a_ref[rows, sl] = og
                    y_ref[rows, sl] = (og * (gg * _sig(gg))).astype(BF16)
                lanes = pl.ds(LG * piece, LG)
                u1_ref[:, lanes] = _conv_taps(win[:, lanes], cw_ref, lanes, TR) + cb_ref[:, lanes]
                piece += 1
        u1 = u1_ref[...]
        mu = jnp.mean(u1, axis=1, keepdims=True)
        uc = u1 - mu
        rstd = lax.rsqrt(jnp.mean(uc * uc, axis=1, keepdims=True) + LN_EPS)
        u2 = uc * rstd * lg_ref[...] + lb_ref[...]
        u3 = (u2 * _sig(u2)).astype(BF16)
        u3_ref[...] = u3
        pw = jnp.dot(u3, wpw_ref[...], preferred_element_type=F32) + bpw_ref[...]
        p_ref[...] = pw
        gc = gc_ref[...]
        y_ref[:, D_ATTN:] = (pw * (gc * _sig(gc))).astype(BF16)

    prev = lambda n: jnp.maximum(RB * n - 1, 0)
    halo = lambda n: jnp.maximum(n * hb - 1, 0)
    vec = pl.BlockSpec((1, D_CONV), lambda n: (0, 0))
    blk = lambda w, j: pl.BlockSpec((TR, w), lambda n: (n, j))
    return _pc(
        body, "mixer_fwd", grid=(nb,),
        in_specs=[blk(D_ATTN, 0),
                  blk(D_KV, 4), pl.BlockSpec((BLOCK, D_KV), lambda n: (prev(n), 4)),
                  blk(D_KV, 5), pl.BlockSpec((BLOCK, D_KV), lambda n: (prev(n), 5)),
                  blk(D_ATTN, 0),
                  _bias_spec(),
                  pl.BlockSpec(memory_space=pltpu.SMEM),
                  blk(D_CONV, 0), pl.BlockSpec((HALO, D_CONV), lambda n: (halo(n), 0)),
                  blk(D_CONV, 1), pl.BlockSpec((HALO, D_CONV), lambda n: (halo(n), 1)),
                  blk(D_CONV, 0),
                  pl.BlockSpec((CONV_ROWS, D_CONV), lambda n: (0, 0)),
                  vec, vec, vec,
                  pl.BlockSpec((D_CONV, D_CONV), lambda n: (0, 0)),
                  vec],
        out_specs=[blk(2 * D_ATTN, 0), blk(D_ATTN, 0), blk(D_CONV, 0), blk(D_CONV, 0), blk(D_CONV, 0)],
        out_shape=[jax.ShapeDtypeStruct((S, 2 * D_ATTN), BF16),
                   jax.ShapeDtypeStruct((S, D_ATTN), F32),
                   jax.ShapeDtypeStruct((S, D_CONV), F32),
                   jax.ShapeDtypeStruct((S, D_CONV), BF16),
                   jax.ShapeDtypeStruct((S, D_CONV), F32)],
        scratch_shapes=[pltpu.VMEM((TR + HALO, D_CONV), F32)],
        compiler_params=_cp(56, ("parallel",)),
    )(qkv, qkv, qkv, qkv, qkv, g_attn, bias, sinks, glu, glu, glu, glu, g_conv, conv_w, conv_b, ln_g, ln_b, w_pw,
      b_pw)


def _conv_bwd_a(dycat, g_conv, p_out, u1, ln_g, ln_b, w_pw):
    S = u1.shape[0]
    T = _conv_tile(S)

    def body(dy_ref, gc_ref, p_ref, u1_ref, lg_ref, lb_ref, wpw_ref,
             dp_ref, dgc_ref, du1_ref, gbpw_ref, glg_ref, glb_ref, gcb_ref):
        i = pl.program_id(0)

        @pl.when(i == 0)
        def _():
            for r in (gbpw_ref, glg_ref, glb_ref, gcb_ref):
                r[...] = jnp.zeros_like(r)

        dy = dy_ref[...]
        gc = gc_ref[...]
        sg = _sig(gc)
        dp = dy * (gc * sg)
        dgc_ref[...] = (dy * p_ref[...] * _dsilu(gc, sg)).astype(BF16)
        gbpw_ref[...] += jnp.sum(dp, axis=0, keepdims=True)
        dpb = dp.astype(BF16)
        dp_ref[...] = dpb
        du3 = lax.dot_general(dpb, wpw_ref[...], NT_DIMS, preferred_element_type=F32)
        u1 = u1_ref[...]
        mu = jnp.mean(u1, axis=1, keepdims=True)
        uc = u1 - mu
        rstd = lax.rsqrt(jnp.mean(uc * uc, axis=1, keepdims=True) + LN_EPS)
        uh = uc * rstd
        lg = lg_ref[...]
        u2 = uh * lg + lb_ref[...]
        s2 = _sig(u2)
        du2 = du3 * _dsilu(u2, s2)
        glg_ref[...] += jnp.sum(du2 * uh, axis=0, keepdims=True)
        glb_ref[...] += jnp.sum(du2, axis=0, keepdims=True)
        duh = du2 * lg
        du1 = rstd * (duh - jnp.mean(duh, axis=1, keepdims=True) - uh * jnp.mean(duh * uh, axis=1, keepdims=True))
        du1_ref[...] = du1
        gcb_ref[...] += jnp.sum(du1, axis=0, keepdims=True)

    vec = pl.BlockSpec((1, D_CONV), lambda i: (0, 0))
    tile = pl.BlockSpec((T, D_CONV), lambda i: (i, 0))
    vshape = jax.ShapeDtypeStruct((1, D_CONV), F32)
    return _pc(
        body, "conv_bwd_a", grid=(S // T,),
        in_specs=[pl.BlockSpec((T, D_CONV), lambda i: (i, 1)), tile, tile, tile, vec, vec,
                  pl.BlockSpec((D_CONV, D_CONV), lambda i: (0, 0))],
        out_specs=[tile, tile, tile, vec, vec, vec, vec],
        out_shape=[jax.ShapeDtypeStruct((S, D_CONV), BF16), jax.ShapeDtypeStruct((S, D_CONV), BF16),
                   jax.ShapeDtypeStruct((S, D_CONV), F32), vshape, vshape, vshape, vshape],
        compiler_params=_cp(48, ("arbitrary",)),
    )(dycat, g_conv, p_out, u1, ln_g, ln_b, w_pw)


def _conv_bwd_b(du1, glu, conv_w):
    S = du1.shape[0]
    T = _conv_tile(S)
    hb = T // HALO
    nt = S // T
    last_h = S // HALO - 1

    def body(du_ref, dun_ref, a_ref, b_ref, cw_ref, dab_ref, gw_ref):
        i = pl.program_id(0)

        @pl.when(i == 0)
        def _():
            gw_ref[...] = jnp.zeros_like(gw_ref)

        for lg in range(D_CONV // 128):
            lanes = pl.ds(128 * lg, 128)
            a = a_ref[:, lanes]
            sb = _sig(b_ref[:, lanes])
            u0 = a * sb
            du1n = jnp.where(i < nt - 1, dun_ref[:, lanes], 0.0)
            win2 = jnp.concatenate([du_ref[:, lanes], du1n], axis=0)
            acc = jnp.zeros((T, 128), F32)
            for s in range(8):
                w2 = _shifted(win2, s)
                for aa in range(4):
                    j = CONV_WIDTH - 1 - (8 * aa + s)
                    if 0 <= j < CONV_WIDTH:
                        xo = w2[8 * aa:8 * aa + T]
                        acc = acc + xo * cw_ref[j:j + 1, lanes]
                        gw_ref[j:j + 1, lanes] += jnp.sum(xo * u0, axis=0, keepdims=True)
            dab_ref[:, lanes] = (acc * sb).astype(BF16)
            dab_ref[:, pl.ds(D_CONV + 128 * lg, 128)] = (acc * a * sb * (1.0 - sb)).astype(BF16)

    nxt = lambda i: jnp.minimum((i + 1) * hb, last_h)
    return _pc(
        body, "conv_bwd_b", grid=(nt,),
        in_specs=[pl.BlockSpec((T, D_CONV), lambda i: (i, 0)),
                  pl.BlockSpec((HALO, D_CONV), lambda i: (nxt(i), 0)),
                  pl.BlockSpec((T, D_CONV), lambda i: (i, 0)),
                  pl.BlockSpec((T, D_CONV), lambda i: (i, 1)),
                  pl.BlockSpec((CONV_ROWS, D_CONV), lambda i: (0, 0))],
        out_specs=[pl.BlockSpec((T, 2 * D_CONV), lambda i: (i, 0)),
                   pl.BlockSpec((CONV_ROWS, D_CONV), lambda i: (0, 0))],
        out_shape=[jax.ShapeDtypeStruct((S, 2 * D_CONV), BF16),
                   jax.ShapeDtypeStruct((CONV_ROWS, D_CONV), F32)],
        compiler_params=_cp(48, ("arbitrary",)),
    )(du1, du1, glu, glu, conv_w)


def _outproj_ln(ycat, w_out, x, target, mod, ln_g, ln_b):
    S = x.shape[0]
    tm = min(256, S)

    def body(yc_ref, w_ref, x_ref, t_ref, gate_ref, lg_ref, lb_ref,
             dz_ref, dy_ref, dyc_ref, loss_ref, glg_ref, glb_ref, dgate_ref):
        i = pl.program_id(0)

        @pl.when(i == 0)
        def _():
            for r in (loss_ref, glg_ref, glb_ref, dgate_ref):
                r[...] = jnp.zeros_like(r)

        w = w_ref[...]
        y = jnp.dot(yc_ref[...], w, preferred_element_type=F32)
        gate = gate_ref[...]
        z = ALPHA * x_ref[...] + gate * y
        mu = jnp.mean(z, axis=1, keepdims=True)
        zc = z - mu
        rstd = lax.rsqrt(jnp.mean(zc * zc, axis=1, keepdims=True) + LN_EPS)
        zh = zc * rstd
        lg = lg_ref[...]
        err = zh * lg + lb_ref[...] - t_ref[...]
        loss_ref[...] += 0.5 * jnp.sum(jnp.sum(err * err, axis=1, keepdims=True)) / D_MODEL
        dout = err * (1.0 / D_MODEL)
        glg_ref[...] += jnp.sum(dout * zh, axis=0, keepdims=True)
        glb_ref[...] += jnp.sum(dout, axis=0, keepdims=True)
        dzh = dout * lg
        dz = rstd * (dzh - jnp.mean(dzh, axis=1, keepdims=True) - zh * jnp.mean(dzh * zh, axis=1, keepdims=True))
        dz_ref[...] = dz
        dgate_ref[...] += jnp.sum(dz * y, axis=0, keepdims=True)
        dy = (dz * gate).astype(BF16)
        dy_ref[...] = dy
        dyc_ref[...] = lax.dot_general(dy, w, NT_DIMS, preferred_element_type=F32).astype(BF16)

    vec = pl.BlockSpec((1, D_MODEL), lambda i: (0, 0))
    tile = pl.BlockSpec((tm, D_MODEL), lambda i: (i, 0))
    vshape = jax.ShapeDtypeStruct((1, D_MODEL), F32)
    return _pc(
        body, "outproj_ln", grid=(S // tm,),
        in_specs=[tile, pl.BlockSpec((D_MODEL, D_MODEL), lambda i: (0, 0), pipeline_mode=pl.Buffered(1)), tile, tile,
                  pl.BlockSpec((1, D_MODEL), lambda i: (0, 2)), vec, vec],
        out_specs=[tile, tile, tile, pl.BlockSpec((8, 128), lambda i: (0, 0)), vec, vec, vec],
        out_shape=[jax.ShapeDtypeStruct((S, D_MODEL), F32), jax.ShapeDtypeStruct((S, D_MODEL), BF16),
                   jax.ShapeDtypeStruct((S, D_MODEL), BF16), jax.ShapeDtypeStruct((8, 128), F32),
                   vshape, vshape, vshape],
        compiler_params=_cp(56, ("arbitrary",)),
    )(ycat, w_out, x, target, mod, ln_g, ln_b)


def _dh_kernel(segs, wt, dz, x, mod, scatter=()):
    S = x.shape[0]
    tm = min(256, S)
    row0 = [0]
    for a in segs:
        row0.append(row0[-1] + a.shape[1])
    assert row0[-1] == wt.shape[0]
    nseg = len(segs)
    ns = len(scatter)
    ni = S // tm

    def body(*refs):
        seg_refs = refs[:nseg]
        w_ref, dz_ref, x_ref, sc_ref = refs[nseg:nseg + 4]
        outs = refs[nseg + 4 + ns:]
        gx_ref, dsh_ref, dsc_ref = outs[:3]
        i = pl.program_id(0)
        if ns:
            cps = _scatter_copies(refs[nseg + 4:nseg + 4 + ns], outs[3:3 + ns], outs[3 + ns:])

        @pl.when(i == 0)
        def _():
            dsh_ref[...] = jnp.zeros_like(dsh_ref)
            dsc_ref[...] = jnp.zeros_like(dsc_ref)
            if ns:
                _comm_start(cps)

        dh = jnp.dot(seg_refs[0][...], w_ref[row0[0]:row0[1], :], preferred_element_type=F32)
        for t in range(1, nseg):
            dh = dh + jnp.dot(seg_refs[t][...], w_ref[row0[t]:row0[t + 1], :], preferred_element_type=F32)
        gx_ref[...] = ALPHA * dz_ref[...] + dh * (1.0 + sc_ref[...])
        dsh_ref[...] += jnp.sum(dh, axis=0, keepdims=True)
        dsc_ref[...] += jnp.sum(dh * x_ref[...], axis=0, keepdims=True)

        if ns:
            @pl.when(i == ni - 1)
            def _():
                _comm_wait(cps)

    tile = pl.BlockSpec((tm, D_MODEL), lambda i: (i, 0))
    vec = pl.BlockSpec((1, D_MODEL), lambda i: (0, 0))
    vshape = jax.ShapeDtypeStruct((1, D_MODEL), F32)
    res = _pc(
        body, "dh_gradx", grid=(ni,),
        in_specs=[pl.BlockSpec((tm, a.shape[1]), lambda i: (i, 0)) for a in segs] + [
            pl.BlockSpec(wt.shape, lambda i: (0, 0), pipeline_mode=pl.Buffered(1)), tile, tile,
            pl.BlockSpec((1, D_MODEL), lambda i: (0, 1))] + [HBM_SPEC] * ns,
        out_specs=[tile, vec, vec] + [HBM_SPEC] * ns,
        out_shape=[jax.ShapeDtypeStruct((S, D_MODEL), F32), vshape, vshape]
        + [jax.ShapeDtypeStruct(f.shape, f.dtype) for f in scatter],
        scratch_shapes=_comm_scratch(ns) if ns else [],
        compiler_params=_cp(58, ("arbitrary",)),
    )(*segs, wt, dz, x, mod, *scatter)
    return res[:3], res[3:]


def _row_tile(rows, cols):
    if rows * cols * 4 <= 2 * MIB or rows % 8:
        return rows
    tr = max(8, (2 * MIB // (cols * 4)) // 8 * 8)
    while rows % tr:
        tr -= 8
    return tr


def _sum8(recv, name):
    _, R, C = recv.shape
    tr = _row_tile(R, C)

    def body(r_ref, o_ref):
        acc = r_ref[0].astype(F32)
        for d in range(1, N_DEV):
            acc = acc + r_ref[d].astype(F32)
        o_ref[...] = acc

    return _pc(
        body, name, grid=(R // tr,),
        in_specs=[pl.BlockSpec((N_DEV, tr, C), lambda i: (0, i, 0))],
        out_specs=pl.BlockSpec((tr, C), lambda i: (i, 0)),
        out_shape=jax.ShapeDtypeStruct((R, C), F32),
        compiler_params=_cp(40, ("parallel",)),
    )(recv)


def _adamw(w, g, m, v, name):
    R, C = w.shape
    tr = _row_tile(R, C)
    parts = g.ndim == 3

    def body(w_ref, g_ref, m_ref, v_ref, *out_refs):
        d_ref, nm_ref, nv_ref = out_refs[-3:]
        if parts:
            g_ = g_ref[0].astype(F32)
            for d in range(1, N_DEV):
                g_ = g_ + g_ref[d].astype(F32)
            out_refs[0][...] = g_
        else:
            g_ = g_ref[...]
        m_ = ADAM_B1 * m_ref[...] + (1.0 - ADAM_B1) * g_
        v_ = ADAM_B2 * v_ref[...] + (1.0 - ADAM_B2) * (g_ * g_)
        m_hat = m_ / (1.0 - ADAM_B1 ** ADAM_STEP)
        v_hat = v_ / (1.0 - ADAM_B2 ** ADAM_STEP)
        d_ref[...] = -ADAM_LR * (m_hat / (jnp.sqrt(v_hat) + ADAM_EPS) + ADAM_WD * w_ref[...])
        nm_ref[...] = m_
        nv_ref[...] = v_

    spec = pl.BlockSpec((tr, C), lambda i: (i, 0))
    gspec = pl.BlockSpec((N_DEV, tr, C), lambda i: (0, i, 0)) if parts else spec
    shape = jax.ShapeDtypeStruct((R, C), F32)
    nout = 4 if parts else 3
    res = _pc(
        body, name, grid=(R // tr,),
        in_specs=[spec, gspec, spec, spec], out_specs=[spec] * nout, out_shape=[shape] * nout,
        compiler_params=_cp(40, ("parallel",)),
    )(w, g, m, v)
    return tuple(res) if parts else (g, *res)


def _small_mm(a, b, name):
    def body(a_ref, b_ref, o_ref):
        o_ref[...] = jnp.dot(a_ref[...], b_ref[...], preferred_element_type=F32)

    return _pc(body, name, out_shape=jax.ShapeDtypeStruct((a.shape[0], b.shape[1]), F32),
               compiler_params=_cp(40))(a, b)


def _all_gather_two_level(shards, name):
    nt = len(shards)

    def body(*refs):
        src, dst = refs[:nt], refs[nt:2 * nt]
        send_sems, recv_sems, local_sems = refs[2 * nt:]
        x, y, c = _me()
        sibling = (x, y, 1 - c)
        chips = [(1 - x, y), (x, 1 - y), (1 - x, 1 - y)]

        def blk(t, px, py, pc):
            return _block(dst[t], 4 * px + 2 * py + pc, src[t].shape[0])

        def copy(t, k, to, block, from_src):
            return _remote(src[t] if from_src else blk(t, *block), blk(t, *block),
                           send_sems.at[t, k], recv_sems.at[t, k], to)

        local = [pltpu.make_async_copy(src[t], blk(t, x, y, c), local_sems.at[t]) for t in range(nt)]
        for cp in local:
            cp.start()
        sends = []
        for t in range(nt):
            sends.append(copy(t, 0, sibling, (x, y, c), True))
            sends += [copy(t, 1 + j, (*chip, c), (x, y, c), True) for j, chip in enumerate(chips)]
        for cp in sends:
            cp.start()
        for j, chip in enumerate(chips):
            for t in range(nt):
                copy(t, 1 + j, (x, y, c), (*chip, c), False).wait_recv()
                fwd = copy(t, 4 + j, sibling, (*chip, c), False)
                fwd.start()
                sends.append(fwd)
        for t in range(nt):
            copy(t, 0, (x, y, c), (x, y, 1 - c), False).wait_recv()
            for j, chip in enumerate(chips):
                copy(t, 4 + j, (x, y, c), (*chip, 1 - c), False).wait_recv()
        for cp in sends:
            cp.wait_send()
        for cp in local:
            cp.wait()

    return _pc(
        body, name,
        in_specs=[HBM_SPEC] * nt, out_specs=[HBM_SPEC] * nt,
        out_shape=_gather_shapes(shards), scratch_shapes=_comm_scratch(nt),
        compiler_params=pltpu.CompilerParams(has_side_effects=True),
    )(*shards)


def _ada_fwd(c, w_ada, b_ada_cols):
    ncol = w_ada.shape[1]

    def body(c_ref, w_ref, b_ref, mod_ref, call_ref, cact, cmat, mloc, send1, recv1, send2, recv2):
        x, y, z = _me()
        me = 4 * x + 2 * y + z
        cv = c_ref[...]
        cact[...] = cv * _sig(cv)
        call_ref[me] = cact[...]
        sends = []
        for k in range(1, N_DEV):
            dev, _ = _peer(k)
            cp = _remote(cact, call_ref.at[me], send1.at[k - 1], recv1.at[k - 1], dev)
            cp.start()
            sends.append(cp)
        for k in range(1, N_DEV):
            dev, blk = _peer(k)
            _remote(cact, call_ref.at[blk], send1.at[k - 1], recv1.at[k - 1], dev).wait_recv()
        cmat[...] = jnp.zeros_like(cmat)
        for b in range(N_DEV):
            cmat[b:b + 1, :] = call_ref[b]
        m = jnp.dot(cmat[...].astype(BF16), w_ref[...].astype(BF16), preferred_element_type=F32) + b_ref[...]
        for b in range(N_DEV):
            mloc[b] = m[b:b + 1, :]
        mod_ref[me] = mloc[me]
        for k in range(1, N_DEV):
            dev, blk = _peer(k)
            cp = _remote(mloc.at[blk], mod_ref.at[me], send2.at[k - 1], recv2.at[k - 1], dev)
            cp.start()
            sends.append(cp)
        for k in range(1, N_DEV):
            dev, blk = _peer(k)
            _remote(mloc.at[me], mod_ref.at[blk], send2.at[k - 1], recv2.at[k - 1], dev).wait_recv()
        for cp in sends:
            cp.wait_send()

    return _pc(
        body, "ada_fwd",
        in_specs=[VMEM_SPEC] * 3, out_specs=[VMEM_SPEC] * 2,
        out_shape=[jax.ShapeDtypeStruct((N_DEV, 1, ncol), F32), jax.ShapeDtypeStruct((N_DEV, 1, D_MODEL), F32)],
        scratch_shapes=[pltpu.VMEM((1, D_MODEL), F32), pltpu.VMEM((16, D_MODEL), F32),
                        pltpu.VMEM((N_DEV, 1, ncol), F32)] + [pltpu.SemaphoreType.DMA((N_DEV - 1,))] * 4,
        compiler_params=pltpu.CompilerParams(has_side_effects=True, vmem_limit_bytes=40 * MIB),
    )(c, w_ada, b_ada_cols)


def _small_gather(vec):
    n = vec.shape[1]

    def body(v_ref, all_ref, sum_ref, send, recv):
        x, y, z = _me()
        me = 4 * x + 2 * y + z
        all_ref[me] = v_ref[...]
        sends = []
        for k in range(1, N_DEV):
            dev, _ = _peer(k)
            cp = _remote(v_ref, all_ref.at[me], send.at[k - 1], recv.at[k - 1], dev)
            cp.start()
            sends.append(cp)
        for k in range(1, N_DEV):
            dev, blk = _peer(k)
            _remote(v_ref, all_ref.at[blk], send.at[k - 1], recv.at[k - 1], dev).wait_recv()
        acc = all_ref[0]
        for d in range(1, N_DEV):
            acc = acc + all_ref[d]
        sum_ref[...] = acc
        for cp in sends:
            cp.wait_send()

    return _pc(
        body, "small_gather",
        in_specs=[VMEM_SPEC], out_specs=[VMEM_SPEC] * 2,
        out_shape=[jax.ShapeDtypeStruct((N_DEV, 1, n), F32), jax.ShapeDtypeStruct((1, n), F32)],
        scratch_shapes=[pltpu.SemaphoreType.DMA((N_DEV - 1,))] * 2,
        compiler_params=pltpu.CompilerParams(has_side_effects=True),
    )(vec)


def _local_step(x, target, mod, wt_in_loc, w_out_loc, w_pw_loc, conv_w_loc, rel_bias, sinks, conv_b, conv_ln_g,
                conv_ln_b, b_pw, ln_g, ln_b):
    bmap = _bucket_map()
    bias = _bias_table(rel_bias, bmap)
    (wt_in,) = _all_gather_two_level([wt_in_loc], "gather_w_in")
    (h, qkv, g_attn, glu, g_conv), (w_out, w_pw, conv_w_blocks) = _inproj(
        x, mod, wt_in, (w_out_loc, w_pw_loc, conv_w_loc))
    conv_w = conv_w_blocks.reshape(N_DEV, CONV_ROWS, 128).transpose(1, 0, 2).reshape(CONV_ROWS, D_CONV)
    ycat, a_out, u1, u3, p_out = _mixer_fwd(qkv, g_attn, glu, g_conv, bias, sinks, conv_w, conv_b, conv_ln_g,
                                            conv_ln_b, w_pw, b_pw)
    dz, dy, dycat, loss, g_ln_g, g_ln_b, dgate = _outproj_ln(ycat, w_out, x, target, mod, ln_g, ln_b)
    gw_out = _matmul_tn([ycat], dy, "grad_w_out")
    dp, dgc, du1, g_bpw, g_clg, g_clb, g_cb = _conv_bwd_a(dycat, g_conv, p_out, u1, conv_ln_g, conv_ln_b, w_pw)
    gw_pw = _matmul_tn([u3], dp, "grad_w_pw")
    dab, g_cw = _conv_bwd_b(du1, glu, conv_w)
    (dqkv, dga, dbias, dsink), (r_out, r_pw) = _attn_bwd(qkv, g_attn, a_out, dycat, bias, sinks,
                                                         scatter=(gw_out, gw_pw))
    g_rb = _relbias_grad(dbias, bmap)
    gwt_in = _matmul_tn([dqkv, dga, dab, dgc], h, "grad_w_in")
    g_cw_blocks = g_cw.reshape(CONV_ROWS, N_DEV, 128).transpose(1, 0, 2).reshape(N_DEV * CONV_ROWS, 128)
    (grad_x, dshift, dscale), (r_in, r_cw) = _dh_kernel([dqkv, dga, dab, dgc], wt_in, dz, x, mod,
                                                        scatter=(gwt_in, g_cw_blocks))
    dmod = jnp.concatenate([dshift, dscale, dgate], axis=1)
    small = dict(dmod=dmod, b_pw=g_bpw, conv_ln_g=g_clg, conv_ln_b=g_clb, conv_b=g_cb, ln_g=g_ln_g, ln_b=g_ln_b,
                 rel_bias=g_rb[:, :N_BUCKETS].reshape(1, N_BUCKETS * N_Q_HEADS),
                 sinks=dsink[0:1, :], loss=loss[0:1, :])
    return grad_x, r_in, r_out, r_pw, r_cw, small


SMALL_FIELDS = (("dmod", 3 * D_MODEL), ("b_pw", D_CONV), ("conv_ln_g", D_CONV), ("conv_ln_b", D_CONV),
                ("conv_b", D_CONV), ("ln_g", D_MODEL), ("ln_b", D_MODEL), ("rel_bias", N_BUCKETS * N_Q_HEADS),
                ("sinks", 128), ("loss", 128))


def _pack(fields):
    parts = []
    for name, width in SMALL_FIELDS:
        v = fields[name].reshape(1, -1).astype(F32)
        if v.shape[1] < width:
            v = jnp.pad(v, ((0, 0), (0, width - v.shape[1])))
        parts.append(v)
    return jnp.concatenate(parts, axis=1)


def _unpack(vec):
    out, off = {}, 0
    for name, width in SMALL_FIELDS:
        out[name] = vec[:, off:off + width]
        off += width
    return out


def kernel(x, c, w_ada, b_ada, w_in, rel_bias, sinks, conv_w, conv_b, conv_ln_g, conv_ln_b, w_pw, b_pw, w_out, ln_g, ln_b, loss_target, m_w_ada, m_b_ada, m_w_in, m_rel_bias, m_sinks, m_conv_w, m_conv_b, m_conv_ln_g, m_conv_ln_b, m_w_pw, m_b_pw, m_w_out, m_ln_g, m_ln_b, v_w_ada, v_b_ada, v_w_in, v_rel_bias, v_sinks, v_conv_w, v_conv_b, v_conv_ln_g, v_conv_ln_b, v_w_pw, v_b_pw, v_w_out, v_ln_g, v_ln_b):
    xi, yi, ci = _me()
    me = 4 * xi + 2 * yi + ci
    ncol = w_ada.shape[2]

    wt_in_loc = w_in[0].T.astype(BF16)
    conv_w_loc = jnp.pad(conv_w[0], ((0, CONV_ROWS - CONV_WIDTH), (0, 0)))

    b_ada_cols = lax.dynamic_slice(b_ada, (0, me * ncol), (1, ncol))
    mod_blocks, c_all = _ada_fwd(c, w_ada[0], b_ada_cols)
    mod = mod_blocks.reshape(1, 3 * D_MODEL)

    grad_x, r_in, r_out, r_pw, r_cw, small = _local_step(
        x[0], loss_target[0], mod, wt_in_loc, w_out[0].astype(BF16), w_pw[0].astype(BF16), conv_w_loc, rel_bias,
        sinks, conv_b, conv_ln_g, conv_ln_b, b_pw, ln_g, ln_b)

    gathered, summed = _small_gather(_pack(small))
    tot = _unpack(summed)
    dmod_all = gathered[:, 0, :3 * D_MODEL]
    loss = tot["loss"][0, 0]

    ct = jnp.zeros((D_MODEL, 128), BF16).at[:, :N_DEV].set(c_all[:, 0, :].T.astype(BF16))
    dm = jnp.zeros((128, ncol), BF16).at[:N_DEV, :].set(
        lax.dynamic_slice(dmod_all, (0, me * ncol), (N_DEV, ncol)).astype(BF16))
    g_w_ada = _small_mm(ct, dm, "grad_w_ada")

    g_conv_w = _sum8(r_cw.reshape(N_DEV, CONV_ROWS, 128), "sum_conv_w")[:CONV_WIDTH]

    grads = {"w_ada": g_w_ada, "conv_w": g_conv_w,
             "w_pw": r_pw.reshape(N_DEV, D_CONV // N_DEV, D_CONV),
             "w_out": r_out.reshape(N_DEV, D_MODEL // N_DEV, D_MODEL)}
    params = {"w_ada": (w_ada, m_w_ada, v_w_ada), "conv_w": (conv_w, m_conv_w, v_conv_w),
              "w_pw": (w_pw, m_w_pw, v_w_pw), "w_out": (w_out, m_w_out, v_w_out)}
    res = {}
    for name, g in grads.items():
        w_, m_, v_ = params[name]
        res[name] = tuple(a[None] for a in _adamw(w_[0], g, m_[0], v_[0], "adamw_" + name))
    upd = _adamw(w_in[0].T, r_in.reshape(N_DEV, D_IN // N_DEV, D_MODEL), m_w_in[0].T, v_w_in[0].T, "adamw_w_in")
    res["w_in"] = tuple(a.T[None] for a in upd)

    small_params = {"b_ada": (b_ada, m_b_ada, v_b_ada), "b_pw": (b_pw, m_b_pw, v_b_pw),
                    "conv_ln_g": (conv_ln_g, m_conv_ln_g, v_conv_ln_g),
                    "conv_ln_b": (conv_ln_b, m_conv_ln_b, v_conv_ln_b), "conv_b": (conv_b, m_conv_b, v_conv_b),
                    "ln_g": (ln_g, m_ln_g, v_ln_g), "ln_b": (ln_b, m_ln_b, v_ln_b),
                    "rel_bias": (rel_bias, m_rel_bias, v_rel_bias), "sinks": (sinks, m_sinks, v_sinks)}
    to_small = lambda n, a: a.T if n == "rel_bias" else a
    key_of = {"b_ada": "dmod"}
    packs = []
    for j in range(3):
        fields = {key_of.get(n, n): to_small(n, t[j]) for n, t in small_params.items()}
        fields["loss"] = jnp.zeros((1, 1), F32)
        packs.append(_pack(fields))
    gsum = summed
    _, d_s, nm_s, nv_s = _adamw(packs[0], gsum, packs[1], packs[2], "adamw_small")
    outs_small = [_unpack(a) for a in (gsum, d_s, nm_s, nv_s)]
    for n, t in small_params.items():
        shape = t[0].shape
        vals = []
        for o in outs_small:
            a = o[key_of.get(n, n)]
            if n == "rel_bias":
                a = a.reshape(N_Q_HEADS, N_BUCKETS).T
            else:
                a = a[:, :shape[1]].reshape(shape)
            vals.append(a)
        res[n] = tuple(vals)

    order = ["w_ada", "b_ada", "w_in", "rel_bias", "sinks", "conv_w", "conv_b", "conv_ln_g", "conv_ln_b", "w_pw",
             "b_pw", "w_out", "ln_g", "ln_b"]
    out = [loss, grad_x[None]]
    for j in range(4):
        out += [res[n][j] for n in order]
    return tuple(out)
```

```python
import math

import jax
import jax.numpy as jnp
from jax import lax
from jax.experimental import pallas as pl
from jax.experimental.pallas import tpu as pltpu

F32, BF16, I32 = jnp.float32, jnp.bfloat16, jnp.int32

D_MODEL = 2048
D_ATTN = 1024
D_CONV = 1024
D_KV = 256
HEAD_DIM = 64
N_Q_HEADS = 16
N_KV_HEADS = 4
GQA = 4
BLOCK = 128
CONV_WIDTH = 31
CONV_ROWS = 32
HALO = 32
N_BUCKETS = 32
MAX_DISTANCE = 128
LN_EPS = 1e-5
ALPHA = 2.0 ** 0.25
D_IN = 5632
N_DEV = 8
NEG = -1e30

ADAM_LR, ADAM_B1, ADAM_B2, ADAM_EPS, ADAM_WD, ADAM_STEP = 0.001, 0.9, 0.999, 1e-08, 0.01, 10

NT_DIMS = (((1,), (1,)), ((), ()))
TN_DIMS = (((0,), (0,)), ((), ()))
MIB = 1 << 20


def _pc(body, name, **kw):
    return pl.pallas_call(body, name=name, **kw)


def _cp(vmem_mib=None, sem=None):
    kw = {}
    if vmem_mib is not None:
        kw["vmem_limit_bytes"] = vmem_mib * MIB
    if sem is not None:
        kw["dimension_semantics"] = sem
    return pltpu.CompilerParams(**kw)


def _sig(x):
    return 1.0 / (1.0 + jnp.exp(-x))


def _dsilu(x, s):
    return s * (1.0 + x * (1.0 - s))


def _me():
    return lax.axis_index("x"), lax.axis_index("y"), lax.axis_index("c")


def _peer(k):
    x, y, c = _me()
    px = 1 - x if k & 4 else x
    py = 1 - y if k & 2 else y
    pc = 1 - c if k & 1 else c
    return (px, py, pc), 4 * px + 2 * py + pc


def _remote(src, dst, send_sem, recv_sem, dev):
    return pltpu.make_async_remote_copy(src_ref=src, dst_ref=dst, send_sem=send_sem, recv_sem=recv_sem,
                                        device_id=dev, device_id_type=pl.DeviceIdType.MESH)


HBM_SPEC = pl.BlockSpec(memory_space=pl.ANY)
VMEM_SPEC = pl.BlockSpec(memory_space=pltpu.VMEM)


def _comm_scratch(nt):
    return [pltpu.SemaphoreType.DMA((nt, N_DEV - 1)), pltpu.SemaphoreType.DMA((nt, N_DEV - 1)),
            pltpu.SemaphoreType.DMA((nt,))]


def _gather_shapes(shards):
    return [jax.ShapeDtypeStruct((N_DEV * s.shape[0], s.shape[1]), s.dtype) for s in shards]


def _block(ref, blk, rows):
    return ref.at[pl.ds(pl.multiple_of(blk * rows, 8), rows), :]


def _gather_copies(src, dst, sems):
    send_sems, recv_sems, local_sems = sems
    x, y, c = _me()
    me = 4 * x + 2 * y + c
    nt = len(src)
    local = [pltpu.make_async_copy(src[t], _block(dst[t], me, src[t].shape[0]), local_sems.at[t]) for t in range(nt)]
    sends, arrivals = [], []
    for k in range(1, N_DEV):
        dev, blk = _peer(k)
        for t in range(nt):
            r = src[t].shape[0]
            pair = (send_sems.at[t, k - 1], recv_sems.at[t, k - 1], dev)
            sends.append(_remote(src[t], _block(dst[t], me, r), *pair))
            arrivals.append(_remote(src[t], _block(dst[t], blk, r), *pair))
    return local, sends, arrivals


def _scatter_copies(src, dst, sems):
    send_sems, recv_sems, local_sems = sems
    x, y, c = _me()
    me = 4 * x + 2 * y + c
    nt = len(src)
    rows = [s.shape[0] // N_DEV for s in src]
    local = [pltpu.make_async_copy(_block(src[t], me, rows[t]), _block(dst[t], me, rows[t]), local_sems.at[t])
             for t in range(nt)]
    sends, arrivals = [], []
    for k in range(1, N_DEV):
        dev, blk = _peer(k)
        for t in range(nt):
            pair = (send_sems.at[t, k - 1], recv_sems.at[t, k - 1], dev)
            sends.append(_remote(_block(src[t], blk, rows[t]), _block(dst[t], me, rows[t]), *pair))
            arrivals.append(_remote(_block(src[t], me, rows[t]), _block(dst[t], blk, rows[t]), *pair))
    return local, sends, arrivals


def _comm_start(cps):
    local, sends, _ = cps
    for cp in local + sends:
        cp.start()


def _comm_wait(cps):
    local, sends, arrivals = cps
    for cp in arrivals:
        cp.wait_recv()
    for cp in sends:
        cp.wait_send()
    for cp in local:
        cp.wait()


IN_PIECES = ((0, 1536, BF16), (1536, 1024, F32), (2560, 2048, F32), (4608, 1024, F32))


def _inproj(x, mod, wt, gather):
    S = x.shape[0]
    tm = min(256, S)
    ni = S // tm
    ng = len(gather)
    npc = len(IN_PIECES)

    def body(*refs):
        x_ref, sh_ref, sc_ref, w_ref = refs[:4]
        outs = refs[4 + ng:]
        h_ref, piece_refs = outs[0], outs[1:1 + npc]
        i = pl.program_id(0)
        if ng:
            cps = _gather_copies(refs[4:4 + ng], outs[1 + npc:1 + npc + ng], outs[1 + npc + ng:])

            @pl.when(i == 0)
            def _():
                _comm_start(cps)

        h = (x_ref[...] * (1.0 + sc_ref[...]) + sh_ref[...]).astype(BF16)
        h_ref[...] = h
        for (r0, n, dt), o_ref in zip(IN_PIECES, piece_refs):
            o_ref[...] = lax.dot_general(h, w_ref[r0:r0 + n, :], NT_DIMS, preferred_element_type=F32).astype(dt)
        if ng:
            @pl.when(i == ni - 1)
            def _():
                _comm_wait(cps)

    tile = lambda n: pl.BlockSpec((tm, n), lambda i: (i, 0))
    res = _pc(
        body, "inproj", grid=(ni,),
        in_specs=[tile(D_MODEL),
                  pl.BlockSpec((1, D_MODEL), lambda i: (0, 0)),
                  pl.BlockSpec((1, D_MODEL), lambda i: (0, 1)),
                  pl.BlockSpec((D_IN, D_MODEL), lambda i: (0, 0), pipeline_mode=pl.Buffered(1))] + [HBM_SPEC] * ng,
        out_specs=[tile(D_MODEL)] + [tile(n) for _, n, _ in IN_PIECES] + [HBM_SPEC] * ng,
        out_shape=[jax.ShapeDtypeStruct((S, D_MODEL), BF16)]
        + [jax.ShapeDtypeStruct((S, n), dt) for _, n, dt in IN_PIECES] + _gather_shapes(gather),
        scratch_shapes=_comm_scratch(ng) if ng else [],
        compiler_params=_cp(56, ("arbitrary",)),
    )(x, mod, mod, wt, *gather)
    return res[:1 + npc], res[1 + npc:]


def _matmul_tn(segs, b, name):
    S, N = b.shape
    tm, ts = 512, min(2048, S)
    ns = S // ts
    bounds = []
    t0 = 0
    for a in segs:
        assert a.shape[1] % tm == 0
        bounds.append((t0, a.shape[1] // tm))
        t0 += a.shape[1] // tm
    nseg = len(segs)

    def body(*refs):
        seg_refs, b_ref = refs[:nseg], refs[nseg]
        o_ref, acc = refs[-2], refs[-1]
        t, s = pl.program_id(0), pl.program_id(1)

        @pl.when(s == 0)
        def _():
            acc[...] = jnp.zeros_like(acc)

        for (s0, sn), r in zip(bounds, seg_refs):
            @pl.when((t >= s0) & (t < s0 + sn))
            def _(r=r):
                acc[...] += lax.dot_general(r[...], b_ref[...], TN_DIMS, preferred_element_type=F32)

        @pl.when(s == ns - 1)
        def _():
            o_ref[...] = acc[...].astype(BF16)

    def seg_spec(s0, sn):
        def index(t, s):
            mine = (t >= s0) & (t < s0 + sn)
            return jnp.where(mine, s, 0), jnp.clip(t - s0, 0, sn - 1)
        return pl.BlockSpec((ts, tm), index)

    return _pc(
        body, name, grid=(t0, ns),
        in_specs=[seg_spec(*bd) for bd in bounds] + [pl.BlockSpec((ts, N), lambda t, s: (s, 0))],
        out_specs=pl.BlockSpec((tm, N), lambda t, s: (t, 0)),
        out_shape=jax.ShapeDtypeStruct((t0 * tm, N), BF16),
        scratch_shapes=[pltpu.VMEM((tm, N), F32)],
        compiler_params=_cp(48, ("parallel", "arbitrary")),
    )(*segs, b)


def _bucket_map():
    qi = jnp.arange(BLOCK, dtype=I32)[:, None]
    kj = jnp.arange(2 * BLOCK, dtype=I32)[None, :]
    dist = qi + BLOCK - kj
    in_window = (dist >= 0) & (dist < BLOCK)
    d0 = jnp.maximum(dist, 0)
    max_exact = N_BUCKETS // 2
    d = jnp.maximum(d0, 1).astype(F32)
    large = max_exact + (jnp.log(d / max_exact) / math.log(MAX_DISTANCE / max_exact)
                         * (N_BUCKETS - max_exact)).astype(I32)
    large = jnp.minimum(large, N_BUCKETS - 1)
    bucket = jnp.where(d0 < max_exact, d0, large)
    return jnp.where(in_window, bucket, -1).astype(I32)


def _bias_table(rel_bias, bmap):
    def body(rb_ref, bm_ref, o_ref):
        h = pl.program_id(0)
        bm = bm_ref[...]
        acc = jnp.full((BLOCK, 2 * BLOCK), NEG, F32)
        for b in range(N_BUCKETS):
            acc = jnp.where(bm == b, rb_ref[b, h], acc)
        o_ref[0] = acc

    return _pc(
        body, "bias_table", grid=(N_Q_HEADS,),
        in_specs=[pl.BlockSpec(memory_space=pltpu.SMEM),
                  pl.BlockSpec((BLOCK, 2 * BLOCK), lambda h: (0, 0))],
        out_specs=pl.BlockSpec((1, BLOCK, 2 * BLOCK), lambda h: (h, 0, 0)),
        out_shape=jax.ShapeDtypeStruct((N_Q_HEADS, BLOCK, 2 * BLOCK), F32),
    )(rel_bias, bmap)


def _bias_spec():
    return pl.BlockSpec((N_Q_HEADS, BLOCK, 2 * BLOCK), lambda n: (0, 0, 0))


def _relbias_grad(dbias, bmap):
    def body(db_ref, bm_ref, o_ref):
        bm = bm_ref[...]
        x = db_ref[0]
        lane = lax.broadcasted_iota(I32, (1, 128), 1)
        row = jnp.zeros((1, 128), F32)
        for b in range(N_BUCKETS):
            row = jnp.where(lane == b, jnp.sum(jnp.where(bm == b, x, 0.0)), row)
        o_ref[0] = jnp.broadcast_to(row, (8, 128))

    out = _pc(
        body, "relbias_grad", grid=(N_Q_HEADS,),
        in_specs=[pl.BlockSpec((1, BLOCK, 2 * BLOCK), lambda h: (h, 0, 0)),
                  pl.BlockSpec((BLOCK, 2 * BLOCK), lambda h: (0, 0))],
        out_specs=pl.BlockSpec((1, 8, 128), lambda h: (h, 0, 0)),
        out_shape=jax.ShapeDtypeStruct((N_Q_HEADS, 8, 128), F32),
    )(dbias, bmap)
    return out[:, 0, :]


Q_SCALE = HEAD_DIM ** -0.5
assert math.frexp(Q_SCALE)[0] == 0.5


def _stack_heads(ref, hk, rows=slice(None)):
    return jnp.concatenate([ref[rows, pl.ds(256 * hk + 64 * g, 64)] for g in range(GQA)], axis=0) * Q_SCALE


def _sink_col(sink_ref, hk):
    row = lax.broadcasted_iota(I32, (GQA * BLOCK, 1), 0)
    s = jnp.full((GQA * BLOCK, 1), sink_ref[0, 4 * hk + 3], F32)
    for g in (2, 1, 0):
        s = jnp.where(row < (g + 1) * BLOCK, sink_ref[0, 4 * hk + g], s)
    return s


def _attn_probs(q4, kw, bias_ref, sink_ref, hk, first_mask):
    s = lax.dot_general(q4, kw, NT_DIMS, preferred_element_type=F32)
    s = s + bias_ref[4 * hk:4 * hk + 4].reshape(GQA * BLOCK, 2 * BLOCK)
    if first_mask is not None:
        s = jnp.where(first_mask, s, NEG)
    sink = _sink_col(sink_ref, hk)
    m = jnp.maximum(jnp.max(s, axis=1, keepdims=True), sink)
    e = jnp.exp(s - m)
    es = jnp.exp(sink - m)
    inv = 1.0 / (jnp.sum(e, axis=1, keepdims=True) + es)
    return e * inv, es * inv


def _attn_bwd(qkv, g_attn, a_out, dycat, bias, sinks, scatter=()):
    S = qkv.shape[0]
    nb = S // BLOCK
    R = GQA * BLOCK
    ns = len(scatter)

    def body(*refs):
        (q_ref, kc_ref, kp_ref, vc_ref, vp_ref, g_ref, a_ref, dy_ref, bias_ref, sink_ref) = refs[:10]
        dqkv_ref, dg_ref, dbias_ref, dsink_ref = refs[10 + ns:14 + ns]
        dq_scr, dq_new, dk_scr, dv_scr, dkw_scr, dvw_scr, ds_scr = refs[14 + 2 * ns:21 + 2 * ns]
        n = pl.program_id(0)
        if ns:
            cps = _scatter_copies(refs[10:10 + ns], refs[14 + ns:14 + 2 * ns], refs[21 + 2 * ns:])

            @pl.when(n == 0)
            def _():
                _comm_start(cps)

        @pl.when(n == 0)
        def _():
            dbias_ref[...] = jnp.zeros_like(dbias_ref)
            ds_scr[...] = jnp.zeros_like(ds_scr)
            dq_scr[...] = jnp.zeros_like(dq_scr)
            dk_scr[...] = jnp.zeros_like(dk_scr)
            dv_scr[...] = jnp.zeros_like(dv_scr)

        @pl.when(n < nb)
        def _():
            kj = lax.broadcasted_iota(I32, (R, 2 * BLOCK), 1)
            first_mask = (n > 0) | (kj >= BLOCK)
            for hk in range(N_KV_HEADS):
                q4 = _stack_heads(q_ref, hk)
                ks = pl.ds(64 * hk, 64)
                kw = jnp.concatenate([kp_ref[:, ks], kc_ref[:, ks]], axis=0)
                vw = jnp.concatenate([vp_ref[:, ks], vc_ref[:, ks]], axis=0)
                p, psink = _attn_probs(q4, kw, bias_ref, sink_ref, hk, first_mask)
                da_parts, a_parts = [], []
                for g in range(GQA):
                    sl = pl.ds(256 * hk + 64 * g, 64)
                    gg = g_ref[:, sl]
                    sg = _sig(gg)
                    dyg = dy_ref[:, sl]
                    ag = a_ref[:, sl]
                    da_parts.append(dyg * (gg * sg))
                    a_parts.append(ag)
                    dg_ref[:, sl] = (dyg * ag * _dsilu(gg, sg)).astype(BF16)
                da4 = jnp.concatenate(da_parts, axis=0)
                a4 = jnp.concatenate(a_parts, axis=0)
                delta = jnp.sum(da4 * a4, axis=1, keepdims=True)
                da4b = da4.astype(BF16)
                dp = lax.dot_general(da4b, vw, NT_DIMS, preferred_element_type=F32)
                ds = p * (dp - delta)
                ds_scr[hk] += -psink * delta
                dbias_ref[4 * hk:4 * hk + 4] += ds.reshape(GQA, BLOCK, 2 * BLOCK)
                dsb = ds.astype(BF16)
                dq4 = jnp.dot(dsb, kw, preferred_element_type=F32) * Q_SCALE
                for g in range(GQA):
                    dq_new[:, pl.ds(256 * hk + 64 * g, 64)] = dq4[BLOCK * g:BLOCK * (g + 1)].astype(BF16)
                dkw_scr[:, ks] = lax.dot_general(dsb, q4, TN_DIMS, preferred_element_type=F32)
                dvw_scr[:, ks] = lax.dot_general(p.astype(BF16), da4b, TN_DIMS, preferred_element_type=F32)

        @pl.when(n == nb)
        def _():
            dkw_scr[0:BLOCK, :] = jnp.zeros((BLOCK, D_KV), F32)
            dvw_scr[0:BLOCK, :] = jnp.zeros((BLOCK, D_KV), F32)

        dqkv_ref[:, 0:D_ATTN] = dq_scr[...]
        dqkv_ref[:, D_ATTN:D_ATTN + D_KV] = (dk_scr[...] + dkw_scr[0:BLOCK, :]).astype(BF16)
        dqkv_ref[:, D_ATTN + D_KV:D_ATTN + 2 * D_KV] = (dv_scr[...] + dvw_scr[0:BLOCK, :]).astype(BF16)
        dq_scr[...] = dq_new[...]
        dk_scr[...] = dkw_scr[BLOCK:2 * BLOCK, :]
        dv_scr[...] = dvw_scr[BLOCK:2 * BLOCK, :]

        @pl.when(n == nb)
        def _():
            lane = lax.broadcasted_iota(I32, (1, 128), 1)
            row = jnp.zeros((1, 128), F32)
            for hk in range(N_KV_HEADS):
                col = ds_scr[hk]
                for g in range(GQA):
                    row = jnp.where(lane == 4 * hk + g, jnp.sum(col[BLOCK * g:BLOCK * (g + 1)]), row)
            dsink_ref[...] = jnp.broadcast_to(row, (8, 128))
            if ns:
                _comm_wait(cps)

    cur = lambda n: jnp.minimum(n, nb - 1)
    prev = lambda n: jnp.clip(n - 1, 0, nb - 1)
    res = _pc(
        body, "attn_bwd", grid=(nb + 1,),
        in_specs=[pl.BlockSpec((BLOCK, D_ATTN), lambda n: (cur(n), 0)),
                  pl.BlockSpec((BLOCK, D_KV), lambda n: (cur(n), 4)),
                  pl.BlockSpec((BLOCK, D_KV), lambda n: (prev(n), 4)),
                  pl.BlockSpec((BLOCK, D_KV), lambda n: (cur(n), 5)),
                  pl.BlockSpec((BLOCK, D_KV), lambda n: (prev(n), 5)),
                  pl.BlockSpec((BLOCK, D_ATTN), lambda n: (cur(n), 0)),
                  pl.BlockSpec((BLOCK, D_ATTN), lambda n: (cur(n), 0)),
                  pl.BlockSpec((BLOCK, D_ATTN), lambda n: (cur(n), 0)),
                  _bias_spec(),
                  pl.BlockSpec(memory_space=pltpu.SMEM)] + [HBM_SPEC] * ns,
        out_specs=[pl.BlockSpec((BLOCK, D_ATTN + 2 * D_KV), lambda n: (prev(n), 0)),
                   pl.BlockSpec((BLOCK, D_ATTN), lambda n: (cur(n), 0)),
                   pl.BlockSpec((N_Q_HEADS, BLOCK, 2 * BLOCK), lambda n: (0, 0, 0)),
                   pl.BlockSpec((8, 128), lambda n: (0, 0))] + [HBM_SPEC] * ns,
        out_shape=[jax.ShapeDtypeStruct((S, D_ATTN + 2 * D_KV), BF16),
                   jax.ShapeDtypeStruct((S, D_ATTN), BF16),
                   jax.ShapeDtypeStruct((N_Q_HEADS, BLOCK, 2 * BLOCK), F32),
                   jax.ShapeDtypeStruct((8, 128), F32)] + [jax.ShapeDtypeStruct(f.shape, f.dtype) for f in scatter],
        scratch_shapes=[pltpu.VMEM((BLOCK, D_ATTN), BF16), pltpu.VMEM((BLOCK, D_ATTN), BF16),
                        pltpu.VMEM((BLOCK, D_KV), F32), pltpu.VMEM((BLOCK, D_KV), F32),
                        pltpu.VMEM((2 * BLOCK, D_KV), F32), pltpu.VMEM((2 * BLOCK, D_KV), F32),
                        pltpu.VMEM((N_KV_HEADS, R, 1), F32)] + (_comm_scratch(ns) if ns else []),
        compiler_params=_cp(48, ("arbitrary",)),
    )(qkv, qkv, qkv, qkv, qkv, g_attn, a_out, dycat, bias, sinks, *scatter)
    return res[:4], res[4:]


def _conv_tile(S):
    return min(256, S)


def _shifted(win, s):
    return win if s == 0 else pltpu.roll(win, win.shape[0] - s, axis=0)


def _conv_taps(win, cw_ref, lanes, rows):
    acc = jnp.zeros((rows, win.shape[1]), F32)
    for s in range(8):
        ws = _shifted(win, s)
        for aa in range(5):
            j = 8 * aa + s - 2
            if 0 <= j < CONV_WIDTH:
                acc = acc + ws[8 * aa:8 * aa + rows] * cw_ref[j:j + 1, lanes]
    return acc


def _mixer_fwd(qkv, g_attn, glu, g_conv, bias, sinks, conv_w, conv_b, ln_g, ln_b, w_pw, b_pw):
    S = qkv.shape[0]
    RB = 2
    TR = RB * BLOCK
    nb = S // TR
    hb = TR // HALO
    LG = D_CONV // (RB * N_KV_HEADS)

    def body(q_ref, kc_ref, kp_ref, vc_ref, vp_ref, g_ref, bias_ref, sink_ref,
             a_in, ah_in, b_in, bh_in, gc_ref, cw_ref, cb_ref, lg_ref, lb_ref, wpw_ref, bpw_ref,
             y_ref, a_ref, u1_ref, u3_ref, p_ref, win):
        n = pl.program_id(0)
        kj = lax.broadcasted_iota(I32, (GQA * BLOCK, 2 * BLOCK), 1)
        first_mask = (n > 0) | (kj >= BLOCK)
        win[0:HALO, :] = jnp.where(n > 0, ah_in[...] * _sig(bh_in[...]), 0.0)
        win[HALO:HALO + TR, :] = a_in[...] * _sig(b_in[...])
        piece = 0
        for s in range(RB):
            rows = slice(BLOCK * s, BLOCK * (s + 1))
            before = slice(BLOCK * (s - 1), BLOCK * s)
            for hk in range(N_KV_HEADS):
                q4 = _stack_heads(q_ref, hk, rows)
                ks = pl.ds(64 * hk, 64)
                k_prev, v_prev = (kp_ref[:, ks], vp_ref[:, ks]) if s == 0 else (kc_ref[before, ks], vc_ref[before, ks])
                kw = jnp.concatenate([k_prev, kc_ref[rows, ks]], axis=0)
                vw = jnp.concatenate([v_prev, vc_ref[rows, ks]], axis=0)
                p, _ = _attn_probs(q4, kw, bias_ref, sink_ref, hk, first_mask if s == 0 else None)
                o4 = jnp.dot(p.astype(BF16), vw, preferred_element_type=F32)
                for g in range(GQA):
                    sl = pl.ds(256 * hk + 64 * g, 64)
                    og = o4[BLOCK * g:BLOCK * (g + 1)]
                    gg = g_ref[rows, sl]
                    a_ref[rows, sl] = og
                    y_ref[rows, sl] = (og * (gg * _sig(gg))).astype(BF16)
                lanes = pl.ds(LG * piece, LG)
                u1_ref[:, lanes] = _conv_taps(win[:, lanes], cw_ref, lanes, TR) + cb_ref[:, lanes]
                piece += 1
        u1 = u1_ref[...]
        mu = jnp.mean(u1, axis=1, keepdims=True)
        uc = u1 - mu
        rstd = lax.rsqrt(jnp.mean(uc * uc, axis=1, keepdims=True) + LN_EPS)
        u2 = uc * rstd * lg_ref[...] + lb_ref[...]
        u3 = (u2 * _sig(u2)).astype(BF16)
        u3_ref[...] = u3
        pw = jnp.dot(u3, wpw_ref[...], preferred_element_type=F32) + bpw_ref[...]
        p_ref[...] = pw
        gc = gc_ref[...]
        y_ref[:, D_ATTN:] = (pw * (gc * _sig(gc))).astype(BF16)

    prev = lambda n: jnp.maximum(RB * n - 1, 0)
    halo = lambda n: jnp.maximum(n * hb - 1, 0)
    vec = pl.BlockSpec((1, D_CONV), lambda n: (0, 0))
    blk = lambda w, j: pl.BlockSpec((TR, w), lambda n: (n, j))
    return _pc(
        body, "mixer_fwd", grid=(nb,),
        in_specs=[blk(D_ATTN, 0),
                  blk(D_KV, 4), pl.BlockSpec((BLOCK, D_KV), lambda n: (prev(n), 4)),
                  blk(D_KV, 5), pl.BlockSpec((BLOCK, D_KV), lambda n: (prev(n), 5)),
                  blk(D_ATTN, 0),
                  _bias_spec(),
                  pl.BlockSpec(memory_space=pltpu.SMEM),
                  blk(D_CONV, 0), pl.BlockSpec((HALO, D_CONV), lambda n: (halo(n), 0)),
                  blk(D_CONV, 1), pl.BlockSpec((HALO, D_CONV), lambda n: (halo(n), 1)),
                  blk(D_CONV, 0),
                  pl.BlockSpec((CONV_ROWS, D_CONV), lambda n: (0, 0)),
                  vec, vec, vec,
                  pl.BlockSpec((D_CONV, D_CONV), lambda n: (0, 0)),
                  vec],
        out_specs=[blk(2 * D_ATTN, 0), blk(D_ATTN, 0), blk(D_CONV, 0), blk(D_CONV, 0), blk(D_CONV, 0)],
        out_shape=[jax.ShapeDtypeStruct((S, 2 * D_ATTN), BF16),
                   jax.ShapeDtypeStruct((S, D_ATTN), F32),
                   jax.ShapeDtypeStruct((S, D_CONV), F32),
                   jax.ShapeDtypeStruct((S, D_CONV), BF16),
                   jax.ShapeDtypeStruct((S, D_CONV), F32)],
        scratch_shapes=[pltpu.VMEM((TR + HALO, D_CONV), F32)],
        compiler_params=_cp(56, ("parallel",)),
    )(qkv, qkv, qkv, qkv, qkv, g_attn, bias, sinks, glu, glu, glu, glu, g_conv, conv_w, conv_b, ln_g, ln_b, w_pw,
      b_pw)


def _conv_bwd_a(dycat, g_conv, p_out, u1, ln_g, ln_b, w_pw):
    S = u1.shape[0]
    T = _conv_tile(S)

    def body(dy_ref, gc_ref, p_ref, u1_ref, lg_ref, lb_ref, wpw_ref,
             dp_ref, dgc_ref, du1_ref, gbpw_ref, glg_ref, glb_ref, gcb_ref):
        i = pl.program_id(0)

        @pl.when(i == 0)
        def _():
            for r in (gbpw_ref, glg_ref, glb_ref, gcb_ref):
                r[...] = jnp.zeros_like(r)

        dy = dy_ref[...]
        gc = gc_ref[...]
        sg = _sig(gc)
        dp = dy * (gc * sg)
        dgc_ref[...] = (dy * p_ref[...] * _dsilu(gc, sg)).astype(BF16)
        gbpw_ref[...] += jnp.sum(dp, axis=0, keepdims=True)
        dpb = dp.astype(BF16)
        dp_ref[...] = dpb
        du3 = lax.dot_general(dpb, wpw_ref[...], NT_DIMS, preferred_element_type=F32)
        u1 = u1_ref[...]
        mu = jnp.mean(u1, axis=1, keepdims=True)
        uc = u1 - mu
        rstd = lax.rsqrt(jnp.mean(uc * uc, axis=1, keepdims=True) + LN_EPS)
        uh = uc * rstd
        lg = lg_ref[...]
        u2 = uh * lg + lb_ref[...]
        s2 = _sig(u2)
        du2 = du3 * _dsilu(u2, s2)
        glg_ref[...] += jnp.sum(du2 * uh, axis=0, keepdims=True)
        glb_ref[...] += jnp.sum(du2, axis=0, keepdims=True)
        duh = du2 * lg
        du1 = rstd * (duh - jnp.mean(duh, axis=1, keepdims=True) - uh * jnp.mean(duh * uh, axis=1, keepdims=True))
        du1_ref[...] = du1
        gcb_ref[...] += jnp.sum(du1, axis=0, keepdims=True)

    vec = pl.BlockSpec((1, D_CONV), lambda i: (0, 0))
    tile = pl.BlockSpec((T, D_CONV), lambda i: (i, 0))
    vshape = jax.ShapeDtypeStruct((1, D_CONV), F32)
    return _pc(
        body, "conv_bwd_a", grid=(S // T,),
        in_specs=[pl.BlockSpec((T, D_CONV), lambda i: (i, 1)), tile, tile, tile, vec, vec,
                  pl.BlockSpec((D_CONV, D_CONV), lambda i: (0, 0))],
        out_specs=[tile, tile, tile, vec, vec, vec, vec],
        out_shape=[jax.ShapeDtypeStruct((S, D_CONV), BF16), jax.ShapeDtypeStruct((S, D_CONV), BF16),
                   jax.ShapeDtypeStruct((S, D_CONV), F32), vshape, vshape, vshape, vshape],
        compiler_params=_cp(48, ("arbitrary",)),
    )(dycat, g_conv, p_out, u1, ln_g, ln_b, w_pw)


def _conv_bwd_b(du1, glu, conv_w):
    S = du1.shape[0]
    T = _conv_tile(S)
    hb = T // HALO
    nt = S // T
    last_h = S // HALO - 1

    def body(du_ref, dun_ref, a_ref, b_ref, cw_ref, dab_ref, gw_ref):
        i = pl.program_id(0)

        @pl.when(i == 0)
        def _():
            gw_ref[...] = jnp.zeros_like(gw_ref)

        for lg in range(D_CONV // 128):
            lanes = pl.ds(128 * lg, 128)
            a = a_ref[:, lanes]
            sb = _sig(b_ref[:, lanes])
            u0 = a * sb
            du1n = jnp.where(i < nt - 1, dun_ref[:, lanes], 0.0)
            win2 = jnp.concatenate([du_ref[:, lanes], du1n], axis=0)
            acc = jnp.zeros((T, 128), F32)
            for s in range(8):
                w2 = _shifted(win2, s)
                for aa in range(4):
                    j = CONV_WIDTH - 1 - (8 * aa + s)
                    if 0 <= j < CONV_WIDTH:
                        xo = w2[8 * aa:8 * aa + T]
                        acc = acc + xo * cw_ref[j:j + 1, lanes]
                        gw_ref[j:j + 1, lanes] += jnp.sum(xo * u0, axis=0, keepdims=True)
            dab_ref[:, lanes] = (acc * sb).astype(BF16)
            dab_ref[:, pl.ds(D_CONV + 128 * lg, 128)] = (acc * a * sb * (1.0 - sb)).astype(BF16)

    nxt = lambda i: jnp.minimum((i + 1) * hb, last_h)
    return _pc(
        body, "conv_bwd_b", grid=(nt,),
        in_specs=[pl.BlockSpec((T, D_CONV), lambda i: (i, 0)),
                  pl.BlockSpec((HALO, D_CONV), lambda i: (nxt(i), 0)),
                  pl.BlockSpec((T, D_CONV), lambda i: (i, 0)),
                  pl.BlockSpec((T, D_CONV), lambda i: (i, 1)),
                  pl.BlockSpec((CONV_ROWS, D_CONV), lambda i: (0, 0))],
        out_specs=[pl.BlockSpec((T, 2 * D_CONV), lambda i: (i, 0)),
                   pl.BlockSpec((CONV_ROWS, D_CONV), lambda i: (0, 0))],
        out_shape=[jax.ShapeDtypeStruct((S, 2 * D_CONV), BF16),
                   jax.ShapeDtypeStruct((CONV_ROWS, D_CONV), F32)],
        compiler_params=_cp(48, ("arbitrary",)),
    )(du1, du1, glu, glu, conv_w)


def _outproj_ln(ycat, w_out, x, target, mod, ln_g, ln_b):
    S = x.shape[0]
    tm = min(256, S)

    def body(yc_ref, w_ref, x_ref, t_ref, gate_ref, lg_ref, lb_ref,
             dz_ref, dy_ref, dyc_ref, loss_ref, glg_ref, glb_ref, dgate_ref):
        i = pl.program_id(0)

        @pl.when(i == 0)
        def _():
            for r in (loss_ref, glg_ref, glb_ref, dgate_ref):
                r[...] = jnp.zeros_like(r)

        w = w_ref[...]
        y = jnp.dot(yc_ref[...], w, preferred_element_type=F32)
        gate = gate_ref[...]
        z = ALPHA * x_ref[...] + gate * y
        mu = jnp.mean(z, axis=1, keepdims=True)
        zc = z - mu
        rstd = lax.rsqrt(jnp.mean(zc * zc, axis=1, keepdims=True) + LN_EPS)
        zh = zc * rstd
        lg = lg_ref[...]
        err = zh * lg + lb_ref[...] - t_ref[...]
        loss_ref[...] += 0.5 * jnp.sum(jnp.sum(err * err, axis=1, keepdims=True)) / D_MODEL
        dout = err * (1.0 / D_MODEL)
        glg_ref[...] += jnp.sum(dout * zh, axis=0, keepdims=True)
        glb_ref[...] += jnp.sum(dout, axis=0, keepdims=True)
        dzh = dout * lg
        dz = rstd * (dzh - jnp.mean(dzh, axis=1, keepdims=True) - zh * jnp.mean(dzh * zh, axis=1, keepdims=True))
        dz_ref[...] = dz
        dgate_ref[...] += jnp.sum(dz * y, axis=0, keepdims=True)
        dy = (dz * gate).astype(BF16)
        dy_ref[...] = dy
        dyc_ref[...] = lax.dot_general(dy, w, NT_DIMS, preferred_element_type=F32).astype(BF16)

    vec = pl.BlockSpec((1, D_MODEL), lambda i: (0, 0))
    tile = pl.BlockSpec((tm, D_MODEL), lambda i: (i, 0))
    vshape = jax.ShapeDtypeStruct((1, D_MODEL), F32)
    return _pc(
        body, "outproj_ln", grid=(S // tm,),
        in_specs=[tile, pl.BlockSpec((D_MODEL, D_MODEL), lambda i: (0, 0), pipeline_mode=pl.Buffered(1)), tile, tile,
                  pl.BlockSpec((1, D_MODEL), lambda i: (0, 2)), vec, vec],
        out_specs=[tile, tile, tile, pl.BlockSpec((8, 128), lambda i: (0, 0)), vec, vec, vec],
        out_shape=[jax.ShapeDtypeStruct((S, D_MODEL), F32), jax.ShapeDtypeStruct((S, D_MODEL), BF16),
                   jax.ShapeDtypeStruct((S, D_MODEL), BF16), jax.ShapeDtypeStruct((8, 128), F32),
                   vshape, vshape, vshape],
        compiler_params=_cp(56, ("arbitrary",)),
    )(ycat, w_out, x, target, mod, ln_g, ln_b)


def _dh_kernel(segs, wt, dz, x, mod, scatter=()):
    S = x.shape[0]
    tm = min(256, S)
    row0 = [0]
    for a in segs:
        row0.append(row0[-1] + a.shape[1])
    assert row0[-1] == wt.shape[0]
    nseg = len(segs)
    ns = len(scatter)
    ni = S // tm

    def body(*refs):
        seg_refs = refs[:nseg]
        w_ref, dz_ref, x_ref, sc_ref = refs[nseg:nseg + 4]
        outs = refs[nseg + 4 + ns:]
        gx_ref, dsh_ref, dsc_ref = outs[:3]
        i = pl.program_id(0)
        if ns:
            cps = _scatter_copies(refs[nseg + 4:nseg + 4 + ns], outs[3:3 + ns], outs[3 + ns:])

        @pl.when(i == 0)
        def _():
            dsh_ref[...] = jnp.zeros_like(dsh_ref)
            dsc_ref[...] = jnp.zeros_like(dsc_ref)
            if ns:
                _comm_start(cps)

        dh = jnp.dot(seg_refs[0][...], w_ref[row0[0]:row0[1], :], preferred_element_type=F32)
        for t in range(1, nseg):
            dh = dh + jnp.dot(seg_refs[t][...], w_ref[row0[t]:row0[t + 1], :], preferred_element_type=F32)
        gx_ref[...] = ALPHA * dz_ref[...] + dh * (1.0 + sc_ref[...])
        dsh_ref[...] += jnp.sum(dh, axis=0, keepdims=True)
        dsc_ref[...] += jnp.sum(dh * x_ref[...], axis=0, keepdims=True)

        if ns:
            @pl.when(i == ni - 1)
            def _():
                _comm_wait(cps)

    tile = pl.BlockSpec((tm, D_MODEL), lambda i: (i, 0))
    vec = pl.BlockSpec((1, D_MODEL), lambda i: (0, 0))
    vshape = jax.ShapeDtypeStruct((1, D_MODEL), F32)
    res = _pc(
        body, "dh_gradx", grid=(ni,),
        in_specs=[pl.BlockSpec((tm, a.shape[1]), lambda i: (i, 0)) for a in segs] + [
            pl.BlockSpec(wt.shape, lambda i: (0, 0), pipeline_mode=pl.Buffered(1)), tile, tile,
            pl.BlockSpec((1, D_MODEL), lambda i: (0, 1))] + [HBM_SPEC] * ns,
        out_specs=[tile, vec, vec] + [HBM_SPEC] * ns,
        out_shape=[jax.ShapeDtypeStruct((S, D_MODEL), F32), vshape, vshape]
        + [jax.ShapeDtypeStruct(f.shape, f.dtype) for f in scatter],
        scratch_shapes=_comm_scratch(ns) if ns else [],
        compiler_params=_cp(58, ("arbitrary",)),
    )(*segs, wt, dz, x, mod, *scatter)
    return res[:3], res[3:]


def _row_tile(rows, cols):
    if rows * cols * 4 <= 2 * MIB or rows % 8:
        return rows
    tr = max(8, (2 * MIB // (cols * 4)) // 8 * 8)
    while rows % tr:
        tr -= 8
    return tr


def _sum8(recv, name):
    _, R, C = recv.shape
    tr = _row_tile(R, C)

    def body(r_ref, o_ref):
        acc = r_ref[0].astype(F32)
        for d in range(1, N_DEV):
            acc = acc + r_ref[d].astype(F32)
        o_ref[...] = acc

    return _pc(
        body, name, grid=(R // tr,),
        in_specs=[pl.BlockSpec((N_DEV, tr, C), lambda i: (0, i, 0))],
        out_specs=pl.BlockSpec((tr, C), lambda i: (i, 0)),
        out_shape=jax.ShapeDtypeStruct((R, C), F32),
        compiler_params=_cp(40, ("parallel",)),
    )(recv)


def _adamw(w, g, m, v, name):
    R, C = w.shape
    tr = _row_tile(R, C)
    parts = g.ndim == 3

    def body(w_ref, g_ref, m_ref, v_ref, *out_refs):
        d_ref, nm_ref, nv_ref = out_refs[-3:]
        if parts:
            g_ = g_ref[0].astype(F32)
            for d in range(1, N_DEV):
                g_ = g_ + g_ref[d].astype(F32)
            out_refs[0][...] = g_
        else:
            g_ = g_ref[...]
        m_ = ADAM_B1 * m_ref[...] + (1.0 - ADAM_B1) * g_
        v_ = ADAM_B2 * v_ref[...] + (1.0 - ADAM_B2) * (g_ * g_)
        m_hat = m_ / (1.0 - ADAM_B1 ** ADAM_STEP)
        v_hat = v_ / (1.0 - ADAM_B2 ** ADAM_STEP)
        d_ref[...] = -ADAM_LR * (m_hat / (jnp.sqrt(v_hat) + ADAM_EPS) + ADAM_WD * w_ref[...])
        nm_ref[...] = m_
        nv_ref[...] = v_

    spec = pl.BlockSpec((tr, C), lambda i: (i, 0))
    gspec = pl.BlockSpec((N_DEV, tr, C), lambda i: (0, i, 0)) if parts else spec
    shape = jax.ShapeDtypeStruct((R, C), F32)
    nout = 4 if parts else 3
    res = _pc(
        body, name, grid=(R // tr,),
        in_specs=[spec, gspec, spec, spec], out_specs=[spec] * nout, out_shape=[shape] * nout,
        compiler_params=_cp(40, ("parallel",)),
    )(w, g, m, v)
    return tuple(res) if parts else (g, *res)


def _small_mm(a, b, name):
    def body(a_ref, b_ref, o_ref):
        o_ref[...] = jnp.dot(a_ref[...], b_ref[...], preferred_element_type=F32)

    return _pc(body, name, out_shape=jax.ShapeDtypeStruct((a.shape[0], b.shape[1]), F32),
               compiler_params=_cp(40))(a, b)


def _all_gather_two_level(shards, name):
    nt = len(shards)
    NK = 9

    def body(*refs):
        src, dst = refs[:nt], refs[nt:2 * nt]
        send_sems, recv_sems, local_sems = refs[2 * nt:]
        x, y, c = _me()
        me, sibling = (x, y, c), (x, y, 1 - c)
        xn, yn, dg = (1 - x, y), (x, 1 - y), (1 - x, 1 - y)

        def part(t, dev, half=None):
            r = src[t].shape[0]
            blk = 4 * dev[0] + 2 * dev[1] + dev[2]
            if half is None:
                return dst[t].at[pl.ds(pl.multiple_of(blk * r, 16), r), :]
            return dst[t].at[pl.ds(pl.multiple_of(blk * r + half * (r // 2), 16), r // 2), :]

        def copy(t, k, to, rows, from_src=False):
            return _remote(src[t] if from_src else rows, rows, send_sems.at[t, k], recv_sems.at[t, k], to)

        local = [pltpu.make_async_copy(src[t], part(t, me), local_sems.at[t]) for t in range(nt)]
        for cp in local:
            cp.start()
        sends = []
        for t in range(nt):
            sends += [copy(t, 0, sibling, part(t, me), True), copy(t, 1, (*xn, c), part(t, me), True),
                      copy(t, 2, (*yn, c), part(t, me), True)]
        for cp in sends:
            cp.start()

        def arrive_then_pass(t, k, rows, passes):
            copy(t, k, me, rows).wait_recv()
            for cp in passes:
                cp.start()
                sends.append(cp)

        for t in range(nt):
            arrive_then_pass(t, 1, part(t, (*xn, c)), [copy(t, 5, sibling, part(t, (*xn, c))),
                                                       copy(t, 3, (*yn, c), part(t, (*xn, c), 0))])
            arrive_then_pass(t, 2, part(t, (*yn, c)), [copy(t, 6, sibling, part(t, (*yn, c))),
                                                       copy(t, 4, (*xn, c), part(t, (*yn, c), 1))])
            arrive_then_pass(t, 3, part(t, (*dg, c), 0), [copy(t, 7, sibling, part(t, (*dg, c), 0))])
            arrive_then_pass(t, 4, part(t, (*dg, c), 1), [copy(t, 8, sibling, part(t, (*dg, c), 1))])
        for t in range(nt):
            copy(t, 0, me, part(t, sibling)).wait_recv()
            copy(t, 5, me, part(t, (*xn, 1 - c))).wait_recv()
            copy(t, 6, me, part(t, (*yn, 1 - c))).wait_recv()
            copy(t, 7, me, part(t, (*dg, 1 - c), 0)).wait_recv()
            copy(t, 8, me, part(t, (*dg, 1 - c), 1)).wait_recv()
        for cp in sends:
            cp.wait_send()
        for cp in local:
            cp.wait()

    return _pc(
        body, name,
        in_specs=[HBM_SPEC] * nt, out_specs=[HBM_SPEC] * nt,
        out_shape=_gather_shapes(shards),
        scratch_shapes=[pltpu.SemaphoreType.DMA((nt, NK)), pltpu.SemaphoreType.DMA((nt, NK)),
                        pltpu.SemaphoreType.DMA((nt,))],
        compiler_params=pltpu.CompilerParams(has_side_effects=True),
    )(*shards)


def _ada_fwd(c, w_ada, b_ada_cols):
    ncol = w_ada.shape[1]

    def body(c_ref, w_ref, b_ref, mod_ref, call_ref, cact, cmat, mloc, send1, recv1, send2, recv2):
        x, y, z = _me()
        me = 4 * x + 2 * y + z
        cv = c_ref[...]
        cact[...] = cv * _sig(cv)
        call_ref[me] = cact[...]
        sends = []
        for k in range(1, N_DEV):
            dev, _ = _peer(k)
            cp = _remote(cact, call_ref.at[me], send1.at[k - 1], recv1.at[k - 1], dev)
            cp.start()
            sends.append(cp)
        for k in range(1, N_DEV):
            dev, blk = _peer(k)
            _remote(cact, call_ref.at[blk], send1.at[k - 1], recv1.at[k - 1], dev).wait_recv()
        cmat[...] = jnp.zeros_like(cmat)
        for b in range(N_DEV):
            cmat[b:b + 1, :] = call_ref[b]
        m = jnp.dot(cmat[...].astype(BF16), w_ref[...].astype(BF16), preferred_element_type=F32) + b_ref[...]
        for b in range(N_DEV):
            mloc[b] = m[b:b + 1, :]
        mod_ref[me] = mloc[me]
        for k in range(1, N_DEV):
            dev, blk = _peer(k)
            cp = _remote(mloc.at[blk], mod_ref.at[me], send2.at[k - 1], recv2.at[k - 1], dev)
            cp.start()
            sends.append(cp)
        for k in range(1, N_DEV):
            dev, blk = _peer(k)
            _remote(mloc.at[me], mod_ref.at[blk], send2.at[k - 1], recv2.at[k - 1], dev).wait_recv()
        for cp in sends:
            cp.wait_send()

    return _pc(
        body, "ada_fwd",
        in_specs=[VMEM_SPEC] * 3, out_specs=[VMEM_SPEC] * 2,
        out_shape=[jax.ShapeDtypeStruct((N_DEV, 1, ncol), F32), jax.ShapeDtypeStruct((N_DEV, 1, D_MODEL), F32)],
        scratch_shapes=[pltpu.VMEM((1, D_MODEL), F32), pltpu.VMEM((16, D_MODEL), F32),
                        pltpu.VMEM((N_DEV, 1, ncol), F32)] + [pltpu.SemaphoreType.DMA((N_DEV - 1,))] * 4,
        compiler_params=pltpu.CompilerParams(has_side_effects=True, vmem_limit_bytes=40 * MIB),
    )(c, w_ada, b_ada_cols)


def _small_gather(vec):
    n = vec.shape[1]

    def body(v_ref, all_ref, sum_ref, send, recv):
        x, y, z = _me()
        me = 4 * x + 2 * y + z
        all_ref[me] = v_ref[...]
        sends = []
        for k in range(1, N_DEV):
            dev, _ = _peer(k)
            cp = _remote(v_ref, all_ref.at[me], send.at[k - 1], recv.at[k - 1], dev)
            cp.start()
            sends.append(cp)
        for k in range(1, N_DEV):
            dev, blk = _peer(k)
            _remote(v_ref, all_ref.at[blk], send.at[k - 1], recv.at[k - 1], dev).wait_recv()
        acc = all_ref[0]
        for d in range(1, N_DEV):
            acc = acc + all_ref[d]
        sum_ref[...] = acc
        for cp in sends:
            cp.wait_send()

    return _pc(
        body, "small_gather",
        in_specs=[VMEM_SPEC], out_specs=[VMEM_SPEC] * 2,
        out_shape=[jax.ShapeDtypeStruct((N_DEV, 1, n), F32), jax.ShapeDtypeStruct((1, n), F32)],
        scratch_shapes=[pltpu.SemaphoreType.DMA((N_DEV - 1,))] * 2,
        compiler_params=pltpu.CompilerParams(has_side_effects=True),
    )(vec)


def _local_step(x, target, mod, wt_in_loc, w_out_loc, w_pw_loc, conv_w_loc, rel_bias, sinks, conv_b, conv_ln_g,
                conv_ln_b, b_pw, ln_g, ln_b):
    bmap = _bucket_map()
    bias = _bias_table(rel_bias, bmap)
    (wt_in,) = _all_gather_two_level([wt_in_loc], "gather_w_in")
    (h, qkv, g_attn, glu, g_conv), (w_out, w_pw, conv_w_blocks) = _inproj(
        x, mod, wt_in, (w_out_loc, w_pw_loc, conv_w_loc))
    conv_w = conv_w_blocks.reshape(N_DEV, CONV_ROWS, 128).transpose(1, 0, 2).reshape(CONV_ROWS, D_CONV)
    ycat, a_out, u1, u3, p_out = _mixer_fwd(qkv, g_attn, glu, g_conv, bias, sinks, conv_w, conv_b, conv_ln_g,
                                            conv_ln_b, w_pw, b_pw)
    dz, dy, dycat, loss, g_ln_g, g_ln_b, dgate = _outproj_ln(ycat, w_out, x, target, mod, ln_g, ln_b)
    gw_out = _matmul_tn([ycat], dy, "grad_w_out")
    dp, dgc, du1, g_bpw, g_clg, g_clb, g_cb = _conv_bwd_a(dycat, g_conv, p_out, u1, conv_ln_g, conv_ln_b, w_pw)
    gw_pw = _matmul_tn([u3], dp, "grad_w_pw")
    dab, g_cw = _conv_bwd_b(du1, glu, conv_w)
    (dqkv, dga, dbias, dsink), (r_out, r_pw) = _attn_bwd(qkv, g_attn, a_out, dycat, bias, sinks,
                                                         scatter=(gw_out, gw_pw))
    g_rb = _relbias_grad(dbias, bmap)
    gwt_in = _matmul_tn([dqkv, dga, dab, dgc], h, "grad_w_in")
    g_cw_blocks = g_cw.reshape(CONV_ROWS, N_DEV, 128).transpose(1, 0, 2).reshape(N_DEV * CONV_ROWS, 128)
    (grad_x, dshift, dscale), (r_in, r_cw) = _dh_kernel([dqkv, dga, dab, dgc], wt_in, dz, x, mod,
                                                        scatter=(gwt_in, g_cw_blocks))
    dmod = jnp.concatenate([dshift, dscale, dgate], axis=1)
    small = dict(dmod=dmod, b_pw=g_bpw, conv_ln_g=g_clg, conv_ln_b=g_clb, conv_b=g_cb, ln_g=g_ln_g, ln_b=g_ln_b,
                 rel_bias=g_rb[:, :N_BUCKETS].reshape(1, N_BUCKETS * N_Q_HEADS),
                 sinks=dsink[0:1, :], loss=loss[0:1, :])
    return grad_x, r_in, r_out, r_pw, r_cw, small


SMALL_FIELDS = (("dmod", 3 * D_MODEL), ("b_pw", D_CONV), ("conv_ln_g", D_CONV), ("conv_ln_b", D_CONV),
                ("conv_b", D_CONV), ("ln_g", D_MODEL), ("ln_b", D_MODEL), ("rel_bias", N_BUCKETS * N_Q_HEADS),
                ("sinks", 128), ("loss", 128))


def _pack(fields):
    parts = []
    for name, width in SMALL_FIELDS:
        v = fields[name].reshape(1, -1).astype(F32)
        if v.shape[1] < width:
            v = jnp.pad(v, ((0, 0), (0, width - v.shape[1])))
        parts.append(v)
    return jnp.concatenate(parts, axis=1)


def _unpack(vec):
    out, off = {}, 0
    for name, width in SMALL_FIELDS:
        out[name] = vec[:, off:off + width]
        off += width
    return out


def kernel(x, c, w_ada, b_ada, w_in, rel_bias, sinks, conv_w, conv_b, conv_ln_g, conv_ln_b, w_pw, b_pw, w_out, ln_g, ln_b, loss_target, m_w_ada, m_b_ada, m_w_in, m_rel_bias, m_sinks, m_conv_w, m_conv_b, m_conv_ln_g, m_conv_ln_b, m_w_pw, m_b_pw, m_w_out, m_ln_g, m_ln_b, v_w_ada, v_b_ada, v_w_in, v_rel_bias, v_sinks, v_conv_w, v_conv_b, v_conv_ln_g, v_conv_ln_b, v_w_pw, v_b_pw, v_w_out, v_ln_g, v_ln_b):
    xi, yi, ci = _me()
    me = 4 * xi + 2 * yi + ci
    ncol = w_ada.shape[2]

    wt_in_loc = w_in[0].T.astype(BF16)
    conv_w_loc = jnp.pad(conv_w[0], ((0, CONV_ROWS - CONV_WIDTH), (0, 0)))

    b_ada_cols = lax.dynamic_slice(b_ada, (0, me * ncol), (1, ncol))
    mod_blocks, c_all = _ada_fwd(c, w_ada[0], b_ada_cols)
    mod = mod_blocks.reshape(1, 3 * D_MODEL)

    grad_x, r_in, r_out, r_pw, r_cw, small = _local_step(
        x[0], loss_target[0], mod, wt_in_loc, w_out[0].astype(BF16), w_pw[0].astype(BF16), conv_w_loc, rel_bias,
        sinks, conv_b, conv_ln_g, conv_ln_b, b_pw, ln_g, ln_b)

    gathered, summed = _small_gather(_pack(small))
    tot = _unpack(summed)
    dmod_all = gathered[:, 0, :3 * D_MODEL]
    loss = tot["loss"][0, 0]

    ct = jnp.zeros((D_MODEL, 128), BF16).at[:, :N_DEV].set(c_all[:, 0, :].T.astype(BF16))
    dm = jnp.zeros((128, ncol), BF16).at[:N_DEV, :].set(
        lax.dynamic_slice(dmod_all, (0, me * ncol), (N_DEV, ncol)).astype(BF16))
    g_w_ada = _small_mm(ct, dm, "grad_w_ada")

    g_conv_w = _sum8(r_cw.reshape(N_DEV, CONV_ROWS, 128), "sum_conv_w")[:CONV_WIDTH]

    grads = {"w_ada": g_w_ada, "conv_w": g_conv_w,
             "w_pw": r_pw.reshape(N_DEV, D_CONV // N_DEV, D_CONV),
             "w_out": r_out.reshape(N_DEV, D_MODEL // N_DEV, D_MODEL)}
    params = {"w_ada": (w_ada, m_w_ada, v_w_ada), "conv_w": (conv_w, m_conv_w, v_conv_w),
              "w_pw": (w_pw, m_w_pw, v_w_pw), "w_out": (w_out, m_w_out, v_w_out)}
    res = {}
    for name, g in grads.items():
        w_, m_, v_ = params[name]
        res[name] = tuple(a[None] for a in _adamw(w_[0], g, m_[0], v_[0], "adamw_" + name))
    upd = _adamw(w_in[0].T, r_in.reshape(N_DEV, D_IN // N_DEV, D_MODEL), m_w_in[0].T, v_w_in[0].T, "adamw_w_in")
    res["w_in"] = tuple(a.T[None] for a in upd)

    small_params = {"b_ada": (b_ada, m_b_ada, v_b_ada), "b_pw": (b_pw, m_b_pw, v_b_pw),
                    "conv_ln_g": (conv_ln_g, m_conv_ln_g, v_conv_ln_g),
                    "conv_ln_b": (conv_ln_b, m_conv_ln_b, v_conv_ln_b), "conv_b": (conv_b, m_conv_b, v_conv_b),
                    "ln_g": (ln_g, m_ln_g, v_ln_g), "ln_b": (ln_b, m_ln_b, v_ln_b),
                    "rel_bias": (rel_bias, m_rel_bias, v_rel_bias), "sinks": (sinks, m_sinks, v_sinks)}
    to_small = lambda n, a: a.T if n == "rel_bias" else a
    key_of = {"b_ada": "dmod"}
    packs = []
    for j in range(3):
        fields = {key_of.get(n, n): to_small(n, t[j]) for n, t in small_params.items()}
        fields["loss"] = jnp.zeros((1, 1), F32)
        packs.append(_pack(fields))
    gsum = summed
    _, d_s, nm_s, nv_s = _adamw(packs[0], gsum, packs[1], packs[2], "adamw_small")
    outs_small = [_unpack(a) for a in (gsum, d_s, nm_s, nv_s)]
    for n, t in small_params.items():
        shape = t[0].shape
        vals = []
        for o in outs_small:
            a = o[key_of.get(n, n)]
            if n == "rel_bias":
                a = a.reshape(N_Q_HEADS, N_BUCKETS).T
            else:
                a = a[:, :shape[1]].reshape(shape)
            vals.append(a)
        res[n] = tuple(vals)

    order = ["w_ada", "b_ada", "w_in", "rel_bias", "sinks", "conv_w", "conv_b", "conv_ln_g", "conv_ln_b", "w_pw",
             "b_pw", "w_out", "ln_g", "ln_b"]
    out = [loss, grad_x[None]]
    for j in range(4):
        out += [res[n][j] for n in order]
    return tuple(out)
```

```python
import math

import jax
import jax.numpy as jnp
from jax import lax
from jax.experimental import pallas as pl
from jax.experimental.pallas import tpu as pltpu

F32, BF16, I32 = jnp.float32, jnp.bfloat16, jnp.int32

D_MODEL = 2048
D_ATTN = 1024
D_CONV = 1024
D_KV = 256
HEAD_DIM = 64
N_Q_HEADS = 16
N_KV_HEADS = 4
GQA = 4
BLOCK = 128
CONV_WIDTH = 31
CONV_ROWS = 32
HALO = 32
N_BUCKETS = 32
MAX_DISTANCE = 128
LN_EPS = 1e-5
ALPHA = 2.0 ** 0.25
D_IN = 5632
N_DEV = 8
NEG = -1e30

ADAM_LR, ADAM_B1, ADAM_B2, ADAM_EPS, ADAM_WD, ADAM_STEP = 0.001, 0.9, 0.999, 1e-08, 0.01, 10

NT_DIMS = (((1,), (1,)), ((), ()))
TN_DIMS = (((0,), (0,)), ((), ()))
MIB = 1 << 20


def _pc(body, name, **kw):
    return pl.pallas_call(body, name=name, **kw)


def _cp(vmem_mib=None, sem=None):
    kw = {}
    if vmem_mib is not None:
        kw["vmem_limit_bytes"] = vmem_mib * MIB
    if sem is not None:
        kw["dimension_semantics"] = sem
    return pltpu.CompilerParams(**kw)


def _sig(x):
    return 1.0 / (1.0 + jnp.exp(-x))


def _dsilu(x, s):
    return s * (1.0 + x * (1.0 - s))


def _me():
    return lax.axis_index("x"), lax.axis_index("y"), lax.axis_index("c")


def _peer(k):
    x, y, c = _me()
    px = 1 - x if k & 4 else x
    py = 1 - y if k & 2 else y
    pc = 1 - c if k & 1 else c
    return (px, py, pc), 4 * px + 2 * py + pc


def _remote(src, dst, send_sem, recv_sem, dev):
    return pltpu.make_async_remote_copy(src_ref=src, dst_ref=dst, send_sem=send_sem, recv_sem=recv_sem,
                                        device_id=dev, device_id_type=pl.DeviceIdType.MESH)


HBM_SPEC = pl.BlockSpec(memory_space=pl.ANY)
VMEM_SPEC = pl.BlockSpec(memory_space=pltpu.VMEM)


def _comm_scratch(nt):
    return [pltpu.SemaphoreType.DMA((nt, N_DEV - 1)), pltpu.SemaphoreType.DMA((nt, N_DEV - 1)),
            pltpu.SemaphoreType.DMA((nt,))]


def _gather_shapes(shards):
    return [jax.ShapeDtypeStruct((N_DEV * s.shape[0], s.shape[1]), s.dtype) for s in shards]


def _block(ref, blk, rows):
    return ref.at[pl.ds(pl.multiple_of(blk * rows, 8), rows), :]


def _gather_copies(src, dst, sems):
    send_sems, recv_sems, local_sems = sems
    x, y, c = _me()
    me = 4 * x + 2 * y + c
    nt = len(src)
    local = [pltpu.make_async_copy(src[t], _block(dst[t], me, src[t].shape[0]), local_sems.at[t]) for t in range(nt)]
    sends, arrivals = [], []
    for k in range(1, N_DEV):
        dev, blk = _peer(k)
        for t in range(nt):
            r = src[t].shape[0]
            pair = (send_sems.at[t, k - 1], recv_sems.at[t, k - 1], dev)
            sends.append(_remote(src[t], _block(dst[t], me, r), *pair))
            arrivals.append(_remote(src[t], _block(dst[t], blk, r), *pair))
    return local, sends, arrivals


def _scatter_copies(src, dst, sems):
    send_sems, recv_sems, local_sems = sems
    x, y, c = _me()
    me = 4 * x + 2 * y + c
    nt = len(src)
    rows = [s.shape[0] // N_DEV for s in src]
    local = [pltpu.make_async_copy(_block(src[t], me, rows[t]), _block(dst[t], me, rows[t]), local_sems.at[t])
             for t in range(nt)]
    sends, arrivals = [], []
    for k in range(1, N_DEV):
        dev, blk = _peer(k)
        for t in range(nt):
            pair = (send_sems.at[t, k - 1], recv_sems.at[t, k - 1], dev)
            sends.append(_remote(_block(src[t], blk, rows[t]), _block(dst[t], me, rows[t]), *pair))
            arrivals.append(_remote(_block(src[t], me, rows[t]), _block(dst[t], blk, rows[t]), *pair))
    return local, sends, arrivals


def _comm_start(cps):
    local, sends, _ = cps
    for cp in local + sends:
        cp.start()


def _comm_wait(cps):
    local, sends, arrivals = cps
    for cp in arrivals:
        cp.wait_recv()
    for cp in sends:
        cp.wait_send()
    for cp in local:
        cp.wait()


IN_PIECES = ((0, 1536, BF16), (1536, 1024, F32), (2560, 2048, F32), (4608, 1024, F32))


def _inproj(x, mod, wt, gather):
    S = x.shape[0]
    tm = min(256, S)
    ni = S // tm
    ng = len(gather)
    npc = len(IN_PIECES)

    def body(*refs):
        x_ref, sh_ref, sc_ref, w_ref = refs[:4]
        outs = refs[4 + ng:]
        h_ref, piece_refs = outs[0], outs[1:1 + npc]
        i = pl.program_id(0)
        if ng:
            cps = _gather_copies(refs[4:4 + ng], outs[1 + npc:1 + npc + ng], outs[1 + npc + ng:])

            @pl.when(i == 0)
            def _():
                _comm_start(cps)

        h = (x_ref[...] * (1.0 + sc_ref[...]) + sh_ref[...]).astype(BF16)
        h_ref[...] = h
        for (r0, n, dt), o_ref in zip(IN_PIECES, piece_refs):
            o_ref[...] = lax.dot_general(h, w_ref[r0:r0 + n, :], NT_DIMS, preferred_element_type=F32).astype(dt)
        if ng:
            @pl.when(i == ni - 1)
            def _():
                _comm_wait(cps)

    tile = lambda n: pl.BlockSpec((tm, n), lambda i: (i, 0))
    res = _pc(
        body, "inproj", grid=(ni,),
        in_specs=[tile(D_MODEL),
                  pl.BlockSpec((1, D_MODEL), lambda i: (0, 0)),
                  pl.BlockSpec((1, D_MODEL), lambda i: (0, 1)),
                  pl.BlockSpec((D_IN, D_MODEL), lambda i: (0, 0), pipeline_mode=pl.Buffered(1))] + [HBM_SPEC] * ng,
        out_specs=[tile(D_MODEL)] + [tile(n) for _, n, _ in IN_PIECES] + [HBM_SPEC] * ng,
        out_shape=[jax.ShapeDtypeStruct((S, D_MODEL), BF16)]
        + [jax.ShapeDtypeStruct((S, n), dt) for _, n, dt in IN_PIECES] + _gather_shapes(gather),
        scratch_shapes=_comm_scratch(ng) if ng else [],
        compiler_params=_cp(56, ("arbitrary",)),
    )(x, mod, mod, wt, *gather)
    return res[:1 + npc], res[1 + npc:]


def _matmul_tn(segs, b, name):
    S, N = b.shape
    tm, ts = 512, min(2048, S)
    ns = S // ts
    bounds = []
    t0 = 0
    for a in segs:
        assert a.shape[1] % tm == 0
        bounds.append((t0, a.shape[1] // tm))
        t0 += a.shape[1] // tm
    nseg = len(segs)

    def body(*refs):
        seg_refs, b_ref = refs[:nseg], refs[nseg]
        o_ref, acc = refs[-2], refs[-1]
        t, s = pl.program_id(0), pl.program_id(1)

        @pl.when(s == 0)
        def _():
            acc[...] = jnp.zeros_like(acc)

        for (s0, sn), r in zip(bounds, seg_refs):
            @pl.when((t >= s0) & (t < s0 + sn))
            def _(r=r):
                acc[...] += lax.dot_general(r[...], b_ref[...], TN_DIMS, preferred_element_type=F32)

        @pl.when(s == ns - 1)
        def _():
            o_ref[...] = acc[...].astype(BF16)

    def seg_spec(s0, sn):
        def index(t, s):
            mine = (t >= s0) & (t < s0 + sn)
            return jnp.where(mine, s, 0), jnp.clip(t - s0, 0, sn - 1)
        return pl.BlockSpec((ts, tm), index)

    return _pc(
        body, name, grid=(t0, ns),
        in_specs=[seg_spec(*bd) for bd in bounds] + [pl.BlockSpec((ts, N), lambda t, s: (s, 0))],
        out_specs=pl.BlockSpec((tm, N), lambda t, s: (t, 0)),
        out_shape=jax.ShapeDtypeStruct((t0 * tm, N), BF16),
        scratch_shapes=[pltpu.VMEM((tm, N), F32)],
        compiler_params=_cp(48, ("parallel", "arbitrary")),
    )(*segs, b)


def _bucket_map():
    qi = jnp.arange(BLOCK, dtype=I32)[:, None]
    kj = jnp.arange(2 * BLOCK, dtype=I32)[None, :]
    dist = qi + BLOCK - kj
    in_window = (dist >= 0) & (dist < BLOCK)
    d0 = jnp.maximum(dist, 0)
    max_exact = N_BUCKETS // 2
    d = jnp.maximum(d0, 1).astype(F32)
    large = max_exact + (jnp.log(d / max_exact) / math.log(MAX_DISTANCE / max_exact)
                         * (N_BUCKETS - max_exact)).astype(I32)
    large = jnp.minimum(large, N_BUCKETS - 1)
    bucket = jnp.where(d0 < max_exact, d0, large)
    return jnp.where(in_window, bucket, -1).astype(I32)


def _bias_table(rel_bias, bmap):
    def body(rb_ref, bm_ref, o_ref):
        bm = bm_ref[...]
        for h in range(N_Q_HEADS):
            acc = jnp.full((BLOCK, 2 * BLOCK), NEG, F32)
            for b in range(N_BUCKETS):
                acc = jnp.where(bm == b, rb_ref[b, h], acc)
            o_ref[h] = acc

    return _pc(
        body, "bias_table",
        in_specs=[pl.BlockSpec(memory_space=pltpu.SMEM), VMEM_SPEC],
        out_specs=VMEM_SPEC,
        out_shape=jax.ShapeDtypeStruct((N_Q_HEADS, BLOCK, 2 * BLOCK), F32),
    )(rel_bias, bmap)


def _bias_spec():
    return pl.BlockSpec((N_Q_HEADS, BLOCK, 2 * BLOCK), lambda n: (0, 0, 0))


def _relbias_grad(dbias, bmap):
    def body(db_ref, bm_ref, o_ref):
        bm = bm_ref[...]
        x = db_ref[0]
        lane = lax.broadcasted_iota(I32, (1, 128), 1)
        row = jnp.zeros((1, 128), F32)
        for b in range(N_BUCKETS):
            row = jnp.where(lane == b, jnp.sum(jnp.where(bm == b, x, 0.0)), row)
        o_ref[0] = jnp.broadcast_to(row, (8, 128))

    out = _pc(
        body, "relbias_grad", grid=(N_Q_HEADS,),
        in_specs=[pl.BlockSpec((1, BLOCK, 2 * BLOCK), lambda h: (h, 0, 0)),
                  pl.BlockSpec((BLOCK, 2 * BLOCK), lambda h: (0, 0))],
        out_specs=pl.BlockSpec((1, 8, 128), lambda h: (h, 0, 0)),
        out_shape=jax.ShapeDtypeStruct((N_Q_HEADS, 8, 128), F32),
    )(dbias, bmap)
    return out[:, 0, :]


Q_SCALE = HEAD_DIM ** -0.5
assert math.frexp(Q_SCALE)[0] == 0.5


def _stack_heads(ref, hk, rows=slice(None)):
    return jnp.concatenate([ref[rows, pl.ds(256 * hk + 64 * g, 64)] for g in range(GQA)], axis=0) * Q_SCALE


def _sink_col(sink_ref, hk):
    row = lax.broadcasted_iota(I32, (GQA * BLOCK, 1), 0)
    s = jnp.full((GQA * BLOCK, 1), sink_ref[0, 4 * hk + 3], F32)
    for g in (2, 1, 0):
        s = jnp.where(row < (g + 1) * BLOCK, sink_ref[0, 4 * hk + g], s)
    return s


def _attn_probs(q4, kw, bias_ref, sink_ref, hk, first_mask):
    s = lax.dot_general(q4, kw, NT_DIMS, preferred_element_type=F32)
    s = s + bias_ref[4 * hk:4 * hk + 4].reshape(GQA * BLOCK, 2 * BLOCK)
    if first_mask is not None:
        s = jnp.where(first_mask, s, NEG)
    sink = _sink_col(sink_ref, hk)
    m = jnp.maximum(jnp.max(s, axis=1, keepdims=True), sink)
    e = jnp.exp(s - m)
    es = jnp.exp(sink - m)
    inv = 1.0 / (jnp.sum(e, axis=1, keepdims=True) + es)
    return e * inv, es * inv


def _attn_bwd(qkv, g_attn, a_out, dycat, bias, sinks, scatter=()):
    S = qkv.shape[0]
    nb = S // BLOCK
    R = GQA * BLOCK
    ns = len(scatter)

    def body(*refs):
        (q_ref, kc_ref, kp_ref, vc_ref, vp_ref, g_ref, a_ref, dy_ref, bias_ref, sink_ref) = refs[:10]
        dqkv_ref, dg_ref, dbias_ref, dsink_ref = refs[10 + ns:14 + ns]
        dq_scr, dq_new, dk_scr, dv_scr, dkw_scr, dvw_scr, ds_scr = refs[14 + 2 * ns:21 + 2 * ns]
        n = pl.program_id(0)
        if ns:
            cps = _scatter_copies(refs[10:10 + ns], refs[14 + ns:14 + 2 * ns], refs[21 + 2 * ns:])

            @pl.when(n == 0)
            def _():
                _comm_start(cps)

        @pl.when(n == 0)
        def _():
            dbias_ref[...] = jnp.zeros_like(dbias_ref)
            ds_scr[...] = jnp.zeros_like(ds_scr)
            dq_scr[...] = jnp.zeros_like(dq_scr)
            dk_scr[...] = jnp.zeros_like(dk_scr)
            dv_scr[...] = jnp.zeros_like(dv_scr)

        @pl.when(n < nb)
        def _():
            kj = lax.broadcasted_iota(I32, (R, 2 * BLOCK), 1)
            first_mask = (n > 0) | (kj >= BLOCK)
            for hk in range(N_KV_HEADS):
                q4 = _stack_heads(q_ref, hk)
                ks = pl.ds(64 * hk, 64)
                kw = jnp.concatenate([kp_ref[:, ks], kc_ref[:, ks]], axis=0)
                vw = jnp.concatenate([vp_ref[:, ks], vc_ref[:, ks]], axis=0)
                p, psink = _attn_probs(q4, kw, bias_ref, sink_ref, hk, first_mask)
                da_parts, a_parts = [], []
                for g in range(GQA):
                    sl = pl.ds(256 * hk + 64 * g, 64)
                    gg = g_ref[:, sl]
                    sg = _sig(gg)
                    dyg = dy_ref[:, sl]
                    ag = a_ref[:, sl]
                    da_parts.append(dyg * (gg * sg))
                    a_parts.append(ag)
                    dg_ref[:, sl] = (dyg * ag * _dsilu(gg, sg)).astype(BF16)
                da4 = jnp.concatenate(da_parts, axis=0)
                a4 = jnp.concatenate(a_parts, axis=0)
                delta = jnp.sum(da4 * a4, axis=1, keepdims=True)
                da4b = da4.astype(BF16)
                dp = lax.dot_general(da4b, vw, NT_DIMS, preferred_element_type=F32)
                ds = p * (dp - delta)
                ds_scr[hk] += -psink * delta
                dbias_ref[4 * hk:4 * hk + 4] += ds.reshape(GQA, BLOCK, 2 * BLOCK)
                dsb = ds.astype(BF16)
                dq4 = jnp.dot(dsb, kw, preferred_element_type=F32) * Q_SCALE
                for g in range(GQA):
                    dq_new[:, pl.ds(256 * hk + 64 * g, 64)] = dq4[BLOCK * g:BLOCK * (g + 1)].astype(BF16)
                dkw_scr[:, ks] = lax.dot_general(dsb, q4, TN_DIMS, preferred_element_type=F32)
                dvw_scr[:, ks] = lax.dot_general(p.astype(BF16), da4b, TN_DIMS, preferred_element_type=F32)

        @pl.when(n == nb)
        def _():
            dkw_scr[0:BLOCK, :] = jnp.zeros((BLOCK, D_KV), F32)
            dvw_scr[0:BLOCK, :] = jnp.zeros((BLOCK, D_KV), F32)

        dqkv_ref[:, 0:D_ATTN] = dq_scr[...]
        dqkv_ref[:, D_ATTN:D_ATTN + D_KV] = (dk_scr[...] + dkw_scr[0:BLOCK, :]).astype(BF16)
        dqkv_ref[:, D_ATTN + D_KV:D_ATTN + 2 * D_KV] = (dv_scr[...] + dvw_scr[0:BLOCK, :]).astype(BF16)
        dq_scr[...] = dq_new[...]
        dk_scr[...] = dkw_scr[BLOCK:2 * BLOCK, :]
        dv_scr[...] = dvw_scr[BLOCK:2 * BLOCK, :]

        @pl.when(n == nb)
        def _():
            lane = lax.broadcasted_iota(I32, (1, 128), 1)
            row = jnp.zeros((1, 128), F32)
            for hk in range(N_KV_HEADS):
                col = ds_scr[hk]
                for g in range(GQA):
                    row = jnp.where(lane == 4 * hk + g, jnp.sum(col[BLOCK * g:BLOCK * (g + 1)]), row)
            dsink_ref[...] = jnp.broadcast_to(row, (8, 128))
            if ns:
                _comm_wait(cps)

    cur = lambda n: jnp.minimum(n, nb - 1)
    prev = lambda n: jnp.clip(n - 1, 0, nb - 1)
    res = _pc(
        body, "attn_bwd", grid=(nb + 1,),
        in_specs=[pl.BlockSpec((BLOCK, D_ATTN), lambda n: (cur(n), 0)),
                  pl.BlockSpec((BLOCK, D_KV), lambda n: (cur(n), 4)),
                  pl.BlockSpec((BLOCK, D_KV), lambda n: (prev(n), 4)),
                  pl.BlockSpec((BLOCK, D_KV), lambda n: (cur(n), 5)),
                  pl.BlockSpec((BLOCK, D_KV), lambda n: (prev(n), 5)),
                  pl.BlockSpec((BLOCK, D_ATTN), lambda n: (cur(n), 0)),
                  pl.BlockSpec((BLOCK, D_ATTN), lambda n: (cur(n), 0)),
                  pl.BlockSpec((BLOCK, D_ATTN), lambda n: (cur(n), 0)),
                  _bias_spec(),
                  pl.BlockSpec(memory_space=pltpu.SMEM)] + [HBM_SPEC] * ns,
        out_specs=[pl.BlockSpec((BLOCK, D_ATTN + 2 * D_KV), lambda n: (prev(n), 0)),
                   pl.BlockSpec((BLOCK, D_ATTN), lambda n: (cur(n), 0)),
                   pl.BlockSpec((N_Q_HEADS, BLOCK, 2 * BLOCK), lambda n: (0, 0, 0)),
                   pl.BlockSpec((8, 128), lambda n: (0, 0))] + [HBM_SPEC] * ns,
        out_shape=[jax.ShapeDtypeStruct((S, D_ATTN + 2 * D_KV), BF16),
                   jax.ShapeDtypeStruct((S, D_ATTN), BF16),
                   jax.ShapeDtypeStruct((N_Q_HEADS, BLOCK, 2 * BLOCK), F32),
                   jax.ShapeDtypeStruct((8, 128), F32)] + [jax.ShapeDtypeStruct(f.shape, f.dtype) for f in scatter],
        scratch_shapes=[pltpu.VMEM((BLOCK, D_ATTN), BF16), pltpu.VMEM((BLOCK, D_ATTN), BF16),
                        pltpu.VMEM((BLOCK, D_KV), F32), pltpu.VMEM((BLOCK, D_KV), F32),
                        pltpu.VMEM((2 * BLOCK, D_KV), F32), pltpu.VMEM((2 * BLOCK, D_KV), F32),
                        pltpu.VMEM((N_KV_HEADS, R, 1), F32)] + (_comm_scratch(ns) if ns else []),
        compiler_params=_cp(48, ("arbitrary",)),
    )(qkv, qkv, qkv, qkv, qkv, g_attn, a_out, dycat, bias, sinks, *scatter)
    return res[:4], res[4:]


def _conv_tile(S):
    return min(256, S)


def _shifted(win, s):
    return win if s == 0 else pltpu.roll(win, win.shape[0] - s, axis=0)


def _conv_taps(win, cw_ref, lanes, rows):
    acc = jnp.zeros((rows, win.shape[1]), F32)
    for s in range(8):
        ws = _shifted(win, s)
        for aa in range(5):
            j = 8 * aa + s - 2
            if 0 <= j < CONV_WIDTH:
                acc = acc + ws[8 * aa:8 * aa + rows] * cw_ref[j:j + 1, lanes]
    return acc


def _mixer_fwd(qkv, g_attn, glu, g_conv, bias, sinks, conv_w, conv_b, ln_g, ln_b, w_pw, b_pw):
    S = qkv.shape[0]
    RB = 2
    TR = RB * BLOCK
    nb = S // TR
    hb = TR // HALO
    LG = D_CONV // (RB * N_KV_HEADS)

    def body(q_ref, kc_ref, kp_ref, vc_ref, vp_ref, g_ref, bias_ref, sink_ref,
             a_in, ah_in, b_in, bh_in, gc_ref, cw_ref, cb_ref, lg_ref, lb_ref, wpw_ref, bpw_ref,
             y_ref, a_ref, u1_ref, u3_ref, p_ref, win):
        n = pl.program_id(0)
        kj = lax.broadcasted_iota(I32, (GQA * BLOCK, 2 * BLOCK), 1)
        first_mask = (n > 0) | (kj >= BLOCK)
        win[0:HALO, :] = jnp.where(n > 0, ah_in[...] * _sig(bh_in[...]), 0.0)
        win[HALO:HALO + TR, :] = a_in[...] * _sig(b_in[...])
        piece = 0
        for s in range(RB):
            rows = slice(BLOCK * s, BLOCK * (s + 1))
            before = slice(BLOCK * (s - 1), BLOCK * s)
            for hk in range(N_KV_HEADS):
                q4 = _stack_heads(q_ref, hk, rows)
                ks = pl.ds(64 * hk, 64)
                k_prev, v_prev = (kp_ref[:, ks], vp_ref[:, ks]) if s == 0 else (kc_ref[before, ks], vc_ref[before, ks])
                kw = jnp.concatenate([k_prev, kc_ref[rows, ks]], axis=0)
                vw = jnp.concatenate([v_prev, vc_ref[rows, ks]], axis=0)
                p, _ = _attn_probs(q4, kw, bias_ref, sink_ref, hk, first_mask if s == 0 else None)
                o4 = jnp.dot(p.astype(BF16), vw, preferred_element_type=F32)
                for g in range(GQA):
                    sl = pl.ds(256 * hk + 64 * g, 64)
                    og = o4[BLOCK * g:BLOCK * (g + 1)]
                    gg = g_ref[rows, sl]
                    a_ref[rows, sl] = og
                    y_ref[rows, sl] = (og * (gg * _sig(gg))).astype(BF16)
                lanes = pl.ds(LG * piece, LG)
                u1_ref[:, lanes] = _conv_taps(win[:, lanes], cw_ref, lanes, TR) + cb_ref[:, lanes]
                piece += 1
        u1 = u1_ref[...]
        mu = jnp.mean(u1, axis=1, keepdims=True)
        uc = u1 - mu
        rstd = lax.rsqrt(jnp.mean(uc * uc, axis=1, keepdims=True) + LN_EPS)
        u2 = uc * rstd * lg_ref[...] + lb_ref[...]
        u3 = (u2 * _sig(u2)).astype(BF16)
        u3_ref[...] = u3
        pw = jnp.dot(u3, wpw_ref[...], preferred_element_type=F32) + bpw_ref[...]
        p_ref[...] = pw
        gc = gc_ref[...]
        y_ref[:, D_ATTN:] = (pw * (gc * _sig(gc))).astype(BF16)

    prev = lambda n: jnp.maximum(RB * n - 1, 0)
    halo = lambda n: jnp.maximum(n * hb - 1, 0)
    vec = pl.BlockSpec((1, D_CONV), lambda n: (0, 0))
    blk = lambda w, j: pl.BlockSpec((TR, w), lambda n: (n, j))
    return _pc(
        body, "mixer_fwd", grid=(nb,),
        in_specs=[blk(D_ATTN, 0),
                  blk(D_KV, 4), pl.BlockSpec((BLOCK, D_KV), lambda n: (prev(n), 4)),
                  blk(D_KV, 5), pl.BlockSpec((BLOCK, D_KV), lambda n: (prev(n), 5)),
                  blk(D_ATTN, 0),
                  _bias_spec(),
                  pl.BlockSpec(memory_space=pltpu.SMEM),
                  blk(D_CONV, 0), pl.BlockSpec((HALO, D_CONV), lambda n: (halo(n), 0)),
                  blk(D_CONV, 1), pl.BlockSpec((HALO, D_CONV), lambda n: (halo(n), 1)),
                  blk(D_CONV, 0),
                  pl.BlockSpec((CONV_ROWS, D_CONV), lambda n: (0, 0)),
                  vec, vec, vec,
                  pl.BlockSpec((D_CONV, D_CONV), lambda n: (0, 0)),
                  vec],
        out_specs=[blk(2 * D_ATTN, 0), blk(D_ATTN, 0), blk(D_CONV, 0), blk(D_CONV, 0), blk(D_CONV, 0)],
        out_shape=[jax.ShapeDtypeStruct((S, 2 * D_ATTN), BF16),
                   jax.ShapeDtypeStruct((S, D_ATTN), F32),
                   jax.ShapeDtypeStruct((S, D_CONV), F32),
                   jax.ShapeDtypeStruct((S, D_CONV), BF16),
                   jax.ShapeDtypeStruct((S, D_CONV), F32)],
        scratch_shapes=[pltpu.VMEM((TR + HALO, D_CONV), F32)],
        compiler_params=_cp(56, ("parallel",)),
    )(qkv, qkv, qkv, qkv, qkv, g_attn, bias, sinks, glu, glu, glu, glu, g_conv, conv_w, conv_b, ln_g, ln_b, w_pw,
      b_pw)


def _conv_bwd_a(dycat, g_conv, p_out, u1, ln_g, ln_b, w_pw):
    S = u1.shape[0]
    T = _conv_tile(S)

    def body(dy_ref, gc_ref, p_ref, u1_ref, lg_ref, lb_ref, wpw_ref,
             dp_ref, dgc_ref, du1_ref, gbpw_ref, glg_ref, glb_ref, gcb_ref):
        i = pl.program_id(0)

        @pl.when(i == 0)
        def _():
            for r in (gbpw_ref, glg_ref, glb_ref, gcb_ref):
                r[...] = jnp.zeros_like(r)

        dy = dy_ref[...]
        gc = gc_ref[...]
        sg = _sig(gc)
        dp = dy * (gc * sg)
        dgc_ref[...] = (dy * p_ref[...] * _dsilu(gc, sg)).astype(BF16)
        gbpw_ref[...] += jnp.sum(dp, axis=0, keepdims=True)
        dpb = dp.astype(BF16)
        dp_ref[...] = dpb
        du3 = lax.dot_general(dpb, wpw_ref[...], NT_DIMS, preferred_element_type=F32)
        u1 = u1_ref[...]
        mu = jnp.mean(u1, axis=1, keepdims=True)
        uc = u1 - mu
        rstd = lax.rsqrt(jnp.mean(uc * uc, axis=1, keepdims=True) + LN_EPS)
        uh = uc * rstd
        lg = lg_ref[...]
        u2 = uh * lg + lb_ref[...]
        s2 = _sig(u2)
        du2 = du3 * _dsilu(u2, s2)
        glg_ref[...] += jnp.sum(du2 * uh, axis=0, keepdims=True)
        glb_ref[...] += jnp.sum(du2, axis=0, keepdims=True)
        duh = du2 * lg
        du1 = rstd * (duh - jnp.mean(duh, axis=1, keepdims=True) - uh * jnp.mean(duh * uh, axis=1, keepdims=True))
        du1_ref[...] = du1
        gcb_ref[...] += jnp.sum(du1, axis=0, keepdims=True)

    vec = pl.BlockSpec((1, D_CONV), lambda i: (0, 0))
    tile = pl.BlockSpec((T, D_CONV), lambda i: (i, 0))
    vshape = jax.ShapeDtypeStruct((1, D_CONV), F32)
    return _pc(
        body, "conv_bwd_a", grid=(S // T,),
        in_specs=[pl.BlockSpec((T, D_CONV), lambda i: (i, 1)), tile, tile, tile, vec, vec,
                  pl.BlockSpec((D_CONV, D_CONV), lambda i: (0, 0))],
        out_specs=[tile, tile, tile, vec, vec, vec, vec],
        out_shape=[jax.ShapeDtypeStruct((S, D_CONV), BF16), jax.ShapeDtypeStruct((S, D_CONV), BF16),
                   jax.ShapeDtypeStruct((S, D_CONV), F32), vshape, vshape, vshape, vshape],
        compiler_params=_cp(48, ("arbitrary",)),
    )(dycat, g_conv, p_out, u1, ln_g, ln_b, w_pw)


def _conv_bwd_b(du1, glu, conv_w):
    S = du1.shape[0]
    T = _conv_tile(S)
    hb = T // HALO
    nt = S // T
    last_h = S // HALO - 1

    def body(du_ref, dun_ref, a_ref, b_ref, cw_ref, dab_ref, gw_ref):
        i = pl.program_id(0)

        @pl.when(i == 0)
        def _():
            gw_ref[...] = jnp.zeros_like(gw_ref)

        for lg in range(D_CONV // 128):
            lanes = pl.ds(128 * lg, 128)
            a = a_ref[:, lanes]
            sb = _sig(b_ref[:, lanes])
            u0 = a * sb
            du1n = jnp.where(i < nt - 1, dun_ref[:, lanes], 0.0)
            win2 = jnp.concatenate([du_ref[:, lanes], du1n], axis=0)
            acc = jnp.zeros((T, 128), F32)
            for s in range(8):
                w2 = _shifted(win2, s)
                for aa in range(4):
                    j = CONV_WIDTH - 1 - (8 * aa + s)
                    if 0 <= j < CONV_WIDTH:
                        xo = w2[8 * aa:8 * aa + T]
                        acc = acc + xo * cw_ref[j:j + 1, lanes]
                        gw_ref[j:j + 1, lanes] += jnp.sum(xo * u0, axis=0, keepdims=True)
            dab_ref[:, lanes] = (acc * sb).astype(BF16)
            dab_ref[:, pl.ds(D_CONV + 128 * lg, 128)] = (acc * a * sb * (1.0 - sb)).astype(BF16)

    nxt = lambda i: jnp.minimum((i + 1) * hb, last_h)
    return _pc(
        body, "conv_bwd_b", grid=(nt,),
        in_specs=[pl.BlockSpec((T, D_CONV), lambda i: (i, 0)),
                  pl.BlockSpec((HALO, D_CONV), lambda i: (nxt(i), 0)),
                  pl.BlockSpec((T, D_CONV), lambda i: (i, 0)),
                  pl.BlockSpec((T, D_CONV), lambda i: (i, 1)),
                  pl.BlockSpec((CONV_ROWS, D_CONV), lambda i: (0, 0))],
        out_specs=[pl.BlockSpec((T, 2 * D_CONV), lambda i: (i, 0)),
                   pl.BlockSpec((CONV_ROWS, D_CONV), lambda i: (0, 0))],
        out_shape=[jax.ShapeDtypeStruct((S, 2 * D_CONV), BF16),
                   jax.ShapeDtypeStruct((CONV_ROWS, D_CONV), F32)],
        compiler_params=_cp(48, ("arbitrary",)),
    )(du1, du1, glu, glu, conv_w)


def _outproj_ln(ycat, w_out, x, target, mod, ln_g, ln_b):
    S = x.shape[0]
    tm = min(256, S)

    def body(yc_ref, w_ref, x_ref, t_ref, gate_ref, lg_ref, lb_ref,
             dz_ref, dy_ref, dyc_ref, loss_ref, glg_ref, glb_ref, dgate_ref):
        i = pl.program_id(0)

        @pl.when(i == 0)
        def _():
            for r in (loss_ref, glg_ref, glb_ref, dgate_ref):
                r[...] = jnp.zeros_like(r)

        w = w_ref[...]
        y = jnp.dot(yc_ref[...], w, preferred_element_type=F32)
        gate = gate_ref[...]
        z = ALPHA * x_ref[...] + gate * y
        mu = jnp.mean(z, axis=1, keepdims=True)
        zc = z - mu
        rstd = lax.rsqrt(jnp.mean(zc * zc, axis=1, keepdims=True) + LN_EPS)
        zh = zc * rstd
        lg = lg_ref[...]
        err = zh * lg + lb_ref[...] - t_ref[...]
        loss_ref[...] += 0.5 * jnp.sum(jnp.sum(err * err, axis=1, keepdims=True)) / D_MODEL
        dout = err * (1.0 / D_MODEL)
        glg_ref[...] += jnp.sum(dout * zh, axis=0, keepdims=True)
        glb_ref[...] += jnp.sum(dout, axis=0, keepdims=True)
        dzh = dout * lg
        dz = rstd * (dzh - jnp.mean(dzh, axis=1, keepdims=True) - zh * jnp.mean(dzh * zh, axis=1, keepdims=True))
        dz_ref[...] = dz
        dgate_ref[...] += jnp.sum(dz * y, axis=0, keepdims=True)
        dy = (dz * gate).astype(BF16)
        dy_ref[...] = dy
        dyc_ref[...] = lax.dot_general(dy, w, NT_DIMS, preferred_element_type=F32).astype(BF16)

    vec = pl.BlockSpec((1, D_MODEL), lambda i: (0, 0))
    tile = pl.BlockSpec((tm, D_MODEL), lambda i: (i, 0))
    vshape = jax.ShapeDtypeStruct((1, D_MODEL), F32)
    return _pc(
        body, "outproj_ln", grid=(S // tm,),
        in_specs=[tile, pl.BlockSpec((D_MODEL, D_MODEL), lambda i: (0, 0), pipeline_mode=pl.Buffered(1)), tile, tile,
                  pl.BlockSpec((1, D_MODEL), lambda i: (0, 2)), vec, vec],
        out_specs=[tile, tile, tile, pl.BlockSpec((8, 128), lambda i: (0, 0)), vec, vec, vec],
        out_shape=[jax.ShapeDtypeStruct((S, D_MODEL), F32), jax.ShapeDtypeStruct((S, D_MODEL), BF16),
                   jax.ShapeDtypeStruct((S, D_MODEL), BF16), jax.ShapeDtypeStruct((8, 128), F32),
                   vshape, vshape, vshape],
        compiler_params=_cp(56, ("arbitrary",)),
    )(ycat, w_out, x, target, mod, ln_g, ln_b)


def _dh_kernel(segs, wt, dz, x, mod, scatter=()):
    S = x.shape[0]
    tm = min(256, S)
    row0 = [0]
    for a in segs:
        row0.append(row0[-1] + a.shape[1])
    assert row0[-1] == wt.shape[0]
    nseg = len(segs)
    ns = len(scatter)
    ni = S // tm

    def body(*refs):
        seg_refs = refs[:nseg]
        w_hbm, dz_ref, x_ref, sc_ref = refs[nseg:nseg + 4]
        outs = refs[nseg + 4 + ns:]
        gx_ref, dsh_ref, dsc_ref = outs[:3]
        w_ref, w_sems = outs[3 + ns:5 + ns]
        i = pl.program_id(0)
        if ns:
            cps = _scatter_copies(refs[nseg + 4:nseg + 4 + ns], outs[3:3 + ns], outs[5 + ns:])

        def w_load(t):
            rows = slice(row0[t], row0[t + 1])
            return pltpu.make_async_copy(w_hbm.at[rows, :], w_ref.at[rows, :], w_sems.at[t])

        @pl.when(i == 0)
        def _():
            for t in range(nseg):
                w_load(t).start()
            dsh_ref[...] = jnp.zeros_like(dsh_ref)
            dsc_ref[...] = jnp.zeros_like(dsc_ref)
            if ns:
                _comm_start(cps)

        dh = None
        for t in range(nseg):
            @pl.when(i == 0)
            def _(t=t):
                w_load(t).wait()

            part = jnp.dot(seg_refs[t][...], w_ref[row0[t]:row0[t + 1], :], preferred_element_type=F32)
            dh = part if dh is None else dh + part
        gx_ref[...] = ALPHA * dz_ref[...] + dh * (1.0 + sc_ref[...])
        dsh_ref[...] += jnp.sum(dh, axis=0, keepdims=True)
        dsc_ref[...] += jnp.sum(dh * x_ref[...], axis=0, keepdims=True)

        if ns:
            @pl.when(i == ni - 1)
            def _():
                _comm_wait(cps)

    tile = pl.BlockSpec((tm, D_MODEL), lambda i: (i, 0))
    vec = pl.BlockSpec((1, D_MODEL), lambda i: (0, 0))
    vshape = jax.ShapeDtypeStruct((1, D_MODEL), F32)
    res = _pc(
        body, "dh_gradx", grid=(ni,),
        in_specs=[pl.BlockSpec((tm, a.shape[1]), lambda i: (i, 0)) for a in segs] + [
            HBM_SPEC, tile, tile,
            pl.BlockSpec((1, D_MODEL), lambda i: (0, 1))] + [HBM_SPEC] * ns,
        out_specs=[tile, vec, vec] + [HBM_SPEC] * ns,
        out_shape=[jax.ShapeDtypeStruct((S, D_MODEL), F32), vshape, vshape]
        + [jax.ShapeDtypeStruct(f.shape, f.dtype) for f in scatter],
        scratch_shapes=[pltpu.VMEM(wt.shape, BF16), pltpu.SemaphoreType.DMA((nseg,))]
        + (_comm_scratch(ns) if ns else []),
        compiler_params=_cp(58, ("arbitrary",)),
    )(*segs, wt, dz, x, mod, *scatter)
    return res[:3], res[3:]


def _row_tile(rows, cols):
    if rows * cols * 4 <= 2 * MIB or rows % 8:
        return rows
    tr = max(8, (2 * MIB // (cols * 4)) // 8 * 8)
    while rows % tr:
        tr -= 8
    return tr


def _sum8(recv, name):
    _, R, C = recv.shape
    tr = _row_tile(R, C)

    def body(r_ref, o_ref):
        acc = r_ref[0].astype(F32)
        for d in range(1, N_DEV):
            acc = acc + r_ref[d].astype(F32)
        o_ref[...] = acc

    return _pc(
        body, name, grid=(R // tr,),
        in_specs=[pl.BlockSpec((N_DEV, tr, C), lambda i: (0, i, 0))],
        out_specs=pl.BlockSpec((tr, C), lambda i: (i, 0)),
        out_shape=jax.ShapeDtypeStruct((R, C), F32),
        compiler_params=_cp(40, ("parallel",)),
    )(recv)


def _adamw(w, g, m, v, name):
    R, C = w.shape
    tr = _row_tile(R, C)
    parts = g.ndim == 3

    def body(w_ref, g_ref, m_ref, v_ref, *out_refs):
        d_ref, nm_ref, nv_ref = out_refs[-3:]
        if parts:
            g_ = g_ref[0].astype(F32)
            for d in range(1, N_DEV):
                g_ = g_ + g_ref[d].astype(F32)
            out_refs[0][...] = g_
        else:
            g_ = g_ref[...]
        m_ = ADAM_B1 * m_ref[...] + (1.0 - ADAM_B1) * g_
        v_ = ADAM_B2 * v_ref[...] + (1.0 - ADAM_B2) * (g_ * g_)
        m_hat = m_ / (1.0 - ADAM_B1 ** ADAM_STEP)
        v_hat = v_ / (1.0 - ADAM_B2 ** ADAM_STEP)
        d_ref[...] = -ADAM_LR * (m_hat / (jnp.sqrt(v_hat) + ADAM_EPS) + ADAM_WD * w_ref[...])
        nm_ref[...] = m_
        nv_ref[...] = v_

    spec = pl.BlockSpec((tr, C), lambda i: (i, 0))
    gspec = pl.BlockSpec((N_DEV, tr, C), lambda i: (0, i, 0)) if parts else spec
    shape = jax.ShapeDtypeStruct((R, C), F32)
    nout = 4 if parts else 3
    res = _pc(
        body, name, grid=(R // tr,),
        in_specs=[spec, gspec, spec, spec], out_specs=[spec] * nout, out_shape=[shape] * nout,
        compiler_params=_cp(40, ("parallel",)),
    )(w, g, m, v)
    return tuple(res) if parts else (g, *res)


def _small_mm(a, b, name):
    def body(a_ref, b_ref, o_ref):
        o_ref[...] = jnp.dot(a_ref[...], b_ref[...], preferred_element_type=F32)

    return _pc(body, name, out_shape=jax.ShapeDtypeStruct((a.shape[0], b.shape[1]), F32),
               compiler_params=_cp(40))(a, b)


def _all_gather_two_level(shards, name, ada=None):
    nt = len(shards)
    NK = 9
    ada_out, ada_scratch = _ada_shapes(ada[1].shape[1]) if ada else ([], [])
    na = 3 if ada else 0

    def body(*refs):
        src, dst = refs[:nt], refs[nt + na:2 * nt + na]
        send_sems, recv_sems, local_sems = refs[2 * nt + na + len(ada_out):2 * nt + na + len(ada_out) + 3]
        x, y, c = _me()
        me, sibling = (x, y, c), (x, y, 1 - c)
        xn, yn, dg = (1 - x, y), (x, 1 - y), (1 - x, 1 - y)

        def part(t, dev, half=None):
            r = src[t].shape[0]
            blk = 4 * dev[0] + 2 * dev[1] + dev[2]
            if half is None:
                return dst[t].at[pl.ds(pl.multiple_of(blk * r, 16), r), :]
            return dst[t].at[pl.ds(pl.multiple_of(blk * r + half * (r // 2), 16), r // 2), :]

        def copy(t, k, to, rows, from_src=False):
            return _remote(src[t] if from_src else rows, rows, send_sems.at[t, k], recv_sems.at[t, k], to)

        local = [pltpu.make_async_copy(src[t], part(t, me), local_sems.at[t]) for t in range(nt)]
        for cp in local:
            cp.start()
        sends = []
        for t in range(nt):
            sends += [copy(t, 0, sibling, part(t, me), True), copy(t, 1, (*xn, c), part(t, me), True),
                      copy(t, 2, (*yn, c), part(t, me), True)]
        for cp in sends:
            cp.start()
        if ada:
            _ada_exchange(*refs[nt:nt + na], *refs[2 * nt + na:2 * nt + na + len(ada_out)],
                          *refs[2 * nt + na + len(ada_out) + 3:])

        def arrive_then_pass(t, k, rows, passes):
            copy(t, k, me, rows).wait_recv()
            for cp in passes:
                cp.start()
                sends.append(cp)

        for t in range(nt):
            arrive_then_pass(t, 1, part(t, (*xn, c)), [copy(t, 5, sibling, part(t, (*xn, c))),
                                                       copy(t, 3, (*yn, c), part(t, (*xn, c), 0))])
            arrive_then_pass(t, 2, part(t, (*yn, c)), [copy(t, 6, sibling, part(t, (*yn, c))),
                                                       copy(t, 4, (*xn, c), part(t, (*yn, c), 1))])
            arrive_then_pass(t, 3, part(t, (*dg, c), 0), [copy(t, 7, sibling, part(t, (*dg, c), 0))])
            arrive_then_pass(t, 4, part(t, (*dg, c), 1), [copy(t, 8, sibling, part(t, (*dg, c), 1))])
        for t in range(nt):
            copy(t, 0, me, part(t, sibling)).wait_recv()
            copy(t, 5, me, part(t, (*xn, 1 - c))).wait_recv()
            copy(t, 6, me, part(t, (*yn, 1 - c))).wait_recv()
            copy(t, 7, me, part(t, (*dg, 1 - c), 0)).wait_recv()
            copy(t, 8, me, part(t, (*dg, 1 - c), 1)).wait_recv()
        for cp in sends:
            cp.wait_send()
        for cp in local:
            cp.wait()

    res = _pc(
        body, name,
        in_specs=[HBM_SPEC] * nt + [VMEM_SPEC] * na, out_specs=[HBM_SPEC] * nt + [VMEM_SPEC] * len(ada_out),
        out_shape=_gather_shapes(shards) + ada_out,
        scratch_shapes=[pltpu.SemaphoreType.DMA((nt, NK)), pltpu.SemaphoreType.DMA((nt, NK)),
                        pltpu.SemaphoreType.DMA((nt,))] + ada_scratch,
        compiler_params=pltpu.CompilerParams(has_side_effects=True, vmem_limit_bytes=40 * MIB),
    )(*shards, *(ada or ()))
    return res[:nt], res[nt:]


def _ada_shapes(ncol):
    out = [jax.ShapeDtypeStruct((N_DEV, 1, ncol), F32), jax.ShapeDtypeStruct((N_DEV, 1, D_MODEL), F32)]
    scratch = [pltpu.VMEM((1, D_MODEL), F32), pltpu.VMEM((16, D_MODEL), F32),
               pltpu.VMEM((N_DEV, 1, ncol), F32)] + [pltpu.SemaphoreType.DMA((N_DEV - 1,))] * 4
    return out, scratch


def _ada_exchange(c_ref, w_ref, b_ref, mod_ref, call_ref, cact, cmat, mloc, send1, recv1, send2, recv2):
    x, y, z = _me()
    me = 4 * x + 2 * y + z
    cv = c_ref[...]
    cact[...] = cv * _sig(cv)
    call_ref[me] = cact[...]
    sends = []
    for k in range(1, N_DEV):
        dev, _ = _peer(k)
        cp = _remote(cact, call_ref.at[me], send1.at[k - 1], recv1.at[k - 1], dev)
        cp.start()
        sends.append(cp)
    for k in range(1, N_DEV):
        dev, blk = _peer(k)
        _remote(cact, call_ref.at[blk], send1.at[k - 1], recv1.at[k - 1], dev).wait_recv()
    cmat[...] = jnp.zeros_like(cmat)
    for b in range(N_DEV):
        cmat[b:b + 1, :] = call_ref[b]
    m = jnp.dot(cmat[...].astype(BF16), w_ref[...].astype(BF16), preferred_element_type=F32) + b_ref[...]
    for b in range(N_DEV):
        mloc[b] = m[b:b + 1, :]
    mod_ref[me] = mloc[me]
    for k in range(1, N_DEV):
        dev, blk = _peer(k)
        cp = _remote(mloc.at[blk], mod_ref.at[me], send2.at[k - 1], recv2.at[k - 1], dev)
        cp.start()
        sends.append(cp)
    for k in range(1, N_DEV):
        dev, blk = _peer(k)
        _remote(mloc.at[me], mod_ref.at[blk], send2.at[k - 1], recv2.at[k - 1], dev).wait_recv()
    for cp in sends:
        cp.wait_send()


def _small_gather(vec):
    n = vec.shape[1]

    def body(v_ref, all_ref, sum_ref, send, recv):
        x, y, z = _me()
        me = 4 * x + 2 * y + z
        all_ref[me] = v_ref[...]
        sends = []
        for k in range(1, N_DEV):
            dev, _ = _peer(k)
            cp = _remote(v_ref, all_ref.at[me], send.at[k - 1], recv.at[k - 1], dev)
            cp.start()
            sends.append(cp)
        for k in range(1, N_DEV):
            dev, blk = _peer(k)
            _remote(v_ref, all_ref.at[blk], send.at[k - 1], recv.at[k - 1], dev).wait_recv()
        acc = all_ref[0]
        for d in range(1, N_DEV):
            acc = acc + all_ref[d]
        sum_ref[...] = acc
        for cp in sends:
            cp.wait_send()

    return _pc(
        body, "small_gather",
        in_specs=[VMEM_SPEC], out_specs=[VMEM_SPEC] * 2,
        out_shape=[jax.ShapeDtypeStruct((N_DEV, 1, n), F32), jax.ShapeDtypeStruct((1, n), F32)],
        scratch_shapes=[pltpu.SemaphoreType.DMA((N_DEV - 1,))] * 2,
        compiler_params=pltpu.CompilerParams(has_side_effects=True),
    )(vec)


def _local_step(x, target, mod, wt_in, w_out_loc, w_pw_loc, conv_w_loc, rel_bias, sinks, conv_b, conv_ln_g,
                conv_ln_b, b_pw, ln_g, ln_b):
    bmap = _bucket_map()
    bias = _bias_table(rel_bias, bmap)
    (h, qkv, g_attn, glu, g_conv), (w_out, w_pw, conv_w_blocks) = _inproj(
        x, mod, wt_in, (w_out_loc, w_pw_loc, conv_w_loc))
    conv_w = conv_w_blocks.reshape(N_DEV, CONV_ROWS, 128).transpose(1, 0, 2).reshape(CONV_ROWS, D_CONV)
    ycat, a_out, u1, u3, p_out = _mixer_fwd(qkv, g_attn, glu, g_conv, bias, sinks, conv_w, conv_b, conv_ln_g,
                                            conv_ln_b, w_pw, b_pw)
    dz, dy, dycat, loss, g_ln_g, g_ln_b, dgate = _outproj_ln(ycat, w_out, x, target, mod, ln_g, ln_b)
    gw_out = _matmul_tn([ycat], dy, "grad_w_out")
    dp, dgc, du1, g_bpw, g_clg, g_clb, g_cb = _conv_bwd_a(dycat, g_conv, p_out, u1, conv_ln_g, conv_ln_b, w_pw)
    gw_pw = _matmul_tn([u3], dp, "grad_w_pw")
    dab, g_cw = _conv_bwd_b(du1, glu, conv_w)
    (dqkv, dga, dbias, dsink), (r_out, r_pw) = _attn_bwd(qkv, g_attn, a_out, dycat, bias, sinks,
                                                         scatter=(gw_out, gw_pw))
    g_rb = _relbias_grad(dbias, bmap)
    gwt_in = _matmul_tn([dqkv, dga, dab, dgc], h, "grad_w_in")
    g_cw_blocks = g_cw.reshape(CONV_ROWS, N_DEV, 128).transpose(1, 0, 2).reshape(N_DEV * CONV_ROWS, 128)
    (grad_x, dshift, dscale), (r_in, r_cw) = _dh_kernel([dqkv, dga, dab, dgc], wt_in, dz, x, mod,
                                                        scatter=(gwt_in, g_cw_blocks))
    dmod = jnp.concatenate([dshift, dscale, dgate], axis=1)
    small = dict(dmod=dmod, b_pw=g_bpw, conv_ln_g=g_clg, conv_ln_b=g_clb, conv_b=g_cb, ln_g=g_ln_g, ln_b=g_ln_b,
                 rel_bias=g_rb[:, :N_BUCKETS].reshape(1, N_BUCKETS * N_Q_HEADS),
                 sinks=dsink[0:1, :], loss=loss[0:1, :])
    return grad_x, r_in, r_out, r_pw, r_cw, small


SMALL_FIELDS = (("dmod", 3 * D_MODEL), ("b_pw", D_CONV), ("conv_ln_g", D_CONV), ("conv_ln_b", D_CONV),
                ("conv_b", D_CONV), ("ln_g", D_MODEL), ("ln_b", D_MODEL), ("rel_bias", N_BUCKETS * N_Q_HEADS),
                ("sinks", 128), ("loss", 128))


def _pack(fields):
    parts = []
    for name, width in SMALL_FIELDS:
        v = fields[name].reshape(1, -1).astype(F32)
        if v.shape[1] < width:
            v = jnp.pad(v, ((0, 0), (0, width - v.shape[1])))
        parts.append(v)
    return jnp.concatenate(parts, axis=1)


def _unpack(vec):
    out, off = {}, 0
    for name, width in SMALL_FIELDS:
        out[name] = vec[:, off:off + width]
        off += width
    return out


def kernel(x, c, w_ada, b_ada, w_in, rel_bias, sinks, conv_w, conv_b, conv_ln_g, conv_ln_b, w_pw, b_pw, w_out, ln_g, ln_b, loss_target, m_w_ada, m_b_ada, m_w_in, m_rel_bias, m_sinks, m_conv_w, m_conv_b, m_conv_ln_g, m_conv_ln_b, m_w_pw, m_b_pw, m_w_out, m_ln_g, m_ln_b, v_w_ada, v_b_ada, v_w_in, v_rel_bias, v_sinks, v_conv_w, v_conv_b, v_conv_ln_g, v_conv_ln_b, v_w_pw, v_b_pw, v_w_out, v_ln_g, v_ln_b):
    xi, yi, ci = _me()
    me = 4 * xi + 2 * yi + ci
    ncol = w_ada.shape[2]

    wt_in_loc = w_in[0].T.astype(BF16)
    conv_w_loc = jnp.pad(conv_w[0], ((0, CONV_ROWS - CONV_WIDTH), (0, 0)))

    b_ada_cols = lax.dynamic_slice(b_ada, (0, me * ncol), (1, ncol))
    (wt_in,), (mod_blocks, c_all) = _all_gather_two_level([wt_in_loc], "gather_w_in",
                                                          ada=(c, w_ada[0], b_ada_cols))
    mod = mod_blocks.reshape(1, 3 * D_MODEL)

    grad_x, r_in, r_out, r_pw, r_cw, small = _local_step(
        x[0], loss_target[0], mod, wt_in, w_out[0].astype(BF16), w_pw[0].astype(BF16), conv_w_loc, rel_bias,
        sinks, conv_b, conv_ln_g, conv_ln_b, b_pw, ln_g, ln_b)

    gathered, summed = _small_gather(_pack(small))
    tot = _unpack(summed)
    dmod_all = gathered[:, 0, :3 * D_MODEL]
    loss = tot["loss"][0, 0]

    ct = jnp.zeros((D_MODEL, 128), BF16).at[:, :N_DEV].set(c_all[:, 0, :].T.astype(BF16))
    dm = jnp.zeros((128, ncol), BF16).at[:N_DEV, :].set(
        lax.dynamic_slice(dmod_all, (0, me * ncol), (N_DEV, ncol)).astype(BF16))
    g_w_ada = _small_mm(ct, dm, "grad_w_ada")

    g_conv_w = _sum8(r_cw.reshape(N_DEV, CONV_ROWS, 128), "sum_conv_w")[:CONV_WIDTH]

    grads = {"w_ada": g_w_ada, "conv_w": g_conv_w,
             "w_pw": r_pw.reshape(N_DEV, D_CONV // N_DEV, D_CONV),
             "w_out": r_out.reshape(N_DEV, D_MODEL // N_DEV, D_MODEL)}
    params = {"w_ada": (w_ada, m_w_ada, v_w_ada), "conv_w": (conv_w, m_conv_w, v_conv_w),
              "w_pw": (w_pw, m_w_pw, v_w_pw), "w_out": (w_out, m_w_out, v_w_out)}
    res = {}
    for name, g in grads.items():
        w_, m_, v_ = params[name]
        res[name] = tuple(a[None] for a in _adamw(w_[0], g, m_[0], v_[0], "adamw_" + name))
    upd = _adamw(w_in[0].T, r_in.reshape(N_DEV, D_IN // N_DEV, D_MODEL), m_w_in[0].T, v_w_in[0].T, "adamw_w_in")
    res["w_in"] = tuple(a.T[None] for a in upd)

    small_params = {"b_ada": (b_ada, m_b_ada, v_b_ada), "b_pw": (b_pw, m_b_pw, v_b_pw),
                    "conv_ln_g": (conv_ln_g, m_conv_ln_g, v_conv_ln_g),
                    "conv_ln_b": (conv_ln_b, m_conv_ln_b, v_conv_ln_b), "conv_b": (conv_b, m_conv_b, v_conv_b),
                    "ln_g": (ln_g, m_ln_g, v_ln_g), "ln_b": (ln_b, m_ln_b, v_ln_b),
                    "rel_bias": (rel_bias, m_rel_bias, v_rel_bias), "sinks": (sinks, m_sinks, v_sinks)}
    to_small = lambda n, a: a.T if n == "rel_bias" else a
    key_of = {"b_ada": "dmod"}
    packs = []
    for j in range(3):
        fields = {key_of.get(n, n): to_small(n, t[j]) for n, t in small_params.items()}
        fields["loss"] = jnp.zeros((1, 1), F32)
        packs.append(_pack(fields))
    gsum = summed
    _, d_s, nm_s, nv_s = _adamw(packs[0], gsum, packs[1], packs[2], "adamw_small")
    outs_small = [_unpack(a) for a in (gsum, d_s, nm_s, nv_s)]
    for n, t in small_params.items():
        shape = t[0].shape
        vals = []
        for o in outs_small:
            a = o[key_of.get(n, n)]
            if n == "rel_bias":
                a = a.reshape(N_Q_HEADS, N_BUCKETS).T
            else:
                a = a[:, :shape[1]].reshape(shape)
            vals.append(a)
        res[n] = tuple(vals)

    order = ["w_ada", "b_ada", "w_in", "rel_bias", "sinks", "conv_w", "conv_b", "conv_ln_g", "conv_ln_b", "w_pw",
             "b_pw", "w_out", "ln_g", "ln_b"]
    out = [loss, grad_x[None]]
    for j in range(4):
        out += [res[n][j] for n in order]
    return tuple(out)
```

```python
import math

import jax
import jax.numpy as jnp
from jax import lax
from jax.experimental import pallas as pl
from jax.experimental.pallas import tpu as pltpu

F32, BF16, I32 = jnp.float32, jnp.bfloat16, jnp.int32

D_MODEL = 2048
D_ATTN = 1024
D_CONV = 1024
D_KV = 256
HEAD_DIM = 64
N_Q_HEADS = 16
N_KV_HEADS = 4
GQA = 4
BLOCK = 128
CONV_WIDTH = 31
CONV_ROWS = 32
HALO = 32
N_BUCKETS = 32
MAX_DISTANCE = 128
LN_EPS = 1e-5
ALPHA = 2.0 ** 0.25
D_IN = 5632
N_DEV = 8
NEG = -1e30

ADAM_LR, ADAM_B1, ADAM_B2, ADAM_EPS, ADAM_WD, ADAM_STEP = 0.001, 0.9, 0.999, 1e-08, 0.01, 10

NT_DIMS = (((1,), (1,)), ((), ()))
TN_DIMS = (((0,), (0,)), ((), ()))
MIB = 1 << 20


def _pc(body, name, **kw):
    return pl.pallas_call(body, name=name, **kw)


def _cp(vmem_mib=None, sem=None):
    kw = {}
    if vmem_mib is not None:
        kw["vmem_limit_bytes"] = vmem_mib * MIB
    if sem is not None:
        kw["dimension_semantics"] = sem
    return pltpu.CompilerParams(**kw)


def _sig(x):
    return 1.0 / (1.0 + jnp.exp(-x))


def _dsilu(x, s):
    return s * (1.0 + x * (1.0 - s))


def _me():
    return lax.axis_index("x"), lax.axis_index("y"), lax.axis_index("c")


def _peer(k):
    x, y, c = _me()
    px = 1 - x if k & 4 else x
    py = 1 - y if k & 2 else y
    pc = 1 - c if k & 1 else c
    return (px, py, pc), 4 * px + 2 * py + pc


def _remote(src, dst, send_sem, recv_sem, dev):
    return pltpu.make_async_remote_copy(src_ref=src, dst_ref=dst, send_sem=send_sem, recv_sem=recv_sem,
                                        device_id=dev, device_id_type=pl.DeviceIdType.MESH)


HBM_SPEC = pl.BlockSpec(memory_space=pl.ANY)
VMEM_SPEC = pl.BlockSpec(memory_space=pltpu.VMEM)


def _comm_scratch(nt):
    return [pltpu.SemaphoreType.DMA((nt, N_DEV - 1)), pltpu.SemaphoreType.DMA((nt, N_DEV - 1)),
            pltpu.SemaphoreType.DMA((nt,))]


def _gather_shapes(shards):
    return [jax.ShapeDtypeStruct((N_DEV * s.shape[0], s.shape[1]), s.dtype) for s in shards]


def _block(ref, blk, rows):
    return ref.at[pl.ds(pl.multiple_of(blk * rows, 8), rows), :]


def _gather_copies(src, dst, sems):
    send_sems, recv_sems, local_sems = sems
    x, y, c = _me()
    me = 4 * x + 2 * y + c
    nt = len(src)
    local = [pltpu.make_async_copy(src[t], _block(dst[t], me, src[t].shape[0]), local_sems.at[t]) for t in range(nt)]
    sends, arrivals = [], []
    for k in range(1, N_DEV):
        dev, blk = _peer(k)
        for t in range(nt):
            r = src[t].shape[0]
            pair = (send_sems.at[t, k - 1], recv_sems.at[t, k - 1], dev)
            sends.append(_remote(src[t], _block(dst[t], me, r), *pair))
            arrivals.append(_remote(src[t], _block(dst[t], blk, r), *pair))
    return local, sends, arrivals


def _scatter_copies(src, dst, sems):
    send_sems, recv_sems, local_sems = sems
    x, y, c = _me()
    me = 4 * x + 2 * y + c
    nt = len(src)
    rows = [s.shape[0] // N_DEV for s in src]
    local = [pltpu.make_async_copy(_block(src[t], me, rows[t]), _block(dst[t], me, rows[t]), local_sems.at[t])
             for t in range(nt)]
    sends, arrivals = [], []
    for k in range(1, N_DEV):
        dev, blk = _peer(k)
        for t in range(nt):
            pair = (send_sems.at[t, k - 1], recv_sems.at[t, k - 1], dev)
            sends.append(_remote(_block(src[t], blk, rows[t]), _block(dst[t], me, rows[t]), *pair))
            arrivals.append(_remote(_block(src[t], me, rows[t]), _block(dst[t], blk, rows[t]), *pair))
    return local, sends, arrivals


def _comm_start(cps):
    local, sends, _ = cps
    for cp in local + sends:
        cp.start()


def _comm_wait(cps):
    local, sends, arrivals = cps
    for cp in arrivals:
        cp.wait_recv()
    for cp in sends:
        cp.wait_send()
    for cp in local:
        cp.wait()


IN_PIECES = ((0, 1536, BF16), (1536, 1024, F32), (2560, 2048, F32), (4608, 1024, F32))


def _inproj(x, mod, wt, gather):
    S = x.shape[0]
    tm = min(256, S)
    ni = S // tm
    ng = len(gather)
    npc = len(IN_PIECES)

    def body(*refs):
        x_ref, sh_ref, sc_ref, w_ref = refs[:4]
        outs = refs[4 + ng:]
        h_ref, piece_refs = outs[0], outs[1:1 + npc]
        i = pl.program_id(0)
        if ng:
            cps = _gather_copies(refs[4:4 + ng], outs[1 + npc:1 + npc + ng], outs[1 + npc + ng:])

            @pl.when(i == 0)
            def _():
                _comm_start(cps)

        h = (x_ref[...] * (1.0 + sc_ref[...]) + sh_ref[...]).astype(BF16)
        h_ref[...] = h
        for (r0, n, dt), o_ref in zip(IN_PIECES, piece_refs):
            o_ref[...] = lax.dot_general(h, w_ref[r0:r0 + n, :], NT_DIMS, preferred_element_type=F32).astype(dt)
        if ng:
            @pl.when(i == ni - 1)
            def _():
                _comm_wait(cps)

    tile = lambda n: pl.BlockSpec((tm, n), lambda i: (i, 0))
    res = _pc(
        body, "inproj", grid=(ni,),
        in_specs=[tile(D_MODEL),
                  pl.BlockSpec((1, D_MODEL), lambda i: (0, 0)),
                  pl.BlockSpec((1, D_MODEL), lambda i: (0, 1)),
                  pl.BlockSpec((D_IN, D_MODEL), lambda i: (0, 0), pipeline_mode=pl.Buffered(1))] + [HBM_SPEC] * ng,
        out_specs=[tile(D_MODEL)] + [tile(n) for _, n, _ in IN_PIECES] + [HBM_SPEC] * ng,
        out_shape=[jax.ShapeDtypeStruct((S, D_MODEL), BF16)]
        + [jax.ShapeDtypeStruct((S, n), dt) for _, n, dt in IN_PIECES] + _gather_shapes(gather),
        scratch_shapes=_comm_scratch(ng) if ng else [],
        compiler_params=_cp(56, ("arbitrary",)),
    )(x, mod, mod, wt, *gather)
    return res[:1 + npc], res[1 + npc:]


def _matmul_tn(segs, b, name):
    S, N = b.shape
    tm, ts = 512, min(2048, S)
    ns = S // ts
    bounds = []
    t0 = 0
    for a in segs:
        assert a.shape[1] % tm == 0
        bounds.append((t0, a.shape[1] // tm))
        t0 += a.shape[1] // tm
    nseg = len(segs)

    def body(*refs):
        seg_refs, b_ref = refs[:nseg], refs[nseg]
        o_ref, acc = refs[-2], refs[-1]
        t, s = pl.program_id(0), pl.program_id(1)

        @pl.when(s == 0)
        def _():
            acc[...] = jnp.zeros_like(acc)

        for (s0, sn), r in zip(bounds, seg_refs):
            @pl.when((t >= s0) & (t < s0 + sn))
            def _(r=r):
                acc[...] += lax.dot_general(r[...], b_ref[...], TN_DIMS, preferred_element_type=F32)

        @pl.when(s == ns - 1)
        def _():
            o_ref[...] = acc[...].astype(BF16)

    def seg_spec(s0, sn):
        def index(t, s):
            mine = (t >= s0) & (t < s0 + sn)
            return jnp.where(mine, s, 0), jnp.clip(t - s0, 0, sn - 1)
        return pl.BlockSpec((ts, tm), index)

    return _pc(
        body, name, grid=(t0, ns),
        in_specs=[seg_spec(*bd) for bd in bounds] + [pl.BlockSpec((ts, N), lambda t, s: (s, 0))],
        out_specs=pl.BlockSpec((tm, N), lambda t, s: (t, 0)),
        out_shape=jax.ShapeDtypeStruct((t0 * tm, N), BF16),
        scratch_shapes=[pltpu.VMEM((tm, N), F32)],
        compiler_params=_cp(48, ("parallel", "arbitrary")),
    )(*segs, b)


def _bucket_map():
    qi = jnp.arange(BLOCK, dtype=I32)[:, None]
    kj = jnp.arange(2 * BLOCK, dtype=I32)[None, :]
    dist = qi + BLOCK - kj
    in_window = (dist >= 0) & (dist < BLOCK)
    d0 = jnp.maximum(dist, 0)
    max_exact = N_BUCKETS // 2
    d = jnp.maximum(d0, 1).astype(F32)
    large = max_exact + (jnp.log(d / max_exact) / math.log(MAX_DISTANCE / max_exact)
                         * (N_BUCKETS - max_exact)).astype(I32)
    large = jnp.minimum(large, N_BUCKETS - 1)
    bucket = jnp.where(d0 < max_exact, d0, large)
    return jnp.where(in_window, bucket, -1).astype(I32)


def _bias_table(rel_bias, bmap):
    def body(rb_ref, bm_ref, o_ref):
        bm = bm_ref[...]
        for h in range(N_Q_HEADS):
            acc = jnp.full((BLOCK, 2 * BLOCK), NEG, F32)
            for b in range(N_BUCKETS):
                acc = jnp.where(bm == b, rb_ref[b, h], acc)
            o_ref[h] = acc

    return _pc(
        body, "bias_table",
        in_specs=[pl.BlockSpec(memory_space=pltpu.SMEM), VMEM_SPEC],
        out_specs=VMEM_SPEC,
        out_shape=jax.ShapeDtypeStruct((N_Q_HEADS, BLOCK, 2 * BLOCK), F32),
    )(rel_bias, bmap)


def _bias_spec():
    return pl.BlockSpec((N_Q_HEADS, BLOCK, 2 * BLOCK), lambda n: (0, 0, 0))


def _relbias_grad(dbias, bmap):
    def body(db_ref, bm_ref, o_ref):
        bm = bm_ref[...]
        x = db_ref[0]
        lane = lax.broadcasted_iota(I32, (1, 128), 1)
        row = jnp.zeros((1, 128), F32)
        for b in range(N_BUCKETS):
            row = jnp.where(lane == b, jnp.sum(jnp.where(bm == b, x, 0.0)), row)
        o_ref[0] = jnp.broadcast_to(row, (8, 128))

    out = _pc(
        body, "relbias_grad", grid=(N_Q_HEADS,),
        in_specs=[pl.BlockSpec((1, BLOCK, 2 * BLOCK), lambda h: (h, 0, 0)),
                  pl.BlockSpec((BLOCK, 2 * BLOCK), lambda h: (0, 0))],
        out_specs=pl.BlockSpec((1, 8, 128), lambda h: (h, 0, 0)),
        out_shape=jax.ShapeDtypeStruct((N_Q_HEADS, 8, 128), F32),
    )(dbias, bmap)
    return out[:, 0, :]


Q_SCALE = HEAD_DIM ** -0.5
assert math.frexp(Q_SCALE)[0] == 0.5


def _stack_heads(ref, hk, rows=slice(None)):
    return jnp.concatenate([ref[rows, pl.ds(256 * hk + 64 * g, 64)] for g in range(GQA)], axis=0) * Q_SCALE


def _sink_col(sink_ref, hk):
    row = lax.broadcasted_iota(I32, (GQA * BLOCK, 1), 0)
    s = jnp.full((GQA * BLOCK, 1), sink_ref[0, 4 * hk + 3], F32)
    for g in (2, 1, 0):
        s = jnp.where(row < (g + 1) * BLOCK, sink_ref[0, 4 * hk + g], s)
    return s


def _attn_probs(q4, kw, bias_ref, sink_ref, hk, first_mask):
    s = lax.dot_general(q4, kw, NT_DIMS, preferred_element_type=F32)
    s = s + bias_ref[4 * hk:4 * hk + 4].reshape(GQA * BLOCK, 2 * BLOCK)
    if first_mask is not None:
        s = jnp.where(first_mask, s, NEG)
    sink = _sink_col(sink_ref, hk)
    m = jnp.maximum(jnp.max(s, axis=1, keepdims=True), sink)
    e = jnp.exp(s - m)
    es = jnp.exp(sink - m)
    inv = 1.0 / (jnp.sum(e, axis=1, keepdims=True) + es)
    return e * inv, es * inv


def _attn_bwd(qkv, g_attn, a_out, dycat, bias, sinks, scatter=()):
    S = qkv.shape[0]
    nb = S // BLOCK
    R = GQA * BLOCK
    ns = len(scatter)

    def body(*refs):
        (q_ref, kc_ref, kp_ref, vc_ref, vp_ref, g_ref, a_ref, dy_ref, bias_ref, sink_ref) = refs[:10]
        dqkv_ref, dg_ref, dbias_ref, dsink_ref = refs[10 + ns:14 + ns]
        dq_scr, dq_new, dk_scr, dv_scr, dkw_scr, dvw_scr, ds_scr = refs[14 + 2 * ns:21 + 2 * ns]
        n = pl.program_id(0)
        if ns:
            cps = _scatter_copies(refs[10:10 + ns], refs[14 + ns:14 + 2 * ns], refs[21 + 2 * ns:])

            @pl.when(n == 0)
            def _():
                _comm_start(cps)

        @pl.when(n == 0)
        def _():
            dbias_ref[...] = jnp.zeros_like(dbias_ref)
            ds_scr[...] = jnp.zeros_like(ds_scr)
            dq_scr[...] = jnp.zeros_like(dq_scr)
            dk_scr[...] = jnp.zeros_like(dk_scr)
            dv_scr[...] = jnp.zeros_like(dv_scr)

        @pl.when(n < nb)
        def _():
            kj = lax.broadcasted_iota(I32, (R, 2 * BLOCK), 1)
            first_mask = (n > 0) | (kj >= BLOCK)
            for hk in range(N_KV_HEADS):
                q4 = _stack_heads(q_ref, hk)
                ks = pl.ds(64 * hk, 64)
                kw = jnp.concatenate([kp_ref[:, ks], kc_ref[:, ks]], axis=0)
                vw = jnp.concatenate([vp_ref[:, ks], vc_ref[:, ks]], axis=0)
                p, psink = _attn_probs(q4, kw, bias_ref, sink_ref, hk, first_mask)
                da_parts, a_parts = [], []
                for g in range(GQA):
                    sl = pl.ds(256 * hk + 64 * g, 64)
                    gg = g_ref[:, sl]
                    sg = _sig(gg)
                    dyg = dy_ref[:, sl]
                    ag = a_ref[:, sl]
                    da_parts.append(dyg * (gg * sg))
                    a_parts.append(ag)
                    dg_ref[:, sl] = (dyg * ag * _dsilu(gg, sg)).astype(BF16)
                da4 = jnp.concatenate(da_parts, axis=0)
                a4 = jnp.concatenate(a_parts, axis=0)
                delta = jnp.sum(da4 * a4, axis=1, keepdims=True)
                da4b = da4.astype(BF16)
                dp = lax.dot_general(da4b, vw, NT_DIMS, preferred_element_type=F32)
                ds = p * (dp - delta)
                ds_scr[hk] += -psink * delta
                dbias_ref[4 * hk:4 * hk + 4] += ds.reshape(GQA, BLOCK, 2 * BLOCK)
                dsb = ds.astype(BF16)
                dq4 = jnp.dot(dsb, kw, preferred_element_type=F32) * Q_SCALE
                for g in range(GQA):
                    dq_new[:, pl.ds(256 * hk + 64 * g, 64)] = dq4[BLOCK * g:BLOCK * (g + 1)].astype(BF16)
                dkw_scr[:, ks] = lax.dot_general(dsb, q4, TN_DIMS, preferred_element_type=F32)
                dvw_scr[:, ks] = lax.dot_general(p.astype(BF16), da4b, TN_DIMS, preferred_element_type=F32)

        @pl.when(n == nb)
        def _():
            dkw_scr[0:BLOCK, :] = jnp.zeros((BLOCK, D_KV), F32)
            dvw_scr[0:BLOCK, :] = jnp.zeros((BLOCK, D_KV), F32)

        dqkv_ref[:, 0:D_ATTN] = dq_scr[...]
        dqkv_ref[:, D_ATTN:D_ATTN + D_KV] = (dk_scr[...] + dkw_scr[0:BLOCK, :]).astype(BF16)
        dqkv_ref[:, D_ATTN + D_KV:D_ATTN + 2 * D_KV] = (dv_scr[...] + dvw_scr[0:BLOCK, :]).astype(BF16)
        dq_scr[...] = dq_new[...]
        dk_scr[...] = dkw_scr[BLOCK:2 * BLOCK, :]
        dv_scr[...] = dvw_scr[BLOCK:2 * BLOCK, :]

        @pl.when(n == nb)
        def _():
            lane = lax.broadcasted_iota(I32, (1, 128), 1)
            row = jnp.zeros((1, 128), F32)
            for hk in range(N_KV_HEADS):
                col = ds_scr[hk]
                for g in range(GQA):
                    row = jnp.where(lane == 4 * hk + g, jnp.sum(col[BLOCK * g:BLOCK * (g + 1)]), row)
            dsink_ref[...] = jnp.broadcast_to(row, (8, 128))
            if ns:
                _comm_wait(cps)

    cur = lambda n: jnp.minimum(n, nb - 1)
    prev = lambda n: jnp.clip(n - 1, 0, nb - 1)
    res = _pc(
        body, "attn_bwd", grid=(nb + 1,),
        in_specs=[pl.BlockSpec((BLOCK, D_ATTN), lambda n: (cur(n), 0)),
                  pl.BlockSpec((BLOCK, D_KV), lambda n: (cur(n), 4)),
                  pl.BlockSpec((BLOCK, D_KV), lambda n: (prev(n), 4)),
                  pl.BlockSpec((BLOCK, D_KV), lambda n: (cur(n), 5)),
                  pl.BlockSpec((BLOCK, D_KV), lambda n: (prev(n), 5)),
                  pl.BlockSpec((BLOCK, D_ATTN), lambda n: (cur(n), 0)),
                  pl.BlockSpec((BLOCK, D_ATTN), lambda n: (cur(n), 0)),
                  pl.BlockSpec((BLOCK, D_ATTN), lambda n: (cur(n), 0)),
                  _bias_spec(),
                  pl.BlockSpec(memory_space=pltpu.SMEM)] + [HBM_SPEC] * ns,
        out_specs=[pl.BlockSpec((BLOCK, D_ATTN + 2 * D_KV), lambda n: (prev(n), 0)),
                   pl.BlockSpec((BLOCK, D_ATTN), lambda n: (cur(n), 0)),
                   pl.BlockSpec((N_Q_HEADS, BLOCK, 2 * BLOCK), lambda n: (0, 0, 0)),
                   pl.BlockSpec((8, 128), lambda n: (0, 0))] + [HBM_SPEC] * ns,
        out_shape=[jax.ShapeDtypeStruct((S, D_ATTN + 2 * D_KV), BF16),
                   jax.ShapeDtypeStruct((S, D_ATTN), BF16),
                   jax.ShapeDtypeStruct((N_Q_HEADS, BLOCK, 2 * BLOCK), F32),
                   jax.ShapeDtypeStruct((8, 128), F32)] + [jax.ShapeDtypeStruct(f.shape, f.dtype) for f in scatter],
        scratch_shapes=[pltpu.VMEM((BLOCK, D_ATTN), BF16), pltpu.VMEM((BLOCK, D_ATTN), BF16),
                        pltpu.VMEM((BLOCK, D_KV), F32), pltpu.VMEM((BLOCK, D_KV), F32),
                        pltpu.VMEM((2 * BLOCK, D_KV), F32), pltpu.VMEM((2 * BLOCK, D_KV), F32),
                        pltpu.VMEM((N_KV_HEADS, R, 1), F32)] + (_comm_scratch(ns) if ns else []),
        compiler_params=_cp(48, ("arbitrary",)),
    )(qkv, qkv, qkv, qkv, qkv, g_attn, a_out, dycat, bias, sinks, *scatter)
    return res[:4], res[4:]


def _conv_tile(S):
    return min(256, S)


def _shifted(win, s):
    return win if s == 0 else pltpu.roll(win, win.shape[0] - s, axis=0)


def _conv_taps(win, cw_ref, lanes, rows):
    acc = jnp.zeros((rows, win.shape[1]), F32)
    for s in range(8):
        ws = _shifted(win, s)
        for aa in range(5):
            j = 8 * aa + s - 2
            if 0 <= j < CONV_WIDTH:
                acc = acc + ws[8 * aa:8 * aa + rows] * cw_ref[j:j + 1, lanes]
    return acc


def _mixer_fwd(qkv, g_attn, glu, g_conv, bias, sinks, conv_w, conv_b, ln_g, ln_b, w_pw, b_pw):
    S = qkv.shape[0]
    RB = 2
    TR = RB * BLOCK
    nb = S // TR
    hb = TR // HALO
    LG = D_CONV // (RB * N_KV_HEADS)

    def body(q_ref, kc_ref, kp_ref, vc_ref, vp_ref, g_ref, bias_ref, sink_ref,
             a_in, ah_in, b_in, bh_in, gc_ref, cw_ref, cb_ref, lg_ref, lb_ref, wpw_ref, bpw_ref,
             y_ref, a_ref, u1_ref, p_ref, win):
        n = pl.program_id(0)
        kj = lax.broadcasted_iota(I32, (GQA * BLOCK, 2 * BLOCK), 1)
        first_mask = (n > 0) | (kj >= BLOCK)
        win[0:HALO, :] = jnp.where(n > 0, ah_in[...] * _sig(bh_in[...]), 0.0)
        win[HALO:HALO + TR, :] = a_in[...] * _sig(b_in[...])
        piece = 0
        for s in range(RB):
            rows = slice(BLOCK * s, BLOCK * (s + 1))
            before = slice(BLOCK * (s - 1), BLOCK * s)
            for hk in range(N_KV_HEADS):
                q4 = _stack_heads(q_ref, hk, rows)
                ks = pl.ds(64 * hk, 64)
                k_prev, v_prev = (kp_ref[:, ks], vp_ref[:, ks]) if s == 0 else (kc_ref[before, ks], vc_ref[before, ks])
                kw = jnp.concatenate([k_prev, kc_ref[rows, ks]], axis=0)
                vw = jnp.concatenate([v_prev, vc_ref[rows, ks]], axis=0)
                p, _ = _attn_probs(q4, kw, bias_ref, sink_ref, hk, first_mask if s == 0 else None)
                o4 = jnp.dot(p.astype(BF16), vw, preferred_element_type=F32)
                for g in range(GQA):
                    sl = pl.ds(256 * hk + 64 * g, 64)
                    og = o4[BLOCK * g:BLOCK * (g + 1)]
                    gg = g_ref[rows, sl]
                    a_ref[rows, sl] = og
                    y_ref[rows, sl] = (og * (gg * _sig(gg))).astype(BF16)
                lanes = pl.ds(LG * piece, LG)
                u1_ref[:, lanes] = _conv_taps(win[:, lanes], cw_ref, lanes, TR) + cb_ref[:, lanes]
                piece += 1
        u1 = u1_ref[...]
        mu = jnp.mean(u1, axis=1, keepdims=True)
        uc = u1 - mu
        rstd = lax.rsqrt(jnp.mean(uc * uc, axis=1, keepdims=True) + LN_EPS)
        u2 = uc * rstd * lg_ref[...] + lb_ref[...]
        u3 = (u2 * _sig(u2)).astype(BF16)
        pw = jnp.dot(u3, wpw_ref[...], preferred_element_type=F32) + bpw_ref[...]
        p_ref[...] = pw
        gc = gc_ref[...]
        y_ref[:, D_ATTN:] = (pw * (gc * _sig(gc))).astype(BF16)

    prev = lambda n: jnp.maximum(RB * n - 1, 0)
    halo = lambda n: jnp.maximum(n * hb - 1, 0)
    vec = pl.BlockSpec((1, D_CONV), lambda n: (0, 0))
    blk = lambda w, j: pl.BlockSpec((TR, w), lambda n: (n, j))
    return _pc(
        body, "mixer_fwd", grid=(nb,),
        in_specs=[blk(D_ATTN, 0),
                  blk(D_KV, 4), pl.BlockSpec((BLOCK, D_KV), lambda n: (prev(n), 4)),
                  blk(D_KV, 5), pl.BlockSpec((BLOCK, D_KV), lambda n: (prev(n), 5)),
                  blk(D_ATTN, 0),
                  _bias_spec(),
                  pl.BlockSpec(memory_space=pltpu.SMEM),
                  blk(D_CONV, 0), pl.BlockSpec((HALO, D_CONV), lambda n: (halo(n), 0)),
                  blk(D_CONV, 1), pl.BlockSpec((HALO, D_CONV), lambda n: (halo(n), 1)),
                  blk(D_CONV, 0),
                  pl.BlockSpec((CONV_ROWS, D_CONV), lambda n: (0, 0)),
                  vec, vec, vec,
                  pl.BlockSpec((D_CONV, D_CONV), lambda n: (0, 0)),
                  vec],
        out_specs=[blk(2 * D_ATTN, 0), blk(D_ATTN, 0), blk(D_CONV, 0), blk(D_CONV, 0)],
        out_shape=[jax.ShapeDtypeStruct((S, 2 * D_ATTN), BF16),
                   jax.ShapeDtypeStruct((S, D_ATTN), F32),
                   jax.ShapeDtypeStruct((S, D_CONV), F32),
                   jax.ShapeDtypeStruct((S, D_CONV), F32)],
        scratch_shapes=[pltpu.VMEM((TR + HALO, D_CONV), F32)],
        compiler_params=_cp(56, ("parallel",)),
    )(qkv, qkv, qkv, qkv, qkv, g_attn, bias, sinks, glu, glu, glu, glu, g_conv, conv_w, conv_b, ln_g, ln_b, w_pw,
      b_pw)


def _conv_bwd_a(dycat, g_conv, p_out, u1, ln_g, ln_b, w_pw):
    S = u1.shape[0]
    T = _conv_tile(S)
    nt = S // T

    def body(dy_ref, gc_ref, p_ref, u1_ref, lg_ref, lb_ref, wpw_ref,
             gw_ref, dgc_ref, du1_ref, gbpw_ref, glg_ref, glb_ref, gcb_ref, gw_acc):
        i = pl.program_id(0)

        @pl.when(i == 0)
        def _():
            for r in (gbpw_ref, glg_ref, glb_ref, gcb_ref, gw_acc):
                r[...] = jnp.zeros_like(r)

        dy = dy_ref[...]
        gc = gc_ref[...]
        sg = _sig(gc)
        dp = dy * (gc * sg)
        dgc_ref[...] = (dy * p_ref[...] * _dsilu(gc, sg)).astype(BF16)
        gbpw_ref[...] += jnp.sum(dp, axis=0, keepdims=True)
        dpb = dp.astype(BF16)
        du3 = lax.dot_general(dpb, wpw_ref[...], NT_DIMS, preferred_element_type=F32)
        u1 = u1_ref[...]
        mu = jnp.mean(u1, axis=1, keepdims=True)
        uc = u1 - mu
        rstd = lax.rsqrt(jnp.mean(uc * uc, axis=1, keepdims=True) + LN_EPS)
        uh = uc * rstd
        lg = lg_ref[...]
        u2 = uh * lg + lb_ref[...]
        s2 = _sig(u2)
        gw_acc[...] += lax.dot_general((u2 * s2).astype(BF16), dpb, TN_DIMS, preferred_element_type=F32)
        du2 = du3 * _dsilu(u2, s2)
        glg_ref[...] += jnp.sum(du2 * uh, axis=0, keepdims=True)
        glb_ref[...] += jnp.sum(du2, axis=0, keepdims=True)
        duh = du2 * lg
        du1 = rstd * (duh - jnp.mean(duh, axis=1, keepdims=True) - uh * jnp.mean(duh * uh, axis=1, keepdims=True))
        du1_ref[...] = du1
        gcb_ref[...] += jnp.sum(du1, axis=0, keepdims=True)

        @pl.when(i == nt - 1)
        def _():
            gw_ref[...] = gw_acc[...].astype(BF16)

    vec = pl.BlockSpec((1, D_CONV), lambda i: (0, 0))
    tile = pl.BlockSpec((T, D_CONV), lambda i: (i, 0))
    square = pl.BlockSpec((D_CONV, D_CONV), lambda i: (0, 0))
    vshape = jax.ShapeDtypeStruct((1, D_CONV), F32)
    return _pc(
        body, "conv_bwd_a", grid=(nt,),
        in_specs=[pl.BlockSpec((T, D_CONV), lambda i: (i, 1)), tile, tile, tile, vec, vec, square],
        out_specs=[square, tile, tile, vec, vec, vec, vec],
        out_shape=[jax.ShapeDtypeStruct((D_CONV, D_CONV), BF16), jax.ShapeDtypeStruct((S, D_CONV), BF16),
                   jax.ShapeDtypeStruct((S, D_CONV), F32), vshape, vshape, vshape, vshape],
        scratch_shapes=[pltpu.VMEM((D_CONV, D_CONV), F32)],
        compiler_params=_cp(48, ("arbitrary",)),
    )(dycat, g_conv, p_out, u1, ln_g, ln_b, w_pw)


def _conv_bwd_b(du1, glu, conv_w):
    S = du1.shape[0]
    T = _conv_tile(S)
    hb = T // HALO
    nt = S // T
    last_h = S // HALO - 1

    def body(du_ref, dun_ref, a_ref, b_ref, cw_ref, dab_ref, gw_ref):
        i = pl.program_id(0)

        @pl.when(i == 0)
        def _():
            gw_ref[...] = jnp.zeros_like(gw_ref)

        for lg in range(D_CONV // 128):
            lanes = pl.ds(128 * lg, 128)
            a = a_ref[:, lanes]
            sb = _sig(b_ref[:, lanes])
            u0 = a * sb
            du1n = jnp.where(i < nt - 1, dun_ref[:, lanes], 0.0)
            win2 = jnp.concatenate([du_ref[:, lanes], du1n], axis=0)
            acc = jnp.zeros((T, 128), F32)
            for s in range(8):
                w2 = _shifted(win2, s)
                for aa in range(4):
                    j = CONV_WIDTH - 1 - (8 * aa + s)
                    if 0 <= j < CONV_WIDTH:
                        xo = w2[8 * aa:8 * aa + T]
                        acc = acc + xo * cw_ref[j:j + 1, lanes]
                        gw_ref[j:j + 1, lanes] += jnp.sum(xo * u0, axis=0, keepdims=True)
            dab_ref[:, lanes] = (acc * sb).astype(BF16)
            dab_ref[:, pl.ds(D_CONV + 128 * lg, 128)] = (acc * a * sb * (1.0 - sb)).astype(BF16)

    nxt = lambda i: jnp.minimum((i + 1) * hb, last_h)
    return _pc(
        body, "conv_bwd_b", grid=(nt,),
        in_specs=[pl.BlockSpec((T, D_CONV), lambda i: (i, 0)),
                  pl.BlockSpec((HALO, D_CONV), lambda i: (nxt(i), 0)),
                  pl.BlockSpec((T, D_CONV), lambda i: (i, 0)),
                  pl.BlockSpec((T, D_CONV), lambda i: (i, 1)),
                  pl.BlockSpec((CONV_ROWS, D_CONV), lambda i: (0, 0))],
        out_specs=[pl.BlockSpec((T, 2 * D_CONV), lambda i: (i, 0)),
                   pl.BlockSpec((CONV_ROWS, D_CONV), lambda i: (0, 0))],
        out_shape=[jax.ShapeDtypeStruct((S, 2 * D_CONV), BF16),
                   jax.ShapeDtypeStruct((CONV_ROWS, D_CONV), F32)],
        compiler_params=_cp(48, ("arbitrary",)),
    )(du1, du1, glu, glu, conv_w)


def _outproj_ln(ycat, w_out, x, target, mod, ln_g, ln_b):
    S = x.shape[0]
    tm = min(256, S)

    def body(yc_ref, w_ref, x_ref, t_ref, gate_ref, lg_ref, lb_ref,
             dz_ref, dy_ref, dyc_ref, loss_ref, glg_ref, glb_ref, dgate_ref):
        i = pl.program_id(0)

        @pl.when(i == 0)
        def _():
            for r in (loss_ref, glg_ref, glb_ref, dgate_ref):
                r[...] = jnp.zeros_like(r)

        w = w_ref[...]
        y = jnp.dot(yc_ref[...], w, preferred_element_type=F32)
        gate = gate_ref[...]
        z = ALPHA * x_ref[...] + gate * y
        mu = jnp.mean(z, axis=1, keepdims=True)
        zc = z - mu
        rstd = lax.rsqrt(jnp.mean(zc * zc, axis=1, keepdims=True) + LN_EPS)
        zh = zc * rstd
        lg = lg_ref[...]
        err = zh * lg + lb_ref[...] - t_ref[...]
        loss_ref[...] += 0.5 * jnp.sum(jnp.sum(err * err, axis=1, keepdims=True)) / D_MODEL
        dout = err * (1.0 / D_MODEL)
        glg_ref[...] += jnp.sum(dout * zh, axis=0, keepdims=True)
        glb_ref[...] += jnp.sum(dout, axis=0, keepdims=True)
        dzh = dout * lg
        dz = rstd * (dzh - jnp.mean(dzh, axis=1, keepdims=True) - zh * jnp.mean(dzh * zh, axis=1, keepdims=True))
        dz_ref[...] = dz
        dgate_ref[...] += jnp.sum(dz * y, axis=0, keepdims=True)
        dy = (dz * gate).astype(BF16)
        dy_ref[...] = dy
        dyc_ref[...] = lax.dot_general(dy, w, NT_DIMS, preferred_element_type=F32).astype(BF16)

    vec = pl.BlockSpec((1, D_MODEL), lambda i: (0, 0))
    tile = pl.BlockSpec((tm, D_MODEL), lambda i: (i, 0))
    vshape = jax.ShapeDtypeStruct((1, D_MODEL), F32)
    return _pc(
        body, "outproj_ln", grid=(S // tm,),
        in_specs=[tile, pl.BlockSpec((D_MODEL, D_MODEL), lambda i: (0, 0), pipeline_mode=pl.Buffered(1)), tile, tile,
                  pl.BlockSpec((1, D_MODEL), lambda i: (0, 2)), vec, vec],
        out_specs=[tile, tile, tile, pl.BlockSpec((8, 128), lambda i: (0, 0)), vec, vec, vec],
        out_shape=[jax.ShapeDtypeStruct((S, D_MODEL), F32), jax.ShapeDtypeStruct((S, D_MODEL), BF16),
                   jax.ShapeDtypeStruct((S, D_MODEL), BF16), jax.ShapeDtypeStruct((8, 128), F32),
                   vshape, vshape, vshape],
        compiler_params=_cp(56, ("arbitrary",)),
    )(ycat, w_out, x, target, mod, ln_g, ln_b)


def _dh_kernel(segs, wt, dz, x, mod, scatter=()):
    S = x.shape[0]
    tm = min(256, S)
    row0 = [0]
    for a in segs:
        row0.append(row0[-1] + a.shape[1])
    assert row0[-1] == wt.shape[0]
    nseg = len(segs)
    ns = len(scatter)
    ni = S // tm

    def body(*refs):
        seg_refs = refs[:nseg]
        w_hbm, dz_ref, x_ref, sc_ref = refs[nseg:nseg + 4]
        outs = refs[nseg + 4 + ns:]
        gx_ref, dsh_ref, dsc_ref = outs[:3]
        w_ref, w_sems = outs[3 + ns:5 + ns]
        i = pl.program_id(0)
        if ns:
            cps = _scatter_copies(refs[nseg + 4:nseg + 4 + ns], outs[3:3 + ns], outs[5 + ns:])

        def w_load(t):
            rows = slice(row0[t], row0[t + 1])
            return pltpu.make_async_copy(w_hbm.at[rows, :], w_ref.at[rows, :], w_sems.at[t])

        @pl.when(i == 0)
        def _():
            for t in range(nseg):
                w_load(t).start()
            dsh_ref[...] = jnp.zeros_like(dsh_ref)
            dsc_ref[...] = jnp.zeros_like(dsc_ref)
            if ns:
                _comm_start(cps)

        dh = None
        for t in range(nseg):
            @pl.when(i == 0)
            def _(t=t):
                w_load(t).wait()

            part = jnp.dot(seg_refs[t][...], w_ref[row0[t]:row0[t + 1], :], preferred_element_type=F32)
            dh = part if dh is None else dh + part
        gx_ref[...] = ALPHA * dz_ref[...] + dh * (1.0 + sc_ref[...])
        dsh_ref[...] += jnp.sum(dh, axis=0, keepdims=True)
        dsc_ref[...] += jnp.sum(dh * x_ref[...], axis=0, keepdims=True)

        if ns:
            @pl.when(i == ni - 1)
            def _():
                _comm_wait(cps)

    tile = pl.BlockSpec((tm, D_MODEL), lambda i: (i, 0))
    vec = pl.BlockSpec((1, D_MODEL), lambda i: (0, 0))
    vshape = jax.ShapeDtypeStruct((1, D_MODEL), F32)
    res = _pc(
        body, "dh_gradx", grid=(ni,),
        in_specs=[pl.BlockSpec((tm, a.shape[1]), lambda i: (i, 0)) for a in segs] + [
            HBM_SPEC, tile, tile,
            pl.BlockSpec((1, D_MODEL), lambda i: (0, 1))] + [HBM_SPEC] * ns,
        out_specs=[tile, vec, vec] + [HBM_SPEC] * ns,
        out_shape=[jax.ShapeDtypeStruct((S, D_MODEL), F32), vshape, vshape]
        + [jax.ShapeDtypeStruct(f.shape, f.dtype) for f in scatter],
        scratch_shapes=[pltpu.VMEM(wt.shape, BF16), pltpu.SemaphoreType.DMA((nseg,))]
        + (_comm_scratch(ns) if ns else []),
        compiler_params=_cp(58, ("arbitrary",)),
    )(*segs, wt, dz, x, mod, *scatter)
    return res[:3], res[3:]


def _row_tile(rows, cols):
    if rows * cols * 4 <= 2 * MIB or rows % 8:
        return rows
    tr = max(8, (2 * MIB // (cols * 4)) // 8 * 8)
    while rows % tr:
        tr -= 8
    return tr


def _sum8(recv, name):
    _, R, C = recv.shape
    tr = _row_tile(R, C)

    def body(r_ref, o_ref):
        acc = r_ref[0].astype(F32)
        for d in range(1, N_DEV):
            acc = acc + r_ref[d].astype(F32)
        o_ref[...] = acc

    return _pc(
        body, name, grid=(R // tr,),
        in_specs=[pl.BlockSpec((N_DEV, tr, C), lambda i: (0, i, 0))],
        out_specs=pl.BlockSpec((tr, C), lambda i: (i, 0)),
        out_shape=jax.ShapeDtypeStruct((R, C), F32),
        compiler_params=_cp(40, ("parallel",)),
    )(recv)


def _adamw(w, g, m, v, name):
    R, C = w.shape
    tr = _row_tile(R, C)
    parts = g.ndim == 3

    def body(w_ref, g_ref, m_ref, v_ref, *out_refs):
        d_ref, nm_ref, nv_ref = out_refs[-3:]
        if parts:
            g_ = g_ref[0].astype(F32)
            for d in range(1, N_DEV):
                g_ = g_ + g_ref[d].astype(F32)
            out_refs[0][...] = g_
        else:
            g_ = g_ref[...]
        m_ = ADAM_B1 * m_ref[...] + (1.0 - ADAM_B1) * g_
        v_ = ADAM_B2 * v_ref[...] + (1.0 - ADAM_B2) * (g_ * g_)
        m_hat = m_ / (1.0 - ADAM_B1 ** ADAM_STEP)
        v_hat = v_ / (1.0 - ADAM_B2 ** ADAM_STEP)
        d_ref[...] = -ADAM_LR * (m_hat / (jnp.sqrt(v_hat) + ADAM_EPS) + ADAM_WD * w_ref[...])
        nm_ref[...] = m_
        nv_ref[...] = v_

    spec = pl.BlockSpec((tr, C), lambda i: (i, 0))
    gspec = pl.BlockSpec((N_DEV, tr, C), lambda i: (0, i, 0)) if parts else spec
    shape = jax.ShapeDtypeStruct((R, C), F32)
    nout = 4 if parts else 3
    res = _pc(
        body, name, grid=(R // tr,),
        in_specs=[spec, gspec, spec, spec], out_specs=[spec] * nout, out_shape=[shape] * nout,
        compiler_params=_cp(40, ("parallel",)),
    )(w, g, m, v)
    return tuple(res) if parts else (g, *res)


def _small_mm(a, b, name):
    def body(a_ref, b_ref, o_ref):
        o_ref[...] = jnp.dot(a_ref[...], b_ref[...], preferred_element_type=F32)

    return _pc(body, name, out_shape=jax.ShapeDtypeStruct((a.shape[0], b.shape[1]), F32),
               compiler_params=_cp(40))(a, b)


def _all_gather_two_level(shards, name, ada=None):
    nt = len(shards)
    NK = 9
    ada_out, ada_scratch = _ada_shapes(ada[1].shape[1]) if ada else ([], [])
    na = 3 if ada else 0

    def body(*refs):
        src, dst = refs[:nt], refs[nt + na:2 * nt + na]
        send_sems, recv_sems, local_sems = refs[2 * nt + na + len(ada_out):2 * nt + na + len(ada_out) + 3]
        x, y, c = _me()
        me, sibling = (x, y, c), (x, y, 1 - c)
        xn, yn, dg = (1 - x, y), (x, 1 - y), (1 - x, 1 - y)

        def part(t, dev, half=None):
            r = src[t].shape[0]
            blk = 4 * dev[0] + 2 * dev[1] + dev[2]
            if half is None:
                return dst[t].at[pl.ds(pl.multiple_of(blk * r, 16), r), :]
            return dst[t].at[pl.ds(pl.multiple_of(blk * r + half * (r // 2), 16), r // 2), :]

        def copy(t, k, to, rows, from_src=False):
            return _remote(src[t] if from_src else rows, rows, send_sems.at[t, k], recv_sems.at[t, k], to)

        local = [pltpu.make_async_copy(src[t], part(t, me), local_sems.at[t]) for t in range(nt)]
        for cp in local:
            cp.start()
        sends = []
        for t in range(nt):
            sends += [copy(t, 0, sibling, part(t, me), True), copy(t, 1, (*xn, c), part(t, me), True),
                      copy(t, 2, (*yn, c), part(t, me), True)]
        for cp in sends:
            cp.start()
        if ada:
            _ada_exchange(*refs[nt:nt + na], *refs[2 * nt + na:2 * nt + na + len(ada_out)],
                          *refs[2 * nt + na + len(ada_out) + 3:])

        def arrive_then_pass(t, k, rows, passes):
            copy(t, k, me, rows).wait_recv()
            for cp in passes:
                cp.start()
                sends.append(cp)

        for t in range(nt):
            arrive_then_pass(t, 1, part(t, (*xn, c)), [copy(t, 5, sibling, part(t, (*xn, c))),
                                                       copy(t, 3, (*yn, c), part(t, (*xn, c), 0))])
            arrive_then_pass(t, 2, part(t, (*yn, c)), [copy(t, 6, sibling, part(t, (*yn, c))),
                                                       copy(t, 4, (*xn, c), part(t, (*yn, c), 1))])
            arrive_then_pass(t, 3, part(t, (*dg, c), 0), [copy(t, 7, sibling, part(t, (*dg, c), 0))])
            arrive_then_pass(t, 4, part(t, (*dg, c), 1), [copy(t, 8, sibling, part(t, (*dg, c), 1))])
        for t in range(nt):
            copy(t, 0, me, part(t, sibling)).wait_recv()
            copy(t, 5, me, part(t, (*xn, 1 - c))).wait_recv()
            copy(t, 6, me, part(t, (*yn, 1 - c))).wait_recv()
            copy(t, 7, me, part(t, (*dg, 1 - c), 0)).wait_recv()
            copy(t, 8, me, part(t, (*dg, 1 - c), 1)).wait_recv()
        for cp in sends:
            cp.wait_send()
        for cp in local:
            cp.wait()

    res = _pc(
        body, name,
        in_specs=[HBM_SPEC] * nt + [VMEM_SPEC] * na, out_specs=[HBM_SPEC] * nt + [VMEM_SPEC] * len(ada_out),
        out_shape=_gather_shapes(shards) + ada_out,
        scratch_shapes=[pltpu.SemaphoreType.DMA((nt, NK)), pltpu.SemaphoreType.DMA((nt, NK)),
                        pltpu.SemaphoreType.DMA((nt,))] + ada_scratch,
        compiler_params=pltpu.CompilerParams(has_side_effects=True, vmem_limit_bytes=40 * MIB),
    )(*shards, *(ada or ()))
    return res[:nt], res[nt:]


def _ada_shapes(ncol):
    out = [jax.ShapeDtypeStruct((N_DEV, 1, ncol), F32), jax.ShapeDtypeStruct((N_DEV, 1, D_MODEL), F32)]
    scratch = [pltpu.VMEM((1, D_MODEL), F32), pltpu.VMEM((16, D_MODEL), F32),
               pltpu.VMEM((N_DEV, 1, ncol), F32)] + [pltpu.SemaphoreType.DMA((N_DEV - 1,))] * 4
    return out, scratch


def _ada_exchange(c_ref, w_ref, b_ref, mod_ref, call_ref, cact, cmat, mloc, send1, recv1, send2, recv2):
    x, y, z = _me()
    me = 4 * x + 2 * y + z
    cv = c_ref[...]
    cact[...] = cv * _sig(cv)
    call_ref[me] = cact[...]
    sends = []
    for k in range(1, N_DEV):
        dev, _ = _peer(k)
        cp = _remote(cact, call_ref.at[me], send1.at[k - 1], recv1.at[k - 1], dev)
        cp.start()
        sends.append(cp)
    for k in range(1, N_DEV):
        dev, blk = _peer(k)
        _remote(cact, call_ref.at[blk], send1.at[k - 1], recv1.at[k - 1], dev).wait_recv()
    cmat[...] = jnp.zeros_like(cmat)
    for b in range(N_DEV):
        cmat[b:b + 1, :] = call_ref[b]
    m = jnp.dot(cmat[...].astype(BF16), w_ref[...].astype(BF16), preferred_element_type=F32) + b_ref[...]
    for b in range(N_DEV):
        mloc[b] = m[b:b + 1, :]
    mod_ref[me] = mloc[me]
    for k in range(1, N_DEV):
        dev, blk = _peer(k)
        cp = _remote(mloc.at[blk], mod_ref.at[me], send2.at[k - 1], recv2.at[k - 1], dev)
        cp.start()
        sends.append(cp)
    for k in range(1, N_DEV):
        dev, blk = _peer(k)
        _remote(mloc.at[me], mod_ref.at[blk], send2.at[k - 1], recv2.at[k - 1], dev).wait_recv()
    for cp in sends:
        cp.wait_send()


def _small_gather(vec):
    n = vec.shape[1]

    def body(v_ref, all_ref, sum_ref, send, recv):
        x, y, z = _me()
        me = 4 * x + 2 * y + z
        all_ref[me] = v_ref[...]
        sends = []
        for k in range(1, N_DEV):
            dev, _ = _peer(k)
            cp = _remote(v_ref, all_ref.at[me], send.at[k - 1], recv.at[k - 1], dev)
            cp.start()
            sends.append(cp)
        for k in range(1, N_DEV):
            dev, blk = _peer(k)
            _remote(v_ref, all_ref.at[blk], send.at[k - 1], recv.at[k - 1], dev).wait_recv()
        acc = all_ref[0]
        for d in range(1, N_DEV):
            acc = acc + all_ref[d]
        sum_ref[...] = acc
        for cp in sends:
            cp.wait_send()

    return _pc(
        body, "small_gather",
        in_specs=[VMEM_SPEC], out_specs=[VMEM_SPEC] * 2,
        out_shape=[jax.ShapeDtypeStruct((N_DEV, 1, n), F32), jax.ShapeDtypeStruct((1, n), F32)],
        scratch_shapes=[pltpu.SemaphoreType.DMA((N_DEV - 1,))] * 2,
        compiler_params=pltpu.CompilerParams(has_side_effects=True),
    )(vec)


def _local_step(x, target, mod, wt_in, w_out_loc, w_pw_loc, conv_w_loc, rel_bias, sinks, conv_b, conv_ln_g,
                conv_ln_b, b_pw, ln_g, ln_b):
    bmap = _bucket_map()
    bias = _bias_table(rel_bias, bmap)
    (h, qkv, g_attn, glu, g_conv), (w_out, w_pw, conv_w_blocks) = _inproj(
        x, mod, wt_in, (w_out_loc, w_pw_loc, conv_w_loc))
    conv_w = conv_w_blocks.reshape(N_DEV, CONV_ROWS, 128).transpose(1, 0, 2).reshape(CONV_ROWS, D_CONV)
    ycat, a_out, u1, p_out = _mixer_fwd(qkv, g_attn, glu, g_conv, bias, sinks, conv_w, conv_b, conv_ln_g,
                                        conv_ln_b, w_pw, b_pw)
    dz, dy, dycat, loss, g_ln_g, g_ln_b, dgate = _outproj_ln(ycat, w_out, x, target, mod, ln_g, ln_b)
    gw_out = _matmul_tn([ycat], dy, "grad_w_out")
    gw_pw, dgc, du1, g_bpw, g_clg, g_clb, g_cb = _conv_bwd_a(dycat, g_conv, p_out, u1, conv_ln_g, conv_ln_b, w_pw)
    dab, g_cw = _conv_bwd_b(du1, glu, conv_w)
    (dqkv, dga, dbias, dsink), (r_out, r_pw) = _attn_bwd(qkv, g_attn, a_out, dycat, bias, sinks,
                                                         scatter=(gw_out, gw_pw))
    g_rb = _relbias_grad(dbias, bmap)
    gwt_in = _matmul_tn([dqkv, dga, dab, dgc], h, "grad_w_in")
    g_cw_blocks = g_cw.reshape(CONV_ROWS, N_DEV, 128).transpose(1, 0, 2).reshape(N_DEV * CONV_ROWS, 128)
    (grad_x, dshift, dscale), (r_in, r_cw) = _dh_kernel([dqkv, dga, dab, dgc], wt_in, dz, x, mod,
                                                        scatter=(gwt_in, g_cw_blocks))
    dmod = jnp.concatenate([dshift, dscale, dgate], axis=1)
    small = dict(dmod=dmod, b_pw=g_bpw, conv_ln_g=g_clg, conv_ln_b=g_clb, conv_b=g_cb, ln_g=g_ln_g, ln_b=g_ln_b,
                 rel_bias=g_rb[:, :N_BUCKETS].reshape(1, N_BUCKETS * N_Q_HEADS),
                 sinks=dsink[0:1, :], loss=loss[0:1, :])
    return grad_x, r_in, r_out, r_pw, r_cw, small


SMALL_FIELDS = (("dmod", 3 * D_MODEL), ("b_pw", D_CONV), ("conv_ln_g", D_CONV), ("conv_ln_b", D_CONV),
                ("conv_b", D_CONV), ("ln_g", D_MODEL), ("ln_b", D_MODEL), ("rel_bias", N_BUCKETS * N_Q_HEADS),
                ("sinks", 128), ("loss", 128))


def _pack(fields):
    parts = []
    for name, width in SMALL_FIELDS:
        v = fields[name].reshape(1, -1).astype(F32)
        if v.shape[1] < width:
            v = jnp.pad(v, ((0, 0), (0, width - v.shape[1])))
        parts.append(v)
    return jnp.concatenate(parts, axis=1)


def _unpack(vec):
    out, off = {}, 0
    for name, width in SMALL_FIELDS:
        out[name] = vec[:, off:off + width]
        off += width
    return out


def kernel(x, c, w_ada, b_ada, w_in, rel_bias, sinks, conv_w, conv_b, conv_ln_g, conv_ln_b, w_pw, b_pw, w_out, ln_g, ln_b, loss_target, m_w_ada, m_b_ada, m_w_in, m_rel_bias, m_sinks, m_conv_w, m_conv_b, m_conv_ln_g, m_conv_ln_b, m_w_pw, m_b_pw, m_w_out, m_ln_g, m_ln_b, v_w_ada, v_b_ada, v_w_in, v_rel_bias, v_sinks, v_conv_w, v_conv_b, v_conv_ln_g, v_conv_ln_b, v_w_pw, v_b_pw, v_w_out, v_ln_g, v_ln_b):
    xi, yi, ci = _me()
    me = 4 * xi + 2 * yi + ci
    ncol = w_ada.shape[2]

    wt_in_loc = w_in[0].T.astype(BF16)
    conv_w_loc = jnp.pad(conv_w[0], ((0, CONV_ROWS - CONV_WIDTH), (0, 0)))

    b_ada_cols = lax.dynamic_slice(b_ada, (0, me * ncol), (1, ncol))
    (wt_in,), (mod_blocks, c_all) = _all_gather_two_level([wt_in_loc], "gather_w_in",
                                                          ada=(c, w_ada[0], b_ada_cols))
    mod = mod_blocks.reshape(1, 3 * D_MODEL)

    grad_x, r_in, r_out, r_pw, r_cw, small = _local_step(
        x[0], loss_target[0], mod, wt_in, w_out[0].astype(BF16), w_pw[0].astype(BF16), conv_w_loc, rel_bias,
        sinks, conv_b, conv_ln_g, conv_ln_b, b_pw, ln_g, ln_b)

    gathered, summed = _small_gather(_pack(small))
    tot = _unpack(summed)
    dmod_all = gathered[:, 0, :3 * D_MODEL]
    loss = tot["loss"][0, 0]

    ct = jnp.zeros((D_MODEL, 128), BF16).at[:, :N_DEV].set(c_all[:, 0, :].T.astype(BF16))
    dm = jnp.zeros((128, ncol), BF16).at[:N_DEV, :].set(
        lax.dynamic_slice(dmod_all, (0, me * ncol), (N_DEV, ncol)).astype(BF16))
    g_w_ada = _small_mm(ct, dm, "grad_w_ada")

    g_conv_w = _sum8(r_cw.reshape(N_DEV, CONV_ROWS, 128), "sum_conv_w")[:CONV_WIDTH]

    grads = {"w_ada": g_w_ada, "conv_w": g_conv_w,
             "w_pw": r_pw.reshape(N_DEV, D_CONV // N_DEV, D_CONV),
             "w_out": r_out.reshape(N_DEV, D_MODEL // N_DEV, D_MODEL)}
    params = {"w_ada": (w_ada, m_w_ada, v_w_ada), "conv_w": (conv_w, m_conv_w, v_conv_w),
              "w_pw": (w_pw, m_w_pw, v_w_pw), "w_out": (w_out, m_w_out, v_w_out)}
    res = {}
    for name, g in grads.items():
        w_, m_, v_ = params[name]
        res[name] = tuple(a[None] for a in _adamw(w_[0], g, m_[0], v_[0], "adamw_" + name))
    upd = _adamw(w_in[0].T, r_in.reshape(N_DEV, D_IN // N_DEV, D_MODEL), m_w_in[0].T, v_w_in[0].T, "adamw_w_in")
    res["w_in"] = tuple(a.T[None] for a in upd)

    small_params = {"b_ada": (b_ada, m_b_ada, v_b_ada), "b_pw": (b_pw, m_b_pw, v_b_pw),
                    "conv_ln_g": (conv_ln_g, m_conv_ln_g, v_conv_ln_g),
                    "conv_ln_b": (conv_ln_b, m_conv_ln_b, v_conv_ln_b), "conv_b": (conv_b, m_conv_b, v_conv_b),
                    "ln_g": (ln_g, m_ln_g, v_ln_g), "ln_b": (ln_b, m_ln_b, v_ln_b),
                    "rel_bias": (rel_bias, m_rel_bias, v_rel_bias), "sinks": (sinks, m_sinks, v_sinks)}
    to_small = lambda n, a: a.T if n == "rel_bias" else a
    key_of = {"b_ada": "dmod"}
    packs = []
    for j in range(3):
        fields = {key_of.get(n, n): to_small(n, t[j]) for n, t in small_params.items()}
        fields["loss"] = jnp.zeros((1, 1), F32)
        packs.append(_pack(fields))
    gsum = summed
    _, d_s, nm_s, nv_s = _adamw(packs[0], gsum, packs[1], packs[2], "adamw_small")
    outs_small = [_unpack(a) for a in (gsum, d_s, nm_s, nv_s)]
    for n, t in small_params.items():
        shape = t[0].shape
        vals = []
        for o in outs_small:
            a = o[key_of.get(n, n)]
            if n == "rel_bias":
                a = a.reshape(N_Q_HEADS, N_BUCKETS).T
            else:
                a = a[:, :shape[1]].reshape(shape)
            vals.append(a)
        res[n] = tuple(vals)

    order = ["w_ada", "b_ada", "w_in", "rel_bias", "sinks", "conv_w", "conv_b", "conv_ln_g", "conv_ln_b", "w_pw",
             "b_pw", "w_out", "ln_g", "ln_b"]
    out = [loss, grad_x[None]]
    for j in range(4):
        out += [res[n][j] for n in order]
    return tuple(out)
```

```python
import math

import jax
import jax.numpy as jnp
from jax import lax
from jax.experimental import pallas as pl
from jax.experimental.pallas import tpu as pltpu

F32, BF16, I32 = jnp.float32, jnp.bfloat16, jnp.int32

D_MODEL = 2048
D_ATTN = 1024
D_CONV = 1024
D_KV = 256
HEAD_DIM = 64
N_Q_HEADS = 16
N_KV_HEADS = 4
GQA = 4
BLOCK = 128
CONV_WIDTH = 31
CONV_ROWS = 32
HALO = 32
N_BUCKETS = 32
MAX_DISTANCE = 128
LN_EPS = 1e-5
ALPHA = 2.0 ** 0.25
D_IN = 5632
N_DEV = 8
NEG = -1e30

ADAM_LR, ADAM_B1, ADAM_B2, ADAM_EPS, ADAM_WD, ADAM_STEP = 0.001, 0.9, 0.999, 1e-08, 0.01, 10

NT_DIMS = (((1,), (1,)), ((), ()))
TN_DIMS = (((0,), (0,)), ((), ()))
MIB = 1 << 20


def _pc(body, name, **kw):
    return pl.pallas_call(body, name=name, **kw)


def _cp(vmem_mib=None, sem=None):
    kw = {}
    if vmem_mib is not None:
        kw["vmem_limit_bytes"] = vmem_mib * MIB
    if sem is not None:
        kw["dimension_semantics"] = sem
    return pltpu.CompilerParams(**kw)


def _sig(x):
    return 1.0 / (1.0 + jnp.exp(-x))


def _dsilu(x, s):
    return s * (1.0 + x * (1.0 - s))


def _me():
    return lax.axis_index("x"), lax.axis_index("y"), lax.axis_index("c")


def _peer(k):
    x, y, c = _me()
    px = 1 - x if k & 4 else x
    py = 1 - y if k & 2 else y
    pc = 1 - c if k & 1 else c
    return (px, py, pc), 4 * px + 2 * py + pc


def _remote(src, dst, send_sem, recv_sem, dev):
    return pltpu.make_async_remote_copy(src_ref=src, dst_ref=dst, send_sem=send_sem, recv_sem=recv_sem,
                                        device_id=dev, device_id_type=pl.DeviceIdType.MESH)


HBM_SPEC = pl.BlockSpec(memory_space=pl.ANY)
VMEM_SPEC = pl.BlockSpec(memory_space=pltpu.VMEM)


def _comm_scratch(nt):
    return [pltpu.SemaphoreType.DMA((nt, N_DEV - 1)), pltpu.SemaphoreType.DMA((nt, N_DEV - 1)),
            pltpu.SemaphoreType.DMA((nt,))]


def _gather_shapes(shards):
    return [jax.ShapeDtypeStruct((N_DEV * s.shape[0], s.shape[1]), s.dtype) for s in shards]


def _block(ref, blk, rows):
    return ref.at[pl.ds(pl.multiple_of(blk * rows, 8), rows), :]


def _gather_copies(src, dst, sems):
    send_sems, recv_sems, local_sems = sems
    x, y, c = _me()
    me = 4 * x + 2 * y + c
    nt = len(src)
    local = [pltpu.make_async_copy(src[t], _block(dst[t], me, src[t].shape[0]), local_sems.at[t]) for t in range(nt)]
    sends, arrivals = [], []
    for k in range(1, N_DEV):
        dev, blk = _peer(k)
        for t in range(nt):
            r = src[t].shape[0]
            pair = (send_sems.at[t, k - 1], recv_sems.at[t, k - 1], dev)
            sends.append(_remote(src[t], _block(dst[t], me, r), *pair))
            arrivals.append(_remote(src[t], _block(dst[t], blk, r), *pair))
    return local, sends, arrivals


def _scatter_copies(src, dst, sems):
    send_sems, recv_sems, local_sems = sems
    x, y, c = _me()
    me = 4 * x + 2 * y + c
    nt = len(src)
    rows = [s.shape[0] // N_DEV for s in src]
    local = [pltpu.make_async_copy(_block(src[t], me, rows[t]), _block(dst[t], me, rows[t]), local_sems.at[t])
             for t in range(nt)]
    sends, arrivals = [], []
    for k in range(1, N_DEV):
        dev, blk = _peer(k)
        for t in range(nt):
            pair = (send_sems.at[t, k - 1], recv_sems.at[t, k - 1], dev)
            sends.append(_remote(_block(src[t], blk, rows[t]), _block(dst[t], me, rows[t]), *pair))
            arrivals.append(_remote(_block(src[t], me, rows[t]), _block(dst[t], blk, rows[t]), *pair))
    return local, sends, arrivals


def _comm_start(cps):
    local, sends, _ = cps
    for cp in local + sends:
        cp.start()


def _comm_wait(cps):
    local, sends, arrivals = cps
    for cp in arrivals:
        cp.wait_recv()
    for cp in sends:
        cp.wait_send()
    for cp in local:
        cp.wait()


IN_PIECES = ((0, 1536, BF16), (1536, 1024, F32), (2560, 2048, F32), (4608, 1024, F32))


def _inproj(x, mod, wt, gather):
    S = x.shape[0]
    tm = min(256, S)
    ni = S // tm
    ng = len(gather)
    npc = len(IN_PIECES)

    def body(*refs):
        x_ref, sh_ref, sc_ref, w_ref = refs[:4]
        outs = refs[4 + ng:]
        h_ref, piece_refs = outs[0], outs[1:1 + npc]
        i = pl.program_id(0)
        if ng:
            cps = _gather_copies(refs[4:4 + ng], outs[1 + npc:1 + npc + ng], outs[1 + npc + ng:])

            @pl.when(i == 0)
            def _():
                _comm_start(cps)

        h = (x_ref[...] * (1.0 + sc_ref[...]) + sh_ref[...]).astype(BF16)
        h_ref[...] = h
        for (r0, n, dt), o_ref in zip(IN_PIECES, piece_refs):
            o_ref[...] = lax.dot_general(h, w_ref[r0:r0 + n, :], NT_DIMS, preferred_element_type=F32).astype(dt)
        if ng:
            @pl.when(i == ni - 1)
            def _():
                _comm_wait(cps)

    tile = lambda n: pl.BlockSpec((tm, n), lambda i: (i, 0))
    res = _pc(
        body, "inproj", grid=(ni,),
        in_specs=[tile(D_MODEL),
                  pl.BlockSpec((1, D_MODEL), lambda i: (0, 0)),
                  pl.BlockSpec((1, D_MODEL), lambda i: (0, 1)),
                  pl.BlockSpec((D_IN, D_MODEL), lambda i: (0, 0), pipeline_mode=pl.Buffered(1))] + [HBM_SPEC] * ng,
        out_specs=[tile(D_MODEL)] + [tile(n) for _, n, _ in IN_PIECES] + [HBM_SPEC] * ng,
        out_shape=[jax.ShapeDtypeStruct((S, D_MODEL), BF16)]
        + [jax.ShapeDtypeStruct((S, n), dt) for _, n, dt in IN_PIECES] + _gather_shapes(gather),
        scratch_shapes=_comm_scratch(ng) if ng else [],
        compiler_params=_cp(56, ("arbitrary",)),
    )(x, mod, mod, wt, *gather)
    return res[:1 + npc], res[1 + npc:]


def _matmul_tn(segs, b, name):
    S, N = b.shape
    tm, ts = 512, min(2048, S)
    ns = S // ts
    bounds = []
    t0 = 0
    for a in segs:
        assert a.shape[1] % tm == 0
        bounds.append((t0, a.shape[1] // tm))
        t0 += a.shape[1] // tm
    nseg = len(segs)

    def body(*refs):
        seg_refs, b_ref = refs[:nseg], refs[nseg]
        o_ref, acc = refs[-2], refs[-1]
        t, s = pl.program_id(0), pl.program_id(1)

        @pl.when(s == 0)
        def _():
            acc[...] = jnp.zeros_like(acc)

        for (s0, sn), r in zip(bounds, seg_refs):
            @pl.when((t >= s0) & (t < s0 + sn))
            def _(r=r):
                acc[...] += lax.dot_general(r[...], b_ref[...], TN_DIMS, preferred_element_type=F32)

        @pl.when(s == ns - 1)
        def _():
            o_ref[...] = acc[...].astype(BF16)

    def seg_spec(s0, sn):
        def index(t, s):
            mine = (t >= s0) & (t < s0 + sn)
            return jnp.where(mine, s, 0), jnp.clip(t - s0, 0, sn - 1)
        return pl.BlockSpec((ts, tm), index)

    return _pc(
        body, name, grid=(t0, ns),
        in_specs=[seg_spec(*bd) for bd in bounds] + [pl.BlockSpec((ts, N), lambda t, s: (s, 0))],
        out_specs=pl.BlockSpec((tm, N), lambda t, s: (t, 0)),
        out_shape=jax.ShapeDtypeStruct((t0 * tm, N), BF16),
        scratch_shapes=[pltpu.VMEM((tm, N), F32)],
        compiler_params=_cp(48, ("parallel", "arbitrary")),
    )(*segs, b)


def _bucket_map():
    qi = jnp.arange(BLOCK, dtype=I32)[:, None]
    kj = jnp.arange(2 * BLOCK, dtype=I32)[None, :]
    dist = qi + BLOCK - kj
    in_window = (dist >= 0) & (dist < BLOCK)
    d0 = jnp.maximum(dist, 0)
    max_exact = N_BUCKETS // 2
    d = jnp.maximum(d0, 1).astype(F32)
    large = max_exact + (jnp.log(d / max_exact) / math.log(MAX_DISTANCE / max_exact)
                         * (N_BUCKETS - max_exact)).astype(I32)
    large = jnp.minimum(large, N_BUCKETS - 1)
    bucket = jnp.where(d0 < max_exact, d0, large)
    return jnp.where(in_window, bucket, -1).astype(I32)


def _bias_table(rel_bias, bmap):
    def body(rb_ref, bm_ref, o_ref):
        bm = bm_ref[...]
        for h in range(N_Q_HEADS):
            acc = jnp.full((BLOCK, 2 * BLOCK), NEG, F32)
            for b in range(N_BUCKETS):
                acc = jnp.where(bm == b, rb_ref[b, h], acc)
            o_ref[h] = acc

    return _pc(
        body, "bias_table",
        in_specs=[pl.BlockSpec(memory_space=pltpu.SMEM), VMEM_SPEC],
        out_specs=VMEM_SPEC,
        out_shape=jax.ShapeDtypeStruct((N_Q_HEADS, BLOCK, 2 * BLOCK), F32),
    )(rel_bias, bmap)


def _bias_spec():
    return pl.BlockSpec((N_Q_HEADS, BLOCK, 2 * BLOCK), lambda n: (0, 0, 0))


def _relbias_grad(dbias, bmap):
    def body(db_ref, bm_ref, o_ref):
        bm = bm_ref[...]
        x = db_ref[0]
        lane = lax.broadcasted_iota(I32, (1, 128), 1)
        row = jnp.zeros((1, 128), F32)
        for b in range(N_BUCKETS):
            row = jnp.where(lane == b, jnp.sum(jnp.where(bm == b, x, 0.0)), row)
        o_ref[0] = jnp.broadcast_to(row, (8, 128))

    out = _pc(
        body, "relbias_grad", grid=(N_Q_HEADS,),
        in_specs=[pl.BlockSpec((1, BLOCK, 2 * BLOCK), lambda h: (h, 0, 0)),
                  pl.BlockSpec((BLOCK, 2 * BLOCK), lambda h: (0, 0))],
        out_specs=pl.BlockSpec((1, 8, 128), lambda h: (h, 0, 0)),
        out_shape=jax.ShapeDtypeStruct((N_Q_HEADS, 8, 128), F32),
    )(dbias, bmap)
    return out[:, 0, :]


Q_SCALE = HEAD_DIM ** -0.5
assert math.frexp(Q_SCALE)[0] == 0.5


def _stack_heads(ref, hk, rows=slice(None)):
    return jnp.concatenate([ref[rows, pl.ds(256 * hk + 64 * g, 64)] for g in range(GQA)], axis=0) * Q_SCALE


def _sink_col(sink_ref, hk):
    row = lax.broadcasted_iota(I32, (GQA * BLOCK, 1), 0)
    s = jnp.full((GQA * BLOCK, 1), sink_ref[0, 4 * hk + 3], F32)
    for g in (2, 1, 0):
        s = jnp.where(row < (g + 1) * BLOCK, sink_ref[0, 4 * hk + g], s)
    return s


def _attn_probs(q4, kw, bias_ref, sink_ref, hk, first_mask):
    s = lax.dot_general(q4, kw, NT_DIMS, preferred_element_type=F32)
    s = s + bias_ref[4 * hk:4 * hk + 4].reshape(GQA * BLOCK, 2 * BLOCK)
    if first_mask is not None:
        s = jnp.where(first_mask, s, NEG)
    sink = _sink_col(sink_ref, hk)
    m = jnp.maximum(jnp.max(s, axis=1, keepdims=True), sink)
    e = jnp.exp(s - m)
    es = jnp.exp(sink - m)
    inv = 1.0 / (jnp.sum(e, axis=1, keepdims=True) + es)
    return e * inv, es * inv


def _attn_bwd(qkv, g_attn, a_out, dycat, bias, sinks, scatter=()):
    S = qkv.shape[0]
    nb = S // BLOCK
    R = GQA * BLOCK
    ns = len(scatter)

    def body(*refs):
        (q_ref, kc_ref, kp_ref, vc_ref, vp_ref, g_ref, a_ref, dy_ref, bias_ref, sink_ref) = refs[:10]
        dqkv_ref, dg_ref, dbias_ref, dsink_ref = refs[10 + ns:14 + ns]
        dq_scr, dq_new, dk_scr, dv_scr, dkw_scr, dvw_scr, ds_scr = refs[14 + 2 * ns:21 + 2 * ns]
        n = pl.program_id(0)
        if ns:
            cps = _scatter_copies(refs[10:10 + ns], refs[14 + ns:14 + 2 * ns], refs[21 + 2 * ns:])

            @pl.when(n == 0)
            def _():
                _comm_start(cps)

        @pl.when(n == 0)
        def _():
            dbias_ref[...] = jnp.zeros_like(dbias_ref)
            ds_scr[...] = jnp.zeros_like(ds_scr)
            dq_scr[...] = jnp.zeros_like(dq_scr)
            dk_scr[...] = jnp.zeros_like(dk_scr)
            dv_scr[...] = jnp.zeros_like(dv_scr)

        @pl.when(n < nb)
        def _():
            kj = lax.broadcasted_iota(I32, (R, 2 * BLOCK), 1)
            first_mask = (n > 0) | (kj >= BLOCK)
            for hk in range(N_KV_HEADS):
                q4 = _stack_heads(q_ref, hk)
                ks = pl.ds(64 * hk, 64)
                kw = jnp.concatenate([kp_ref[:, ks], kc_ref[:, ks]], axis=0)
                vw = jnp.concatenate([vp_ref[:, ks], vc_ref[:, ks]], axis=0)
                p, psink = _attn_probs(q4, kw, bias_ref, sink_ref, hk, first_mask)
                da_parts, a_parts = [], []
                for g in range(GQA):
                    sl = pl.ds(256 * hk + 64 * g, 64)
                    gg = g_ref[:, sl]
                    sg = _sig(gg)
                    dyg = dy_ref[:, sl]
                    ag = a_ref[:, sl]
                    da_parts.append(dyg * (gg * sg))
                    a_parts.append(ag)
                    dg_ref[:, sl] = (dyg * ag * _dsilu(gg, sg)).astype(BF16)
                da4 = jnp.concatenate(da_parts, axis=0)
                a4 = jnp.concatenate(a_parts, axis=0)
                delta = jnp.sum(da4 * a4, axis=1, keepdims=True)
                da4b = da4.astype(BF16)
                dp = lax.dot_general(da4b, vw, NT_DIMS, preferred_element_type=F32)
                ds = p * (dp - delta)
                ds_scr[hk] += -psink * delta
                dbias_ref[4 * hk:4 * hk + 4] += ds.reshape(GQA, BLOCK, 2 * BLOCK)
                dsb = ds.astype(BF16)
                dq4 = jnp.dot(dsb, kw, preferred_element_type=F32) * Q_SCALE
                for g in range(GQA):
                    dq_new[:, pl.ds(256 * hk + 64 * g, 64)] = dq4[BLOCK * g:BLOCK * (g + 1)].astype(BF16)
                dkw_scr[:, ks] = lax.dot_general(dsb, q4, TN_DIMS, preferred_element_type=F32)
                dvw_scr[:, ks] = lax.dot_general(p.astype(BF16), da4b, TN_DIMS, preferred_element_type=F32)

        @pl.when(n == nb)
        def _():
            dkw_scr[0:BLOCK, :] = jnp.zeros((BLOCK, D_KV), F32)
            dvw_scr[0:BLOCK, :] = jnp.zeros((BLOCK, D_KV), F32)

        dqkv_ref[:, 0:D_ATTN] = dq_scr[...]
        dqkv_ref[:, D_ATTN:D_ATTN + D_KV] = (dk_scr[...] + dkw_scr[0:BLOCK, :]).astype(BF16)
        dqkv_ref[:, D_ATTN + D_KV:D_ATTN + 2 * D_KV] = (dv_scr[...] + dvw_scr[0:BLOCK, :]).astype(BF16)
        dq_scr[...] = dq_new[...]
        dk_scr[...] = dkw_scr[BLOCK:2 * BLOCK, :]
        dv_scr[...] = dvw_scr[BLOCK:2 * BLOCK, :]

        @pl.when(n == nb)
        def _():
            lane = lax.broadcasted_iota(I32, (1, 128), 1)
            row = jnp.zeros((1, 128), F32)
            for hk in range(N_KV_HEADS):
                col = ds_scr[hk]
                for g in range(GQA):
                    row = jnp.where(lane == 4 * hk + g, jnp.sum(col[BLOCK * g:BLOCK * (g + 1)]), row)
            dsink_ref[...] = jnp.broadcast_to(row, (8, 128))
            if ns:
                _comm_wait(cps)

    cur = lambda n: jnp.minimum(n, nb - 1)
    prev = lambda n: jnp.clip(n - 1, 0, nb - 1)
    res = _pc(
        body, "attn_bwd", grid=(nb + 1,),
        in_specs=[pl.BlockSpec((BLOCK, D_ATTN), lambda n: (cur(n), 0)),
                  pl.BlockSpec((BLOCK, D_KV), lambda n: (cur(n), 4)),
                  pl.BlockSpec((BLOCK, D_KV), lambda n: (prev(n), 4)),
                  pl.BlockSpec((BLOCK, D_KV), lambda n: (cur(n), 5)),
                  pl.BlockSpec((BLOCK, D_KV), lambda n: (prev(n), 5)),
                  pl.BlockSpec((BLOCK, D_ATTN), lambda n: (cur(n), 0)),
                  pl.BlockSpec((BLOCK, D_ATTN), lambda n: (cur(n), 0)),
                  pl.BlockSpec((BLOCK, D_ATTN), lambda n: (cur(n), 0)),
                  _bias_spec(),
                  pl.BlockSpec(memory_space=pltpu.SMEM)] + [HBM_SPEC] * ns,
        out_specs=[pl.BlockSpec((BLOCK, D_ATTN + 2 * D_KV), lambda n: (prev(n), 0)),
                   pl.BlockSpec((BLOCK, D_ATTN), lambda n: (cur(n), 0)),
                   pl.BlockSpec((N_Q_HEADS, BLOCK, 2 * BLOCK), lambda n: (0, 0, 0)),
                   pl.BlockSpec((8, 128), lambda n: (0, 0))] + [HBM_SPEC] * ns,
        out_shape=[jax.ShapeDtypeStruct((S, D_ATTN + 2 * D_KV), BF16),
                   jax.ShapeDtypeStruct((S, D_ATTN), BF16),
                   jax.ShapeDtypeStruct((N_Q_HEADS, BLOCK, 2 * BLOCK), F32),
                   jax.ShapeDtypeStruct((8, 128), F32)] + [jax.ShapeDtypeStruct(f.shape, f.dtype) for f in scatter],
        scratch_shapes=[pltpu.VMEM((BLOCK, D_ATTN), BF16), pltpu.VMEM((BLOCK, D_ATTN), BF16),
                        pltpu.VMEM((BLOCK, D_KV), F32), pltpu.VMEM((BLOCK, D_KV), F32),
                        pltpu.VMEM((2 * BLOCK, D_KV), F32), pltpu.VMEM((2 * BLOCK, D_KV), F32),
                        pltpu.VMEM((N_KV_HEADS, R, 1), F32)] + (_comm_scratch(ns) if ns else []),
        compiler_params=_cp(48, ("arbitrary",)),
    )(qkv, qkv, qkv, qkv, qkv, g_attn, a_out, dycat, bias, sinks, *scatter)
    return res[:4], res[4:]


def _conv_tile(S):
    return min(256, S)


def _shifted(win, s):
    return win if s == 0 else pltpu.roll(win, win.shape[0] - s, axis=0)


def _conv_taps(win, cw_ref, lanes, rows):
    acc = jnp.zeros((rows, win.shape[1]), F32)
    for s in range(8):
        ws = _shifted(win, s)
        for aa in range(5):
            j = 8 * aa + s - 2
            if 0 <= j < CONV_WIDTH:
                acc = acc + ws[8 * aa:8 * aa + rows] * cw_ref[j:j + 1, lanes]
    return acc


def _mixer_fwd(qkv, g_attn, glu, g_conv, bias, sinks, conv_w, conv_b, ln_g, ln_b, w_pw, b_pw):
    S = qkv.shape[0]
    RB = 2
    TR = RB * BLOCK
    nb = S // TR
    hb = TR // HALO
    LG = D_CONV // (RB * N_KV_HEADS)

    def body(q_ref, kc_ref, kp_ref, vc_ref, vp_ref, g_ref, bias_ref, sink_ref,
             a_in, ah_in, b_in, bh_in, gc_ref, cw_ref, cb_ref, lg_ref, lb_ref, wpw_ref, bpw_ref,
             y_ref, a_ref, u1_ref, p_ref, win):
        n = pl.program_id(0)
        kj = lax.broadcasted_iota(I32, (GQA * BLOCK, 2 * BLOCK), 1)
        first_mask = (n > 0) | (kj >= BLOCK)
        win[0:HALO, :] = jnp.where(n > 0, ah_in[...] * _sig(bh_in[...]), 0.0)
        win[HALO:HALO + TR, :] = a_in[...] * _sig(b_in[...])
        piece = 0
        for s in range(RB):
            rows = slice(BLOCK * s, BLOCK * (s + 1))
            before = slice(BLOCK * (s - 1), BLOCK * s)
            for hk in range(N_KV_HEADS):
                q4 = _stack_heads(q_ref, hk, rows)
                ks = pl.ds(64 * hk, 64)
                k_prev, v_prev = (kp_ref[:, ks], vp_ref[:, ks]) if s == 0 else (kc_ref[before, ks], vc_ref[before, ks])
                kw = jnp.concatenate([k_prev, kc_ref[rows, ks]], axis=0)
                vw = jnp.concatenate([v_prev, vc_ref[rows, ks]], axis=0)
                p, _ = _attn_probs(q4, kw, bias_ref, sink_ref, hk, first_mask if s == 0 else None)
                o4 = jnp.dot(p.astype(BF16), vw, preferred_element_type=F32)
                for g in range(GQA):
                    sl = pl.ds(256 * hk + 64 * g, 64)
                    og = o4[BLOCK * g:BLOCK * (g + 1)]
                    gg = g_ref[rows, sl]
                    a_ref[rows, sl] = og
                    y_ref[rows, sl] = (og * (gg * _sig(gg))).astype(BF16)
                lanes = pl.ds(LG * piece, LG)
                u1_ref[:, lanes] = _conv_taps(win[:, lanes], cw_ref, lanes, TR) + cb_ref[:, lanes]
                piece += 1
        u1 = u1_ref[...]
        mu = jnp.mean(u1, axis=1, keepdims=True)
        uc = u1 - mu
        rstd = lax.rsqrt(jnp.mean(uc * uc, axis=1, keepdims=True) + LN_EPS)
        u2 = uc * rstd * lg_ref[...] + lb_ref[...]
        u3 = (u2 * _sig(u2)).astype(BF16)
        pw = jnp.dot(u3, wpw_ref[...], preferred_element_type=F32) + bpw_ref[...]
        p_ref[...] = pw
        gc = gc_ref[...]
        y_ref[:, D_ATTN:] = (pw * (gc * _sig(gc))).astype(BF16)

    prev = lambda n: jnp.maximum(RB * n - 1, 0)
    halo = lambda n: jnp.maximum(n * hb - 1, 0)
    vec = pl.BlockSpec((1, D_CONV), lambda n: (0, 0))
    blk = lambda w, j: pl.BlockSpec((TR, w), lambda n: (n, j))
    return _pc(
        body, "mixer_fwd", grid=(nb,),
        in_specs=[blk(D_ATTN, 0),
                  blk(D_KV, 4), pl.BlockSpec((BLOCK, D_KV), lambda n: (prev(n), 4)),
                  blk(D_KV, 5), pl.BlockSpec((BLOCK, D_KV), lambda n: (prev(n), 5)),
                  blk(D_ATTN, 0),
                  _bias_spec(),
                  pl.BlockSpec(memory_space=pltpu.SMEM),
                  blk(D_CONV, 0), pl.BlockSpec((HALO, D_CONV), lambda n: (halo(n), 0)),
                  blk(D_CONV, 1), pl.BlockSpec((HALO, D_CONV), lambda n: (halo(n), 1)),
                  blk(D_CONV, 0),
                  pl.BlockSpec((CONV_ROWS, D_CONV), lambda n: (0, 0)),
                  vec, vec, vec,
                  pl.BlockSpec((D_CONV, D_CONV), lambda n: (0, 0)),
                  vec],
        out_specs=[blk(2 * D_ATTN, 0), blk(D_ATTN, 0), blk(D_CONV, 0), blk(D_CONV, 0)],
        out_shape=[jax.ShapeDtypeStruct((S, 2 * D_ATTN), BF16),
                   jax.ShapeDtypeStruct((S, D_ATTN), F32),
                   jax.ShapeDtypeStruct((S, D_CONV), F32),
                   jax.ShapeDtypeStruct((S, D_CONV), F32)],
        scratch_shapes=[pltpu.VMEM((TR + HALO, D_CONV), F32)],
        compiler_params=_cp(56, ("parallel",)),
    )(qkv, qkv, qkv, qkv, qkv, g_attn, bias, sinks, glu, glu, glu, glu, g_conv, conv_w, conv_b, ln_g, ln_b, w_pw,
      b_pw)


def _conv_bwd_a(dycat, g_conv, p_out, u1, ln_g, ln_b, w_pw):
    S = u1.shape[0]
    T = _conv_tile(S)
    nt = S // T

    def body(dy_ref, gc_ref, p_ref, u1_ref, lg_ref, lb_ref, wpw_ref,
             gw_ref, dgc_ref, du1_ref, gbpw_ref, glg_ref, glb_ref, gcb_ref, gw_acc):
        i = pl.program_id(0)

        @pl.when(i == 0)
        def _():
            for r in (gbpw_ref, glg_ref, glb_ref, gcb_ref, gw_acc):
                r[...] = jnp.zeros_like(r)

        dy = dy_ref[...]
        gc = gc_ref[...]
        sg = _sig(gc)
        dp = dy * (gc * sg)
        dgc_ref[...] = (dy * p_ref[...] * _dsilu(gc, sg)).astype(BF16)
        gbpw_ref[...] += jnp.sum(dp, axis=0, keepdims=True)
        dpb = dp.astype(BF16)
        du3 = lax.dot_general(dpb, wpw_ref[...], NT_DIMS, preferred_element_type=F32)
        u1 = u1_ref[...]
        mu = jnp.mean(u1, axis=1, keepdims=True)
        uc = u1 - mu
        rstd = lax.rsqrt(jnp.mean(uc * uc, axis=1, keepdims=True) + LN_EPS)
        uh = uc * rstd
        lg = lg_ref[...]
        u2 = uh * lg + lb_ref[...]
        s2 = _sig(u2)
        gw_acc[...] += lax.dot_general((u2 * s2).astype(BF16), dpb, TN_DIMS, preferred_element_type=F32)
        du2 = du3 * _dsilu(u2, s2)
        glg_ref[...] += jnp.sum(du2 * uh, axis=0, keepdims=True)
        glb_ref[...] += jnp.sum(du2, axis=0, keepdims=True)
        duh = du2 * lg
        du1 = rstd * (duh - jnp.mean(duh, axis=1, keepdims=True) - uh * jnp.mean(duh * uh, axis=1, keepdims=True))
        du1_ref[...] = du1
        gcb_ref[...] += jnp.sum(du1, axis=0, keepdims=True)

        @pl.when(i == nt - 1)
        def _():
            gw_ref[...] = gw_acc[...].astype(BF16)

    vec = pl.BlockSpec((1, D_CONV), lambda i: (0, 0))
    tile = pl.BlockSpec((T, D_CONV), lambda i: (i, 0))
    square = pl.BlockSpec((D_CONV, D_CONV), lambda i: (0, 0))
    vshape = jax.ShapeDtypeStruct((1, D_CONV), F32)
    return _pc(
        body, "conv_bwd_a", grid=(nt,),
        in_specs=[pl.BlockSpec((T, D_CONV), lambda i: (i, 1)), tile, tile, tile, vec, vec, square],
        out_specs=[square, tile, tile, vec, vec, vec, vec],
        out_shape=[jax.ShapeDtypeStruct((D_CONV, D_CONV), BF16), jax.ShapeDtypeStruct((S, D_CONV), BF16),
                   jax.ShapeDtypeStruct((S, D_CONV), F32), vshape, vshape, vshape, vshape],
        scratch_shapes=[pltpu.VMEM((D_CONV, D_CONV), F32)],
        compiler_params=_cp(48, ("arbitrary",)),
    )(dycat, g_conv, p_out, u1, ln_g, ln_b, w_pw)


def _conv_bwd_b(du1, glu, conv_w):
    S = du1.shape[0]
    T = _conv_tile(S)
    hb = T // HALO
    nt = S // T
    last_h = S // HALO - 1

    def body(du_ref, dun_ref, a_ref, b_ref, cw_ref, dab_ref, gw_ref):
        i = pl.program_id(0)

        @pl.when(i == 0)
        def _():
            gw_ref[...] = jnp.zeros_like(gw_ref)

        for lg in range(D_CONV // 128):
            lanes = pl.ds(128 * lg, 128)
            a = a_ref[:, lanes]
            sb = _sig(b_ref[:, lanes])
            u0 = a * sb
            du1n = jnp.where(i < nt - 1, dun_ref[:, lanes], 0.0)
            win2 = jnp.concatenate([du_ref[:, lanes], du1n], axis=0)
            acc = jnp.zeros((T, 128), F32)
            for s in range(8):
                w2 = _shifted(win2, s)
                for aa in range(4):
                    j = CONV_WIDTH - 1 - (8 * aa + s)
                    if 0 <= j < CONV_WIDTH:
                        xo = w2[8 * aa:8 * aa + T]
                        acc = acc + xo * cw_ref[j:j + 1, lanes]
                        gw_ref[j:j + 1, lanes] += jnp.sum(xo * u0, axis=0, keepdims=True)
            dab_ref[:, lanes] = (acc * sb).astype(BF16)
            dab_ref[:, pl.ds(D_CONV + 128 * lg, 128)] = (acc * a * sb * (1.0 - sb)).astype(BF16)

    nxt = lambda i: jnp.minimum((i + 1) * hb, last_h)
    return _pc(
        body, "conv_bwd_b", grid=(nt,),
        in_specs=[pl.BlockSpec((T, D_CONV), lambda i: (i, 0)),
                  pl.BlockSpec((HALO, D_CONV), lambda i: (nxt(i), 0)),
                  pl.BlockSpec((T, D_CONV), lambda i: (i, 0)),
                  pl.BlockSpec((T, D_CONV), lambda i: (i, 1)),
                  pl.BlockSpec((CONV_ROWS, D_CONV), lambda i: (0, 0))],
        out_specs=[pl.BlockSpec((T, 2 * D_CONV), lambda i: (i, 0)),
                   pl.BlockSpec((CONV_ROWS, D_CONV), lambda i: (0, 0))],
        out_shape=[jax.ShapeDtypeStruct((S, 2 * D_CONV), BF16),
                   jax.ShapeDtypeStruct((CONV_ROWS, D_CONV), F32)],
        compiler_params=_cp(48, ("arbitrary",)),
    )(du1, du1, glu, glu, conv_w)


def _outproj_ln(ycat, w_out, x, target, mod, ln_g, ln_b):
    S = x.shape[0]
    tm = min(256, S)

    def body(yc_ref, w_ref, x_ref, t_ref, gate_ref, lg_ref, lb_ref,
             dz_ref, dy_ref, dyc_ref, loss_ref, glg_ref, glb_ref, dgate_ref):
        i = pl.program_id(0)

        @pl.when(i == 0)
        def _():
            for r in (loss_ref, glg_ref, glb_ref, dgate_ref):
                r[...] = jnp.zeros_like(r)

        w = w_ref[...]
        y = jnp.dot(yc_ref[...], w, preferred_element_type=F32)
        gate = gate_ref[...]
        z = ALPHA * x_ref[...] + gate * y
        mu = jnp.mean(z, axis=1, keepdims=True)
        zc = z - mu
        rstd = lax.rsqrt(jnp.mean(zc * zc, axis=1, keepdims=True) + LN_EPS)
        zh = zc * rstd
        lg = lg_ref[...]
        err = zh * lg + lb_ref[...] - t_ref[...]
        loss_ref[...] += 0.5 * jnp.sum(jnp.sum(err * err, axis=1, keepdims=True)) / D_MODEL
        dout = err * (1.0 / D_MODEL)
        glg_ref[...] += jnp.sum(dout * zh, axis=0, keepdims=True)
        glb_ref[...] += jnp.sum(dout, axis=0, keepdims=True)
        dzh = dout * lg
        dz = rstd * (dzh - jnp.mean(dzh, axis=1, keepdims=True) - zh * jnp.mean(dzh * zh, axis=1, keepdims=True))
        dz_ref[...] = dz
        dgate_ref[...] += jnp.sum(dz * y, axis=0, keepdims=True)
        dy = (dz * gate).astype(BF16)
        dy_ref[...] = dy
        dyc_ref[...] = lax.dot_general(dy, w, NT_DIMS, preferred_element_type=F32).astype(BF16)

    vec = pl.BlockSpec((1, D_MODEL), lambda i: (0, 0))
    tile = pl.BlockSpec((tm, D_MODEL), lambda i: (i, 0))
    vshape = jax.ShapeDtypeStruct((1, D_MODEL), F32)
    return _pc(
        body, "outproj_ln", grid=(S // tm,),
        in_specs=[tile, pl.BlockSpec((D_MODEL, D_MODEL), lambda i: (0, 0), pipeline_mode=pl.Buffered(1)), tile, tile,
                  pl.BlockSpec((1, D_MODEL), lambda i: (0, 2)), vec, vec],
        out_specs=[tile, tile, tile, pl.BlockSpec((8, 128), lambda i: (0, 0)), vec, vec, vec],
        out_shape=[jax.ShapeDtypeStruct((S, D_MODEL), F32), jax.ShapeDtypeStruct((S, D_MODEL), BF16),
                   jax.ShapeDtypeStruct((S, D_MODEL), BF16), jax.ShapeDtypeStruct((8, 128), F32),
                   vshape, vshape, vshape],
        compiler_params=_cp(56, ("arbitrary",)),
    )(ycat, w_out, x, target, mod, ln_g, ln_b)


def _dh_kernel(segs, wt, dz, x, mod, scatter=()):
    S = x.shape[0]
    tm = min(256, S)
    row0 = [0]
    for a in segs:
        row0.append(row0[-1] + a.shape[1])
    assert row0[-1] == wt.shape[0]
    nseg = len(segs)
    ns = len(scatter)
    ni = S // tm

    def body(*refs):
        seg_refs = refs[:nseg]
        w_hbm, dz_ref, x_ref, sc_ref = refs[nseg:nseg + 4]
        outs = refs[nseg + 4 + ns:]
        gx_ref, dsh_ref, dsc_ref = outs[:3]
        w_ref, w_sems = outs[3 + ns:5 + ns]
        i = pl.program_id(0)
        if ns:
            cps = _scatter_copies(refs[nseg + 4:nseg + 4 + ns], outs[3:3 + ns], outs[5 + ns:])

        def w_load(t):
            rows = slice(row0[t], row0[t + 1])
            return pltpu.make_async_copy(w_hbm.at[rows, :], w_ref.at[rows, :], w_sems.at[t])

        @pl.when(i == 0)
        def _():
            for t in range(nseg):
                w_load(t).start()
            dsh_ref[...] = jnp.zeros_like(dsh_ref)
            dsc_ref[...] = jnp.zeros_like(dsc_ref)
            if ns:
                _comm_start(cps)

        dh = None
        for t in range(nseg):
            @pl.when(i == 0)
            def _(t=t):
                w_load(t).wait()

            part = jnp.dot(seg_refs[t][...], w_ref[row0[t]:row0[t + 1], :], preferred_element_type=F32)
            dh = part if dh is None else dh + part
        gx_ref[...] = ALPHA * dz_ref[...] + dh * (1.0 + sc_ref[...])
        dsh_ref[...] += jnp.sum(dh, axis=0, keepdims=True)
        dsc_ref[...] += jnp.sum(dh * x_ref[...], axis=0, keepdims=True)

        if ns:
            @pl.when(i == ni - 1)
            def _():
                _comm_wait(cps)

    tile = pl.BlockSpec((tm, D_MODEL), lambda i: (i, 0))
    vec = pl.BlockSpec((1, D_MODEL), lambda i: (0, 0))
    vshape = jax.ShapeDtypeStruct((1, D_MODEL), F32)
    res = _pc(
        body, "dh_gradx", grid=(ni,),
        in_specs=[pl.BlockSpec((tm, a.shape[1]), lambda i: (i, 0)) for a in segs] + [
            HBM_SPEC, tile, tile,
            pl.BlockSpec((1, D_MODEL), lambda i: (0, 1))] + [HBM_SPEC] * ns,
        out_specs=[tile, vec, vec] + [HBM_SPEC] * ns,
        out_shape=[jax.ShapeDtypeStruct((S, D_MODEL), F32), vshape, vshape]
        + [jax.ShapeDtypeStruct(f.shape, f.dtype) for f in scatter],
        scratch_shapes=[pltpu.VMEM(wt.shape, BF16), pltpu.SemaphoreType.DMA((nseg,))]
        + (_comm_scratch(ns) if ns else []),
        compiler_params=_cp(58, ("arbitrary",)),
    )(*segs, wt, dz, x, mod, *scatter)
    return res[:3], res[3:]


def _row_tile(rows, cols):
    if rows * cols * 4 <= 2 * MIB or rows % 8:
        return rows
    tr = max(8, (2 * MIB // (cols * 4)) // 8 * 8)
    while rows % tr:
        tr -= 8
    return tr


def _sum8(recv, name):
    _, R, C = recv.shape
    tr = _row_tile(R, C)

    def body(r_ref, o_ref):
        acc = r_ref[0].astype(F32)
        for d in range(1, N_DEV):
            acc = acc + r_ref[d].astype(F32)
        o_ref[...] = acc

    return _pc(
        body, name, grid=(R // tr,),
        in_specs=[pl.BlockSpec((N_DEV, tr, C), lambda i: (0, i, 0))],
        out_specs=pl.BlockSpec((tr, C), lambda i: (i, 0)),
        out_shape=jax.ShapeDtypeStruct((R, C), F32),
        compiler_params=_cp(40, ("parallel",)),
    )(recv)


def _adam_math(w, g, m, v):
    m = ADAM_B1 * m + (1.0 - ADAM_B1) * g
    v = ADAM_B2 * v + (1.0 - ADAM_B2) * (g * g)
    m_hat = m / (1.0 - ADAM_B1 ** ADAM_STEP)
    v_hat = v / (1.0 - ADAM_B2 ** ADAM_STEP)
    return -ADAM_LR * (m_hat / (jnp.sqrt(v_hat) + ADAM_EPS) + ADAM_WD * w), m, v


def _adamw_rows(gsum, offsets, params, name):
    npar = len(params)

    def body(g_ref, *refs):
        ins, outs = refs[:3 * npar], refs[3 * npar:]
        for p in range(npar):
            w_ref, m_ref, v_ref = ins[3 * p:3 * p + 3]
            go_ref, d_ref, nm_ref, nv_ref = outs[4 * p:4 * p + 4]
            g = g_ref[:, offsets[p]:offsets[p] + w_ref.shape[1]]
            go_ref[...] = g
            d_ref[...], nm_ref[...], nv_ref[...] = _adam_math(w_ref[...], g, m_ref[...], v_ref[...])

    flat = [a for t in params for a in t]
    res = _pc(
        body, name,
        in_specs=[VMEM_SPEC] * (1 + 3 * npar), out_specs=[VMEM_SPEC] * (4 * npar),
        out_shape=[jax.ShapeDtypeStruct(t[0].shape, F32) for t in params for _ in range(4)],
    )(gsum, *flat)
    return [tuple(res[4 * p:4 * p + 4]) for p in range(npar)]


def _adamw(w, g, m, v, name):
    R, C = w.shape
    tr = _row_tile(R, C)
    parts = g.ndim == 3

    def body(w_ref, g_ref, m_ref, v_ref, *out_refs):
        d_ref, nm_ref, nv_ref = out_refs[-3:]
        if parts:
            g_ = g_ref[0].astype(F32)
            for d in range(1, N_DEV):
                g_ = g_ + g_ref[d].astype(F32)
            out_refs[0][...] = g_
        else:
            g_ = g_ref[...]
        d_ref[...], nm_ref[...], nv_ref[...] = _adam_math(w_ref[...], g_, m_ref[...], v_ref[...])

    spec = pl.BlockSpec((tr, C), lambda i: (i, 0))
    gspec = pl.BlockSpec((N_DEV, tr, C), lambda i: (0, i, 0)) if parts else spec
    shape = jax.ShapeDtypeStruct((R, C), F32)
    nout = 4 if parts else 3
    res = _pc(
        body, name, grid=(R // tr,),
        in_specs=[spec, gspec, spec, spec], out_specs=[spec] * nout, out_shape=[shape] * nout,
        compiler_params=_cp(40, ("parallel",)),
    )(w, g, m, v)
    return tuple(res) if parts else (g, *res)


def _small_mm(a, b, name):
    def body(a_ref, b_ref, o_ref):
        o_ref[...] = jnp.dot(a_ref[...], b_ref[...], preferred_element_type=F32)

    return _pc(body, name, out_shape=jax.ShapeDtypeStruct((a.shape[0], b.shape[1]), F32),
               compiler_params=_cp(40))(a, b)


def _all_gather_two_level(shards, name, ada=None):
    nt = len(shards)
    NK = 9
    ada_out, ada_scratch = _ada_shapes(ada[1].shape[1]) if ada else ([], [])
    na = 3 if ada else 0

    def body(*refs):
        src, dst = refs[:nt], refs[nt + na:2 * nt + na]
        send_sems, recv_sems, local_sems = refs[2 * nt + na + len(ada_out):2 * nt + na + len(ada_out) + 3]
        x, y, c = _me()
        me, sibling = (x, y, c), (x, y, 1 - c)
        xn, yn, dg = (1 - x, y), (x, 1 - y), (1 - x, 1 - y)

        def part(t, dev, half=None):
            r = src[t].shape[0]
            blk = 4 * dev[0] + 2 * dev[1] + dev[2]
            if half is None:
                return dst[t].at[pl.ds(pl.multiple_of(blk * r, 16), r), :]
            return dst[t].at[pl.ds(pl.multiple_of(blk * r + half * (r // 2), 16), r // 2), :]

        def copy(t, k, to, rows, from_src=False):
            return _remote(src[t] if from_src else rows, rows, send_sems.at[t, k], recv_sems.at[t, k], to)

        local = [pltpu.make_async_copy(src[t], part(t, me), local_sems.at[t]) for t in range(nt)]
        for cp in local:
            cp.start()
        sends = []
        for t in range(nt):
            sends += [copy(t, 0, sibling, part(t, me), True), copy(t, 1, (*xn, c), part(t, me), True),
                      copy(t, 2, (*yn, c), part(t, me), True)]
        for cp in sends:
            cp.start()
        if ada:
            _ada_exchange(*refs[nt:nt + na], *refs[2 * nt + na:2 * nt + na + len(ada_out)],
                          *refs[2 * nt + na + len(ada_out) + 3:])

        def arrive_then_pass(t, k, rows, passes):
            copy(t, k, me, rows).wait_recv()
            for cp in passes:
                cp.start()
                sends.append(cp)

        for t in range(nt):
            arrive_then_pass(t, 1, part(t, (*xn, c)), [copy(t, 5, sibling, part(t, (*xn, c))),
                                                       copy(t, 3, (*yn, c), part(t, (*xn, c), 0))])
            arrive_then_pass(t, 2, part(t, (*yn, c)), [copy(t, 6, sibling, part(t, (*yn, c))),
                                                       copy(t, 4, (*xn, c), part(t, (*yn, c), 1))])
            arrive_then_pass(t, 3, part(t, (*dg, c), 0), [copy(t, 7, sibling, part(t, (*dg, c), 0))])
            arrive_then_pass(t, 4, part(t, (*dg, c), 1), [copy(t, 8, sibling, part(t, (*dg, c), 1))])
        for t in range(nt):
            copy(t, 0, me, part(t, sibling)).wait_recv()
            copy(t, 5, me, part(t, (*xn, 1 - c))).wait_recv()
            copy(t, 6, me, part(t, (*yn, 1 - c))).wait_recv()
            copy(t, 7, me, part(t, (*dg, 1 - c), 0)).wait_recv()
            copy(t, 8, me, part(t, (*dg, 1 - c), 1)).wait_recv()
        for cp in sends:
            cp.wait_send()
        for cp in local:
            cp.wait()

    res = _pc(
        body, name,
        in_specs=[HBM_SPEC] * nt + [VMEM_SPEC] * na, out_specs=[HBM_SPEC] * nt + [VMEM_SPEC] * len(ada_out),
        out_shape=_gather_shapes(shards) + ada_out,
        scratch_shapes=[pltpu.SemaphoreType.DMA((nt, NK)), pltpu.SemaphoreType.DMA((nt, NK)),
                        pltpu.SemaphoreType.DMA((nt,))] + ada_scratch,
        compiler_params=pltpu.CompilerParams(has_side_effects=True, vmem_limit_bytes=40 * MIB),
    )(*shards, *(ada or ()))
    return res[:nt], res[nt:]


def _ada_shapes(ncol):
    out = [jax.ShapeDtypeStruct((N_DEV, 1, ncol), F32), jax.ShapeDtypeStruct((N_DEV, 1, D_MODEL), F32)]
    scratch = [pltpu.VMEM((1, D_MODEL), F32), pltpu.VMEM((16, D_MODEL), F32),
               pltpu.VMEM((N_DEV, 1, ncol), F32)] + [pltpu.SemaphoreType.DMA((N_DEV - 1,))] * 4
    return out, scratch


def _ada_exchange(c_ref, w_ref, b_ref, mod_ref, call_ref, cact, cmat, mloc, send1, recv1, send2, recv2):
    x, y, z = _me()
    me = 4 * x + 2 * y + z
    cv = c_ref[...]
    cact[...] = cv * _sig(cv)
    call_ref[me] = cact[...]
    sends = []
    for k in range(1, N_DEV):
        dev, _ = _peer(k)
        cp = _remote(cact, call_ref.at[me], send1.at[k - 1], recv1.at[k - 1], dev)
        cp.start()
        sends.append(cp)
    for k in range(1, N_DEV):
        dev, blk = _peer(k)
        _remote(cact, call_ref.at[blk], send1.at[k - 1], recv1.at[k - 1], dev).wait_recv()
    cmat[...] = jnp.zeros_like(cmat)
    for b in range(N_DEV):
        cmat[b:b + 1, :] = call_ref[b]
    m = jnp.dot(cmat[...].astype(BF16), w_ref[...].astype(BF16), preferred_element_type=F32) + b_ref[...]
    for b in range(N_DEV):
        mloc[b] = m[b:b + 1, :]
    mod_ref[me] = mloc[me]
    for k in range(1, N_DEV):
        dev, blk = _peer(k)
        cp = _remote(mloc.at[blk], mod_ref.at[me], send2.at[k - 1], recv2.at[k - 1], dev)
        cp.start()
        sends.append(cp)
    for k in range(1, N_DEV):
        dev, blk = _peer(k)
        _remote(mloc.at[me], mod_ref.at[blk], send2.at[k - 1], recv2.at[k - 1], dev).wait_recv()
    for cp in sends:
        cp.wait_send()


def _small_gather(vec):
    n = vec.shape[1]

    def body(v_ref, all_ref, sum_ref, send, recv):
        x, y, z = _me()
        me = 4 * x + 2 * y + z
        all_ref[me] = v_ref[...]
        sends = []
        for k in range(1, N_DEV):
            dev, _ = _peer(k)
            cp = _remote(v_ref, all_ref.at[me], send.at[k - 1], recv.at[k - 1], dev)
            cp.start()
            sends.append(cp)
        for k in range(1, N_DEV):
            dev, blk = _peer(k)
            _remote(v_ref, all_ref.at[blk], send.at[k - 1], recv.at[k - 1], dev).wait_recv()
        acc = all_ref[0]
        for d in range(1, N_DEV):
            acc = acc + all_ref[d]
        sum_ref[...] = acc
        for cp in sends:
            cp.wait_send()

    return _pc(
        body, "small_gather",
        in_specs=[VMEM_SPEC], out_specs=[VMEM_SPEC] * 2,
        out_shape=[jax.ShapeDtypeStruct((N_DEV, 1, n), F32), jax.ShapeDtypeStruct((1, n), F32)],
        scratch_shapes=[pltpu.SemaphoreType.DMA((N_DEV - 1,))] * 2,
        compiler_params=pltpu.CompilerParams(has_side_effects=True),
    )(vec)


def _local_step(x, target, mod, wt_in, w_out_loc, w_pw_loc, conv_w_loc, rel_bias, sinks, conv_b, conv_ln_g,
                conv_ln_b, b_pw, ln_g, ln_b):
    bmap = _bucket_map()
    bias = _bias_table(rel_bias, bmap)
    (h, qkv, g_attn, glu, g_conv), (w_out, w_pw, conv_w_blocks) = _inproj(
        x, mod, wt_in, (w_out_loc, w_pw_loc, conv_w_loc))
    conv_w = conv_w_blocks.reshape(N_DEV, CONV_ROWS, 128).transpose(1, 0, 2).reshape(CONV_ROWS, D_CONV)
    ycat, a_out, u1, p_out = _mixer_fwd(qkv, g_attn, glu, g_conv, bias, sinks, conv_w, conv_b, conv_ln_g,
                                        conv_ln_b, w_pw, b_pw)
    dz, dy, dycat, loss, g_ln_g, g_ln_b, dgate = _outproj_ln(ycat, w_out, x, target, mod, ln_g, ln_b)
    gw_out = _matmul_tn([ycat], dy, "grad_w_out")
    gw_pw, dgc, du1, g_bpw, g_clg, g_clb, g_cb = _conv_bwd_a(dycat, g_conv, p_out, u1, conv_ln_g, conv_ln_b, w_pw)
    dab, g_cw = _conv_bwd_b(du1, glu, conv_w)
    (dqkv, dga, dbias, dsink), (r_out, r_pw) = _attn_bwd(qkv, g_attn, a_out, dycat, bias, sinks,
                                                         scatter=(gw_out, gw_pw))
    g_rb = _relbias_grad(dbias, bmap)
    gwt_in = _matmul_tn([dqkv, dga, dab, dgc], h, "grad_w_in")
    g_cw_blocks = g_cw.reshape(CONV_ROWS, N_DEV, 128).transpose(1, 0, 2).reshape(N_DEV * CONV_ROWS, 128)
    (grad_x, dshift, dscale), (r_in, r_cw) = _dh_kernel([dqkv, dga, dab, dgc], wt_in, dz, x, mod,
                                                        scatter=(gwt_in, g_cw_blocks))
    dmod = jnp.concatenate([dshift, dscale, dgate], axis=1)
    small = dict(dmod=dmod, b_pw=g_bpw, conv_ln_g=g_clg, conv_ln_b=g_clb, conv_b=g_cb, ln_g=g_ln_g, ln_b=g_ln_b,
                 rel_bias=g_rb[:, :N_BUCKETS].reshape(1, N_BUCKETS * N_Q_HEADS),
                 sinks=dsink[0:1, :], loss=loss[0:1, :])
    return grad_x, r_in, r_out, r_pw, r_cw, small


SMALL_FIELDS = (("dmod", 3 * D_MODEL), ("b_pw", D_CONV), ("conv_ln_g", D_CONV), ("conv_ln_b", D_CONV),
                ("conv_b", D_CONV), ("ln_g", D_MODEL), ("ln_b", D_MODEL), ("rel_bias", N_BUCKETS * N_Q_HEADS),
                ("sinks", 128), ("loss", 128))


def _pack(fields):
    parts = []
    for name, width in SMALL_FIELDS:
        v = fields[name].reshape(1, -1).astype(F32)
        if v.shape[1] < width:
            v = jnp.pad(v, ((0, 0), (0, width - v.shape[1])))
        parts.append(v)
    return jnp.concatenate(parts, axis=1)


def _unpack(vec):
    out, off = {}, 0
    for name, width in SMALL_FIELDS:
        out[name] = vec[:, off:off + width]
        off += width
    return out


def kernel(x, c, w_ada, b_ada, w_in, rel_bias, sinks, conv_w, conv_b, conv_ln_g, conv_ln_b, w_pw, b_pw, w_out, ln_g, ln_b, loss_target, m_w_ada, m_b_ada, m_w_in, m_rel_bias, m_sinks, m_conv_w, m_conv_b, m_conv_ln_g, m_conv_ln_b, m_w_pw, m_b_pw, m_w_out, m_ln_g, m_ln_b, v_w_ada, v_b_ada, v_w_in, v_rel_bias, v_sinks, v_conv_w, v_conv_b, v_conv_ln_g, v_conv_ln_b, v_w_pw, v_b_pw, v_w_out, v_ln_g, v_ln_b):
    xi, yi, ci = _me()
    me = 4 * xi + 2 * yi + ci
    ncol = w_ada.shape[2]

    wt_in_loc = w_in[0].T.astype(BF16)
    conv_w_loc = jnp.pad(conv_w[0], ((0, CONV_ROWS - CONV_WIDTH), (0, 0)))

    b_ada_cols = lax.dynamic_slice(b_ada, (0, me * ncol), (1, ncol))
    (wt_in,), (mod_blocks, c_all) = _all_gather_two_level([wt_in_loc], "gather_w_in",
                                                          ada=(c, w_ada[0], b_ada_cols))
    mod = mod_blocks.reshape(1, 3 * D_MODEL)

    grad_x, r_in, r_out, r_pw, r_cw, small = _local_step(
        x[0], loss_target[0], mod, wt_in, w_out[0].astype(BF16), w_pw[0].astype(BF16), conv_w_loc, rel_bias,
        sinks, conv_b, conv_ln_g, conv_ln_b, b_pw, ln_g, ln_b)

    gathered, summed = _small_gather(_pack(small))
    tot = _unpack(summed)
    dmod_all = gathered[:, 0, :3 * D_MODEL]
    loss = tot["loss"][0, 0]

    ct = jnp.zeros((D_MODEL, 128), BF16).at[:, :N_DEV].set(c_all[:, 0, :].T.astype(BF16))
    dm = jnp.zeros((128, ncol), BF16).at[:N_DEV, :].set(
        lax.dynamic_slice(dmod_all, (0, me * ncol), (N_DEV, ncol)).astype(BF16))
    g_w_ada = _small_mm(ct, dm, "grad_w_ada")

    g_conv_w = _sum8(r_cw.reshape(N_DEV, CONV_ROWS, 128), "sum_conv_w")[:CONV_WIDTH]

    grads = {"w_ada": g_w_ada, "conv_w": g_conv_w,
             "w_pw": r_pw.reshape(N_DEV, D_CONV // N_DEV, D_CONV),
             "w_out": r_out.reshape(N_DEV, D_MODEL // N_DEV, D_MODEL)}
    params = {"w_ada": (w_ada, m_w_ada, v_w_ada), "conv_w": (conv_w, m_conv_w, v_conv_w),
              "w_pw": (w_pw, m_w_pw, v_w_pw), "w_out": (w_out, m_w_out, v_w_out)}
    res = {}
    for name, g in grads.items():
        w_, m_, v_ = params[name]
        res[name] = tuple(a[None] for a in _adamw(w_[0], g, m_[0], v_[0], "adamw_" + name))
    upd = _adamw(w_in[0].T, r_in.reshape(N_DEV, D_IN // N_DEV, D_MODEL), m_w_in[0].T, v_w_in[0].T, "adamw_w_in")
    res["w_in"] = tuple(a.T[None] for a in upd)

    rows = {"b_ada": (b_ada, m_b_ada, v_b_ada), "b_pw": (b_pw, m_b_pw, v_b_pw),
            "conv_ln_g": (conv_ln_g, m_conv_ln_g, v_conv_ln_g), "conv_ln_b": (conv_ln_b, m_conv_ln_b, v_conv_ln_b),
            "conv_b": (conv_b, m_conv_b, v_conv_b), "ln_g": (ln_g, m_ln_g, v_ln_g), "ln_b": (ln_b, m_ln_b, v_ln_b),
            "sinks": (sinks, m_sinks, v_sinks)}
    field_of = {"b_ada": "dmod"}
    offsets, off = {}, 0
    for fname, width in SMALL_FIELDS:
        offsets[fname] = off
        off += width
    row_res = _adamw_rows(summed, [offsets[field_of.get(n, n)] for n in rows], list(rows.values()), "adamw_rows")
    for n, r in zip(rows, row_res):
        res[n] = r
    flat = lambda a: a.T.reshape(1, N_BUCKETS * N_Q_HEADS)
    upd = _adamw(flat(rel_bias), tot["rel_bias"], flat(m_rel_bias), flat(v_rel_bias), "adamw_rel_bias")
    res["rel_bias"] = tuple(a.reshape(N_Q_HEADS, N_BUCKETS).T for a in upd)

    order = ["w_ada", "b_ada", "w_in", "rel_bias", "sinks", "conv_w", "conv_b", "conv_ln_g", "conv_ln_b", "w_pw",
             "b_pw", "w_out", "ln_g", "ln_b"]
    out = [loss, grad_x[None]]
    for j in range(4):
        out += [res[n][j] for n in order]
    return tuple(out)
```

```python
import math

import jax
import jax.numpy as jnp
from jax import lax
from jax.experimental import pallas as pl
from jax.experimental.pallas import tpu as pltpu

F32, BF16, I32 = jnp.float32, jnp.bfloat16, jnp.int32

D_MODEL = 2048
D_ATTN = 1024
D_CONV = 1024
D_KV = 256
HEAD_DIM = 64
N_Q_HEADS = 16
N_KV_HEADS = 4
GQA = 4
BLOCK = 128
CONV_WIDTH = 31
CONV_ROWS = 32
HALO = 32
N_BUCKETS = 32
MAX_DISTANCE = 128
LN_EPS = 1e-5
ALPHA = 2.0 ** 0.25
D_IN = 5632
N_DEV = 8
NEG = -1e30

ADAM_LR, ADAM_B1, ADAM_B2, ADAM_EPS, ADAM_WD, ADAM_STEP = 0.001, 0.9, 0.999, 1e-08, 0.01, 10

NT_DIMS = (((1,), (1,)), ((), ()))
TN_DIMS = (((0,), (0,)), ((), ()))
MIB = 1 << 20


def _pc(body, name, **kw):
    return pl.pallas_call(body, name=name, **kw)


def _cp(vmem_mib=None, sem=None):
    kw = {}
    if vmem_mib is not None:
        kw["vmem_limit_bytes"] = vmem_mib * MIB
    if sem is not None:
        kw["dimension_semantics"] = sem
    return pltpu.CompilerParams(**kw)


def _sig(x):
    return 1.0 / (1.0 + jnp.exp(-x))


def _dsilu(x, s):
    return s * (1.0 + x * (1.0 - s))


def _me():
    return lax.axis_index("x"), lax.axis_index("y"), lax.axis_index("c")


def _peer(k):
    x, y, c = _me()
    px = 1 - x if k & 4 else x
    py = 1 - y if k & 2 else y
    pc = 1 - c if k & 1 else c
    return (px, py, pc), 4 * px + 2 * py + pc


def _remote(src, dst, send_sem, recv_sem, dev):
    return pltpu.make_async_remote_copy(src_ref=src, dst_ref=dst, send_sem=send_sem, recv_sem=recv_sem,
                                        device_id=dev, device_id_type=pl.DeviceIdType.MESH)


HBM_SPEC = pl.BlockSpec(memory_space=pl.ANY)
VMEM_SPEC = pl.BlockSpec(memory_space=pltpu.VMEM)


def _comm_scratch(nt):
    return [pltpu.SemaphoreType.DMA((nt, N_DEV - 1)), pltpu.SemaphoreType.DMA((nt, N_DEV - 1)),
            pltpu.SemaphoreType.DMA((nt,))]


def _gather_shapes(shards):
    return [jax.ShapeDtypeStruct((N_DEV * s.shape[0], s.shape[1]), s.dtype) for s in shards]


def _block(ref, blk, rows):
    return ref.at[pl.ds(pl.multiple_of(blk * rows, 8), rows), :]


def _gather_copies(src, dst, sems):
    send_sems, recv_sems, local_sems = sems
    x, y, c = _me()
    me = 4 * x + 2 * y + c
    nt = len(src)
    local = [pltpu.make_async_copy(src[t], _block(dst[t], me, src[t].shape[0]), local_sems.at[t]) for t in range(nt)]
    sends, arrivals = [], []
    for k in range(1, N_DEV):
        dev, blk = _peer(k)
        for t in range(nt):
            r = src[t].shape[0]
            pair = (send_sems.at[t, k - 1], recv_sems.at[t, k - 1], dev)
            sends.append(_remote(src[t], _block(dst[t], me, r), *pair))
            arrivals.append(_remote(src[t], _block(dst[t], blk, r), *pair))
    return local, sends, arrivals


def _scatter_copies(src, dst, sems):
    send_sems, recv_sems, local_sems = sems
    x, y, c = _me()
    me = 4 * x + 2 * y + c
    nt = len(src)
    rows = [s.shape[0] // N_DEV for s in src]
    local = [pltpu.make_async_copy(_block(src[t], me, rows[t]), _block(dst[t], me, rows[t]), local_sems.at[t])
             for t in range(nt)]
    sends, arrivals = [], []
    for k in range(1, N_DEV):
        dev, blk = _peer(k)
        for t in range(nt):
            pair = (send_sems.at[t, k - 1], recv_sems.at[t, k - 1], dev)
            sends.append(_remote(_block(src[t], blk, rows[t]), _block(dst[t], me, rows[t]), *pair))
            arrivals.append(_remote(_block(src[t], me, rows[t]), _block(dst[t], blk, rows[t]), *pair))
    return local, sends, arrivals


def _comm_start(cps):
    local, sends, _ = cps
    for cp in local + sends:
        cp.start()


def _comm_wait(cps):
    local, sends, arrivals = cps
    for cp in arrivals:
        cp.wait_recv()
    for cp in sends:
        cp.wait_send()
    for cp in local:
        cp.wait()


IN_PIECES = ((0, 1536, BF16), (1536, 1024, F32), (2560, 2048, F32), (4608, 1024, F32))


def _inproj(x, mod, wt, gather):
    S = x.shape[0]
    tm = min(256, S)
    ni = S // tm
    ng = len(gather)
    npc = len(IN_PIECES)

    def body(*refs):
        x_ref, sh_ref, sc_ref, w_ref = refs[:4]
        outs = refs[4 + ng:]
        h_ref, piece_refs = outs[0], outs[1:1 + npc]
        i = pl.program_id(0)
        if ng:
            cps = _gather_copies(refs[4:4 + ng], outs[1 + npc:1 + npc + ng], outs[1 + npc + ng:])

            @pl.when(i == 0)
            def _():
                _comm_start(cps)

        h = (x_ref[...] * (1.0 + sc_ref[...]) + sh_ref[...]).astype(BF16)
        h_ref[...] = h
        for (r0, n, dt), o_ref in zip(IN_PIECES, piece_refs):
            o_ref[...] = lax.dot_general(h, w_ref[r0:r0 + n, :], NT_DIMS, preferred_element_type=F32).astype(dt)
        if ng:
            @pl.when(i == ni - 1)
            def _():
                _comm_wait(cps)

    tile = lambda n: pl.BlockSpec((tm, n), lambda i: (i, 0))
    res = _pc(
        body, "inproj", grid=(ni,),
        in_specs=[tile(D_MODEL),
                  pl.BlockSpec((1, D_MODEL), lambda i: (0, 0)),
                  pl.BlockSpec((1, D_MODEL), lambda i: (0, 1)),
                  pl.BlockSpec((D_IN, D_MODEL), lambda i: (0, 0), pipeline_mode=pl.Buffered(1))] + [HBM_SPEC] * ng,
        out_specs=[tile(D_MODEL)] + [tile(n) for _, n, _ in IN_PIECES] + [HBM_SPEC] * ng,
        out_shape=[jax.ShapeDtypeStruct((S, D_MODEL), BF16)]
        + [jax.ShapeDtypeStruct((S, n), dt) for _, n, dt in IN_PIECES] + _gather_shapes(gather),
        scratch_shapes=_comm_scratch(ng) if ng else [],
        compiler_params=_cp(56, ("arbitrary",)),
    )(x, mod, mod, wt, *gather)
    return res[:1 + npc], res[1 + npc:]


def _matmul_tn(segs, b, name):
    S, N = b.shape
    tm, ts = 512, min(2048, S)
    ns = S // ts
    bounds = []
    t0 = 0
    for a in segs:
        assert a.shape[1] % tm == 0
        bounds.append((t0, a.shape[1] // tm))
        t0 += a.shape[1] // tm
    nseg = len(segs)

    def body(*refs):
        seg_refs, b_ref = refs[:nseg], refs[nseg]
        o_ref, acc = refs[-2], refs[-1]
        t, s = pl.program_id(0), pl.program_id(1)

        @pl.when(s == 0)
        def _():
            acc[...] = jnp.zeros_like(acc)

        for (s0, sn), r in zip(bounds, seg_refs):
            @pl.when((t >= s0) & (t < s0 + sn))
            def _(r=r):
                acc[...] += lax.dot_general(r[...], b_ref[...], TN_DIMS, preferred_element_type=F32)

        @pl.when(s == ns - 1)
        def _():
            o_ref[...] = acc[...].astype(BF16)

    def seg_spec(s0, sn):
        def index(t, s):
            mine = (t >= s0) & (t < s0 + sn)
            return jnp.where(mine, s, 0), jnp.clip(t - s0, 0, sn - 1)
        return pl.BlockSpec((ts, tm), index)

    return _pc(
        body, name, grid=(t0, ns),
        in_specs=[seg_spec(*bd) for bd in bounds] + [pl.BlockSpec((ts, N), lambda t, s: (s, 0))],
        out_specs=pl.BlockSpec((tm, N), lambda t, s: (t, 0)),
        out_shape=jax.ShapeDtypeStruct((t0 * tm, N), BF16),
        scratch_shapes=[pltpu.VMEM((tm, N), F32)],
        compiler_params=_cp(48, ("parallel", "arbitrary")),
    )(*segs, b)


def _bucket_map():
    qi = jnp.arange(BLOCK, dtype=I32)[:, None]
    kj = jnp.arange(2 * BLOCK, dtype=I32)[None, :]
    dist = qi + BLOCK - kj
    in_window = (dist >= 0) & (dist < BLOCK)
    d0 = jnp.maximum(dist, 0)
    max_exact = N_BUCKETS // 2
    d = jnp.maximum(d0, 1).astype(F32)
    large = max_exact + (jnp.log(d / max_exact) / math.log(MAX_DISTANCE / max_exact)
                         * (N_BUCKETS - max_exact)).astype(I32)
    large = jnp.minimum(large, N_BUCKETS - 1)
    bucket = jnp.where(d0 < max_exact, d0, large)
    return jnp.where(in_window, bucket, -1).astype(I32)


def _bias_table(rel_bias, bmap):
    def body(rb_ref, bm_ref, o_ref):
        bm = bm_ref[...]
        for h in range(N_Q_HEADS):
            acc = jnp.full((BLOCK, 2 * BLOCK), NEG, F32)
            for b in range(N_BUCKETS):
                acc = jnp.where(bm == b, rb_ref[b, h], acc)
            o_ref[h] = acc

    return _pc(
        body, "bias_table",
        in_specs=[pl.BlockSpec(memory_space=pltpu.SMEM), VMEM_SPEC],
        out_specs=VMEM_SPEC,
        out_shape=jax.ShapeDtypeStruct((N_Q_HEADS, BLOCK, 2 * BLOCK), F32),
    )(rel_bias, bmap)


def _bias_spec():
    return pl.BlockSpec((N_Q_HEADS, BLOCK, 2 * BLOCK), lambda n: (0, 0, 0))


def _relbias_grad(dbias, bmap):
    def body(db_ref, bm_ref, o_ref):
        bm = bm_ref[...]
        x = db_ref[0]
        lane = lax.broadcasted_iota(I32, (1, 128), 1)
        row = jnp.zeros((1, 128), F32)
        for b in range(N_BUCKETS):
            row = jnp.where(lane == b, jnp.sum(jnp.where(bm == b, x, 0.0)), row)
        o_ref[0] = jnp.broadcast_to(row, (8, 128))

    out = _pc(
        body, "relbias_grad", grid=(N_Q_HEADS,),
        in_specs=[pl.BlockSpec((1, BLOCK, 2 * BLOCK), lambda h: (h, 0, 0)),
                  pl.BlockSpec((BLOCK, 2 * BLOCK), lambda h: (0, 0))],
        out_specs=pl.BlockSpec((1, 8, 128), lambda h: (h, 0, 0)),
        out_shape=jax.ShapeDtypeStruct((N_Q_HEADS, 8, 128), F32),
    )(dbias, bmap)
    return out[:, 0, :]


Q_SCALE = HEAD_DIM ** -0.5
assert math.frexp(Q_SCALE)[0] == 0.5


def _stack_heads(ref, hk, rows=slice(None)):
    return jnp.concatenate([ref[rows, pl.ds(256 * hk + 64 * g, 64)] for g in range(GQA)], axis=0) * Q_SCALE


def _sink_col(sink_ref, hk):
    row = lax.broadcasted_iota(I32, (GQA * BLOCK, 1), 0)
    s = jnp.full((GQA * BLOCK, 1), sink_ref[0, 4 * hk + 3], F32)
    for g in (2, 1, 0):
        s = jnp.where(row < (g + 1) * BLOCK, sink_ref[0, 4 * hk + g], s)
    return s


def _attn_probs(q4, kw, bias_ref, sink_ref, hk, first_mask):
    s = lax.dot_general(q4, kw, NT_DIMS, preferred_element_type=F32)
    s = s + bias_ref[4 * hk:4 * hk + 4].reshape(GQA * BLOCK, 2 * BLOCK)
    if first_mask is not None:
        s = jnp.where(first_mask, s, NEG)
    sink = _sink_col(sink_ref, hk)
    m = jnp.maximum(jnp.max(s, axis=1, keepdims=True), sink)
    e = jnp.exp(s - m)
    es = jnp.exp(sink - m)
    inv = 1.0 / (jnp.sum(e, axis=1, keepdims=True) + es)
    return e * inv, es * inv


def _attn_bwd(qkv, g_attn, a_out, dycat, bias, sinks, scatter=()):
    S = qkv.shape[0]
    nb = S // BLOCK
    R = GQA * BLOCK
    ns = len(scatter)

    def body(*refs):
        (q_ref, kc_ref, kp_ref, vc_ref, vp_ref, g_ref, a_ref, dy_ref, bias_ref, sink_ref) = refs[:10]
        dqkv_ref, dg_ref, dbias_ref, dsink_ref = refs[10 + ns:14 + ns]
        dq_scr, dq_new, dk_scr, dv_scr, dkw_scr, dvw_scr, ds_scr = refs[14 + 2 * ns:21 + 2 * ns]
        n = pl.program_id(0)
        if ns:
            cps = _scatter_copies(refs[10:10 + ns], refs[14 + ns:14 + 2 * ns], refs[21 + 2 * ns:])

            @pl.when(n == 0)
            def _():
                _comm_start(cps)

        @pl.when(n == 0)
        def _():
            dbias_ref[...] = jnp.zeros_like(dbias_ref)
            ds_scr[...] = jnp.zeros_like(ds_scr)
            dq_scr[...] = jnp.zeros_like(dq_scr)
            dk_scr[...] = jnp.zeros_like(dk_scr)
            dv_scr[...] = jnp.zeros_like(dv_scr)

        @pl.when(n < nb)
        def _():
            kj = lax.broadcasted_iota(I32, (R, 2 * BLOCK), 1)
            first_mask = (n > 0) | (kj >= BLOCK)
            for hk in range(N_KV_HEADS):
                q4 = _stack_heads(q_ref, hk)
                ks = pl.ds(64 * hk, 64)
                kw = jnp.concatenate([kp_ref[:, ks], kc_ref[:, ks]], axis=0)
                vw = jnp.concatenate([vp_ref[:, ks], vc_ref[:, ks]], axis=0)
                p, psink = _attn_probs(q4, kw, bias_ref, sink_ref, hk, first_mask)
                da_parts, a_parts = [], []
                for g in range(GQA):
                    sl = pl.ds(256 * hk + 64 * g, 64)
                    gg = g_ref[:, sl]
                    sg = _sig(gg)
                    dyg = dy_ref[:, sl]
                    ag = a_ref[:, sl]
                    da_parts.append(dyg * (gg * sg))
                    a_parts.append(ag)
                    dg_ref[:, sl] = (dyg * ag * _dsilu(gg, sg)).astype(BF16)
                da4 = jnp.concatenate(da_parts, axis=0)
                a4 = jnp.concatenate(a_parts, axis=0)
                delta = jnp.sum(da4 * a4, axis=1, keepdims=True)
                da4b = da4.astype(BF16)
                dp = lax.dot_general(da4b, vw, NT_DIMS, preferred_element_type=F32)
                ds = p * (dp - delta)
                ds_scr[hk] += -psink * delta
                dbias_ref[4 * hk:4 * hk + 4] += ds.reshape(GQA, BLOCK, 2 * BLOCK)
                dsb = ds.astype(BF16)
                dq4 = jnp.dot(dsb, kw, preferred_element_type=F32) * Q_SCALE
                for g in range(GQA):
                    dq_new[:, pl.ds(256 * hk + 64 * g, 64)] = dq4[BLOCK * g:BLOCK * (g + 1)].astype(BF16)
                dkw_scr[:, ks] = lax.dot_general(dsb, q4, TN_DIMS, preferred_element_type=F32)
                dvw_scr[:, ks] = lax.dot_general(p.astype(BF16), da4b, TN_DIMS, preferred_element_type=F32)

        @pl.when(n == nb)
        def _():
            dkw_scr[0:BLOCK, :] = jnp.zeros((BLOCK, D_KV), F32)
            dvw_scr[0:BLOCK, :] = jnp.zeros((BLOCK, D_KV), F32)

        dqkv_ref[:, 0:D_ATTN] = dq_scr[...]
        dqkv_ref[:, D_ATTN:D_ATTN + D_KV] = (dk_scr[...] + dkw_scr[0:BLOCK, :]).astype(BF16)
        dqkv_ref[:, D_ATTN + D_KV:D_ATTN + 2 * D_KV] = (dv_scr[...] + dvw_scr[0:BLOCK, :]).astype(BF16)
        dq_scr[...] = dq_new[...]
        dk_scr[...] = dkw_scr[BLOCK:2 * BLOCK, :]
        dv_scr[...] = dvw_scr[BLOCK:2 * BLOCK, :]

        @pl.when(n == nb)
        def _():
            lane = lax.broadcasted_iota(I32, (1, 128), 1)
            row = jnp.zeros((1, 128), F32)
            for hk in range(N_KV_HEADS):
                col = ds_scr[hk]
                for g in range(GQA):
                    row = jnp.where(lane == 4 * hk + g, jnp.sum(col[BLOCK * g:BLOCK * (g + 1)]), row)
            dsink_ref[...] = jnp.broadcast_to(row, (8, 128))
            if ns:
                _comm_wait(cps)

    cur = lambda n: jnp.minimum(n, nb - 1)
    prev = lambda n: jnp.clip(n - 1, 0, nb - 1)
    res = _pc(
        body, "attn_bwd", grid=(nb + 1,),
        in_specs=[pl.BlockSpec((BLOCK, D_ATTN), lambda n: (cur(n), 0)),
                  pl.BlockSpec((BLOCK, D_KV), lambda n: (cur(n), 4)),
                  pl.BlockSpec((BLOCK, D_KV), lambda n: (prev(n), 4)),
                  pl.BlockSpec((BLOCK, D_KV), lambda n: (cur(n), 5)),
                  pl.BlockSpec((BLOCK, D_KV), lambda n: (prev(n), 5)),
                  pl.BlockSpec((BLOCK, D_ATTN), lambda n: (cur(n), 0)),
                  pl.BlockSpec((BLOCK, D_ATTN), lambda n: (cur(n), 0)),
                  pl.BlockSpec((BLOCK, D_ATTN), lambda n: (cur(n), 0)),
                  _bias_spec(),
                  pl.BlockSpec(memory_space=pltpu.SMEM)] + [HBM_SPEC] * ns,
        out_specs=[pl.BlockSpec((BLOCK, D_ATTN + 2 * D_KV), lambda n: (prev(n), 0)),
                   pl.BlockSpec((BLOCK, D_ATTN), lambda n: (cur(n), 0)),
                   pl.BlockSpec((N_Q_HEADS, BLOCK, 2 * BLOCK), lambda n: (0, 0, 0)),
                   pl.BlockSpec((8, 128), lambda n: (0, 0))] + [HBM_SPEC] * ns,
        out_shape=[jax.ShapeDtypeStruct((S, D_ATTN + 2 * D_KV), BF16),
                   jax.ShapeDtypeStruct((S, D_ATTN), BF16),
                   jax.ShapeDtypeStruct((N_Q_HEADS, BLOCK, 2 * BLOCK), F32),
                   jax.ShapeDtypeStruct((8, 128), F32)] + [jax.ShapeDtypeStruct(f.shape, f.dtype) for f in scatter],
        scratch_shapes=[pltpu.VMEM((BLOCK, D_ATTN), BF16), pltpu.VMEM((BLOCK, D_ATTN), BF16),
                        pltpu.VMEM((BLOCK, D_KV), F32), pltpu.VMEM((BLOCK, D_KV), F32),
                        pltpu.VMEM((2 * BLOCK, D_KV), F32), pltpu.VMEM((2 * BLOCK, D_KV), F32),
                        pltpu.VMEM((N_KV_HEADS, R, 1), F32)] + (_comm_scratch(ns) if ns else []),
        compiler_params=_cp(48, ("arbitrary",)),
    )(qkv, qkv, qkv, qkv, qkv, g_attn, a_out, dycat, bias, sinks, *scatter)
    return res[:4], res[4:]


def _conv_tile(S):
    return min(256, S)


def _shifted(win, s):
    return win if s == 0 else pltpu.roll(win, win.shape[0] - s, axis=0)


def _conv_taps(win, cw_ref, lanes, rows):
    acc = jnp.zeros((rows, win.shape[1]), F32)
    for s in range(8):
        ws = _shifted(win, s)
        for aa in range(5):
            j = 8 * aa + s - 2
            if 0 <= j < CONV_WIDTH:
                acc = acc + ws[8 * aa:8 * aa + rows] * cw_ref[j:j + 1, lanes]
    return acc


def _mixer_fwd(qkv, g_attn, glu, g_conv, bias, sinks, conv_w, conv_b, ln_g, ln_b, w_pw, b_pw):
    S = qkv.shape[0]
    RB = 2
    TR = RB * BLOCK
    nb = S // TR
    hb = TR // HALO
    LG = D_CONV // (RB * N_KV_HEADS)

    def body(q_ref, kc_ref, kp_ref, vc_ref, vp_ref, g_ref, bias_ref, sink_ref,
             a_in, ah_in, b_in, bh_in, gc_ref, cw_ref, cb_ref, lg_ref, lb_ref, wpw_ref, bpw_ref,
             y_ref, a_ref, u1_ref, p_ref, win):
        n = pl.program_id(0)
        kj = lax.broadcasted_iota(I32, (GQA * BLOCK, 2 * BLOCK), 1)
        first_mask = (n > 0) | (kj >= BLOCK)
        win[0:HALO, :] = jnp.where(n > 0, ah_in[...] * _sig(bh_in[...]), 0.0)
        win[HALO:HALO + TR, :] = a_in[...] * _sig(b_in[...])
        piece = 0
        for s in range(RB):
            rows = slice(BLOCK * s, BLOCK * (s + 1))
            before = slice(BLOCK * (s - 1), BLOCK * s)
            for hk in range(N_KV_HEADS):
                q4 = _stack_heads(q_ref, hk, rows)
                ks = pl.ds(64 * hk, 64)
                k_prev, v_prev = (kp_ref[:, ks], vp_ref[:, ks]) if s == 0 else (kc_ref[before, ks], vc_ref[before, ks])
                kw = jnp.concatenate([k_prev, kc_ref[rows, ks]], axis=0)
                vw = jnp.concatenate([v_prev, vc_ref[rows, ks]], axis=0)
                p, _ = _attn_probs(q4, kw, bias_ref, sink_ref, hk, first_mask if s == 0 else None)
                o4 = jnp.dot(p.astype(BF16), vw, preferred_element_type=F32)
                for g in range(GQA):
                    sl = pl.ds(256 * hk + 64 * g, 64)
                    og = o4[BLOCK * g:BLOCK * (g + 1)]
                    gg = g_ref[rows, sl]
                    a_ref[rows, sl] = og
                    y_ref[rows, sl] = (og * (gg * _sig(gg))).astype(BF16)
                lanes = pl.ds(LG * piece, LG)
                u1_ref[:, lanes] = _conv_taps(win[:, lanes], cw_ref, lanes, TR) + cb_ref[:, lanes]
                piece += 1
        u1 = u1_ref[...]
        mu = jnp.mean(u1, axis=1, keepdims=True)
        uc = u1 - mu
        rstd = lax.rsqrt(jnp.mean(uc * uc, axis=1, keepdims=True) + LN_EPS)
        u2 = uc * rstd * lg_ref[...] + lb_ref[...]
        u3 = (u2 * _sig(u2)).astype(BF16)
        pw = jnp.dot(u3, wpw_ref[...], preferred_element_type=F32) + bpw_ref[...]
        p_ref[...] = pw
        gc = gc_ref[...]
        y_ref[:, D_ATTN:] = (pw * (gc * _sig(gc))).astype(BF16)

    prev = lambda n: jnp.maximum(RB * n - 1, 0)
    halo = lambda n: jnp.maximum(n * hb - 1, 0)
    vec = pl.BlockSpec((1, D_CONV), lambda n: (0, 0))
    blk = lambda w, j: pl.BlockSpec((TR, w), lambda n: (n, j))
    return _pc(
        body, "mixer_fwd", grid=(nb,),
        in_specs=[blk(D_ATTN, 0),
                  blk(D_KV, 4), pl.BlockSpec((BLOCK, D_KV), lambda n: (prev(n), 4)),
                  blk(D_KV, 5), pl.BlockSpec((BLOCK, D_KV), lambda n: (prev(n), 5)),
                  blk(D_ATTN, 0),
                  _bias_spec(),
                  pl.BlockSpec(memory_space=pltpu.SMEM),
                  blk(D_CONV, 0), pl.BlockSpec((HALO, D_CONV), lambda n: (halo(n), 0)),
                  blk(D_CONV, 1), pl.BlockSpec((HALO, D_CONV), lambda n: (halo(n), 1)),
                  blk(D_CONV, 0),
                  pl.BlockSpec((CONV_ROWS, D_CONV), lambda n: (0, 0)),
                  vec, vec, vec,
                  pl.BlockSpec((D_CONV, D_CONV), lambda n: (0, 0)),
                  vec],
        out_specs=[blk(2 * D_ATTN, 0), blk(D_ATTN, 0), blk(D_CONV, 0), blk(D_CONV, 0)],
        out_shape=[jax.ShapeDtypeStruct((S, 2 * D_ATTN), BF16),
                   jax.ShapeDtypeStruct((S, D_ATTN), F32),
                   jax.ShapeDtypeStruct((S, D_CONV), F32),
                   jax.ShapeDtypeStruct((S, D_CONV), F32)],
        scratch_shapes=[pltpu.VMEM((TR + HALO, D_CONV), F32)],
        compiler_params=_cp(56, ("parallel",)),
    )(qkv, qkv, qkv, qkv, qkv, g_attn, bias, sinks, glu, glu, glu, glu, g_conv, conv_w, conv_b, ln_g, ln_b, w_pw,
      b_pw)


def _conv_bwd_a(dycat, g_conv, p_out, u1, ln_g, ln_b, w_pw):
    S = u1.shape[0]
    T = _conv_tile(S)
    nt = S // T

    def body(dy_ref, gc_ref, p_ref, u1_ref, lg_ref, lb_ref, wpw_ref,
             gw_ref, dgc_ref, du1_ref, gbpw_ref, glg_ref, glb_ref, gcb_ref, gw_acc):
        i = pl.program_id(0)

        @pl.when(i == 0)
        def _():
            for r in (gbpw_ref, glg_ref, glb_ref, gcb_ref, gw_acc):
                r[...] = jnp.zeros_like(r)

        dy = dy_ref[...]
        gc = gc_ref[...]
        sg = _sig(gc)
        dp = dy * (gc * sg)
        dgc_ref[...] = (dy * p_ref[...] * _dsilu(gc, sg)).astype(BF16)
        gbpw_ref[...] += jnp.sum(dp, axis=0, keepdims=True)
        dpb = dp.astype(BF16)
        du3 = lax.dot_general(dpb, wpw_ref[...], NT_DIMS, preferred_element_type=F32)
        u1 = u1_ref[...]
        mu = jnp.mean(u1, axis=1, keepdims=True)
        uc = u1 - mu
        rstd = lax.rsqrt(jnp.mean(uc * uc, axis=1, keepdims=True) + LN_EPS)
        uh = uc * rstd
        lg = lg_ref[...]
        u2 = uh * lg + lb_ref[...]
        s2 = _sig(u2)
        gw_acc[...] += lax.dot_general((u2 * s2).astype(BF16), dpb, TN_DIMS, preferred_element_type=F32)
        du2 = du3 * _dsilu(u2, s2)
        glg_ref[...] += jnp.sum(du2 * uh, axis=0, keepdims=True)
        glb_ref[...] += jnp.sum(du2, axis=0, keepdims=True)
        duh = du2 * lg
        du1 = rstd * (duh - jnp.mean(duh, axis=1, keepdims=True) - uh * jnp.mean(duh * uh, axis=1, keepdims=True))
        du1_ref[...] = du1
        gcb_ref[...] += jnp.sum(du1, axis=0, keepdims=True)

        @pl.when(i == nt - 1)
        def _():
            gw_ref[...] = gw_acc[...].astype(BF16)

    vec = pl.BlockSpec((1, D_CONV), lambda i: (0, 0))
    tile = pl.BlockSpec((T, D_CONV), lambda i: (i, 0))
    square = pl.BlockSpec((D_CONV, D_CONV), lambda i: (0, 0))
    vshape = jax.ShapeDtypeStruct((1, D_CONV), F32)
    return _pc(
        body, "conv_bwd_a", grid=(nt,),
        in_specs=[pl.BlockSpec((T, D_CONV), lambda i: (i, 1)), tile, tile, tile, vec, vec, square],
        out_specs=[square, tile, tile, vec, vec, vec, vec],
        out_shape=[jax.ShapeDtypeStruct((D_CONV, D_CONV), BF16), jax.ShapeDtypeStruct((S, D_CONV), BF16),
                   jax.ShapeDtypeStruct((S, D_CONV), F32), vshape, vshape, vshape, vshape],
        scratch_shapes=[pltpu.VMEM((D_CONV, D_CONV), F32)],
        compiler_params=_cp(48, ("arbitrary",)),
    )(dycat, g_conv, p_out, u1, ln_g, ln_b, w_pw)


def _conv_bwd_b(du1, glu, conv_w):
    S = du1.shape[0]
    T = _conv_tile(S)
    hb = T // HALO
    nt = S // T
    last_h = S // HALO - 1

    def body(du_ref, dun_ref, a_ref, b_ref, cw_ref, dab_ref, gw_ref):
        i = pl.program_id(0)

        @pl.when(i == 0)
        def _():
            gw_ref[...] = jnp.zeros_like(gw_ref)

        for lg in range(D_CONV // 128):
            lanes = pl.ds(128 * lg, 128)
            a = a_ref[:, lanes]
            sb = _sig(b_ref[:, lanes])
            u0 = a * sb
            du1n = jnp.where(i < nt - 1, dun_ref[:, lanes], 0.0)
            win2 = jnp.concatenate([du_ref[:, lanes], du1n], axis=0)
            acc = jnp.zeros((T, 128), F32)
            for s in range(8):
                w2 = _shifted(win2, s)
                for aa in range(4):
                    j = CONV_WIDTH - 1 - (8 * aa + s)
                    if 0 <= j < CONV_WIDTH:
                        xo = w2[8 * aa:8 * aa + T]
                        acc = acc + xo * cw_ref[j:j + 1, lanes]
                        gw_ref[j:j + 1, lanes] += jnp.sum(xo * u0, axis=0, keepdims=True)
            dab_ref[:, lanes] = (acc * sb).astype(BF16)
            dab_ref[:, pl.ds(D_CONV + 128 * lg, 128)] = (acc * a * sb * (1.0 - sb)).astype(BF16)

    nxt = lambda i: jnp.minimum((i + 1) * hb, last_h)
    return _pc(
        body, "conv_bwd_b", grid=(nt,),
        in_specs=[pl.BlockSpec((T, D_CONV), lambda i: (i, 0)),
                  pl.BlockSpec((HALO, D_CONV), lambda i: (nxt(i), 0)),
                  pl.BlockSpec((T, D_CONV), lambda i: (i, 0)),
                  pl.BlockSpec((T, D_CONV), lambda i: (i, 1)),
                  pl.BlockSpec((CONV_ROWS, D_CONV), lambda i: (0, 0))],
        out_specs=[pl.BlockSpec((T, 2 * D_CONV), lambda i: (i, 0)),
                   pl.BlockSpec((CONV_ROWS, D_CONV), lambda i: (0, 0))],
        out_shape=[jax.ShapeDtypeStruct((S, 2 * D_CONV), BF16),
                   jax.ShapeDtypeStruct((CONV_ROWS, D_CONV), F32)],
        compiler_params=_cp(48, ("arbitrary",)),
    )(du1, du1, glu, glu, conv_w)


def _outproj_ln(ycat, w_out, x, target, mod, ln_g, ln_b):
    S = x.shape[0]
    tm = min(256, S)

    def body(yc_ref, w_ref, x_ref, t_ref, gate_ref, lg_ref, lb_ref,
             dz_ref, dy_ref, dyc_ref, loss_ref, glg_ref, glb_ref, dgate_ref):
        i = pl.program_id(0)

        @pl.when(i == 0)
        def _():
            for r in (loss_ref, glg_ref, glb_ref, dgate_ref):
                r[...] = jnp.zeros_like(r)

        w = w_ref[...]
        y = jnp.dot(yc_ref[...], w, preferred_element_type=F32)
        gate = gate_ref[...]
        z = ALPHA * x_ref[...] + gate * y
        mu = jnp.mean(z, axis=1, keepdims=True)
        zc = z - mu
        rstd = lax.rsqrt(jnp.mean(zc * zc, axis=1, keepdims=True) + LN_EPS)
        zh = zc * rstd
        lg = lg_ref[...]
        err = zh * lg + lb_ref[...] - t_ref[...]
        loss_ref[...] += 0.5 * jnp.sum(jnp.sum(err * err, axis=1, keepdims=True)) / D_MODEL
        dout = err * (1.0 / D_MODEL)
        glg_ref[...] += jnp.sum(dout * zh, axis=0, keepdims=True)
        glb_ref[...] += jnp.sum(dout, axis=0, keepdims=True)
        dzh = dout * lg
        dz = rstd * (dzh - jnp.mean(dzh, axis=1, keepdims=True) - zh * jnp.mean(dzh * zh, axis=1, keepdims=True))
        dz_ref[...] = dz
        dgate_ref[...] += jnp.sum(dz * y, axis=0, keepdims=True)
        dy = (dz * gate).astype(BF16)
        dy_ref[...] = dy
        dyc_ref[...] = lax.dot_general(dy, w, NT_DIMS, preferred_element_type=F32).astype(BF16)

    vec = pl.BlockSpec((1, D_MODEL), lambda i: (0, 0))
    tile = pl.BlockSpec((tm, D_MODEL), lambda i: (i, 0))
    vshape = jax.ShapeDtypeStruct((1, D_MODEL), F32)
    return _pc(
        body, "outproj_ln", grid=(S // tm,),
        in_specs=[tile, pl.BlockSpec((D_MODEL, D_MODEL), lambda i: (0, 0), pipeline_mode=pl.Buffered(1)), tile, tile,
                  pl.BlockSpec((1, D_MODEL), lambda i: (0, 2)), vec, vec],
        out_specs=[tile, tile, tile, pl.BlockSpec((8, 128), lambda i: (0, 0)), vec, vec, vec],
        out_shape=[jax.ShapeDtypeStruct((S, D_MODEL), F32), jax.ShapeDtypeStruct((S, D_MODEL), BF16),
                   jax.ShapeDtypeStruct((S, D_MODEL), BF16), jax.ShapeDtypeStruct((8, 128), F32),
                   vshape, vshape, vshape],
        compiler_params=_cp(56, ("arbitrary",)),
    )(ycat, w_out, x, target, mod, ln_g, ln_b)


def _dh_kernel(segs, wt, dz, x, mod, scatter=()):
    S = x.shape[0]
    tm = min(256, S)
    row0 = [0]
    for a in segs:
        row0.append(row0[-1] + a.shape[1])
    assert row0[-1] == wt.shape[0]
    nseg = len(segs)
    ns = len(scatter)
    ni = S // tm

    def body(*refs):
        seg_refs = refs[:nseg]
        w_hbm, dz_ref, x_ref, sc_ref = refs[nseg:nseg + 4]
        outs = refs[nseg + 4 + ns:]
        gx_ref, dsh_ref, dsc_ref = outs[:3]
        w_ref, w_sems = outs[3 + ns:5 + ns]
        i = pl.program_id(0)
        if ns:
            cps = _scatter_copies(refs[nseg + 4:nseg + 4 + ns], outs[3:3 + ns], outs[5 + ns:])

        def w_load(t):
            rows = slice(row0[t], row0[t + 1])
            return pltpu.make_async_copy(w_hbm.at[rows, :], w_ref.at[rows, :], w_sems.at[t])

        @pl.when(i == 0)
        def _():
            for t in range(nseg):
                w_load(t).start()
            dsh_ref[...] = jnp.zeros_like(dsh_ref)
            dsc_ref[...] = jnp.zeros_like(dsc_ref)
            if ns:
                _comm_start(cps)

        dh = None
        for t in range(nseg):
            @pl.when(i == 0)
            def _(t=t):
                w_load(t).wait()

            part = jnp.dot(seg_refs[t][...], w_ref[row0[t]:row0[t + 1], :], preferred_element_type=F32)
            dh = part if dh is None else dh + part
        gx_ref[...] = ALPHA * dz_ref[...] + dh * (1.0 + sc_ref[...])
        dsh_ref[...] += jnp.sum(dh, axis=0, keepdims=True)
        dsc_ref[...] += jnp.sum(dh * x_ref[...], axis=0, keepdims=True)

        if ns:
            @pl.when(i == ni - 1)
            def _():
                _comm_wait(cps)

    tile = pl.BlockSpec((tm, D_MODEL), lambda i: (i, 0))
    vec = pl.BlockSpec((1, D_MODEL), lambda i: (0, 0))
    vshape = jax.ShapeDtypeStruct((1, D_MODEL), F32)
    res = _pc(
        body, "dh_gradx", grid=(ni,),
        in_specs=[pl.BlockSpec((tm, a.shape[1]), lambda i: (i, 0)) for a in segs] + [
            HBM_SPEC, tile, tile,
            pl.BlockSpec((1, D_MODEL), lambda i: (0, 1))] + [HBM_SPEC] * ns,
        out_specs=[tile, vec, vec] + [HBM_SPEC] * ns,
        out_shape=[jax.ShapeDtypeStruct((S, D_MODEL), F32), vshape, vshape]
        + [jax.ShapeDtypeStruct(f.shape, f.dtype) for f in scatter],
        scratch_shapes=[pltpu.VMEM(wt.shape, BF16), pltpu.SemaphoreType.DMA((nseg,))]
        + (_comm_scratch(ns) if ns else []),
        compiler_params=_cp(58, ("arbitrary",)),
    )(*segs, wt, dz, x, mod, *scatter)
    return res[:3], res[3:]


def _row_tile(rows, cols):
    if rows * cols * 4 <= 2 * MIB or rows % 8:
        return rows
    tr = max(8, (2 * MIB // (cols * 4)) // 8 * 8)
    while rows % tr:
        tr -= 8
    return tr


def _sum8(recv, name):
    _, R, C = recv.shape
    tr = _row_tile(R, C)

    def body(r_ref, o_ref):
        acc = r_ref[0].astype(F32)
        for d in range(1, N_DEV):
            acc = acc + r_ref[d].astype(F32)
        o_ref[...] = acc

    return _pc(
        body, name, grid=(R // tr,),
        in_specs=[pl.BlockSpec((N_DEV, tr, C), lambda i: (0, i, 0))],
        out_specs=pl.BlockSpec((tr, C), lambda i: (i, 0)),
        out_shape=jax.ShapeDtypeStruct((R, C), F32),
        compiler_params=_cp(40, ("parallel",)),
    )(recv)


def _adam_math(w, g, m, v):
    m = ADAM_B1 * m + (1.0 - ADAM_B1) * g
    v = ADAM_B2 * v + (1.0 - ADAM_B2) * (g * g)
    m_hat = m / (1.0 - ADAM_B1 ** ADAM_STEP)
    v_hat = v / (1.0 - ADAM_B2 ** ADAM_STEP)
    return -ADAM_LR * (m_hat / (jnp.sqrt(v_hat) + ADAM_EPS) + ADAM_WD * w), m, v


def _adamw_rows(gsum, offsets, params, name):
    npar = len(params)

    def body(g_ref, *refs):
        ins, outs = refs[:3 * npar], refs[3 * npar:]
        for p in range(npar):
            w_ref, m_ref, v_ref = ins[3 * p:3 * p + 3]
            go_ref, d_ref, nm_ref, nv_ref = outs[4 * p:4 * p + 4]
            g = g_ref[:, offsets[p]:offsets[p] + w_ref.shape[1]]
            go_ref[...] = g
            d_ref[...], nm_ref[...], nv_ref[...] = _adam_math(w_ref[...], g, m_ref[...], v_ref[...])

    flat = [a for t in params for a in t]
    res = _pc(
        body, name,
        in_specs=[VMEM_SPEC] * (1 + 3 * npar), out_specs=[VMEM_SPEC] * (4 * npar),
        out_shape=[jax.ShapeDtypeStruct(t[0].shape, F32) for t in params for _ in range(4)],
    )(gsum, *flat)
    return [tuple(res[4 * p:4 * p + 4]) for p in range(npar)]


def _adamw(w, g, m, v, name):
    R, C = w.shape
    tr = _row_tile(R, C)
    prod = isinstance(g, tuple)
    parts = not prod and g.ndim == 3
    g_args = list(g) if prod else [g]
    ng = len(g_args)

    def body(w_ref, *refs):
        g_refs, (m_ref, v_ref), out_refs = refs[:ng], refs[ng:ng + 2], refs[ng + 2:]
        d_ref, nm_ref, nv_ref = out_refs[-3:]
        if prod:
            g_ = jnp.dot(g_refs[0][...], g_refs[1][...], preferred_element_type=F32)
        elif parts:
            g_ = g_refs[0][0].astype(F32)
            for d in range(1, N_DEV):
                g_ = g_ + g_refs[0][d].astype(F32)
        else:
            g_ = g_refs[0][...]
        if prod or parts:
            out_refs[0][...] = g_
        d_ref[...], nm_ref[...], nv_ref[...] = _adam_math(w_ref[...], g_, m_ref[...], v_ref[...])

    spec = pl.BlockSpec((tr, C), lambda i: (i, 0))
    if prod:
        k = g[0].shape[1]
        g_specs = [pl.BlockSpec((tr, k), lambda i: (i, 0)), pl.BlockSpec((k, C), lambda i: (0, 0))]
    else:
        g_specs = [pl.BlockSpec((N_DEV, tr, C), lambda i: (0, i, 0)) if parts else spec]
    shape = jax.ShapeDtypeStruct((R, C), F32)
    nout = 4 if (prod or parts) else 3
    res = _pc(
        body, name, grid=(R // tr,),
        in_specs=[spec] + g_specs + [spec, spec], out_specs=[spec] * nout, out_shape=[shape] * nout,
        compiler_params=_cp(40, ("parallel",)),
    )(w, *g_args, m, v)
    return tuple(res) if nout == 4 else (g, *res)


def _all_gather_two_level(shards, name, ada=None):
    nt = len(shards)
    NK = 9
    ada_out, ada_scratch = _ada_shapes(ada[1].shape[1]) if ada else ([], [])
    na = 3 if ada else 0

    def body(*refs):
        src, dst = refs[:nt], refs[nt + na:2 * nt + na]
        send_sems, recv_sems, local_sems = refs[2 * nt + na + len(ada_out):2 * nt + na + len(ada_out) + 3]
        x, y, c = _me()
        me, sibling = (x, y, c), (x, y, 1 - c)
        xn, yn, dg = (1 - x, y), (x, 1 - y), (1 - x, 1 - y)

        def part(t, dev, half=None):
            r = src[t].shape[0]
            blk = 4 * dev[0] + 2 * dev[1] + dev[2]
            if half is None:
                return dst[t].at[pl.ds(pl.multiple_of(blk * r, 16), r), :]
            return dst[t].at[pl.ds(pl.multiple_of(blk * r + half * (r // 2), 16), r // 2), :]

        def copy(t, k, to, rows, from_src=False):
            return _remote(src[t] if from_src else rows, rows, send_sems.at[t, k], recv_sems.at[t, k], to)

        local = [pltpu.make_async_copy(src[t], part(t, me), local_sems.at[t]) for t in range(nt)]
        for cp in local:
            cp.start()
        sends = []
        for t in range(nt):
            sends += [copy(t, 0, sibling, part(t, me), True), copy(t, 1, (*xn, c), part(t, me), True),
                      copy(t, 2, (*yn, c), part(t, me), True)]
        for cp in sends:
            cp.start()
        if ada:
            _ada_exchange(*refs[nt:nt + na], *refs[2 * nt + na:2 * nt + na + len(ada_out)],
                          *refs[2 * nt + na + len(ada_out) + 3:])

        def arrive_then_pass(t, k, rows, passes):
            copy(t, k, me, rows).wait_recv()
            for cp in passes:
                cp.start()
                sends.append(cp)

        for t in range(nt):
            arrive_then_pass(t, 1, part(t, (*xn, c)), [copy(t, 5, sibling, part(t, (*xn, c))),
                                                       copy(t, 3, (*yn, c), part(t, (*xn, c), 0))])
            arrive_then_pass(t, 2, part(t, (*yn, c)), [copy(t, 6, sibling, part(t, (*yn, c))),
                                                       copy(t, 4, (*xn, c), part(t, (*yn, c), 1))])
            arrive_then_pass(t, 3, part(t, (*dg, c), 0), [copy(t, 7, sibling, part(t, (*dg, c), 0))])
            arrive_then_pass(t, 4, part(t, (*dg, c), 1), [copy(t, 8, sibling, part(t, (*dg, c), 1))])
        for t in range(nt):
            copy(t, 0, me, part(t, sibling)).wait_recv()
            copy(t, 5, me, part(t, (*xn, 1 - c))).wait_recv()
            copy(t, 6, me, part(t, (*yn, 1 - c))).wait_recv()
            copy(t, 7, me, part(t, (*dg, 1 - c), 0)).wait_recv()
            copy(t, 8, me, part(t, (*dg, 1 - c), 1)).wait_recv()
        for cp in sends:
            cp.wait_send()
        for cp in local:
            cp.wait()

    res = _pc(
        body, name,
        in_specs=[HBM_SPEC] * nt + [VMEM_SPEC] * na, out_specs=[HBM_SPEC] * nt + [VMEM_SPEC] * len(ada_out),
        out_shape=_gather_shapes(shards) + ada_out,
        scratch_shapes=[pltpu.SemaphoreType.DMA((nt, NK)), pltpu.SemaphoreType.DMA((nt, NK)),
                        pltpu.SemaphoreType.DMA((nt,))] + ada_scratch,
        compiler_params=pltpu.CompilerParams(has_side_effects=True, vmem_limit_bytes=40 * MIB),
    )(*shards, *(ada or ()))
    return res[:nt], res[nt:]


def _ada_shapes(ncol):
    out = [jax.ShapeDtypeStruct((N_DEV, 1, ncol), F32), jax.ShapeDtypeStruct((N_DEV, 1, D_MODEL), F32)]
    scratch = [pltpu.VMEM((1, D_MODEL), F32), pltpu.VMEM((16, D_MODEL), F32),
               pltpu.VMEM((N_DEV, 1, ncol), F32)] + [pltpu.SemaphoreType.DMA((N_DEV - 1,))] * 4
    return out, scratch


def _ada_exchange(c_ref, w_ref, b_ref, mod_ref, call_ref, cact, cmat, mloc, send1, recv1, send2, recv2):
    x, y, z = _me()
    me = 4 * x + 2 * y + z
    cv = c_ref[...]
    cact[...] = cv * _sig(cv)
    call_ref[me] = cact[...]
    sends = []
    for k in range(1, N_DEV):
        dev, _ = _peer(k)
        cp = _remote(cact, call_ref.at[me], send1.at[k - 1], recv1.at[k - 1], dev)
        cp.start()
        sends.append(cp)
    for k in range(1, N_DEV):
        dev, blk = _peer(k)
        _remote(cact, call_ref.at[blk], send1.at[k - 1], recv1.at[k - 1], dev).wait_recv()
    cmat[...] = jnp.zeros_like(cmat)
    for b in range(N_DEV):
        cmat[b:b + 1, :] = call_ref[b]
    m = jnp.dot(cmat[...].astype(BF16), w_ref[...].astype(BF16), preferred_element_type=F32) + b_ref[...]
    for b in range(N_DEV):
        mloc[b] = m[b:b + 1, :]
    mod_ref[me] = mloc[me]
    for k in range(1, N_DEV):
        dev, blk = _peer(k)
        cp = _remote(mloc.at[blk], mod_ref.at[me], send2.at[k - 1], recv2.at[k - 1], dev)
        cp.start()
        sends.append(cp)
    for k in range(1, N_DEV):
        dev, blk = _peer(k)
        _remote(mloc.at[me], mod_ref.at[blk], send2.at[k - 1], recv2.at[k - 1], dev).wait_recv()
    for cp in sends:
        cp.wait_send()


def _small_gather(pieces):
    widths = [p.shape[1] for p in pieces]
    n = sum(widths)
    npc = len(pieces)

    def body(*refs):
        all_ref, sum_ref, v_ref, send, recv = refs[npc:]
        off = 0
        for r, w in zip(refs[:npc], widths):
            v_ref[:, off:off + w] = r[0:1, :]
            off += w
        x, y, z = _me()
        me = 4 * x + 2 * y + z
        all_ref[me] = v_ref[...]
        sends = []
        for k in range(1, N_DEV):
            dev, _ = _peer(k)
            cp = _remote(v_ref, all_ref.at[me], send.at[k - 1], recv.at[k - 1], dev)
            cp.start()
            sends.append(cp)
        for k in range(1, N_DEV):
            dev, blk = _peer(k)
            _remote(v_ref, all_ref.at[blk], send.at[k - 1], recv.at[k - 1], dev).wait_recv()
        acc = all_ref[0]
        for d in range(1, N_DEV):
            acc = acc + all_ref[d]
        sum_ref[...] = acc
        for cp in sends:
            cp.wait_send()

    return _pc(
        body, "small_gather",
        in_specs=[VMEM_SPEC] * npc, out_specs=[VMEM_SPEC] * 2,
        out_shape=[jax.ShapeDtypeStruct((N_DEV, 1, n), F32), jax.ShapeDtypeStruct((1, n), F32)],
        scratch_shapes=[pltpu.VMEM((1, n), F32)] + [pltpu.SemaphoreType.DMA((N_DEV - 1,))] * 2,
        compiler_params=pltpu.CompilerParams(has_side_effects=True),
    )(*pieces)


def _local_step(x, target, mod, wt_in, w_out_loc, w_pw_loc, conv_w_loc, rel_bias, sinks, conv_b, conv_ln_g,
                conv_ln_b, b_pw, ln_g, ln_b):
    bmap = _bucket_map()
    bias = _bias_table(rel_bias, bmap)
    (h, qkv, g_attn, glu, g_conv), (w_out, w_pw, conv_w_blocks) = _inproj(
        x, mod, wt_in, (w_out_loc, w_pw_loc, conv_w_loc))
    conv_w = conv_w_blocks.reshape(N_DEV, CONV_ROWS, 128).transpose(1, 0, 2).reshape(CONV_ROWS, D_CONV)
    ycat, a_out, u1, p_out = _mixer_fwd(qkv, g_attn, glu, g_conv, bias, sinks, conv_w, conv_b, conv_ln_g,
                                        conv_ln_b, w_pw, b_pw)
    dz, dy, dycat, loss, g_ln_g, g_ln_b, dgate = _outproj_ln(ycat, w_out, x, target, mod, ln_g, ln_b)
    gw_out = _matmul_tn([ycat], dy, "grad_w_out")
    gw_pw, dgc, du1, g_bpw, g_clg, g_clb, g_cb = _conv_bwd_a(dycat, g_conv, p_out, u1, conv_ln_g, conv_ln_b, w_pw)
    dab, g_cw = _conv_bwd_b(du1, glu, conv_w)
    (dqkv, dga, dbias, dsink), (r_out, r_pw) = _attn_bwd(qkv, g_attn, a_out, dycat, bias, sinks,
                                                         scatter=(gw_out, gw_pw))
    g_rb = _relbias_grad(dbias, bmap)
    gwt_in = _matmul_tn([dqkv, dga, dab, dgc], h, "grad_w_in")
    g_cw_blocks = g_cw.reshape(CONV_ROWS, N_DEV, 128).transpose(1, 0, 2).reshape(N_DEV * CONV_ROWS, 128)
    (grad_x, dshift, dscale), (r_in, r_cw) = _dh_kernel([dqkv, dga, dab, dgc], wt_in, dz, x, mod,
                                                        scatter=(gwt_in, g_cw_blocks))
    small = [dshift, dscale, dgate, g_bpw, g_clg, g_clb, g_cb, g_ln_g, g_ln_b,
             g_rb[:, :N_BUCKETS].reshape(1, N_BUCKETS * N_Q_HEADS), dsink, loss]
    return grad_x, r_in, r_out, r_pw, r_cw, small


SMALL_FIELDS = (("dmod", 3 * D_MODEL), ("b_pw", D_CONV), ("conv_ln_g", D_CONV), ("conv_ln_b", D_CONV),
                ("conv_b", D_CONV), ("ln_g", D_MODEL), ("ln_b", D_MODEL), ("rel_bias", N_BUCKETS * N_Q_HEADS),
                ("sinks", 128), ("loss", 128))


def _unpack(vec):
    out, off = {}, 0
    for name, width in SMALL_FIELDS:
        out[name] = vec[:, off:off + width]
        off += width
    return out


def kernel(x, c, w_ada, b_ada, w_in, rel_bias, sinks, conv_w, conv_b, conv_ln_g, conv_ln_b, w_pw, b_pw, w_out, ln_g, ln_b, loss_target, m_w_ada, m_b_ada, m_w_in, m_rel_bias, m_sinks, m_conv_w, m_conv_b, m_conv_ln_g, m_conv_ln_b, m_w_pw, m_b_pw, m_w_out, m_ln_g, m_ln_b, v_w_ada, v_b_ada, v_w_in, v_rel_bias, v_sinks, v_conv_w, v_conv_b, v_conv_ln_g, v_conv_ln_b, v_w_pw, v_b_pw, v_w_out, v_ln_g, v_ln_b):
    xi, yi, ci = _me()
    me = 4 * xi + 2 * yi + ci
    ncol = w_ada.shape[2]

    wt_in_loc = w_in[0].T.astype(BF16)
    conv_w_loc = jnp.pad(conv_w[0], ((0, CONV_ROWS - CONV_WIDTH), (0, 0)))

    b_ada_cols = lax.dynamic_slice(b_ada, (0, me * ncol), (1, ncol))
    (wt_in,), (mod_blocks, c_all) = _all_gather_two_level([wt_in_loc], "gather_w_in",
                                                          ada=(c, w_ada[0], b_ada_cols))
    mod = mod_blocks.reshape(1, 3 * D_MODEL)

    grad_x, r_in, r_out, r_pw, r_cw, small = _local_step(
        x[0], loss_target[0], mod, wt_in, w_out[0].astype(BF16), w_pw[0].astype(BF16), conv_w_loc, rel_bias,
        sinks, conv_b, conv_ln_g, conv_ln_b, b_pw, ln_g, ln_b)

    gathered, summed = _small_gather(small)
    tot = _unpack(summed)
    dmod_all = gathered[:, 0, :3 * D_MODEL]
    loss = tot["loss"][0, 0]

    ct = jnp.zeros((D_MODEL, 128), BF16).at[:, :N_DEV].set(c_all[:, 0, :].T.astype(BF16))
    dm = jnp.zeros((128, ncol), BF16).at[:N_DEV, :].set(
        lax.dynamic_slice(dmod_all, (0, me * ncol), (N_DEV, ncol)).astype(BF16))
    g_w_ada = (ct, dm)

    g_conv_w = _sum8(r_cw.reshape(N_DEV, CONV_ROWS, 128), "sum_conv_w")[:CONV_WIDTH]

    grads = {"w_ada": g_w_ada, "conv_w": g_conv_w,
             "w_pw": r_pw.reshape(N_DEV, D_CONV // N_DEV, D_CONV),
             "w_out": r_out.reshape(N_DEV, D_MODEL // N_DEV, D_MODEL)}
    params = {"w_ada": (w_ada, m_w_ada, v_w_ada), "conv_w": (conv_w, m_conv_w, v_conv_w),
              "w_pw": (w_pw, m_w_pw, v_w_pw), "w_out": (w_out, m_w_out, v_w_out)}
    res = {}
    for name, g in grads.items():
        w_, m_, v_ = params[name]
        res[name] = tuple(a[None] for a in _adamw(w_[0], g, m_[0], v_[0], "adamw_" + name))
    upd = _adamw(w_in[0].T, r_in.reshape(N_DEV, D_IN // N_DEV, D_MODEL), m_w_in[0].T, v_w_in[0].T, "adamw_w_in")
    res["w_in"] = tuple(a.T[None] for a in upd)

    rows = {"b_ada": (b_ada, m_b_ada, v_b_ada), "b_pw": (b_pw, m_b_pw, v_b_pw),
            "conv_ln_g": (conv_ln_g, m_conv_ln_g, v_conv_ln_g), "conv_ln_b": (conv_ln_b, m_conv_ln_b, v_conv_ln_b),
            "conv_b": (conv_b, m_conv_b, v_conv_b), "ln_g": (ln_g, m_ln_g, v_ln_g), "ln_b": (ln_b, m_ln_b, v_ln_b),
            "sinks": (sinks, m_sinks, v_sinks)}
    field_of = {"b_ada": "dmod"}
    offsets, off = {}, 0
    for fname, width in SMALL_FIELDS:
        offsets[fname] = off
        off += width
    row_res = _adamw_rows(summed, [offsets[field_of.get(n, n)] for n in rows], list(rows.values()), "adamw_rows")
    for n, r in zip(rows, row_res):
        res[n] = r
    flat = lambda a: a.T.reshape(1, N_BUCKETS * N_Q_HEADS)
    upd = _adamw(flat(rel_bias), tot["rel_bias"], flat(m_rel_bias), flat(v_rel_bias), "adamw_rel_bias")
    res["rel_bias"] = tuple(a.reshape(N_Q_HEADS, N_BUCKETS).T for a in upd)

    order = ["w_ada", "b_ada", "w_in", "rel_bias", "sinks", "conv_w", "conv_b", "conv_ln_g", "conv_ln_b", "w_pw",
             "b_pw", "w_out", "ln_g", "ln_b"]
    out = [loss, grad_x[None]]
    for j in range(4):
        out += [res[n][j] for n in order]
    return tuple(out)
```
